```python
import math
import jax, jax.numpy as jnp
from jax import lax
import numpy as np


D_MODEL = 2048
BATCH = 4
SEQ = 2048
DEPTH = 1

GRID_W = 64
CTX_LEN = 256

D_MIX = D_MODEL
D_SSM = D_MIX // 2
D_CONV = D_MIX - D_SSM
SSM_GROUP = 16
SSM_GROUPS = D_SSM // SSM_GROUP
SSM_STATE = 64
DT_MIN = 1e-3
DT_MAX = 1e-1
CONV_WIDTH = 3
IN_SPLITS = (D_SSM, D_SSM + D_CONV, D_SSM + 2 * D_CONV)
D_IN = D_SSM + 3 * D_CONV

N_EXPERTS = 64
N_EXPERT_GROUPS = 8
TOPK_GROUPS = 4
TOP_K = 8
D_EXPERT = 512
D_SHARED = 512
ROUTED_SCALE = 2.5
EXPERT_BLOCK = 128

N_MOD = 6
EPS = 1e-6

kernel_name = 'hybrid_s5_shortconv_moe_dit_layer'


def rmsnorm(x, g):
    xf = x.astype(jnp.float32)
    y = xf * lax.rsqrt(jnp.mean(xf * xf, axis=-1, keepdims=True) + EPS)
    return y.astype(x.dtype) * g


def swiglu(x, wg, wu, wd):
    return (jax.nn.silu(x @ wg) * (x @ wu)) @ wd


def zoh(lam_re, lam_im, b_re, b_im, log_dt):
    lam = lax.complex(lam_re.astype(jnp.float32), lam_im.astype(jnp.float32))
    dt = jnp.exp(log_dt.astype(jnp.float32))[:, None]
    lam_bar = jnp.exp(lam * dt)
    b = lax.complex(b_re.astype(jnp.float32), b_im.astype(jnp.float32))
    b_bar = ((lam_bar - 1.0) / lam)[..., None] * b
    return lam_bar, b_bar


def _ssm_combine(left, right):
    a1, b1 = left
    a2, b2 = right
    return a1 * a2, a2 * b1 + b2


def linear_scan(bu, lam_bar, h0, reverse):
    if reverse:
        bu = jnp.flip(bu, axis=1)
    if h0 is not None:
        bu = bu.at[:, 0].add(lam_bar * h0)
    a = jnp.broadcast_to(lam_bar, (1, bu.shape[1]) + lam_bar.shape)
    _, h = lax.associative_scan(_ssm_combine, (a, bu), axis=1)
    return jnp.flip(h, axis=1) if reverse else h


def ssm_readout(h, c_mat):
    y = jnp.einsum('blgp,gnp->blgn', h, c_mat).real
    return y.reshape(h.shape[0], h.shape[1], D_SSM)


def s5_glu(y, w_glu):
    g = jax.nn.gelu(y)
    ga, gb = jnp.split(g @ w_glu, 2, axis=-1)
    return ga * jax.nn.sigmoid(gb)


def s5_bidirectional(ux, uc, lam_re, lam_im, b_re, b_im, c_re, c_im, log_dt, d_skip, w_glu, ctx_out):
    bsz, seq, _ = ux.shape
    ugx = ux.astype(jnp.float32).reshape(bsz, seq, SSM_GROUPS, SSM_GROUP)
    ugc = uc.astype(jnp.float32).reshape(bsz, uc.shape[1], SSM_GROUPS, SSM_GROUP)
    yx = d_skip * ux
    yc = d_skip * uc if ctx_out else None
    for d in range(2):
        rev = d == 1
        lam_bar, b_bar = zoh(lam_re[d], lam_im[d], b_re[d], b_im[d], log_dt[d])
        c_mat = lax.complex(c_re[d].astype(jnp.float32), c_im[d].astype(jnp.float32))
        hc = linear_scan(jnp.einsum('blgn,gpn->blgp', ugc, b_bar), lam_bar, None, rev)
        h0 = hc[:, 0] if rev else hc[:, -1]
        hx = linear_scan(jnp.einsum('blgn,gpn->blgp', ugx, b_bar), lam_bar, h0, rev)
        yx = yx + ssm_readout(hx, c_mat).astype(ux.dtype)
        if ctx_out:
            yc = yc + ssm_readout(hc, c_mat).astype(uc.dtype)
    return s5_glu(yx, w_glu), (s5_glu(yc, w_glu) if ctx_out else None)


def centred_conv(z, w, b):
    pad = CONV_WIDTH // 2
    n = z.shape[-2]
    zp = jnp.pad(z, [(0, 0)] * (z.ndim - 2) + [(pad, pad), (0, 0)])
    out = b
    for j in range(CONV_WIDTH):
        out = out + zp[..., j:j + n, :] * w[j]
    return out


def short_conv_mixer(bg, cg, v, w, b, rows):
    z = cg * v
    if rows is None:
        y = centred_conv(z, w, b)
    else:
        bsz, seq, ch = z.shape
        y = centred_conv(z.reshape(bsz, rows, GRID_W, ch), w, b).reshape(bsz, seq, ch)
    return bg * y


def merge_heads(ssm_y, conv_y, mix_norm_g, w_out):
    heads = jnp.concatenate([rmsnorm(ssm_y, mix_norm_g[:D_SSM]),
                             rmsnorm(conv_y, mix_norm_g[D_SSM:])], axis=-1)
    return heads @ w_out


def route(h, router_w, router_bias):
    t = h.shape[0]
    scores = jax.nn.sigmoid((h @ router_w).astype(jnp.float32))
    biased = scores + router_bias.astype(jnp.float32)
    grouped = biased.reshape(t, N_EXPERT_GROUPS, N_EXPERTS // N_EXPERT_GROUPS)
    group_score = jnp.sum(lax.top_k(grouped, 2)[0], axis=-1)
    _, top_groups = lax.top_k(group_score, TOPK_GROUPS)
    group_mask = jnp.sum(jax.nn.one_hot(top_groups, N_EXPERT_GROUPS, dtype=jnp.float32), axis=-2) > 0
    expert_mask = jnp.repeat(group_mask, N_EXPERTS // N_EXPERT_GROUPS, axis=-1)
    _, idx = lax.top_k(jnp.where(expert_mask, biased, -jnp.inf), TOP_K)
    w = jnp.take_along_axis(scores, idx, axis=-1)
    w = w / jnp.sum(w, axis=-1, keepdims=True) * ROUTED_SCALE
    return idx, w


def routed_experts(h, idx, wts, w_gate, w_up, w_down):
    t, d = h.shape
    n_assign = t * TOP_K
    flat_e = idx.reshape(n_assign)
    flat_tok = jnp.repeat(jnp.arange(t, dtype=jnp.int32), TOP_K)
    flat_w = wts.reshape(n_assign)
    order = jnp.argsort(flat_e)
    se, stok, sw = flat_e[order], flat_tok[order], flat_w[order]
    counts = jnp.bincount(flat_e, length=N_EXPERTS)
    starts = jnp.cumsum(counts) - counts
    padded = (counts + EXPERT_BLOCK - 1) // EXPERT_BLOCK * EXPERT_BLOCK
    pends = jnp.cumsum(padded)
    pstarts = pends - padded
    dest = pstarts[se] + (jnp.arange(n_assign) - starts[se])
    n_blocks = -(-n_assign // EXPERT_BLOCK) + N_EXPERTS
    n_rows = n_blocks * EXPERT_BLOCK
    rows_tok = jnp.zeros((n_rows,), jnp.int32).at[dest].set(stok)
    rows_w = jnp.zeros((n_rows,), h.dtype).at[dest].set(sw.astype(h.dtype))
    block_start = jnp.arange(n_blocks, dtype=pends.dtype) * EXPERT_BLOCK
    block_e = jnp.minimum(jnp.searchsorted(pends, block_start, side='right'), N_EXPERTS - 1)

    def block_ffn(args):
        tok, e = args
        return swiglu(h[tok], w_gate[e], w_up[e], w_down[e])

    ys = lax.map(block_ffn, (rows_tok.reshape(n_blocks, EXPERT_BLOCK), block_e))
    ys = ys.reshape(n_rows, d) * rows_w[:, None]
    return jnp.zeros_like(h).at[rows_tok].add(ys)


def moe_ffn(h, router_w, router_bias, w_gate, w_up, w_down, ws_gate, ws_up, ws_down):
    shape = h.shape
    ht = h.reshape(-1, shape[-1])
    idx, wts = route(ht, router_w, router_bias)
    routed = routed_experts(ht, idx, wts, w_gate, w_up, w_down)
    shared = swiglu(ht, ws_gate, ws_up, ws_down)
    return (routed + shared).reshape(shape)


def setup_inputs(seed: int = 0) -> dict:
    key = jax.random.key(seed)
    ks = jax.random.split(key, 32)
    f32 = jnp.float32

    def nrm(k, shape, scale):
        return jax.random.normal(k, shape, f32) * scale

    n_idx = jnp.arange(SSM_STATE, dtype=f32)
    return {
        'x': nrm(ks[0], (BATCH, SEQ, D_MODEL), 1.0),
        'c': nrm(ks[1], (BATCH, D_MODEL), 1.0),
        'ctx': nrm(ks[2], (BATCH, CTX_LEN, D_MODEL), 1.0),
        'c_ctx': nrm(ks[3], (D_MODEL,), 1.0),
        'norm1_g': 1.0 + nrm(ks[4], (DEPTH, D_MODEL), 0.02),
        'norm2_g': 1.0 + nrm(ks[5], (DEPTH, D_MODEL), 0.02),
        'w_ada': nrm(ks[6], (DEPTH, D_MODEL, N_MOD * D_MODEL), 0.5 * D_MODEL ** -0.5),
        'b_ada': nrm(ks[7], (DEPTH, N_MOD * D_MODEL), 0.02),
        'w_in': nrm(ks[8], (DEPTH, D_MODEL, D_IN), D_MODEL ** -0.5),
        'ssm_lam_re': -0.5 + nrm(ks[9], (DEPTH, 2, SSM_GROUPS, SSM_STATE), 0.01),
        'ssm_lam_im': math.pi * n_idx + nrm(ks[10], (DEPTH, 2, SSM_GROUPS, SSM_STATE), 0.01),
        'ssm_b_re': nrm(ks[11], (DEPTH, 2, SSM_GROUPS, SSM_STATE, SSM_GROUP), (2 * SSM_GROUP) ** -0.5),
        'ssm_b_im': nrm(ks[12], (DEPTH, 2, SSM_GROUPS, SSM_STATE, SSM_GROUP), (2 * SSM_GROUP) ** -0.5),
        'ssm_c_re': nrm(ks[13], (DEPTH, 2, SSM_GROUPS, SSM_GROUP, SSM_STATE), (2 * SSM_STATE) ** -0.5),
        'ssm_c_im': nrm(ks[14], (DEPTH, 2, SSM_GROUPS, SSM_GROUP, SSM_STATE), (2 * SSM_STATE) ** -0.5),
        'ssm_log_dt': jax.random.uniform(ks[15], (DEPTH, 2, SSM_GROUPS), f32,
                                         minval=math.log(DT_MIN), maxval=math.log(DT_MAX)),
        'ssm_d': nrm(ks[16], (DEPTH, D_SSM), 0.5),
        'ssm_w_glu': nrm(ks[17], (DEPTH, D_SSM, 2 * D_SSM), D_SSM ** -0.5),
        'conv_w': nrm(ks[18], (DEPTH, CONV_WIDTH, D_CONV), CONV_WIDTH ** -0.5),
        'conv_b': nrm(ks[19], (DEPTH, D_CONV), 0.01),
        'mix_norm_g': 1.0 + nrm(ks[20], (DEPTH, D_MIX), 0.02),
        'w_out': nrm(ks[21], (DEPTH, D_MIX, D_MODEL), D_MIX ** -0.5),
        'router_w': nrm(ks[22], (DEPTH, D_MODEL, N_EXPERTS), D_MODEL ** -0.5),
        'router_bias': nrm(ks[23], (DEPTH, N_EXPERTS), 0.01),
        'exp_w_gate': nrm(ks[24], (DEPTH, N_EXPERTS, D_MODEL, D_EXPERT), D_MODEL ** -0.5),
        'exp_w_up': nrm(ks[25], (DEPTH, N_EXPERTS, D_MODEL, D_EXPERT), D_MODEL ** -0.5),
        'exp_w_down': nrm(ks[26], (DEPTH, N_EXPERTS, D_EXPERT, D_MODEL), D_EXPERT ** -0.5),
        'shared_w_gate': nrm(ks[27], (DEPTH, D_MODEL, D_SHARED), D_MODEL ** -0.5),
        'shared_w_up': nrm(ks[28], (DEPTH, D_MODEL, D_SHARED), D_MODEL ** -0.5),
        'shared_w_down': nrm(ks[29], (DEPTH, D_SHARED, D_MODEL), D_SHARED ** -0.5),
        'final_g': 1.0 + nrm(ks[30], (D_MODEL,), 0.02),
    }


def reference(x, c, ctx, c_ctx, norm1_g, norm2_g, w_ada, b_ada, w_in,
              ssm_lam_re, ssm_lam_im, ssm_b_re, ssm_b_im, ssm_c_re, ssm_c_im, ssm_log_dt,
              ssm_d, ssm_w_glu, conv_w, conv_b, mix_norm_g, w_out,
              router_w, router_bias, exp_w_gate, exp_w_up, exp_w_down,
              shared_w_gate, shared_w_up, shared_w_down, final_g):
    rows = x.shape[1] // GRID_W
    for layer in range(DEPTH):
        ctx_out = layer < DEPTH - 1
        mod_x = jnp.split(jax.nn.silu(c) @ w_ada[layer] + b_ada[layer], N_MOD, axis=-1)
        sh1, sc1, g1, sh2, sc2, g2 = [m[:, None, :] for m in mod_x]
        csh1, csc1, cg1, csh2, csc2, cg2 = jnp.split(
            jax.nn.silu(c_ctx) @ w_ada[layer] + b_ada[layer], N_MOD, axis=-1)

        hx = rmsnorm(x, norm1_g[layer]) * (1 + sc1) + sh1
        hc = rmsnorm(ctx, norm1_g[layer]) * (1 + csc1) + csh1
        ux, bx, cx, vx = jnp.split(hx @ w_in[layer], IN_SPLITS, axis=-1)
        if ctx_out:
            uc, bc, cc, vc = jnp.split(hc @ w_in[layer], IN_SPLITS, axis=-1)
        else:
            uc = hc @ w_in[layer][:, :D_SSM]
        ssm_x, ssm_c = s5_bidirectional(ux, uc, ssm_lam_re[layer], ssm_lam_im[layer],
                                        ssm_b_re[layer], ssm_b_im[layer], ssm_c_re[layer],
                                        ssm_c_im[layer], ssm_log_dt[layer], ssm_d[layer],
                                        ssm_w_glu[layer], ctx_out)
        conv_x = short_conv_mixer(bx, cx, vx, conv_w[layer], conv_b[layer], rows)
        x = x + g1 * merge_heads(ssm_x, conv_x, mix_norm_g[layer], w_out[layer])

        hx2 = rmsnorm(x, norm2_g[layer]) * (1 + sc2) + sh2
        x = x + g2 * moe_ffn(hx2, router_w[layer], router_bias[layer], exp_w_gate[layer],
                             exp_w_up[layer], exp_w_down[layer], shared_w_gate[layer],
                             shared_w_up[layer], shared_w_down[layer])

        if ctx_out:
            conv_c = short_conv_mixer(bc, cc, vc, conv_w[layer], conv_b[layer], None)
            ctx = ctx + cg1 * merge_heads(ssm_c, conv_c, mix_norm_g[layer], w_out[layer])
            hc2 = rmsnorm(ctx, norm2_g[layer]) * (1 + csc2) + csh2
            ctx = ctx + cg2 * moe_ffn(hc2, router_w[layer], router_bias[layer], exp_w_gate[layer],
                                      exp_w_up[layer], exp_w_down[layer], shared_w_gate[layer],
                                      shared_w_up[layer], shared_w_down[layer])
    return rmsnorm(x, final_g)
```

```python
import functools
import math

import jax
import jax.numpy as jnp
from jax import lax
from jax.experimental import pallas as pl
from jax.experimental.pallas import tpu as pltpu

D_MODEL = 2048
BATCH = 4
SEQ = 2048
CTX_LEN = 256
GRID_W = 64
D_SSM = 1024
D_CONV = 1024
SSM_GROUP = 16
SSM_GROUPS = 64
SSM_STATE = 64
N_EXPERTS = 64
N_EXPERT_GROUPS = 8
GROUP_SIZE = N_EXPERTS // N_EXPERT_GROUPS
TOPK_GROUPS = 4
TOP_K = 8
D_EXPERT = 512
D_SHARED = 512
ROUTED_SCALE = 2.5
N_MOD = 6
EPS = 1e-6

N_TOK = BATCH * SEQ
N_CTX_TOK = BATCH * CTX_LEN

CHUNK = 16
CHUNK_W = CHUNK * SSM_GROUP
STATE_W = 2 * SSM_STATE
N_CTX_CHUNKS = CTX_LEN // CHUNK
N_X_CHUNKS = SEQ // CHUNK
N_SEQ_CHUNKS = N_CTX_CHUNKS + N_X_CHUNKS
SCAN_ROWS = N_SEQ_CHUNKS * BATCH
S5_GROUPS_PER_STEP = 8

TM = 256
VMEM_LIMIT = 56 * 1024 * 1024

_F32 = jnp.float32
_BF16 = jnp.bfloat16


def _cparams(*sem):
    return pltpu.CompilerParams(dimension_semantics=sem, vmem_limit_bytes=VMEM_LIMIT)


def _const_spec(shape):
    nd = len(shape)
    return pl.BlockSpec(shape, lambda *_: (0,) * nd, pipeline_mode=pl.Buffered(1))


def _rms_scale(xf):
    return lax.rsqrt(jnp.mean(xf * xf, axis=-1, keepdims=True) + EPS)


def _silu(x):
    return x * jax.nn.sigmoid(x)


def _ada_kernel(c_ref, w_ref, b_ref, o_ref):
    s = _silu(c_ref[...])
    o_ref[...] = jnp.dot(s, w_ref[...], preferred_element_type=_F32) + b_ref[...]


def _ada_mod(c8, w_ada, b_ada):
    n = w_ada.shape[1]
    tn = 1024
    return pl.pallas_call(
        _ada_kernel,
        grid=(n // tn,),
        in_specs=[pl.BlockSpec((8, D_MODEL), lambda j: (0, 0)),
                  pl.BlockSpec((D_MODEL, tn), lambda j: (0, j)),
                  pl.BlockSpec((1, tn), lambda j: (0, j))],
        out_specs=pl.BlockSpec((8, tn), lambda j: (0, j)),
        out_shape=jax.ShapeDtypeStruct((8, n), _F32),
        compiler_params=_cparams("arbitrary"),
        name="ada_mod",
    )(c8, w_ada, b_ada)


def _modulated_norm(x_ref, mod_ref, g_ref, shift_row, scale_row):
    xf = x_ref[...]
    m = mod_ref[0]
    h = xf * _rms_scale(xf) * g_ref[...]
    return h * (1.0 + m[scale_row:scale_row + 1]) + m[shift_row:shift_row + 1]


def _inproj_kernel(x_ref, mod_ref, g_ref, w_ref, cw_ref, cb_ref, u_ref, conv_ref):
    h = _modulated_norm(x_ref, mod_ref, g_ref, 0, 1).astype(_BF16)
    u_ref[...] = jnp.dot(h, w_ref[:, 0:D_SSM], preferred_element_type=_F32)
    tm = x_ref.shape[0]
    pos = lax.broadcasted_iota(jnp.int32, (tm, 1), 0) % GRID_W
    not_first = (pos != 0).astype(_F32)
    not_last = (pos != GRID_W - 1).astype(_F32)
    cw = cw_ref[...]
    nc = 256
    for j in range(D_CONV // nc):
        lo = j * nc
        bg = jnp.dot(h, w_ref[:, D_SSM + lo:D_SSM + lo + nc], preferred_element_type=_F32)
        cg = jnp.dot(h, w_ref[:, D_SSM + D_CONV + lo:D_SSM + D_CONV + lo + nc], preferred_element_type=_F32)
        v = jnp.dot(h, w_ref[:, D_SSM + 2 * D_CONV + lo:D_SSM + 2 * D_CONV + lo + nc],
                    preferred_element_type=_F32)
        z = cg * v
        z_prev = pltpu.roll(z, 1, axis=0) * not_first
        z_next = pltpu.roll(z, tm - 1, axis=0) * not_last
        y = (cb_ref[:, lo:lo + nc] + z_prev * cw[0:1, lo:lo + nc] + z * cw[1:2, lo:lo + nc]
             + z_next * cw[2:3, lo:lo + nc])
        conv_ref[:, lo:lo + nc] = bg * y


def _inproj(x2, mod3, norm_g, w_in_bf, conv_w, conv_b):
    d_in = w_in_bf.shape[1]
    tiles_per_batch = SEQ // TM
    return pl.pallas_call(
        _inproj_kernel,
        grid=(N_TOK // TM,),
        in_specs=[pl.BlockSpec((TM, D_MODEL), lambda i: (i, 0)),
                  pl.BlockSpec((1, N_MOD, D_MODEL), lambda i: (i // tiles_per_batch, 0, 0)),
                  _const_spec((1, D_MODEL)),
                  _const_spec((D_MODEL, d_in)),
                  _const_spec((8, D_CONV)),
                  _const_spec((1, D_CONV))],
        out_specs=[pl.BlockSpec((TM, D_SSM), lambda i: (i, 0)),
                   pl.BlockSpec((TM, D_CONV), lambda i: (i, 0))],
        out_shape=[jax.ShapeDtypeStruct((N_TOK, D_SSM), _F32),
                   jax.ShapeDtypeStruct((N_TOK, D_CONV), _F32)],
        compiler_params=_cparams("arbitrary"),
        name="in_proj",
    )(x2, mod3, norm_g, w_in_bf, conv_w, conv_b)


def _ctxproj_kernel(x_ref, mod_ref, g_ref, w_ref, u_ref):
    h = _modulated_norm(x_ref, mod_ref, g_ref, 0, 1).astype(_BF16)
    u_ref[...] = jnp.dot(h, w_ref[...], preferred_element_type=_F32)


def _ctxproj(ctx2, mod3, norm_g, w_u_bf):
    return pl.pallas_call(
        _ctxproj_kernel,
        grid=(N_CTX_TOK // TM,),
        in_specs=[pl.BlockSpec((TM, D_MODEL), lambda i: (i, 0)),
                  pl.BlockSpec((1, N_MOD, D_MODEL), lambda i: (BATCH, 0, 0)),
                  _const_spec((1, D_MODEL)),
                  _const_spec((D_MODEL, D_SSM))],
        out_specs=pl.BlockSpec((TM, D_SSM), lambda i: (i, 0)),
        out_shape=jax.ShapeDtypeStruct((N_CTX_TOK, D_SSM), _F32),
        compiler_params=_cparams("arbitrary"),
        name="ctx_proj",
    )(ctx2, mod3, norm_g, w_u_bf)


def _s5_param_kernel(pcol_ref, prow_ref, ct_ref, bt_ref, toep_ref, bpow_ref, cpow_ref, dec_ref):
    lane = lambda shape: lax.broadcasted_iota(jnp.int32, shape, 1)
    sub = lambda shape: lax.broadcasted_iota(jnp.int32, shape, 0)

    rep = (lane((SSM_GROUP, CHUNK_W)) % SSM_GROUP == sub((SSM_GROUP, CHUNK_W))).astype(_F32)
    sgn_col = jnp.where(sub((STATE_W, 1)) < SSM_STATE, 1.0, -1.0).astype(_F32)
    sgn_row = jnp.where(lane((1, STATE_W)) < SSM_STATE, -1.0, 1.0).astype(_F32)
    blk_row = (lane((1, CHUNK_W)) // SSM_GROUP).astype(_F32)
    blk_col = (sub((CHUNK_W, 1)) // SSM_GROUP).astype(_F32)

    strips = []
    for d in range(2):
        pc = pcol_ref[d, 0]
        a_col = pc[:, 0:1] * pc[:, 2:3]
        th_col = pc[:, 1:2] * pc[:, 2:3]
        pr = prow_ref[d, 0]
        lam_re, lam_im, dt = pr[0:1], pr[1:2], pr[2:3]
        a_row = lam_re * dt
        th_row = lam_im * dt

        ct = jnp.dot(ct_ref[d, 0], rep, preferred_element_type=_F32,
                     precision=lax.Precision.HIGHEST)
        ca = ct * sgn_col
        cb = -pltpu.roll(ct, SSM_STATE, axis=0)

        def cpow(e_row):
            mag = jnp.exp(a_col * e_row)
            ang = th_col * e_row
            return mag * jnp.cos(ang) * ca + mag * jnp.sin(ang) * cb

        mag1 = jnp.exp(a_row)
        nr = mag1 * jnp.cos(th_row) - 1.0
        ni = mag1 * jnp.sin(th_row)
        den = lam_re * lam_re + lam_im * lam_im
        kr = (nr * lam_re + ni * lam_im) / den
        ki = (ni * lam_re - nr * lam_im) / den
        b1 = bt_ref[d, 0]
        b2 = pltpu.roll(b1, SSM_STATE, axis=1) * sgn_row
        u1 = kr * b1 + ki * b2
        u2 = kr * b2 - ki * b1
        u1t = jnp.concatenate([u1] * CHUNK, axis=0)
        u2t = jnp.concatenate([u2] * CHUNK, axis=0)

        def bpow(e_col):
            mag = jnp.exp(a_row * e_col)
            ang = th_row * e_col
            return mag * jnp.cos(ang) * u1t + mag * jnp.sin(ang) * u2t

        if d == 0:
            e_strip, e_b, e_c = blk_row, (CHUNK - 1.0) - blk_col, blk_row + 1.0
        else:
            e_strip, e_b, e_c = (CHUNK - 1.0) - blk_row, blk_col, float(CHUNK) - blk_row
        strips.append(jnp.dot(u1, cpow(e_strip), preferred_element_type=_F32,
                              precision=lax.Precision.HIGHEST))
        bpow_ref[0, :, d * STATE_W:(d + 1) * STATE_W] = bpow(e_b).astype(_BF16)
        cpow_ref[0, d * STATE_W:(d + 1) * STATE_W, :] = cpow(e_c).astype(_BF16)

        mag16 = jnp.exp(a_row * float(CHUNK))
        dec_ref[0, 2 * d:2 * d + 1, :] = mag16 * jnp.cos(th_row * float(CHUNK))
        dec_ref[0, 2 * d + 1:2 * d + 2, :] = mag16 * jnp.sin(th_row * float(CHUNK)) * sgn_row

    zeros = jnp.zeros((SSM_GROUP, CHUNK_W), _F32)
    strip = (jnp.concatenate([strips[1], zeros], axis=1)
             + pltpu.roll(jnp.concatenate([strips[0], zeros], axis=1), CHUNK_W - SSM_GROUP, axis=1))
    for i in range(CHUNK):
        off = (CHUNK - 1 - i) * SSM_GROUP
        win = strip if off == 0 else pltpu.roll(strip, 2 * CHUNK_W - off, axis=1)
        toep_ref[0, i * SSM_GROUP:(i + 1) * SSM_GROUP, :] = win[:, 0:CHUNK_W].astype(_BF16)


def _s5_params(pcol, prow, ct2, bt1):
    g_spec = lambda shape: pl.BlockSpec(shape, lambda g: (0, g, 0, 0))
    o_spec = lambda shape: pl.BlockSpec(shape, lambda g: (g, 0, 0))
    return pl.pallas_call(
        _s5_param_kernel,
        grid=(SSM_GROUPS,),
        in_specs=[g_spec((2, 1, STATE_W, 4)), g_spec((2, 1, 8, STATE_W)),
                  g_spec((2, 1, STATE_W, SSM_GROUP)), g_spec((2, 1, SSM_GROUP, STATE_W))],
        out_specs=[o_spec((1, CHUNK_W, CHUNK_W)), o_spec((1, CHUNK_W, 2 * STATE_W)),
                   o_spec((1, 2 * STATE_W, CHUNK_W)), o_spec((1, 4, STATE_W))],
        out_shape=[jax.ShapeDtypeStruct((SSM_GROUPS, CHUNK_W, CHUNK_W), _BF16),
                   jax.ShapeDtypeStruct((SSM_GROUPS, CHUNK_W, 2 * STATE_W), _BF16),
                   jax.ShapeDtypeStruct((SSM_GROUPS, 2 * STATE_W, CHUNK_W), _BF16),
                   jax.ShapeDtypeStruct((SSM_GROUPS, 4, STATE_W), _F32)],
        compiler_params=_cparams("arbitrary"),
        name="s5_params",
    )(pcol, prow, ct2, bt1)


def _s5_kernel(u_ref, toep_ref, bpow_ref, cpow_ref, dec_ref, y_ref, s_ref, h_ref):
    ng = u_ref.shape[0]
    for g in range(ng):
        s_ref[g] = jnp.dot(u_ref[g], bpow_ref[g], preferred_element_type=_F32)

    pair = 2 * BATCH
    n_ctx_pairs = N_CTX_CHUNKS // 2
    n_pairs = N_SEQ_CHUNKS // 2
    lower = lax.broadcasted_iota(jnp.int32, (pair, STATE_W), 0) < BATCH
    zero = jnp.zeros((pair, STATE_W), _F32)

    def step(t, carry):
        tb = jnp.where(t < n_ctx_pairs, n_ctx_pairs - 1 - t, n_pairs + n_ctx_pairs - 1 - t)
        rf = pl.multiple_of(t * pair, pair)
        rb = pl.multiple_of(tb * pair, pair)
        out = []
        for g in range(ng):
            hf, hb = carry[2 * g], carry[2 * g + 1]
            dg = dec_ref[g]
            mul_f = lambda v: dg[0:1] * v + dg[1:2] * pltpu.roll(v, SSM_STATE, axis=1)
            mul_b = lambda v: dg[2:3] * v + dg[3:4] * pltpu.roll(v, SSM_STATE, axis=1)

            sf = s_ref[g, pl.ds(rf, pair), 0:STATE_W]
            mid = mul_f(hf) + jnp.where(lower, sf, pltpu.roll(sf, BATCH, axis=0))
            h_ref[g, pl.ds(rf, pair), 0:STATE_W] = jnp.where(lower, hf, mid)
            end = mul_f(mid) + sf
            hf = jnp.where(lower, pltpu.roll(end, BATCH, axis=0), end)

            sb = s_ref[g, pl.ds(rb, pair), STATE_W:2 * STATE_W]
            mid = mul_b(hb) + jnp.where(lower, pltpu.roll(sb, BATCH, axis=0), sb)
            h_ref[g, pl.ds(rb, pair), STATE_W:2 * STATE_W] = jnp.where(lower, mid, hb)
            end = mul_b(mid) + sb
            hb = jnp.where(lower, end, pltpu.roll(end, BATCH, axis=0))
            out += [hf, hb]
        return tuple(out)

    lax.fori_loop(0, n_pairs, step, (zero,) * (2 * ng))

    x0 = N_CTX_CHUNKS * BATCH
    for g in range(ng):
        ux = u_ref[g, x0:SCAN_ROWS, :]
        hx = h_ref[g, x0:SCAN_ROWS, :].astype(_BF16)
        y_ref[g] = (jnp.dot(ux, toep_ref[g], preferred_element_type=_F32)
                    + jnp.dot(hx, cpow_ref[g], preferred_element_type=_F32))


def _s5_scan(u2, toep, bpow, cpow, dec):
    ng = S5_GROUPS_PER_STEP
    n_x_rows = N_X_CHUNKS * BATCH
    spec = lambda r, c: pl.BlockSpec((ng, r, c), lambda i: (i, 0, 0))
    return pl.pallas_call(
        _s5_kernel,
        grid=(SSM_GROUPS // ng,),
        in_specs=[spec(SCAN_ROWS, CHUNK_W), spec(CHUNK_W, CHUNK_W), spec(CHUNK_W, 2 * STATE_W),
                  spec(2 * STATE_W, CHUNK_W), spec(4, STATE_W)],
        out_specs=spec(n_x_rows, CHUNK_W),
        out_shape=jax.ShapeDtypeStruct((SSM_GROUPS, n_x_rows, CHUNK_W), _F32),
        scratch_shapes=[pltpu.VMEM((ng, SCAN_ROWS, 2 * STATE_W), _F32),
                        pltpu.VMEM((ng, SCAN_ROWS, 2 * STATE_W), _F32)],
        compiler_params=_cparams("arbitrary"),
        name="s5_scan",
    )(u2, toep, bpow, cpow, dec)


def _merge_kernel(ys_ref, u_ref, conv_ref, x_ref, mod_ref, dskip_ref, mixg_ref, n2g_ref,
                  wglu_ref, wout_ref, rw_ref, x1_ref, h2_ref, logit_ref):
    m = mod_ref[0]
    yx = dskip_ref[...] * u_ref[...] + ys_ref[...]
    c0 = math.sqrt(2.0 / math.pi)
    ge = 0.5 * yx * (1.0 + jnp.tanh(c0 * (yx + 0.044715 * (yx * yx * yx))))
    z = jnp.dot(ge.astype(_BF16), wglu_ref[...], preferred_element_type=_F32)
    ssm_y = z[:, 0:D_SSM] * jax.nn.sigmoid(z[:, D_SSM:2 * D_SSM])
    conv_y = conv_ref[...]
    mixg = mixg_ref[...]
    heads_a = (ssm_y * _rms_scale(ssm_y) * mixg[:, 0:D_SSM]).astype(_BF16)
    heads_b = (conv_y * _rms_scale(conv_y) * mixg[:, D_SSM:]).astype(_BF16)
    mix = (jnp.dot(heads_a, wout_ref[0:D_SSM, :], preferred_element_type=_F32)
           + jnp.dot(heads_b, wout_ref[D_SSM:, :], preferred_element_type=_F32))
    x1 = x_ref[...] + m[2:3] * mix
    x1_ref[...] = x1
    h2 = x1 * _rms_scale(x1) * n2g_ref[...] * (1.0 + m[4:5]) + m[3:4]
    h_hi = h2.astype(_BF16)
    h2_ref[...] = h_hi
    h_lo = (h2 - h_hi.astype(_F32)).astype(_BF16)
    nl = logit_ref.shape[1]
    p = jnp.dot(h_hi, rw_ref[...], preferred_element_type=_F32)
    q = jnp.dot(h_lo, rw_ref[:, 0:nl], preferred_element_type=_F32)
    logit_ref[...] = p[:, 0:nl] + p[:, nl:2 * nl] + q


def _merge(ys, u, conv, x2, mod3, dskip, mixg, n2g, wglu_bf, wout_bf, rw_split):
    tiles_per_batch = SEQ // TM
    tok = lambda w: pl.BlockSpec((TM, w), lambda i: (i, 0))
    nl = rw_split.shape[1] // 2
    return pl.pallas_call(
        _merge_kernel,
        grid=(N_TOK // TM,),
        in_specs=[tok(D_SSM), tok(D_SSM), tok(D_CONV), tok(D_MODEL),
                  pl.BlockSpec((1, N_MOD, D_MODEL), lambda i: (i // tiles_per_batch, 0, 0)),
                  _const_spec((1, D_SSM)), _const_spec((1, D_MODEL)), _const_spec((1, D_MODEL)),
                  _const_spec((D_SSM, 2 * D_SSM)), _const_spec((D_MODEL, D_MODEL)),
                  _const_spec((D_MODEL, 2 * nl))],
        out_specs=[tok(D_MODEL), tok(D_MODEL), tok(nl)],
        out_shape=[jax.ShapeDtypeStruct((N_TOK, D_MODEL), _F32),
                   jax.ShapeDtypeStruct((N_TOK, D_MODEL), _BF16),
                   jax.ShapeDtypeStruct((N_TOK, nl), _F32)],
        compiler_params=_cparams("arbitrary"),
        name="merge_heads",
    )(ys, u, conv, x2, mod3, dskip, mixg, n2g, wglu_bf, wout_bf, rw_split)


def _route_kernel(logit_ref, bias_ref, w_ref):
    shape = logit_ref.shape
    lane = lax.broadcasted_iota(jnp.int32, shape, 1)
    eid = lane % N_EXPERTS
    scores = jax.nn.sigmoid(logit_ref[...])
    biased = scores + bias_ref[...]
    nlanes = shape[1]

    def seg_reduce(v, op):
        d = 1
        while d < GROUP_SIZE:
            up = pltpu.roll(v, d, axis=1)
            dn = pltpu.roll(v, nlanes - d, axis=1)
            v = op(v, jnp.where((lane & d) != 0, up, dn))
            d *= 2
        return v

    neg = jnp.float32(-jnp.inf)
    m1 = seg_reduce(biased, jnp.maximum)
    first = seg_reduce(jnp.where(biased == m1, lane, nlanes), jnp.minimum)
    m2 = seg_reduce(jnp.where(lane == first, neg, biased), jnp.maximum)
    gscore = m1 + m2
    gid = eid // GROUP_SIZE
    beaten = jnp.zeros(shape, jnp.int32)
    for s in range(1, N_EXPERT_GROUPS):
        other = pltpu.roll(gscore, nlanes - s * GROUP_SIZE, axis=1)
        ogid = (gid + s) % N_EXPERT_GROUPS
        beats = (other > gscore) | ((other == gscore) & (ogid < gid))
        beaten = beaten + beats.astype(jnp.int32)
    v = jnp.where(beaten < TOPK_GROUPS, biased, neg)

    sel = jnp.zeros(shape, jnp.bool_)
    for _ in range(TOP_K):
        m = jnp.max(v, axis=1, keepdims=True)
        pick_id = jnp.min(jnp.where(v == m, eid, N_EXPERTS), axis=1, keepdims=True)
        pick = eid == pick_id
        sel = sel | pick
        v = jnp.where(pick, neg, v)
    w = jnp.where(sel & (lane < N_EXPERTS), scores, 0.0)
    w_ref[...] = w / jnp.sum(w, axis=1, keepdims=True) * ROUTED_SCALE


def _route(logits, bias2):
    nl = logits.shape[1]
    return pl.pallas_call(
        _route_kernel,
        grid=(N_TOK // TM,),
        in_specs=[pl.BlockSpec((TM, nl), lambda i: (i, 0)), _const_spec((1, nl))],
        out_specs=pl.BlockSpec((TM, nl), lambda i: (i, 0)),
        out_shape=jax.ShapeDtypeStruct((N_TOK, nl), _F32),
        compiler_params=_cparams("arbitrary"),
        name="route",
    )(logits, bias2)


def _experts_kernel(h_ref, w_ref, wg_ref, wu_ref, wd_ref, o_ref):
    e = pl.program_id(1)

    @pl.when(e == 0)
    def _():
        o_ref[...] = jnp.zeros_like(o_ref)

    h = h_ref[...].astype(_BF16)
    lane = lax.broadcasted_iota(jnp.int32, w_ref.shape, 1)
    we = jnp.sum(jnp.where(lane == e, w_ref[...], 0.0), axis=1, keepdims=True)
    a = jnp.dot(h, wg_ref[0].astype(_BF16), preferred_element_type=_F32)
    b = jnp.dot(h, wu_ref[0].astype(_BF16), preferred_element_type=_F32)
    act = (_silu(a) * b * we).astype(_BF16)
    o_ref[...] += jnp.dot(act, wd_ref[0].astype(_BF16), preferred_element_type=_F32)


def _experts(h2, wdense, w_gate, w_up, w_down):
    tm = 512
    nl = wdense.shape[1]
    return pl.pallas_call(
        _experts_kernel,
        grid=(N_TOK // tm, N_EXPERTS),
        in_specs=[pl.BlockSpec((tm, D_MODEL), lambda i, e: (i, 0)),
                  pl.BlockSpec((tm, nl), lambda i, e: (i, 0)),
                  pl.BlockSpec((1, D_MODEL, D_EXPERT), lambda i, e: (e, 0, 0)),
                  pl.BlockSpec((1, D_MODEL, D_EXPERT), lambda i, e: (e, 0, 0)),
                  pl.BlockSpec((1, D_EXPERT, D_MODEL), lambda i, e: (e, 0, 0))],
        out_specs=pl.BlockSpec((tm, D_MODEL), lambda i, e: (i, 0)),
        out_shape=jax.ShapeDtypeStruct((N_TOK, D_MODEL), _F32),
        compiler_params=_cparams("arbitrary", "arbitrary"),
        name="experts",
    )(h2, wdense, w_gate, w_up, w_down)


def _final_kernel(x1_ref, h_ref, r_ref, mod_ref, wsg_ref, wsu_ref, wsd_ref, fg_ref, o_ref):
    m = mod_ref[0]
    h = h_ref[...].astype(_BF16)
    a = jnp.dot(h, wsg_ref[...], preferred_element_type=_F32)
    b = jnp.dot(h, wsu_ref[...], preferred_element_type=_F32)
    shared = jnp.dot((_silu(a) * b).astype(_BF16), wsd_ref[...], preferred_element_type=_F32)
    y = x1_ref[...] + m[5:6] * (r_ref[...] + shared)
    o_ref[...] = y * _rms_scale(y) * fg_ref[...]


def _final(x1, h2, routed, mod3, wsg_bf, wsu_bf, wsd_bf, final_g):
    tiles_per_batch = SEQ // TM
    tok = pl.BlockSpec((TM, D_MODEL), lambda i: (i, 0))
    return pl.pallas_call(
        _final_kernel,
        grid=(N_TOK // TM,),
        in_specs=[tok, tok, tok,
                  pl.BlockSpec((1, N_MOD, D_MODEL), lambda i: (i // tiles_per_batch, 0, 0)),
                  _const_spec((D_MODEL, D_SHARED)), _const_spec((D_MODEL, D_SHARED)),
                  _const_spec((D_SHARED, D_MODEL)), _const_spec((1, D_MODEL))],
        out_specs=tok,
        out_shape=jax.ShapeDtypeStruct((N_TOK, D_MODEL), _F32),
        compiler_params=_cparams("arbitrary"),
        name="final",
    )(x1, h2, routed, mod3, wsg_bf, wsu_bf, wsd_bf, final_g)


def _s5_param_layouts(lam_re, lam_im, b_re, b_im, c_re, c_im, log_dt):
    dt = jnp.exp(log_dt.astype(_F32))
    two = lambda a: jnp.concatenate([a, a], axis=-1)
    lr, li = two(lam_re), two(lam_im)
    dtb = jnp.broadcast_to(dt[:, :, None], lr.shape)
    zeros = jnp.zeros_like(lr)
    pcol = jnp.stack([lr, li, dtb, zeros], axis=-1)
    prow = jnp.stack([lr, li, dtb] + [zeros] * 5, axis=2)
    ct2 = jnp.concatenate([jnp.swapaxes(c_re, -1, -2), jnp.swapaxes(c_im, -1, -2)], axis=2)
    bt1 = jnp.concatenate([jnp.swapaxes(b_re, -1, -2), jnp.swapaxes(b_im, -1, -2)], axis=3)
    return pcol, prow, ct2, bt1


def kernel(x, c, ctx, c_ctx, norm1_g, norm2_g, w_ada, b_ada, w_in, ssm_lam_re, ssm_lam_im, ssm_b_re, ssm_b_im, ssm_c_re, ssm_c_im, ssm_log_dt, ssm_d, ssm_w_glu, conv_w, conv_b, mix_norm_g, w_out, router_w, router_bias, exp_w_gate, exp_w_up, exp_w_down, shared_w_gate, shared_w_up, shared_w_down, final_g):
    layer = 0
    x2 = x.reshape(N_TOK, D_MODEL)
    ctx2 = ctx.reshape(N_CTX_TOK, D_MODEL)

    c8 = jnp.concatenate([c, c_ctx[None, :], jnp.zeros((8 - BATCH - 1, D_MODEL), _F32)], axis=0)
    mod = _ada_mod(c8, w_ada[layer], b_ada[layer][None, :])
    mod3 = mod.reshape(8, N_MOD, D_MODEL)

    w_in_bf = w_in[layer].astype(_BF16)
    conv_w8 = jnp.concatenate([conv_w[layer], jnp.zeros((8 - conv_w.shape[1], D_CONV), _F32)], axis=0)
    u_x, conv_x = _inproj(x2, mod3, norm1_g[layer][None, :], w_in_bf, conv_w8, conv_b[layer][None, :])
    u_c = _ctxproj(ctx2, mod3, norm1_g[layer][None, :], w_in_bf[:, :D_SSM])

    pcol, prow, ct2, bt1 = _s5_param_layouts(ssm_lam_re[layer], ssm_lam_im[layer], ssm_b_re[layer],
                                             ssm_b_im[layer], ssm_c_re[layer], ssm_c_im[layer],
                                             ssm_log_dt[layer])
    toep, bpow, cpow, dec = _s5_params(pcol, prow, ct2, bt1)

    def to_chunks(u, n_chunks):
        u = u.astype(_BF16).reshape(BATCH, n_chunks, CHUNK, SSM_GROUPS, SSM_GROUP)
        return u.transpose(3, 1, 0, 2, 4).reshape(SSM_GROUPS, n_chunks * BATCH, CHUNK_W)

    u2 = jnp.concatenate([to_chunks(u_c, N_CTX_CHUNKS), to_chunks(u_x, N_X_CHUNKS)], axis=1)
    y2 = _s5_scan(u2, toep, bpow, cpow, dec)
    ys = (y2.reshape(SSM_GROUPS, N_X_CHUNKS, BATCH, CHUNK, SSM_GROUP)
          .transpose(2, 1, 3, 0, 4).reshape(N_TOK, D_SSM))

    rw = router_w[layer]
    rw_hi = rw.astype(_BF16)
    rw_lo = (rw - rw_hi.astype(_F32)).astype(_BF16)
    rw_split = jnp.concatenate([rw_hi, rw_hi, rw_lo, rw_lo], axis=1)
    x1, h2, logits = _merge(ys, u_x, conv_x, x2, mod3, ssm_d[layer][None, :], mix_norm_g[layer][None, :],
                            norm2_g[layer][None, :], ssm_w_glu[layer].astype(_BF16),
                            w_out[layer].astype(_BF16), rw_split)

    bias2 = jnp.concatenate([router_bias[layer], router_bias[layer]])[None, :]
    wdense = _route(logits, bias2)
    routed = _experts(h2, wdense, exp_w_gate[layer], exp_w_up[layer], exp_w_down[layer])

    out = _final(x1, h2, routed, mod3, shared_w_gate[layer].astype(_BF16), shared_w_up[layer].astype(_BF16),
                 shared_w_down[layer].astype(_BF16), final_g[None, :])
    return out.reshape(BATCH, SEQ, D_MODEL)
```

```python
import functools
import math

import jax
import jax.numpy as jnp
from jax import lax
from jax.experimental import pallas as pl
from jax.experimental.pallas import tpu as pltpu

D_MODEL = 2048
BATCH = 4
SEQ = 2048
CTX_LEN = 256
GRID_W = 64
D_SSM = 1024
D_CONV = 1024
SSM_GROUP = 16
SSM_GROUPS = 64
SSM_STATE = 64
N_EXPERTS = 64
N_EXPERT_GROUPS = 8
GROUP_SIZE = N_EXPERTS // N_EXPERT_GROUPS
TOPK_GROUPS = 4
TOP_K = 8
D_EXPERT = 512
D_SHARED = 512
ROUTED_SCALE = 2.5
N_MOD = 6
EPS = 1e-6

N_TOK = BATCH * SEQ
N_CTX_TOK = BATCH * CTX_LEN

CHUNK = 16
CHUNK_W = CHUNK * SSM_GROUP
STATE_W = 2 * SSM_STATE
N_CTX_CHUNKS = CTX_LEN // CHUNK
N_X_CHUNKS = SEQ // CHUNK
N_SEQ_CHUNKS = N_CTX_CHUNKS + N_X_CHUNKS
SCAN_ROWS = N_SEQ_CHUNKS * BATCH
S5_GROUPS_PER_STEP = 8

TM = 256
ROW_TILES = D_MODEL // 128
EXPERT_BM = 256
N_ASSIGN = N_TOK * TOP_K
N_BLOCKS = N_ASSIGN // EXPERT_BM + N_EXPERTS
N_ROWS = N_BLOCKS * EXPERT_BM
TM_COMBINE = 128
VMEM_LIMIT = 56 * 1024 * 1024

_F32 = jnp.float32
_BF16 = jnp.bfloat16


def _cparams(*sem):
    return pltpu.CompilerParams(dimension_semantics=sem, vmem_limit_bytes=VMEM_LIMIT)


def _const_spec(shape):
    nd = len(shape)
    return pl.BlockSpec(shape, lambda *_: (0,) * nd, pipeline_mode=pl.Buffered(1))


def _rms_scale(xf):
    return lax.rsqrt(jnp.mean(xf * xf, axis=-1, keepdims=True) + EPS)


def _silu(x):
    return x * jax.nn.sigmoid(x)


def _ada_kernel(c_ref, w_ref, b_ref, o_ref):
    s = _silu(c_ref[...])
    o_ref[...] = jnp.dot(s, w_ref[...], preferred_element_type=_F32) + b_ref[...]


def _ada_mod(c8, w_ada, b_ada):
    n = w_ada.shape[1]
    tn = 1024
    return pl.pallas_call(
        _ada_kernel,
        grid=(n // tn,),
        in_specs=[pl.BlockSpec((8, D_MODEL), lambda j: (0, 0)),
                  pl.BlockSpec((D_MODEL, tn), lambda j: (0, j)),
                  pl.BlockSpec((1, tn), lambda j: (0, j))],
        out_specs=pl.BlockSpec((8, tn), lambda j: (0, j)),
        out_shape=jax.ShapeDtypeStruct((8, n), _F32),
        compiler_params=_cparams("arbitrary"),
        name="ada_mod",
    )(c8, w_ada, b_ada)


def _modulated_norm(x_ref, mod_ref, g_ref, shift_row, scale_row):
    xf = x_ref[...]
    m = mod_ref[0]
    h = xf * _rms_scale(xf) * g_ref[...]
    return h * (1.0 + m[scale_row:scale_row + 1]) + m[shift_row:shift_row + 1]


def _inproj_kernel(x_ref, mod_ref, g_ref, w_ref, cw_ref, cb_ref, u_ref, conv_ref):
    h = _modulated_norm(x_ref, mod_ref, g_ref, 0, 1).astype(_BF16)
    u_ref[...] = jnp.dot(h, w_ref[:, 0:D_SSM], preferred_element_type=_F32)
    tm = x_ref.shape[0]
    pos = lax.broadcasted_iota(jnp.int32, (tm, 1), 0) % GRID_W
    not_first = (pos != 0).astype(_F32)
    not_last = (pos != GRID_W - 1).astype(_F32)
    cw = cw_ref[...]
    nc = 256
    for j in range(D_CONV // nc):
        lo = j * nc
        bg = jnp.dot(h, w_ref[:, D_SSM + lo:D_SSM + lo + nc], preferred_element_type=_F32)
        cg = jnp.dot(h, w_ref[:, D_SSM + D_CONV + lo:D_SSM + D_CONV + lo + nc], preferred_element_type=_F32)
        v = jnp.dot(h, w_ref[:, D_SSM + 2 * D_CONV + lo:D_SSM + 2 * D_CONV + lo + nc],
                    preferred_element_type=_F32)
        z = cg * v
        z_prev = pltpu.roll(z, 1, axis=0) * not_first
        z_next = pltpu.roll(z, tm - 1, axis=0) * not_last
        y = (cb_ref[:, lo:lo + nc] + z_prev * cw[0:1, lo:lo + nc] + z * cw[1:2, lo:lo + nc]
             + z_next * cw[2:3, lo:lo + nc])
        conv_ref[:, lo:lo + nc] = bg * y


def _inproj(x2, mod3, norm_g, w_in_bf, conv_w, conv_b):
    d_in = w_in_bf.shape[1]
    tiles_per_batch = SEQ // TM
    return pl.pallas_call(
        _inproj_kernel,
        grid=(N_TOK // TM,),
        in_specs=[pl.BlockSpec((TM, D_MODEL), lambda i: (i, 0)),
                  pl.BlockSpec((1, N_MOD, D_MODEL), lambda i: (i // tiles_per_batch, 0, 0)),
                  _const_spec((1, D_MODEL)),
                  _const_spec((D_MODEL, d_in)),
                  _const_spec((8, D_CONV)),
                  _const_spec((1, D_CONV))],
        out_specs=[pl.BlockSpec((TM, D_SSM), lambda i: (i, 0)),
                   pl.BlockSpec((TM, D_CONV), lambda i: (i, 0))],
        out_shape=[jax.ShapeDtypeStruct((N_TOK, D_SSM), _F32),
                   jax.ShapeDtypeStruct((N_TOK, D_CONV), _F32)],
        compiler_params=_cparams("arbitrary"),
        name="in_proj",
    )(x2, mod3, norm_g, w_in_bf, conv_w, conv_b)


def _ctxproj_kernel(x_ref, mod_ref, g_ref, w_ref, u_ref):
    h = _modulated_norm(x_ref, mod_ref, g_ref, 0, 1).astype(_BF16)
    u_ref[...] = jnp.dot(h, w_ref[...], preferred_element_type=_F32)


def _ctxproj(ctx2, mod3, norm_g, w_u_bf):
    return pl.pallas_call(
        _ctxproj_kernel,
        grid=(N_CTX_TOK // TM,),
        in_specs=[pl.BlockSpec((TM, D_MODEL), lambda i: (i, 0)),
                  pl.BlockSpec((1, N_MOD, D_MODEL), lambda i: (BATCH, 0, 0)),
                  _const_spec((1, D_MODEL)),
                  _const_spec((D_MODEL, D_SSM))],
        out_specs=pl.BlockSpec((TM, D_SSM), lambda i: (i, 0)),
        out_shape=jax.ShapeDtypeStruct((N_CTX_TOK, D_SSM), _F32),
        compiler_params=_cparams("arbitrary"),
        name="ctx_proj",
    )(ctx2, mod3, norm_g, w_u_bf)


def _s5_param_kernel(pcol_ref, prow_ref, ct_ref, bt_ref, toep_ref, bpow_ref, cpow_ref, dec_ref):
    lane = lambda shape: lax.broadcasted_iota(jnp.int32, shape, 1)
    sub = lambda shape: lax.broadcasted_iota(jnp.int32, shape, 0)

    rep = (lane((SSM_GROUP, CHUNK_W)) % SSM_GROUP == sub((SSM_GROUP, CHUNK_W))).astype(_F32)
    sgn_col = jnp.where(sub((STATE_W, 1)) < SSM_STATE, 1.0, -1.0).astype(_F32)
    sgn_row = jnp.where(lane((1, STATE_W)) < SSM_STATE, -1.0, 1.0).astype(_F32)
    blk_row = (lane((1, CHUNK_W)) // SSM_GROUP).astype(_F32)
    blk_col = (sub((CHUNK_W, 1)) // SSM_GROUP).astype(_F32)

    strips = []
    for d in range(2):
        pc = pcol_ref[d, 0]
        a_col = pc[:, 0:1] * pc[:, 2:3]
        th_col = pc[:, 1:2] * pc[:, 2:3]
        pr = prow_ref[d, 0]
        lam_re, lam_im, dt = pr[0:1], pr[1:2], pr[2:3]
        a_row = lam_re * dt
        th_row = lam_im * dt

        ct = jnp.dot(ct_ref[d, 0], rep, preferred_element_type=_F32,
                     precision=lax.Precision.HIGHEST)
        ca = ct * sgn_col
        cb = -pltpu.roll(ct, SSM_STATE, axis=0)

        def cpow(e_row):
            mag = jnp.exp(a_col * e_row)
            ang = th_col * e_row
            return mag * jnp.cos(ang) * ca + mag * jnp.sin(ang) * cb

        mag1 = jnp.exp(a_row)
        nr = mag1 * jnp.cos(th_row) - 1.0
        ni = mag1 * jnp.sin(th_row)
        den = lam_re * lam_re + lam_im * lam_im
        kr = (nr * lam_re + ni * lam_im) / den
        ki = (ni * lam_re - nr * lam_im) / den
        b1 = bt_ref[d, 0]
        b2 = pltpu.roll(b1, SSM_STATE, axis=1) * sgn_row
        u1 = kr * b1 + ki * b2
        u2 = kr * b2 - ki * b1
        u1t = jnp.concatenate([u1] * CHUNK, axis=0)
        u2t = jnp.concatenate([u2] * CHUNK, axis=0)

        def bpow(e_col):
            mag = jnp.exp(a_row * e_col)
            ang = th_row * e_col
            return mag * jnp.cos(ang) * u1t + mag * jnp.sin(ang) * u2t

        if d == 0:
            e_strip, e_b, e_c = blk_row, (CHUNK - 1.0) - blk_col, blk_row + 1.0
        else:
            e_strip, e_b, e_c = (CHUNK - 1.0) - blk_row, blk_col, float(CHUNK) - blk_row
        strips.append(jnp.dot(u1, cpow(e_strip), preferred_element_type=_F32,
                              precision=lax.Precision.HIGHEST))
        bpow_ref[0, :, d * STATE_W:(d + 1) * STATE_W] = bpow(e_b).astype(_BF16)
        cpow_ref[0, d * STATE_W:(d + 1) * STATE_W, :] = cpow(e_c).astype(_BF16)

        mag16 = jnp.exp(a_row * float(CHUNK))
        dec_ref[0, 2 * d:2 * d + 1, :] = mag16 * jnp.cos(th_row * float(CHUNK))
        dec_ref[0, 2 * d + 1:2 * d + 2, :] = mag16 * jnp.sin(th_row * float(CHUNK)) * sgn_row

    zeros = jnp.zeros((SSM_GROUP, CHUNK_W), _F32)
    strip = (jnp.concatenate([strips[1], zeros], axis=1)
             + pltpu.roll(jnp.concatenate([strips[0], zeros], axis=1), CHUNK_W - SSM_GROUP, axis=1))
    for i in range(CHUNK):
        off = (CHUNK - 1 - i) * SSM_GROUP
        win = strip if off == 0 else pltpu.roll(strip, 2 * CHUNK_W - off, axis=1)
        toep_ref[0, i * SSM_GROUP:(i + 1) * SSM_GROUP, :] = win[:, 0:CHUNK_W].astype(_BF16)


def _s5_params(pcol, prow, ct2, bt1):
    g_spec = lambda shape: pl.BlockSpec(shape, lambda g: (0, g, 0, 0))
    o_spec = lambda shape: pl.BlockSpec(shape, lambda g: (g, 0, 0))
    return pl.pallas_call(
        _s5_param_kernel,
        grid=(SSM_GROUPS,),
        in_specs=[g_spec((2, 1, STATE_W, 4)), g_spec((2, 1, 8, STATE_W)),
                  g_spec((2, 1, STATE_W, SSM_GROUP)), g_spec((2, 1, SSM_GROUP, STATE_W))],
        out_specs=[o_spec((1, CHUNK_W, CHUNK_W)), o_spec((1, CHUNK_W, 2 * STATE_W)),
                   o_spec((1, 2 * STATE_W, CHUNK_W)), o_spec((1, 4, STATE_W))],
        out_shape=[jax.ShapeDtypeStruct((SSM_GROUPS, CHUNK_W, CHUNK_W), _BF16),
                   jax.ShapeDtypeStruct((SSM_GROUPS, CHUNK_W, 2 * STATE_W), _BF16),
                   jax.ShapeDtypeStruct((SSM_GROUPS, 2 * STATE_W, CHUNK_W), _BF16),
                   jax.ShapeDtypeStruct((SSM_GROUPS, 4, STATE_W), _F32)],
        compiler_params=_cparams("arbitrary"),
        name="s5_params",
    )(pcol, prow, ct2, bt1)


def _s5_kernel(u_ref, toep_ref, bpow_ref, cpow_ref, dec_ref, y_ref, s_ref, h_ref):
    ng = u_ref.shape[0]
    for g in range(ng):
        s_ref[g] = jnp.dot(u_ref[g], bpow_ref[g], preferred_element_type=_F32)

    pair = 2 * BATCH
    n_ctx_pairs = N_CTX_CHUNKS // 2
    n_pairs = N_SEQ_CHUNKS // 2
    lower = lax.broadcasted_iota(jnp.int32, (pair, STATE_W), 0) < BATCH
    zero = jnp.zeros((pair, STATE_W), _F32)

    def step(t, carry):
        tb = jnp.where(t < n_ctx_pairs, n_ctx_pairs - 1 - t, n_pairs + n_ctx_pairs - 1 - t)
        rf = pl.multiple_of(t * pair, pair)
        rb = pl.multiple_of(tb * pair, pair)
        out = []
        for g in range(ng):
            hf, hb = carry[2 * g], carry[2 * g + 1]
            dg = dec_ref[g]
            mul_f = lambda v: dg[0:1] * v + dg[1:2] * pltpu.roll(v, SSM_STATE, axis=1)
            mul_b = lambda v: dg[2:3] * v + dg[3:4] * pltpu.roll(v, SSM_STATE, axis=1)

            sf = s_ref[g, pl.ds(rf, pair), 0:STATE_W]
            mid = mul_f(hf) + jnp.where(lower, sf, pltpu.roll(sf, BATCH, axis=0))
            h_ref[g, pl.ds(rf, pair), 0:STATE_W] = jnp.where(lower, hf, mid)
            end = mul_f(mid) + sf
            hf = jnp.where(lower, pltpu.roll(end, BATCH, axis=0), end)

            sb = s_ref[g, pl.ds(rb, pair), STATE_W:2 * STATE_W]
            mid = mul_b(hb) + jnp.where(lower, pltpu.roll(sb, BATCH, axis=0), sb)
            h_ref[g, pl.ds(rb, pair), STATE_W:2 * STATE_W] = jnp.where(lower, mid, hb)
            end = mul_b(mid) + sb
            hb = jnp.where(lower, end, pltpu.roll(end, BATCH, axis=0))
            out += [hf, hb]
        return tuple(out)

    lax.fori_loop(0, n_pairs, step, (zero,) * (2 * ng))

    x0 = N_CTX_CHUNKS * BATCH
    for g in range(ng):
        ux = u_ref[g, x0:SCAN_ROWS, :]
        hx = h_ref[g, x0:SCAN_ROWS, :].astype(_BF16)
        y_ref[g] = (jnp.dot(ux, toep_ref[g], preferred_element_type=_F32)
                    + jnp.dot(hx, cpow_ref[g], preferred_element_type=_F32))


def _s5_scan(u2, toep, bpow, cpow, dec):
    ng = S5_GROUPS_PER_STEP
    n_x_rows = N_X_CHUNKS * BATCH
    spec = lambda r, c: pl.BlockSpec((ng, r, c), lambda i: (i, 0, 0))
    return pl.pallas_call(
        _s5_kernel,
        grid=(SSM_GROUPS // ng,),
        in_specs=[spec(SCAN_ROWS, CHUNK_W), spec(CHUNK_W, CHUNK_W), spec(CHUNK_W, 2 * STATE_W),
                  spec(2 * STATE_W, CHUNK_W), spec(4, STATE_W)],
        out_specs=spec(n_x_rows, CHUNK_W),
        out_shape=jax.ShapeDtypeStruct((SSM_GROUPS, n_x_rows, CHUNK_W), _F32),
        scratch_shapes=[pltpu.VMEM((ng, SCAN_ROWS, 2 * STATE_W), _F32),
                        pltpu.VMEM((ng, SCAN_ROWS, 2 * STATE_W), _F32)],
        compiler_params=_cparams("arbitrary"),
        name="s5_scan",
    )(u2, toep, bpow, cpow, dec)


def _merge_kernel(ys_ref, u_ref, conv_ref, x_ref, mod_ref, dskip_ref, mixg_ref, n2g_ref,
                  wglu_ref, wout_ref, rwt_ref, x1_ref, h2_ref, h2s_ref, logit_ref):
    m = mod_ref[0]
    yx = dskip_ref[...] * u_ref[...] + ys_ref[...]
    c0 = math.sqrt(2.0 / math.pi)
    ge = 0.5 * yx * (1.0 + jnp.tanh(c0 * (yx + 0.044715 * (yx * yx * yx))))
    z = jnp.dot(ge.astype(_BF16), wglu_ref[...], preferred_element_type=_F32)
    ssm_y = z[:, 0:D_SSM] * jax.nn.sigmoid(z[:, D_SSM:2 * D_SSM])
    conv_y = conv_ref[...]
    mixg = mixg_ref[...]
    heads_a = (ssm_y * _rms_scale(ssm_y) * mixg[:, 0:D_SSM]).astype(_BF16)
    heads_b = (conv_y * _rms_scale(conv_y) * mixg[:, D_SSM:]).astype(_BF16)
    mix = (jnp.dot(heads_a, wout_ref[0:D_SSM, :], preferred_element_type=_F32)
           + jnp.dot(heads_b, wout_ref[D_SSM:, :], preferred_element_type=_F32))
    x1 = x_ref[...] + m[2:3] * mix
    x1_ref[...] = x1
    h2 = x1 * _rms_scale(x1) * n2g_ref[...] * (1.0 + m[4:5]) + m[3:4]
    h_hi = h2.astype(_BF16)
    h2_ref[...] = h_hi
    tm = x_ref.shape[0]
    for j in range(ROW_TILES):
        h2s_ref[pl.ds(j, tm, stride=ROW_TILES), :] = h2[:, j * 128:(j + 1) * 128]
    h_lo = (h2 - h_hi.astype(_F32)).astype(_BF16)
    nt = (((1,), (1,)), ((), ()))
    p = lax.dot_general(rwt_ref[...], h_hi, nt, preferred_element_type=_F32)
    q = lax.dot_general(rwt_ref[0:N_EXPERTS, :], h_lo, nt, preferred_element_type=_F32)
    logit_ref[...] = p[0:N_EXPERTS] + p[N_EXPERTS:2 * N_EXPERTS] + q


def _merge(ys, u, conv, x2, mod3, dskip, mixg, n2g, wglu_bf, wout_bf, rwt):
    tiles_per_batch = SEQ // TM
    tok = lambda w: pl.BlockSpec((TM, w), lambda i: (i, 0))
    return pl.pallas_call(
        _merge_kernel,
        grid=(N_TOK // TM,),
        in_specs=[tok(D_SSM), tok(D_SSM), tok(D_CONV), tok(D_MODEL),
                  pl.BlockSpec((1, N_MOD, D_MODEL), lambda i: (i // tiles_per_batch, 0, 0)),
                  _const_spec((1, D_SSM)), _const_spec((1, D_MODEL)), _const_spec((1, D_MODEL)),
                  _const_spec((D_SSM, 2 * D_SSM)), _const_spec((D_MODEL, D_MODEL)),
                  _const_spec((2 * N_EXPERTS, D_MODEL))],
        out_specs=[tok(D_MODEL), tok(D_MODEL),
                   pl.BlockSpec((TM * ROW_TILES, 128), lambda i: (i, 0)),
                   pl.BlockSpec((N_EXPERTS, TM), lambda i: (0, i))],
        out_shape=[jax.ShapeDtypeStruct((N_TOK, D_MODEL), _F32),
                   jax.ShapeDtypeStruct((N_TOK, D_MODEL), _BF16),
                   jax.ShapeDtypeStruct((N_TOK * ROW_TILES, 128), _F32),
                   jax.ShapeDtypeStruct((N_EXPERTS, N_TOK), _F32)],
        compiler_params=_cparams("arbitrary"),
        name="merge_heads",
    )(ys, u, conv, x2, mod3, dskip, mixg, n2g, wglu_bf, wout_bf, rwt)


def _route_kernel(logit_ref, bias_ref, eid_ref, w_ref, rank_ref, cnt_ref, carry_ref):
    @pl.when(pl.program_id(0) == 0)
    def _():
        carry_ref[...] = jnp.zeros_like(carry_ref)

    tm = logit_ref.shape[1]
    neg = jnp.float32(-jnp.inf)
    scores = jax.nn.sigmoid(logit_ref[...])
    biased = scores + bias_ref[:, 0:1]
    sub = lax.broadcasted_iota(jnp.int32, (GROUP_SIZE, tm), 0)
    rows = lambda a, g: a[g * GROUP_SIZE:(g + 1) * GROUP_SIZE]
    ngrp = N_EXPERT_GROUPS

    gscore = []
    for g in range(ngrp):
        bg = rows(biased, g)
        m1 = jnp.max(bg, axis=0, keepdims=True)
        first = jnp.min(jnp.where(bg == m1, sub, GROUP_SIZE), axis=0, keepdims=True)
        m2 = jnp.max(jnp.where(sub == first, neg, bg), axis=0, keepdims=True)
        gscore.append(m1 + m2)
    v = []
    for g in range(ngrp):
        beaten = jnp.zeros((1, tm), jnp.int32)
        for o in range(ngrp):
            if o != g:
                beats = (gscore[o] >= gscore[g]) if o < g else (gscore[o] > gscore[g])
                beaten = beaten + beats.astype(jnp.int32)
        v.append(jnp.where(beaten < TOPK_GROUPS, rows(biased, g), neg))
    eids = [sub + g * GROUP_SIZE for g in range(ngrp)]

    pick_ids, pick_masks = [], []
    for _ in range(TOP_K):
        m = functools.reduce(jnp.maximum, v)
        m = jnp.max(m, axis=0, keepdims=True)
        cand = functools.reduce(jnp.minimum, [jnp.where(v[g] == m, eids[g], N_EXPERTS) for g in range(ngrp)])
        pick_id = jnp.min(cand, axis=0, keepdims=True)
        masks = [eids[g] == pick_id for g in range(ngrp)]
        v = [jnp.where(masks[g], neg, v[g]) for g in range(ngrp)]
        pick_ids.append(pick_id)
        pick_masks.append(masks)

    sel = jnp.concatenate(
        [functools.reduce(jnp.logical_or, [pick_masks[k][g] for k in range(TOP_K)]).astype(_F32)
         for g in range(ngrp)], axis=0)
    before = (lax.broadcasted_iota(jnp.int32, (tm, tm), 0)
              < lax.broadcasted_iota(jnp.int32, (tm, tm), 1)).astype(_BF16)
    base = jnp.dot(sel.astype(_BF16), before, preferred_element_type=_F32) + carry_ref[:, 0:1]

    def gather_pick(a, k):
        parts = [jnp.sum(jnp.where(pick_masks[k][g], rows(a, g), 0.0), axis=0, keepdims=True)
                 for g in range(ngrp)]
        return functools.reduce(jnp.add, parts)

    picked = [gather_pick(scores, k) for k in range(TOP_K)]
    denom = functools.reduce(jnp.add, picked)
    for k in range(TOP_K):
        eid_ref[k:k + 1, :] = pick_ids[k]
        w_ref[k:k + 1, :] = picked[k] / denom * ROUTED_SCALE
        rank_ref[k:k + 1, :] = gather_pick(base, k).astype(jnp.int32)
    carry_ref[...] = carry_ref[...] + jnp.sum(sel, axis=1, keepdims=True)
    cnt_ref[...] = carry_ref[...]


def _route(logits_t, bias_col):
    tok = pl.BlockSpec((TOP_K, TM), lambda i: (0, i))
    return pl.pallas_call(
        _route_kernel,
        grid=(N_TOK // TM,),
        in_specs=[pl.BlockSpec((N_EXPERTS, TM), lambda i: (0, i)), _const_spec((N_EXPERTS, 128))],
        out_specs=[tok, tok, tok, pl.BlockSpec((N_EXPERTS, 128), lambda i: (0, 0))],
        out_shape=[jax.ShapeDtypeStruct((TOP_K, N_TOK), jnp.int32),
                   jax.ShapeDtypeStruct((TOP_K, N_TOK), _F32),
                   jax.ShapeDtypeStruct((TOP_K, N_TOK), jnp.int32),
                   jax.ShapeDtypeStruct((N_EXPERTS, 128), _F32)],
        scratch_shapes=[pltpu.VMEM((N_EXPERTS, 128), _F32)],
        compiler_params=_cparams("arbitrary"),
        name="route",
    )(logits_t, bias_col)


def _pow2_pieces():
    bit = EXPERT_BM // 2
    while bit >= 1:
        yield bit
        bit //= 2


def _dispatch_kernel(dest_ref, padstart_ref, padn_ref, h_ref, xs_ref, zero_ref, sem, zsem):
    i = pl.program_id(0)
    tm = h_ref.shape[0] // ROW_TILES
    base = i * tm

    def issue(t, c):
        src = h_ref.at[pl.ds(pl.multiple_of(t * ROW_TILES, ROW_TILES), ROW_TILES)]
        for k in range(TOP_K):
            d = pl.multiple_of(dest_ref[k, base + t] * ROW_TILES, ROW_TILES)
            pltpu.make_async_copy(src, xs_ref.at[pl.ds(d, ROW_TILES)], sem).start()
        return c

    lax.fori_loop(0, tm, issue, 0)

    @pl.when(i == 0)
    def _():
        zero_ref[...] = jnp.zeros_like(zero_ref)

        def pieces(e, act):
            n = padn_ref[e]
            start = padstart_ref[e]
            for bit in _pow2_pieces():
                d = pl.multiple_of(start * ROW_TILES, ROW_TILES)
                cp = pltpu.make_async_copy(zero_ref.at[pl.ds(0, bit * ROW_TILES)],
                                           xs_ref.at[pl.ds(d, bit * ROW_TILES)], zsem)
                pl.when((n & bit) != 0)(functools.partial(act, cp))
                start = start + (n & bit)

        def zstart(e, c):
            pieces(e, lambda cp: cp.start())
            return c

        def zwait(e, c):
            pieces(e, lambda cp: cp.wait())
            return c

        lax.fori_loop(0, N_EXPERTS, zstart, 0)
        lax.fori_loop(0, N_EXPERTS, zwait, 0)

    for _ in range(TOP_K):
        pltpu.make_async_copy(h_ref, xs_ref.at[pl.ds(0, tm * ROW_TILES)], sem).wait()


def _dispatch(dest, pad_start, pad_n, h2s):
    tm = TM
    grid_spec = pltpu.PrefetchScalarGridSpec(
        num_scalar_prefetch=3,
        grid=(N_TOK // tm,),
        in_specs=[pl.BlockSpec((tm * ROW_TILES, 128), lambda i, *_: (i, 0))],
        out_specs=pl.BlockSpec(memory_space=pl.ANY),
        scratch_shapes=[pltpu.VMEM((EXPERT_BM // 2 * ROW_TILES, 128), _F32),
                        pltpu.SemaphoreType.DMA(()), pltpu.SemaphoreType.DMA(())],
    )
    return pl.pallas_call(
        _dispatch_kernel,
        grid_spec=grid_spec,
        out_shape=jax.ShapeDtypeStruct((N_ROWS * ROW_TILES, 128), _F32),
        compiler_params=_cparams("arbitrary"),
        name="dispatch",
    )(dest, pad_start, pad_n, h2s)


def _experts_kernel(be_ref, nact_ref, xs_ref, wg_ref, wu_ref, wd_ref, ys_ref, wg_bf, wu_bf, wd_bf):
    i = pl.program_id(0)

    @pl.when(i < nact_ref[0])
    def _():
        @pl.when((i == 0) | (be_ref[i] != be_ref[jnp.maximum(i - 1, 0)]))
        def _():
            wg_bf[...] = wg_ref[0].astype(_BF16)
            wu_bf[...] = wu_ref[0].astype(_BF16)
            wd_bf[...] = wd_ref[0].astype(_BF16)

        bm = EXPERT_BM
        x = jnp.concatenate([xs_ref[pl.ds(j, bm, stride=ROW_TILES), :] for j in range(ROW_TILES)],
                            axis=1).astype(_BF16)
        a = jnp.dot(x, wg_bf[...], preferred_element_type=_F32)
        b = jnp.dot(x, wu_bf[...], preferred_element_type=_F32)
        y = jnp.dot((_silu(a) * b).astype(_BF16), wd_bf[...], preferred_element_type=_F32)
        for j in range(ROW_TILES):
            ys_ref[pl.ds(j, bm, stride=ROW_TILES), :] = y[:, j * 128:(j + 1) * 128]


def _experts(block_e, n_active, xs, w_gate, w_up, w_down):
    bm = EXPERT_BM
    row_blk = lambda i, be, na: (jnp.minimum(i, na[0] - 1), 0)
    grid_spec = pltpu.PrefetchScalarGridSpec(
        num_scalar_prefetch=2,
        grid=(N_BLOCKS,),
        in_specs=[pl.BlockSpec((bm * ROW_TILES, 128), row_blk),
                  pl.BlockSpec((1, D_MODEL, D_EXPERT), lambda i, be, na: (be[i], 0, 0)),
                  pl.BlockSpec((1, D_MODEL, D_EXPERT), lambda i, be, na: (be[i], 0, 0)),
                  pl.BlockSpec((1, D_EXPERT, D_MODEL), lambda i, be, na: (be[i], 0, 0))],
        out_specs=pl.BlockSpec((bm * ROW_TILES, 128), row_blk),
        scratch_shapes=[pltpu.VMEM((D_MODEL, D_EXPERT), _BF16), pltpu.VMEM((D_MODEL, D_EXPERT), _BF16),
                        pltpu.VMEM((D_EXPERT, D_MODEL), _BF16)],
    )
    return pl.pallas_call(
        _experts_kernel,
        grid_spec=grid_spec,
        out_shape=jax.ShapeDtypeStruct((N_ROWS * ROW_TILES, 128), _F32),
        compiler_params=_cparams("arbitrary"),
        name="experts",
    )(block_e, n_active, xs, w_gate, w_up, w_down)


def _final_kernel(dest_ref, x1_ref, h_ref, wt_ref, mod_ref, wsg_ref, wsu_ref, wsd_ref, fg_ref, ys_ref,
                  o_ref, buf_ref, sem):
    i = pl.program_id(0)
    tm = x1_ref.shape[0]
    base = i * tm

    def issue(t, c):
        for k in range(TOP_K):
            s = pl.multiple_of(dest_ref[k, base + t] * ROW_TILES, ROW_TILES)
            d = pl.multiple_of((k * tm + t) * ROW_TILES, ROW_TILES)
            pltpu.make_async_copy(ys_ref.at[pl.ds(s, ROW_TILES)], buf_ref.at[pl.ds(d, ROW_TILES)], sem).start()
        return c

    lax.fori_loop(0, tm, issue, 0)

    m = mod_ref[0]
    h = h_ref[...]
    a = jnp.dot(h, wsg_ref[...], preferred_element_type=_F32)
    b = jnp.dot(h, wsu_ref[...], preferred_element_type=_F32)
    shared = jnp.dot((_silu(a) * b).astype(_BF16), wsd_ref[...], preferred_element_type=_F32)

    for k in range(TOP_K):
        blk = pl.ds(k * tm * ROW_TILES, tm * ROW_TILES)
        pltpu.make_async_copy(ys_ref.at[pl.ds(0, tm * ROW_TILES)], buf_ref.at[blk], sem).wait()

    wt = wt_ref[...]
    chunks = []
    for j in range(ROW_TILES):
        acc = None
        for k in range(TOP_K):
            part = wt[:, k:k + 1] * buf_ref[pl.ds(k * tm * ROW_TILES + j, tm, stride=ROW_TILES), :]
            acc = part if acc is None else acc + part
        chunks.append(acc)
    routed = jnp.concatenate(chunks, axis=1)
    y = x1_ref[...] + m[5:6] * (routed + shared)
    o_ref[...] = y * _rms_scale(y) * fg_ref[...]


def _final(dest, x1, h2, w_tok, mod3, wsg_bf, wsu_bf, wsd_bf, final_g, ys):
    tm = TM_COMBINE
    tiles_per_batch = SEQ // tm
    tok = lambda w: pl.BlockSpec((tm, w), lambda i, *_: (i, 0))
    const = lambda shape: pl.BlockSpec(shape, lambda i, *_: (0,) * len(shape), pipeline_mode=pl.Buffered(1))
    grid_spec = pltpu.PrefetchScalarGridSpec(
        num_scalar_prefetch=1,
        grid=(N_TOK // tm,),
        in_specs=[tok(D_MODEL), tok(D_MODEL), tok(TOP_K),
                  pl.BlockSpec((1, N_MOD, D_MODEL), lambda i, *_: (i // tiles_per_batch, 0, 0)),
                  const((D_MODEL, D_SHARED)), const((D_MODEL, D_SHARED)), const((D_SHARED, D_MODEL)),
                  const((1, D_MODEL)),
                  pl.BlockSpec(memory_space=pl.ANY)],
        out_specs=tok(D_MODEL),
        scratch_shapes=[pltpu.VMEM((TOP_K * tm * ROW_TILES, 128), _F32), pltpu.SemaphoreType.DMA(())],
    )
    return pl.pallas_call(
        _final_kernel,
        grid_spec=grid_spec,
        out_shape=jax.ShapeDtypeStruct((N_TOK, D_MODEL), _F32),
        compiler_params=_cparams("arbitrary"),
        name="final",
    )(dest, x1, h2, w_tok, mod3, wsg_bf, wsu_bf, wsd_bf, final_g, ys)


def _s5_param_layouts(lam_re, lam_im, b_re, b_im, c_re, c_im, log_dt):
    dt = jnp.exp(log_dt.astype(_F32))
    two = lambda a: jnp.concatenate([a, a], axis=-1)
    lr, li = two(lam_re), two(lam_im)
    dtb = jnp.broadcast_to(dt[:, :, None], lr.shape)
    zeros = jnp.zeros_like(lr)
    pcol = jnp.stack([lr, li, dtb, zeros], axis=-1)
    prow = jnp.stack([lr, li, dtb] + [zeros] * 5, axis=2)
    ct2 = jnp.concatenate([jnp.swapaxes(c_re, -1, -2), jnp.swapaxes(c_im, -1, -2)], axis=2)
    bt1 = jnp.concatenate([jnp.swapaxes(b_re, -1, -2), jnp.swapaxes(b_im, -1, -2)], axis=3)
    return pcol, prow, ct2, bt1


def kernel(x, c, ctx, c_ctx, norm1_g, norm2_g, w_ada, b_ada, w_in, ssm_lam_re, ssm_lam_im, ssm_b_re, ssm_b_im, ssm_c_re, ssm_c_im, ssm_log_dt, ssm_d, ssm_w_glu, conv_w, conv_b, mix_norm_g, w_out, router_w, router_bias, exp_w_gate, exp_w_up, exp_w_down, shared_w_gate, shared_w_up, shared_w_down, final_g):
    layer = 0
    x2 = x.reshape(N_TOK, D_MODEL)
    ctx2 = ctx.reshape(N_CTX_TOK, D_MODEL)

    c8 = jnp.concatenate([c, c_ctx[None, :], jnp.zeros((8 - BATCH - 1, D_MODEL), _F32)], axis=0)
    mod = _ada_mod(c8, w_ada[layer], b_ada[layer][None, :])
    mod3 = mod.reshape(8, N_MOD, D_MODEL)

    w_in_bf = w_in[layer].astype(_BF16)
    conv_w8 = jnp.concatenate([conv_w[layer], jnp.zeros((8 - conv_w.shape[1], D_CONV), _F32)], axis=0)
    u_x, conv_x = _inproj(x2, mod3, norm1_g[layer][None, :], w_in_bf, conv_w8, conv_b[layer][None, :])
    u_c = _ctxproj(ctx2, mod3, norm1_g[layer][None, :], w_in_bf[:, :D_SSM])

    pcol, prow, ct2, bt1 = _s5_param_layouts(ssm_lam_re[layer], ssm_lam_im[layer], ssm_b_re[layer],
                                             ssm_b_im[layer], ssm_c_re[layer], ssm_c_im[layer],
                                             ssm_log_dt[layer])
    toep, bpow, cpow, dec = _s5_params(pcol, prow, ct2, bt1)

    def to_chunks(u, n_chunks):
        u = u.astype(_BF16).reshape(BATCH, n_chunks, CHUNK, SSM_GROUPS, SSM_GROUP)
        return u.transpose(3, 1, 0, 2, 4).reshape(SSM_GROUPS, n_chunks * BATCH, CHUNK_W)

    u2 = jnp.concatenate([to_chunks(u_c, N_CTX_CHUNKS), to_chunks(u_x, N_X_CHUNKS)], axis=1)
    y2 = _s5_scan(u2, toep, bpow, cpow, dec)
    ys = (y2.reshape(SSM_GROUPS, N_X_CHUNKS, BATCH, CHUNK, SSM_GROUP)
          .transpose(2, 1, 3, 0, 4).reshape(N_TOK, D_SSM))

    rw = router_w[layer]
    rw_hi = rw.astype(_BF16)
    rw_lo = (rw - rw_hi.astype(_F32)).astype(_BF16)
    rwt = jnp.concatenate([rw_hi.T, rw_lo.T], axis=0)
    x1, h2, h2s, logits_t = _merge(ys, u_x, conv_x, x2, mod3, ssm_d[layer][None, :],
                                   mix_norm_g[layer][None, :], norm2_g[layer][None, :],
                                   ssm_w_glu[layer].astype(_BF16), w_out[layer].astype(_BF16), rwt)

    bias_col = jnp.broadcast_to(router_bias[layer][:, None], (N_EXPERTS, 128))
    eid, w_k, rank, cnt = _route(logits_t, bias_col)

    counts = cnt[:, 0].astype(jnp.int32)
    padded = (counts + EXPERT_BM - 1) // EXPERT_BM * EXPERT_BM
    pend = jnp.cumsum(padded)
    pstart = pend - padded
    dest = pstart[eid] + rank
    n_active = (pend[-1] // EXPERT_BM).astype(jnp.int32)
    blk = jnp.minimum(jnp.arange(N_BLOCKS, dtype=jnp.int32), n_active - 1)
    ends_before = (pend[None, :] <= (blk * EXPERT_BM)[:, None]).astype(jnp.int32)
    block_e = jnp.minimum(jnp.sum(ends_before, axis=1), N_EXPERTS - 1)

    xs = _dispatch(dest, pstart + counts, padded - counts, h2s)
    ys_rows = _experts(block_e, n_active[None], xs, exp_w_gate[layer], exp_w_up[layer], exp_w_down[layer])

    out = _final(dest, x1, h2, w_k.T, mod3, shared_w_gate[layer].astype(_BF16),
                 shared_w_up[layer].astype(_BF16), shared_w_down[layer].astype(_BF16), final_g[None, :], ys_rows)
    return out.reshape(BATCH, SEQ, D_MODEL)
```

```python
import functools
import math

import jax
import jax.numpy as jnp
from jax import lax
from jax.experimental import pallas as pl
from jax.experimental.pallas import tpu as pltpu

D_MODEL = 2048
BATCH = 4
SEQ = 2048
CTX_LEN = 256
GRID_W = 64
D_SSM = 1024
D_CONV = 1024
SSM_GROUP = 16
SSM_GROUPS = 64
SSM_STATE = 64
N_EXPERTS = 64
N_EXPERT_GROUPS = 8
GROUP_SIZE = N_EXPERTS // N_EXPERT_GROUPS
TOPK_GROUPS = 4
TOP_K = 8
D_EXPERT = 512
D_SHARED = 512
ROUTED_SCALE = 2.5
N_MOD = 6
EPS = 1e-6

N_TOK = BATCH * SEQ
N_CTX_TOK = BATCH * CTX_LEN

CHUNK = 16
CHUNK_W = CHUNK * SSM_GROUP
STATE_W = 2 * SSM_STATE
N_CTX_CHUNKS = CTX_LEN // CHUNK
N_X_CHUNKS = SEQ // CHUNK
N_SEQ_CHUNKS = N_CTX_CHUNKS + N_X_CHUNKS
SCAN_ROWS = N_SEQ_CHUNKS * BATCH
S5_GROUPS_PER_STEP = 8

TM = 256
ROW_TILES = D_MODEL // 128
EXPERT_BM = 256
N_ASSIGN = N_TOK * TOP_K
N_BLOCKS = N_ASSIGN // EXPERT_BM + N_EXPERTS
N_ROWS = N_BLOCKS * EXPERT_BM
TM_COMBINE = 128
VMEM_LIMIT = 56 * 1024 * 1024

_F32 = jnp.float32
_BF16 = jnp.bfloat16


def _cparams(*sem):
    return pltpu.CompilerParams(dimension_semantics=sem, vmem_limit_bytes=VMEM_LIMIT)


def _const_spec(shape):
    nd = len(shape)
    return pl.BlockSpec(shape, lambda *_: (0,) * nd, pipeline_mode=pl.Buffered(1))


def _rms_scale(xf):
    return lax.rsqrt(jnp.mean(xf * xf, axis=-1, keepdims=True) + EPS)


def _silu(x):
    return x * jax.nn.sigmoid(x)


def _ada_kernel(c_ref, w_ref, b_ref, o_ref):
    s = _silu(c_ref[...])
    o_ref[...] = jnp.dot(s, w_ref[...], preferred_element_type=_F32) + b_ref[...]


def _ada_mod(c8, w_ada, b_ada):
    n = w_ada.shape[1]
    tn = 1024
    return pl.pallas_call(
        _ada_kernel,
        grid=(n // tn,),
        in_specs=[pl.BlockSpec((8, D_MODEL), lambda j: (0, 0)),
                  pl.BlockSpec((D_MODEL, tn), lambda j: (0, j)),
                  pl.BlockSpec((1, tn), lambda j: (0, j))],
        out_specs=pl.BlockSpec((8, tn), lambda j: (0, j)),
        out_shape=jax.ShapeDtypeStruct((8, n), _F32),
        compiler_params=_cparams("arbitrary"),
        name="ada_mod",
    )(c8, w_ada, b_ada)


def _modulated_norm(x_ref, mod_ref, g_ref, shift_row, scale_row):
    xf = x_ref[...]
    m = mod_ref[0]
    h = xf * _rms_scale(xf) * g_ref[...]
    return h * (1.0 + m[scale_row:scale_row + 1]) + m[shift_row:shift_row + 1]


def _inproj_kernel(x_ref, mod_ref, g_ref, w_ref, cw_ref, cb_ref, u_ref, conv_ref):
    h = _modulated_norm(x_ref, mod_ref, g_ref, 0, 1).astype(_BF16)
    u_ref[...] = jnp.dot(h, w_ref[:, 0:D_SSM], preferred_element_type=_F32)
    tm = x_ref.shape[0]
    pos = lax.broadcasted_iota(jnp.int32, (tm, 1), 0) % GRID_W
    not_first = (pos != 0).astype(_F32)
    not_last = (pos != GRID_W - 1).astype(_F32)
    cw = cw_ref[...]
    nc = 256
    for j in range(D_CONV // nc):
        lo = j * nc
        bg = jnp.dot(h, w_ref[:, D_SSM + lo:D_SSM + lo + nc], preferred_element_type=_F32)
        cg = jnp.dot(h, w_ref[:, D_SSM + D_CONV + lo:D_SSM + D_CONV + lo + nc], preferred_element_type=_F32)
        v = jnp.dot(h, w_ref[:, D_SSM + 2 * D_CONV + lo:D_SSM + 2 * D_CONV + lo + nc],
                    preferred_element_type=_F32)
        z = cg * v
        z_prev = pltpu.roll(z, 1, axis=0) * not_first
        z_next = pltpu.roll(z, tm - 1, axis=0) * not_last
        y = (cb_ref[:, lo:lo + nc] + z_prev * cw[0:1, lo:lo + nc] + z * cw[1:2, lo:lo + nc]
             + z_next * cw[2:3, lo:lo + nc])
        conv_ref[:, lo:lo + nc] = bg * y


def _inproj(x2, mod3, norm_g, w_in_bf, conv_w, conv_b):
    d_in = w_in_bf.shape[1]
    tiles_per_batch = SEQ // TM
    return pl.pallas_call(
        _inproj_kernel,
        grid=(N_TOK // TM,),
        in_specs=[pl.BlockSpec((TM, D_MODEL), lambda i: (i, 0)),
                  pl.BlockSpec((1, N_MOD, D_MODEL), lambda i: (i // tiles_per_batch, 0, 0)),
                  _const_spec((1, D_MODEL)),
                  _const_spec((D_MODEL, d_in)),
                  _const_spec((8, D_CONV)),
                  _const_spec((1, D_CONV))],
        out_specs=[pl.BlockSpec((TM, D_SSM), lambda i: (i, 0)),
                   pl.BlockSpec((TM, D_CONV), lambda i: (i, 0))],
        out_shape=[jax.ShapeDtypeStruct((N_TOK, D_SSM), _F32),
                   jax.ShapeDtypeStruct((N_TOK, D_CONV), _F32)],
        compiler_params=_cparams("arbitrary"),
        name="in_proj",
    )(x2, mod3, norm_g, w_in_bf, conv_w, conv_b)


def _ctxproj_kernel(x_ref, mod_ref, g_ref, w_ref, u_ref):
    h = _modulated_norm(x_ref, mod_ref, g_ref, 0, 1).astype(_BF16)
    u_ref[...] = jnp.dot(h, w_ref[...], preferred_element_type=_F32)


def _ctxproj(ctx2, mod3, norm_g, w_u_bf):
    return pl.pallas_call(
        _ctxproj_kernel,
        grid=(N_CTX_TOK // TM,),
        in_specs=[pl.BlockSpec((TM, D_MODEL), lambda i: (i, 0)),
                  pl.BlockSpec((1, N_MOD, D_MODEL), lambda i: (BATCH, 0, 0)),
                  _const_spec((1, D_MODEL)),
                  _const_spec((D_MODEL, D_SSM))],
        out_specs=pl.BlockSpec((TM, D_SSM), lambda i: (i, 0)),
        out_shape=jax.ShapeDtypeStruct((N_CTX_TOK, D_SSM), _F32),
        compiler_params=_cparams("arbitrary"),
        name="ctx_proj",
    )(ctx2, mod3, norm_g, w_u_bf)


N_POW = 32


def _s5_param_kernel(prow_ref, ct_ref, bt_ref, toep_ref, bpow_ref, cpow_ref, dec_ref):
    lane = lambda shape: lax.broadcasted_iota(jnp.int32, shape, 1)
    sub = lambda shape: lax.broadcasted_iota(jnp.int32, shape, 0)

    def split2(a):
        a_hi = a.astype(_BF16)
        return a_hi, (a - a_hi.astype(_F32)).astype(_BF16)

    rep = (lane((SSM_GROUP, CHUNK_W)) % SSM_GROUP == sub((SSM_GROUP, CHUNK_W))).astype(_BF16)
    sgn_col = jnp.where(sub((STATE_W, 1)) < SSM_STATE, 1.0, -1.0).astype(_F32)
    sgn_row = jnp.where(lane((1, STATE_W)) < SSM_STATE, -1.0, 1.0).astype(_F32)
    tau_col = sub((N_POW, 1)).astype(_F32)
    blk_l = lane((N_POW, CHUNK_W)) // SSM_GROUP
    tau_s = sub((N_POW, CHUNK_W))
    blk_r = sub((CHUNK_W, N_POW)) // SSM_GROUP
    tau_l = lane((CHUNK_W, N_POW))
    pick_l = lambda e: (tau_s == e).astype(_BF16)
    pick_r = lambda e: (tau_l == e).astype(_BF16)
    tn = (((0,), (0,)), ((), ()))

    strips = []
    for d in range(2):
        pr = prow_ref[d, 0]
        lam_re, lam_im, dt = pr[0:1], pr[1:2], jnp.exp(pr[2:3])
        mag = jnp.exp(tau_col * (lam_re * dt))
        ang = tau_col * (lam_im * dt)
        pw_re = mag * jnp.cos(ang)
        pw_im = mag * jnp.sin(ang)
        pw4 = jnp.concatenate(split2(pw_re) + split2(pw_im), axis=1)

        c_hi, c_lo = split2(ct_ref[d, 0])
        ct = (jnp.dot(c_hi, rep, preferred_element_type=_F32)
              + jnp.dot(c_lo, rep, preferred_element_type=_F32))
        ca = ct * sgn_col
        cb = -pltpu.roll(ct, SSM_STATE, axis=0)

        def cpow(sel):
            o = lax.dot_general(pw4, sel, tn, preferred_element_type=_F32)
            x = o[0:STATE_W] + o[STATE_W:2 * STATE_W]
            y = o[2 * STATE_W:3 * STATE_W] + o[3 * STATE_W:4 * STATE_W]
            return x * ca + y * cb

        nr = pw_re[1:2] - 1.0
        ni = pw_im[1:2]
        den = lam_re * lam_re + lam_im * lam_im
        kr = (nr * lam_re + ni * lam_im) / den
        ki = (ni * lam_re - nr * lam_im) / den
        b1 = bt_ref[d, 0]
        b2 = pltpu.roll(b1, SSM_STATE, axis=1) * sgn_row
        u1 = kr * b1 + ki * b2
        u2 = kr * b2 - ki * b1
        u1t = jnp.concatenate([u1] * CHUNK, axis=0)
        u2t = jnp.concatenate([u2] * CHUNK, axis=0)

        def bpow(sel):
            o = jnp.dot(sel, pw4, preferred_element_type=_F32)
            x = o[:, 0:STATE_W] + o[:, STATE_W:2 * STATE_W]
            y = o[:, 2 * STATE_W:3 * STATE_W] + o[:, 3 * STATE_W:4 * STATE_W]
            return x * u1t + y * u2t

        if d == 0:
            e_strip, e_b, e_c = blk_l, (CHUNK - 1) - blk_r, blk_l + 1
        else:
            e_strip, e_b, e_c = (CHUNK - 1) - blk_l, blk_r, CHUNK - blk_l
        u_hi, u_lo = split2(u1)
        k_hi, k_lo = split2(cpow(pick_l(e_strip)))
        strips.append(jnp.dot(u_hi, k_hi, preferred_element_type=_F32)
                      + jnp.dot(u_hi, k_lo, preferred_element_type=_F32)
                      + jnp.dot(u_lo, k_hi, preferred_element_type=_F32))
        bpow_ref[0, :, d * STATE_W:(d + 1) * STATE_W] = bpow(pick_r(e_b)).astype(_BF16)
        cpow_ref[0, d * STATE_W:(d + 1) * STATE_W, :] = cpow(pick_l(e_c)).astype(_BF16)

        dec_ref[0, 2 * d:2 * d + 1, :] = pw_re[CHUNK:CHUNK + 1]
        dec_ref[0, 2 * d + 1:2 * d + 2, :] = pw_im[CHUNK:CHUNK + 1] * sgn_row

    zeros = jnp.zeros((SSM_GROUP, CHUNK_W), _F32)
    strip = (jnp.concatenate([strips[1], zeros], axis=1)
             + pltpu.roll(jnp.concatenate([strips[0], zeros], axis=1), CHUNK_W - SSM_GROUP, axis=1))
    for i in range(CHUNK):
        off = (CHUNK - 1 - i) * SSM_GROUP
        win = strip if off == 0 else pltpu.roll(strip, 2 * CHUNK_W - off, axis=1)
        toep_ref[0, i * SSM_GROUP:(i + 1) * SSM_GROUP, :] = win[:, 0:CHUNK_W].astype(_BF16)


def _s5_params(prow, ct2, bt1):
    g_spec = lambda shape: pl.BlockSpec(shape, lambda g: (0, g, 0, 0))
    o_spec = lambda shape: pl.BlockSpec(shape, lambda g: (g, 0, 0))
    return pl.pallas_call(
        _s5_param_kernel,
        grid=(SSM_GROUPS,),
        in_specs=[g_spec((2, 1, 8, STATE_W)),
                  g_spec((2, 1, STATE_W, SSM_GROUP)), g_spec((2, 1, SSM_GROUP, STATE_W))],
        out_specs=[o_spec((1, CHUNK_W, CHUNK_W)), o_spec((1, CHUNK_W, 2 * STATE_W)),
                   o_spec((1, 2 * STATE_W, CHUNK_W)), o_spec((1, 4, STATE_W))],
        out_shape=[jax.ShapeDtypeStruct((SSM_GROUPS, CHUNK_W, CHUNK_W), _BF16),
                   jax.ShapeDtypeStruct((SSM_GROUPS, CHUNK_W, 2 * STATE_W), _BF16),
                   jax.ShapeDtypeStruct((SSM_GROUPS, 2 * STATE_W, CHUNK_W), _BF16),
                   jax.ShapeDtypeStruct((SSM_GROUPS, 4, STATE_W), _F32)],
        compiler_params=_cparams("arbitrary"),
        name="s5_params",
    )(prow, ct2, bt1)


def _s5_kernel(ux_ref, uc_ref, toep_ref, bpow_ref, cpow_ref, dec_ref, y_ref, u_ref, s_ref, h_ref, y2_ref):
    ng = S5_GROUPS_PER_STEP
    gw = SSM_GROUP
    x0 = N_CTX_CHUNKS * BATCH

    half = CHUNK // 2
    blk = 32
    lane_blk = lax.broadcasted_iota(jnp.int32, (1, 128), 1) // gw

    def to_chunk_layout(src_ref, tok0, n_chunks, row0):
        for c0 in range(0, n_chunks, blk):
            nb = min(blk, n_chunks - c0)
            for hf in range(2):
                acc = [None] * ng
                for il in range(half):
                    p = src_ref[pl.ds(tok0 + c0 * CHUNK + hf * half + il, nb, stride=CHUNK), :]
                    for g in range(ng):
                        r = p if il == g else pltpu.roll(p, ((il - g) * gw) % 128, axis=1)
                        acc[g] = r if il == 0 else jnp.where(lane_blk == il, r, acc[g])
                for g in range(ng):
                    u_ref[g, hf, pl.ds(row0 + c0 * BATCH, nb, stride=BATCH), :] = acc[g]

    for b in range(BATCH):
        to_chunk_layout(uc_ref, b * CTX_LEN, N_CTX_CHUNKS, b)
        to_chunk_layout(ux_ref, b * SEQ, N_X_CHUNKS, x0 + b)

    chunk_rows = lambda ref, g, lo: jnp.concatenate([ref[g, 0, lo:, :], ref[g, 1, lo:, :]], axis=1)
    for g in range(ng):
        s_ref[g] = jnp.dot(chunk_rows(u_ref, g, 0).astype(_BF16), bpow_ref[g], preferred_element_type=_F32)

    pair = 2 * BATCH
    n_ctx_pairs = N_CTX_CHUNKS // 2
    n_pairs = N_SEQ_CHUNKS // 2
    lower = lax.broadcasted_iota(jnp.int32, (pair, STATE_W), 0) < BATCH
    zero = jnp.zeros((pair, STATE_W), _F32)

    def step(t, carry):
        tb = jnp.where(t < n_ctx_pairs, n_ctx_pairs - 1 - t, n_pairs + n_ctx_pairs - 1 - t)
        rf = pl.multiple_of(t * pair, pair)
        rb = pl.multiple_of(tb * pair, pair)
        out = []
        for g in range(ng):
            hf, hb = carry[2 * g], carry[2 * g + 1]
            dg = dec_ref[g]
            mul_f = lambda v: dg[0:1] * v + dg[1:2] * pltpu.roll(v, SSM_STATE, axis=1)
            mul_b = lambda v: dg[2:3] * v + dg[3:4] * pltpu.roll(v, SSM_STATE, axis=1)

            sf = s_ref[g, pl.ds(rf, pair), 0:STATE_W]
            mid = mul_f(hf) + jnp.where(lower, sf, pltpu.roll(sf, BATCH, axis=0))
            h_ref[g, pl.ds(rf, pair), 0:STATE_W] = jnp.where(lower, hf, mid)
            end = mul_f(mid) + sf
            hf = jnp.where(lower, pltpu.roll(end, BATCH, axis=0), end)

            sb = s_ref[g, pl.ds(rb, pair), STATE_W:2 * STATE_W]
            mid = mul_b(hb) + jnp.where(lower, pltpu.roll(sb, BATCH, axis=0), sb)
            h_ref[g, pl.ds(rb, pair), STATE_W:2 * STATE_W] = jnp.where(lower, mid, hb)
            end = mul_b(mid) + sb
            hb = jnp.where(lower, end, pltpu.roll(end, BATCH, axis=0))
            out += [hf, hb]
        return tuple(out)

    lax.fori_loop(0, n_pairs, step, (zero,) * (2 * ng))

    for g in range(ng):
        ux = chunk_rows(u_ref, g, x0).astype(_BF16)
        hx = h_ref[g, x0:SCAN_ROWS, :].astype(_BF16)
        y2 = (jnp.dot(ux, toep_ref[g], preferred_element_type=_F32)
              + jnp.dot(hx, cpow_ref[g], preferred_element_type=_F32))
        y2_ref[g, 0] = y2[:, 0:128]
        y2_ref[g, 1] = y2[:, 128:256]

    for b in range(BATCH):
        for c0 in range(0, N_X_CHUNKS, blk):
            for hf in range(2):
                pieces = [y2_ref[g, hf, pl.ds(b + c0 * BATCH, blk, stride=BATCH), :] for g in range(ng)]
                for jl in range(half):
                    acc = None
                    for g in range(ng):
                        r = pieces[g] if jl == g else pltpu.roll(pieces[g], ((g - jl) * gw) % 128, axis=1)
                        acc = r if g == 0 else jnp.where(lane_blk == g, r, acc)
                    y_ref[pl.ds(b * SEQ + c0 * CHUNK + hf * half + jl, blk, stride=CHUNK), :] = acc


def _s5_scan(u_x, u_c, toep, bpow, cpow, dec):
    ng = S5_GROUPS_PER_STEP
    n_x_rows = N_X_CHUNKS * BATCH
    lanes = ng * SSM_GROUP
    spec = lambda r, c: pl.BlockSpec((ng, r, c), lambda i: (i, 0, 0))
    col = lambda rows: pl.BlockSpec((rows, lanes), lambda i: (0, i))
    return pl.pallas_call(
        _s5_kernel,
        grid=(SSM_GROUPS // ng,),
        in_specs=[col(N_TOK), col(N_CTX_TOK), spec(CHUNK_W, CHUNK_W), spec(CHUNK_W, 2 * STATE_W),
                  spec(2 * STATE_W, CHUNK_W), spec(4, STATE_W)],
        out_specs=col(N_TOK),
        out_shape=jax.ShapeDtypeStruct((N_TOK, D_SSM), _F32),
        scratch_shapes=[pltpu.VMEM((ng, 2, SCAN_ROWS, 128), _F32),
                        pltpu.VMEM((ng, SCAN_ROWS, 2 * STATE_W), _F32),
                        pltpu.VMEM((ng, SCAN_ROWS, 2 * STATE_W), _F32),
                        pltpu.VMEM((ng, 2, n_x_rows, 128), _F32)],
        compiler_params=_cparams("arbitrary"),
        name="s5_scan",
    )(u_x, u_c, toep, bpow, cpow, dec)


def _merge_kernel(ys_ref, u_ref, conv_ref, x_ref, mod_ref, dskip_ref, mixg_ref, n2g_ref,
                  wglu_ref, wout_ref, rwt_ref, x1_ref, h2_ref, h2s_ref, logit_ref):
    m = mod_ref[0]
    yx = dskip_ref[...] * u_ref[...] + ys_ref[...]
    c0 = math.sqrt(2.0 / math.pi)
    ge = 0.5 * yx * (1.0 + jnp.tanh(c0 * (yx + 0.044715 * (yx * yx * yx))))
    z = jnp.dot(ge.astype(_BF16), wglu_ref[...], preferred_element_type=_F32)
    ssm_y = z[:, 0:D_SSM] * jax.nn.sigmoid(z[:, D_SSM:2 * D_SSM])
    conv_y = conv_ref[...]
    mixg = mixg_ref[...]
    heads_a = (ssm_y * _rms_scale(ssm_y) * mixg[:, 0:D_SSM]).astype(_BF16)
    heads_b = (conv_y * _rms_scale(conv_y) * mixg[:, D_SSM:]).astype(_BF16)
    mix = (jnp.dot(heads_a, wout_ref[0:D_SSM, :], preferred_element_type=_F32)
           + jnp.dot(heads_b, wout_ref[D_SSM:, :], preferred_element_type=_F32))
    x1 = x_ref[...] + m[2:3] * mix
    x1_ref[...] = x1
    h2 = x1 * _rms_scale(x1) * n2g_ref[...] * (1.0 + m[4:5]) + m[3:4]
    h_hi = h2.astype(_BF16)
    h2_ref[...] = h_hi
    tm = x_ref.shape[0]
    for j in range(ROW_TILES):
        h2s_ref[pl.ds(j, tm, stride=ROW_TILES), :] = h2[:, j * 128:(j + 1) * 128]
    h_lo = (h2 - h_hi.astype(_F32)).astype(_BF16)
    nt = (((1,), (1,)), ((), ()))
    p = lax.dot_general(rwt_ref[...], h_hi, nt, preferred_element_type=_F32)
    q = lax.dot_general(rwt_ref[0:N_EXPERTS, :], h_lo, nt, preferred_element_type=_F32)
    logit_ref[...] = p[0:N_EXPERTS] + p[N_EXPERTS:2 * N_EXPERTS] + q


def _merge(ys, u, conv, x2, mod3, dskip, mixg, n2g, wglu_bf, wout_bf, rwt):
    tiles_per_batch = SEQ // TM
    tok = lambda w: pl.BlockSpec((TM, w), lambda i: (i, 0))
    return pl.pallas_call(
        _merge_kernel,
        grid=(N_TOK // TM,),
        in_specs=[tok(D_SSM), tok(D_SSM), tok(D_CONV), tok(D_MODEL),
                  pl.BlockSpec((1, N_MOD, D_MODEL), lambda i: (i // tiles_per_batch, 0, 0)),
                  _const_spec((1, D_SSM)), _const_spec((1, D_MODEL)), _const_spec((1, D_MODEL)),
                  _const_spec((D_SSM, 2 * D_SSM)), _const_spec((D_MODEL, D_MODEL)),
                  _const_spec((2 * N_EXPERTS, D_MODEL))],
        out_specs=[tok(D_MODEL), tok(D_MODEL),
                   pl.BlockSpec((TM * ROW_TILES, 128), lambda i: (i, 0)),
                   pl.BlockSpec((N_EXPERTS, TM), lambda i: (0, i))],
        out_shape=[jax.ShapeDtypeStruct((N_TOK, D_MODEL), _F32),
                   jax.ShapeDtypeStruct((N_TOK, D_MODEL), _BF16),
                   jax.ShapeDtypeStruct((N_TOK * ROW_TILES, 128), _F32),
                   jax.ShapeDtypeStruct((N_EXPERTS, N_TOK), _F32)],
        compiler_params=_cparams("arbitrary"),
        name="merge_heads",
    )(ys, u, conv, x2, mod3, dskip, mixg, n2g, wglu_bf, wout_bf, rwt)


def _route_kernel(logit_ref, bias_ref, eid_ref, w_ref, rank_ref, cnt_ref, carry_ref):
    @pl.when(pl.program_id(0) == 0)
    def _():
        carry_ref[...] = jnp.zeros_like(carry_ref)

    tm = logit_ref.shape[1]
    neg = jnp.float32(-jnp.inf)
    scores = jax.nn.sigmoid(logit_ref[...])
    biased = scores + bias_ref[:, 0:1]
    sub = lax.broadcasted_iota(jnp.int32, (GROUP_SIZE, tm), 0)
    rows = lambda a, g: a[g * GROUP_SIZE:(g + 1) * GROUP_SIZE]
    ngrp = N_EXPERT_GROUPS

    gscore = []
    for g in range(ngrp):
        bg = rows(biased, g)
        m1 = jnp.max(bg, axis=0, keepdims=True)
        first = jnp.min(jnp.where(bg == m1, sub, GROUP_SIZE), axis=0, keepdims=True)
        m2 = jnp.max(jnp.where(sub == first, neg, bg), axis=0, keepdims=True)
        gscore.append(m1 + m2)
    v = []
    for g in range(ngrp):
        beaten = jnp.zeros((1, tm), jnp.int32)
        for o in range(ngrp):
            if o != g:
                beats = (gscore[o] >= gscore[g]) if o < g else (gscore[o] > gscore[g])
                beaten = beaten + beats.astype(jnp.int32)
        v.append(jnp.where(beaten < TOPK_GROUPS, rows(biased, g), neg))
    eids = [sub + g * GROUP_SIZE for g in range(ngrp)]

    pick_ids, pick_masks = [], []
    for _ in range(TOP_K):
        m = functools.reduce(jnp.maximum, v)
        m = jnp.max(m, axis=0, keepdims=True)
        cand = functools.reduce(jnp.minimum, [jnp.where(v[g] == m, eids[g], N_EXPERTS) for g in range(ngrp)])
        pick_id = jnp.min(cand, axis=0, keepdims=True)
        masks = [eids[g] == pick_id for g in range(ngrp)]
        v = [jnp.where(masks[g], neg, v[g]) for g in range(ngrp)]
        pick_ids.append(pick_id)
        pick_masks.append(masks)

    sel = jnp.concatenate(
        [functools.reduce(jnp.logical_or, [pick_masks[k][g] for k in range(TOP_K)]).astype(_F32)
         for g in range(ngrp)], axis=0)
    before = (lax.broadcasted_iota(jnp.int32, (tm, tm), 0)
              < lax.broadcasted_iota(jnp.int32, (tm, tm), 1)).astype(_BF16)
    base = jnp.dot(sel.astype(_BF16), before, preferred_element_type=_F32) + carry_ref[:, 0:1]

    def gather_pick(a, k):
        parts = [jnp.sum(jnp.where(pick_masks[k][g], rows(a, g), 0.0), axis=0, keepdims=True)
                 for g in range(ngrp)]
        return functools.reduce(jnp.add, parts)

    picked = [gather_pick(scores, k) for k in range(TOP_K)]
    denom = functools.reduce(jnp.add, picked)
    for k in range(TOP_K):
        eid_ref[k:k + 1, :] = pick_ids[k]
        w_ref[k:k + 1, :] = picked[k] / denom * ROUTED_SCALE
        rank_ref[k:k + 1, :] = gather_pick(base, k).astype(jnp.int32)
    carry_ref[...] = carry_ref[...] + jnp.sum(sel, axis=1, keepdims=True)
    cnt_ref[...] = carry_ref[...]


def _route(logits_t, bias_col):
    tok = pl.BlockSpec((TOP_K, TM), lambda i: (0, i))
    return pl.pallas_call(
        _route_kernel,
        grid=(N_TOK // TM,),
        in_specs=[pl.BlockSpec((N_EXPERTS, TM), lambda i: (0, i)), _const_spec((N_EXPERTS, 128))],
        out_specs=[tok, tok, tok, pl.BlockSpec((N_EXPERTS, 128), lambda i: (0, 0))],
        out_shape=[jax.ShapeDtypeStruct((TOP_K, N_TOK), jnp.int32),
                   jax.ShapeDtypeStruct((TOP_K, N_TOK), _F32),
                   jax.ShapeDtypeStruct((TOP_K, N_TOK), jnp.int32),
                   jax.ShapeDtypeStruct((N_EXPERTS, 128), _F32)],
        scratch_shapes=[pltpu.VMEM((N_EXPERTS, 128), _F32)],
        compiler_params=_cparams("arbitrary"),
        name="route",
    )(logits_t, bias_col)


def _pow2_pieces():
    bit = EXPERT_BM // 2
    while bit >= 1:
        yield bit
        bit //= 2


def _dispatch_kernel(dest_ref, padstart_ref, padn_ref, h_ref, xs_ref, zero_ref, sem, zsem):
    i = pl.program_id(0)
    tm = h_ref.shape[0] // ROW_TILES
    base = i * tm

    def issue(t, c):
        src = h_ref.at[pl.ds(pl.multiple_of(t * ROW_TILES, ROW_TILES), ROW_TILES)]
        for k in range(TOP_K):
            d = pl.multiple_of(dest_ref[k, base + t] * ROW_TILES, ROW_TILES)
            pltpu.make_async_copy(src, xs_ref.at[pl.ds(d, ROW_TILES)], sem).start()
        return c

    lax.fori_loop(0, tm, issue, 0)

    @pl.when(i == 0)
    def _():
        zero_ref[...] = jnp.zeros_like(zero_ref)

        def pieces(e, act):
            n = padn_ref[e]
            start = padstart_ref[e]
            for bit in _pow2_pieces():
                d = pl.multiple_of(start * ROW_TILES, ROW_TILES)
                cp = pltpu.make_async_copy(zero_ref.at[pl.ds(0, bit * ROW_TILES)],
                                           xs_ref.at[pl.ds(d, bit * ROW_TILES)], zsem)
                pl.when((n & bit) != 0)(functools.partial(act, cp))
                start = start + (n & bit)

        def zstart(e, c):
            pieces(e, lambda cp: cp.start())
            return c

        def zwait(e, c):
            pieces(e, lambda cp: cp.wait())
            return c

        lax.fori_loop(0, N_EXPERTS, zstart, 0)
        lax.fori_loop(0, N_EXPERTS, zwait, 0)

    for _ in range(TOP_K):
        pltpu.make_async_copy(h_ref, xs_ref.at[pl.ds(0, tm * ROW_TILES)], sem).wait()


def _dispatch(dest, pad_start, pad_n, h2s):
    tm = TM
    grid_spec = pltpu.PrefetchScalarGridSpec(
        num_scalar_prefetch=3,
        grid=(N_TOK // tm,),
        in_specs=[pl.BlockSpec((tm * ROW_TILES, 128), lambda i, *_: (i, 0))],
        out_specs=pl.BlockSpec(memory_space=pl.ANY),
        scratch_shapes=[pltpu.VMEM((EXPERT_BM // 2 * ROW_TILES, 128), _F32),
                        pltpu.SemaphoreType.DMA(()), pltpu.SemaphoreType.DMA(())],
    )
    return pl.pallas_call(
        _dispatch_kernel,
        grid_spec=grid_spec,
        out_shape=jax.ShapeDtypeStruct((N_ROWS * ROW_TILES, 128), _F32),
        compiler_params=_cparams("arbitrary"),
        name="dispatch",
    )(dest, pad_start, pad_n, h2s)


def _experts_kernel(be_ref, nact_ref, xs_ref, wg_ref, wu_ref, wd_ref, ys_ref, wg_bf, wu_bf, wd_bf):
    i = pl.program_id(0)

    @pl.when(i < nact_ref[0])
    def _():
        @pl.when((i == 0) | (be_ref[i] != be_ref[jnp.maximum(i - 1, 0)]))
        def _():
            wg_bf[...] = wg_ref[0].astype(_BF16)
            wu_bf[...] = wu_ref[0].astype(_BF16)
            wd_bf[...] = wd_ref[0].astype(_BF16)

        bm = EXPERT_BM
        x = jnp.concatenate([xs_ref[pl.ds(j, bm, stride=ROW_TILES), :] for j in range(ROW_TILES)],
                            axis=1).astype(_BF16)
        a = jnp.dot(x, wg_bf[...], preferred_element_type=_F32)
        b = jnp.dot(x, wu_bf[...], preferred_element_type=_F32)
        y = jnp.dot((_silu(a) * b).astype(_BF16), wd_bf[...], preferred_element_type=_F32)
        for j in range(ROW_TILES):
            ys_ref[pl.ds(j, bm, stride=ROW_TILES), :] = y[:, j * 128:(j + 1) * 128]


def _experts(block_e, n_active, xs, w_gate, w_up, w_down):
    bm = EXPERT_BM
    row_blk = lambda i, be, na: (jnp.minimum(i, na[0] - 1), 0)
    grid_spec = pltpu.PrefetchScalarGridSpec(
        num_scalar_prefetch=2,
        grid=(N_BLOCKS,),
        in_specs=[pl.BlockSpec((bm * ROW_TILES, 128), row_blk),
                  pl.BlockSpec((1, D_MODEL, D_EXPERT), lambda i, be, na: (be[i], 0, 0)),
                  pl.BlockSpec((1, D_MODEL, D_EXPERT), lambda i, be, na: (be[i], 0, 0)),
                  pl.BlockSpec((1, D_EXPERT, D_MODEL), lambda i, be, na: (be[i], 0, 0))],
        out_specs=pl.BlockSpec((bm * ROW_TILES, 128), row_blk),
        scratch_shapes=[pltpu.VMEM((D_MODEL, D_EXPERT), _BF16), pltpu.VMEM((D_MODEL, D_EXPERT), _BF16),
                        pltpu.VMEM((D_EXPERT, D_MODEL), _BF16)],
    )
    return pl.pallas_call(
        _experts_kernel,
        grid_spec=grid_spec,
        out_shape=jax.ShapeDtypeStruct((N_ROWS * ROW_TILES, 128), _F32),
        compiler_params=_cparams("arbitrary"),
        name="experts",
    )(block_e, n_active, xs, w_gate, w_up, w_down)


def _final_kernel(dest_ref, x1_ref, h_ref, wt_ref, mod_ref, wsg_ref, wsu_ref, wsd_ref, fg_ref, ys_ref,
                  o_ref, buf_ref, sem):
    i = pl.program_id(0)
    tm = x1_ref.shape[0]
    base = i * tm

    def issue(t, c):
        for k in range(TOP_K):
            s = pl.multiple_of(dest_ref[k, base + t] * ROW_TILES, ROW_TILES)
            d = pl.multiple_of((k * tm + t) * ROW_TILES, ROW_TILES)
            pltpu.make_async_copy(ys_ref.at[pl.ds(s, ROW_TILES)], buf_ref.at[pl.ds(d, ROW_TILES)], sem).start()
        return c

    lax.fori_loop(0, tm, issue, 0)

    m = mod_ref[0]
    h = h_ref[...]
    a = jnp.dot(h, wsg_ref[...], preferred_element_type=_F32)
    b = jnp.dot(h, wsu_ref[...], preferred_element_type=_F32)
    shared = jnp.dot((_silu(a) * b).astype(_BF16), wsd_ref[...], preferred_element_type=_F32)

    for k in range(TOP_K):
        blk = pl.ds(k * tm * ROW_TILES, tm * ROW_TILES)
        pltpu.make_async_copy(ys_ref.at[pl.ds(0, tm * ROW_TILES)], buf_ref.at[blk], sem).wait()

    wt = wt_ref[...]
    chunks = []
    for j in range(ROW_TILES):
        acc = None
        for k in range(TOP_K):
            part = wt[:, k:k + 1] * buf_ref[pl.ds(k * tm * ROW_TILES + j, tm, stride=ROW_TILES), :]
            acc = part if acc is None else acc + part
        chunks.append(acc)
    routed = jnp.concatenate(chunks, axis=1)
    y = x1_ref[...] + m[5:6] * (routed + shared)
    o_ref[...] = y * _rms_scale(y) * fg_ref[...]


def _final(dest, x1, h2, w_tok, mod3, wsg_bf, wsu_bf, wsd_bf, final_g, ys):
    tm = TM_COMBINE
    tiles_per_batch = SEQ // tm
    tok = lambda w: pl.BlockSpec((tm, w), lambda i, *_: (i, 0))
    const = lambda shape: pl.BlockSpec(shape, lambda i, *_: (0,) * len(shape), pipeline_mode=pl.Buffered(1))
    grid_spec = pltpu.PrefetchScalarGridSpec(
        num_scalar_prefetch=1,
        grid=(N_TOK // tm,),
        in_specs=[tok(D_MODEL), tok(D_MODEL), tok(TOP_K),
                  pl.BlockSpec((1, N_MOD, D_MODEL), lambda i, *_: (i // tiles_per_batch, 0, 0)),
                  const((D_MODEL, D_SHARED)), const((D_MODEL, D_SHARED)), const((D_SHARED, D_MODEL)),
                  const((1, D_MODEL)),
                  pl.BlockSpec(memory_space=pl.ANY)],
        out_specs=tok(D_MODEL),
        scratch_shapes=[pltpu.VMEM((TOP_K * tm * ROW_TILES, 128), _F32), pltpu.SemaphoreType.DMA(())],
    )
    return pl.pallas_call(
        _final_kernel,
        grid_spec=grid_spec,
        out_shape=jax.ShapeDtypeStruct((N_TOK, D_MODEL), _F32),
        compiler_params=_cparams("arbitrary"),
        name="final",
    )(dest, x1, h2, w_tok, mod3, wsg_bf, wsu_bf, wsd_bf, final_g, ys)


def _s5_param_layouts(lam_re, lam_im, b_re, b_im, c_re, c_im, log_dt):
    two = lambda a: jnp.concatenate([a, a], axis=-1)
    lr, li = two(lam_re), two(lam_im)
    dtb = jnp.broadcast_to(log_dt.astype(_F32)[:, :, None], lr.shape)
    zeros = jnp.zeros_like(lr)
    prow = jnp.stack([lr, li, dtb] + [zeros] * 5, axis=2)
    ct2 = jnp.concatenate([jnp.swapaxes(c_re, -1, -2), jnp.swapaxes(c_im, -1, -2)], axis=2)
    bt1 = jnp.concatenate([jnp.swapaxes(b_re, -1, -2), jnp.swapaxes(b_im, -1, -2)], axis=3)
    return prow, ct2, bt1


def kernel(x, c, ctx, c_ctx, norm1_g, norm2_g, w_ada, b_ada, w_in, ssm_lam_re, ssm_lam_im, ssm_b_re, ssm_b_im, ssm_c_re, ssm_c_im, ssm_log_dt, ssm_d, ssm_w_glu, conv_w, conv_b, mix_norm_g, w_out, router_w, router_bias, exp_w_gate, exp_w_up, exp_w_down, shared_w_gate, shared_w_up, shared_w_down, final_g):
    layer = 0
    x2 = x.reshape(N_TOK, D_MODEL)
    ctx2 = ctx.reshape(N_CTX_TOK, D_MODEL)

    c8 = jnp.concatenate([c, c_ctx[None, :], jnp.zeros((8 - BATCH - 1, D_MODEL), _F32)], axis=0)
    mod = _ada_mod(c8, w_ada[layer], b_ada[layer][None, :])
    mod3 = mod.reshape(8, N_MOD, D_MODEL)

    w_in_bf = w_in[layer].astype(_BF16)
    conv_w8 = jnp.concatenate([conv_w[layer], jnp.zeros((8 - conv_w.shape[1], D_CONV), _F32)], axis=0)
    u_x, conv_x = _inproj(x2, mod3, norm1_g[layer][None, :], w_in_bf, conv_w8, conv_b[layer][None, :])
    u_c = _ctxproj(ctx2, mod3, norm1_g[layer][None, :], w_in_bf[:, :D_SSM])

    prow, ct2, bt1 = _s5_param_layouts(ssm_lam_re[layer], ssm_lam_im[layer], ssm_b_re[layer],
                                       ssm_b_im[layer], ssm_c_re[layer], ssm_c_im[layer], ssm_log_dt[layer])
    toep, bpow, cpow, dec = _s5_params(prow, ct2, bt1)
    ys = _s5_scan(u_x, u_c, toep, bpow, cpow, dec)

    rw = router_w[layer]
    rw_hi = rw.astype(_BF16)
    rw_lo = (rw - rw_hi.astype(_F32)).astype(_BF16)
    rwt = jnp.concatenate([rw_hi.T, rw_lo.T], axis=0)
    x1, h2, h2s, logits_t = _merge(ys, u_x, conv_x, x2, mod3, ssm_d[layer][None, :],
                                   mix_norm_g[layer][None, :], norm2_g[layer][None, :],
                                   ssm_w_glu[layer].astype(_BF16), w_out[layer].astype(_BF16), rwt)

    bias_col = jnp.broadcast_to(router_bias[layer][:, None], (N_EXPERTS, 128))
    eid, w_k, rank, cnt = _route(logits_t, bias_col)

    counts = cnt[:, 0].astype(jnp.int32)
    padded = (counts + EXPERT_BM - 1) // EXPERT_BM * EXPERT_BM
    pend = jnp.cumsum(padded)
    pstart = pend - padded
    is_e = eid[:, :, None] == jnp.arange(N_EXPERTS, dtype=jnp.int32)
    dest = jnp.sum(jnp.where(is_e, pstart, 0), axis=-1) + rank
    n_active = (pend[-1] // EXPERT_BM).astype(jnp.int32)
    blk = jnp.minimum(jnp.arange(N_BLOCKS, dtype=jnp.int32), n_active - 1)
    ends_before = (pend[None, :] <= (blk * EXPERT_BM)[:, None]).astype(jnp.int32)
    block_e = jnp.minimum(jnp.sum(ends_before, axis=1), N_EXPERTS - 1)

    xs = _dispatch(dest, pstart + counts, padded - counts, h2s)
    ys_rows = _experts(block_e, n_active[None], xs, exp_w_gate[layer], exp_w_up[layer], exp_w_down[layer])

    out = _final(dest, x1, h2, w_k.T, mod3, shared_w_gate[layer].astype(_BF16),
                 shared_w_up[layer].astype(_BF16), shared_w_down[layer].astype(_BF16), final_g[None, :], ys_rows)
    return out.reshape(BATCH, SEQ, D_MODEL)
```

```python
import functools
import math

import jax
import jax.numpy as jnp
from jax import lax
from jax.experimental import pallas as pl
from jax.experimental.pallas import tpu as pltpu

D_MODEL = 2048
BATCH = 4
SEQ = 2048
CTX_LEN = 256
GRID_W = 64
D_SSM = 1024
D_CONV = 1024
SSM_GROUP = 16
SSM_GROUPS = 64
SSM_STATE = 64
N_EXPERTS = 64
N_EXPERT_GROUPS = 8
GROUP_SIZE = N_EXPERTS // N_EXPERT_GROUPS
TOPK_GROUPS = 4
TOP_K = 8
D_EXPERT = 512
D_SHARED = 512
ROUTED_SCALE = 2.5
N_MOD = 6
EPS = 1e-6

N_TOK = BATCH * SEQ
N_CTX_TOK = BATCH * CTX_LEN

CHUNK = 16
CHUNK_W = CHUNK * SSM_GROUP
STATE_W = 2 * SSM_STATE
N_CTX_CHUNKS = CTX_LEN // CHUNK
N_X_CHUNKS = SEQ // CHUNK
N_SEQ_CHUNKS = N_CTX_CHUNKS + N_X_CHUNKS
SCAN_ROWS = N_SEQ_CHUNKS * BATCH
S5_GROUPS_PER_STEP = 8

TM = 256
EXPERT_BM = 256
N_ASSIGN = N_TOK * TOP_K
N_BLOCKS = N_ASSIGN // EXPERT_BM + N_EXPERTS
N_ROWS = N_BLOCKS * EXPERT_BM
TM_COMBINE = 128
VMEM_LIMIT = 56 * 1024 * 1024

_F32 = jnp.float32
_BF16 = jnp.bfloat16


def _cparams(*sem):
    return pltpu.CompilerParams(dimension_semantics=sem, vmem_limit_bytes=VMEM_LIMIT)


def _const_spec(shape):
    nd = len(shape)
    return pl.BlockSpec(shape, lambda *_: (0,) * nd, pipeline_mode=pl.Buffered(1))


def _rms_scale(xf):
    return lax.rsqrt(jnp.mean(xf * xf, axis=-1, keepdims=True) + EPS)


def _silu(x):
    return x * jax.nn.sigmoid(x)


def _ada_kernel(c_ref, w_ref, b_ref, o_ref):
    s = _silu(c_ref[...])
    o_ref[...] = jnp.dot(s, w_ref[...], preferred_element_type=_F32) + b_ref[...]


def _ada_mod(c8, w_ada, b_ada):
    n = w_ada.shape[1]
    tn = 1024
    return pl.pallas_call(
        _ada_kernel,
        grid=(n // tn,),
        in_specs=[pl.BlockSpec((8, D_MODEL), lambda j: (0, 0)),
                  pl.BlockSpec((D_MODEL, tn), lambda j: (0, j)),
                  pl.BlockSpec((1, tn), lambda j: (0, j))],
        out_specs=pl.BlockSpec((8, tn), lambda j: (0, j)),
        out_shape=jax.ShapeDtypeStruct((8, n), _F32),
        compiler_params=_cparams("arbitrary"),
        name="ada_mod",
    )(c8, w_ada, b_ada)


def _modulated_norm(x_ref, mod_ref, g_ref, shift_row, scale_row):
    xf = x_ref[...]
    m = mod_ref[0]
    h = xf * _rms_scale(xf) * g_ref[...]
    return h * (1.0 + m[scale_row:scale_row + 1]) + m[shift_row:shift_row + 1]


def _inproj_kernel(x_ref, mod_ref, g_ref, w_ref, cw_ref, cb_ref, u_ref, conv_ref):
    h = _modulated_norm(x_ref, mod_ref, g_ref, 0, 1).astype(_BF16)
    u_ref[...] = jnp.dot(h, w_ref[:, 0:D_SSM], preferred_element_type=_F32)
    tm = x_ref.shape[0]
    pos = lax.broadcasted_iota(jnp.int32, (tm, 1), 0) % GRID_W
    not_first = (pos != 0).astype(_F32)
    not_last = (pos != GRID_W - 1).astype(_F32)
    cw = cw_ref[...]
    nc = 256
    for j in range(D_CONV // nc):
        lo = j * nc
        bg = jnp.dot(h, w_ref[:, D_SSM + lo:D_SSM + lo + nc], preferred_element_type=_F32)
        cg = jnp.dot(h, w_ref[:, D_SSM + D_CONV + lo:D_SSM + D_CONV + lo + nc], preferred_element_type=_F32)
        v = jnp.dot(h, w_ref[:, D_SSM + 2 * D_CONV + lo:D_SSM + 2 * D_CONV + lo + nc],
                    preferred_element_type=_F32)
        z = cg * v
        z_prev = pltpu.roll(z, 1, axis=0) * not_first
        z_next = pltpu.roll(z, tm - 1, axis=0) * not_last
        y = (cb_ref[:, lo:lo + nc] + z_prev * cw[0:1, lo:lo + nc] + z * cw[1:2, lo:lo + nc]
             + z_next * cw[2:3, lo:lo + nc])
        conv_ref[:, lo:lo + nc] = bg * y


def _inproj(x2, mod3, norm_g, w_in_bf, conv_w, conv_b):
    d_in = w_in_bf.shape[1]
    tiles_per_batch = SEQ // TM
    return pl.pallas_call(
        _inproj_kernel,
        grid=(N_TOK // TM,),
        in_specs=[pl.BlockSpec((TM, D_MODEL), lambda i: (i, 0)),
                  pl.BlockSpec((1, N_MOD, D_MODEL), lambda i: (i // tiles_per_batch, 0, 0)),
                  _const_spec((1, D_MODEL)),
                  _const_spec((D_MODEL, d_in)),
                  _const_spec((8, D_CONV)),
                  _const_spec((1, D_CONV))],
        out_specs=[pl.BlockSpec((TM, D_SSM), lambda i: (i, 0)),
                   pl.BlockSpec((TM, D_CONV), lambda i: (i, 0))],
        out_shape=[jax.ShapeDtypeStruct((N_TOK, D_SSM), _F32),
                   jax.ShapeDtypeStruct((N_TOK, D_CONV), _F32)],
        compiler_params=_cparams("arbitrary"),
        name="in_proj",
    )(x2, mod3, norm_g, w_in_bf, conv_w, conv_b)


def _ctxproj_kernel(x_ref, mod_ref, g_ref, w_ref, u_ref):
    h = _modulated_norm(x_ref, mod_ref, g_ref, 0, 1).astype(_BF16)
    u_ref[...] = jnp.dot(h, w_ref[...], preferred_element_type=_F32)


def _ctxproj(ctx2, mod3, norm_g, w_u_bf):
    return pl.pallas_call(
        _ctxproj_kernel,
        grid=(N_CTX_TOK // TM,),
        in_specs=[pl.BlockSpec((TM, D_MODEL), lambda i: (i, 0)),
                  pl.BlockSpec((1, N_MOD, D_MODEL), lambda i: (BATCH, 0, 0)),
                  _const_spec((1, D_MODEL)),
                  _const_spec((D_MODEL, D_SSM))],
        out_specs=pl.BlockSpec((TM, D_SSM), lambda i: (i, 0)),
        out_shape=jax.ShapeDtypeStruct((N_CTX_TOK, D_SSM), _F32),
        compiler_params=_cparams("arbitrary"),
        name="ctx_proj",
    )(ctx2, mod3, norm_g, w_u_bf)


N_POW = 32


def _s5_param_kernel(prow_ref, ct_ref, bt_ref, toep_ref, bpow_ref, cpow_ref, dec_ref):
    lane = lambda shape: lax.broadcasted_iota(jnp.int32, shape, 1)
    sub = lambda shape: lax.broadcasted_iota(jnp.int32, shape, 0)

    def split2(a):
        a_hi = a.astype(_BF16)
        return a_hi, (a - a_hi.astype(_F32)).astype(_BF16)

    rep = (lane((SSM_GROUP, CHUNK_W)) % SSM_GROUP == sub((SSM_GROUP, CHUNK_W))).astype(_BF16)
    sgn_col = jnp.where(sub((STATE_W, 1)) < SSM_STATE, 1.0, -1.0).astype(_F32)
    sgn_row = jnp.where(lane((1, STATE_W)) < SSM_STATE, -1.0, 1.0).astype(_F32)
    tau_col = sub((N_POW, 1)).astype(_F32)
    blk_l = lane((N_POW, CHUNK_W)) // SSM_GROUP
    tau_s = sub((N_POW, CHUNK_W))
    blk_r = sub((CHUNK_W, N_POW)) // SSM_GROUP
    tau_l = lane((CHUNK_W, N_POW))
    pick_l = lambda e: (tau_s == e).astype(_BF16)
    pick_r = lambda e: (tau_l == e).astype(_BF16)
    tn = (((0,), (0,)), ((), ()))

    strips = []
    for d in range(2):
        pr = prow_ref[d, 0]
        lam_re, lam_im, dt = pr[0:1], pr[1:2], jnp.exp(pr[2:3])
        mag = jnp.exp(tau_col * (lam_re * dt))
        ang = tau_col * (lam_im * dt)
        pw_re = mag * jnp.cos(ang)
        pw_im = mag * jnp.sin(ang)
        pw4 = jnp.concatenate(split2(pw_re) + split2(pw_im), axis=1)

        c_hi, c_lo = split2(ct_ref[d, 0])
        ct = (jnp.dot(c_hi, rep, preferred_element_type=_F32)
              + jnp.dot(c_lo, rep, preferred_element_type=_F32))
        ca = ct * sgn_col
        cb = -pltpu.roll(ct, SSM_STATE, axis=0)

        def cpow(sel):
            o = lax.dot_general(pw4, sel, tn, preferred_element_type=_F32)
            x = o[0:STATE_W] + o[STATE_W:2 * STATE_W]
            y = o[2 * STATE_W:3 * STATE_W] + o[3 * STATE_W:4 * STATE_W]
            return x * ca + y * cb

        nr = pw_re[1:2] - 1.0
        ni = pw_im[1:2]
        den = lam_re * lam_re + lam_im * lam_im
        kr = (nr * lam_re + ni * lam_im) / den
        ki = (ni * lam_re - nr * lam_im) / den
        b1 = bt_ref[d, 0]
        b2 = pltpu.roll(b1, SSM_STATE, axis=1) * sgn_row
        u1 = kr * b1 + ki * b2
        u2 = kr * b2 - ki * b1
        u1t = jnp.concatenate([u1] * CHUNK, axis=0)
        u2t = jnp.concatenate([u2] * CHUNK, axis=0)

        def bpow(sel):
            o = jnp.dot(sel, pw4, preferred_element_type=_F32)
            x = o[:, 0:STATE_W] + o[:, STATE_W:2 * STATE_W]
            y = o[:, 2 * STATE_W:3 * STATE_W] + o[:, 3 * STATE_W:4 * STATE_W]
            return x * u1t + y * u2t

        if d == 0:
            e_strip, e_b, e_c = blk_l, (CHUNK - 1) - blk_r, blk_l + 1
        else:
            e_strip, e_b, e_c = (CHUNK - 1) - blk_l, blk_r, CHUNK - blk_l
        u_hi, u_lo = split2(u1)
        k_hi, k_lo = split2(cpow(pick_l(e_strip)))
        strips.append(jnp.dot(u_hi, k_hi, preferred_element_type=_F32)
                      + jnp.dot(u_hi, k_lo, preferred_element_type=_F32)
                      + jnp.dot(u_lo, k_hi, preferred_element_type=_F32))
        bpow_ref[0, :, d * STATE_W:(d + 1) * STATE_W] = bpow(pick_r(e_b)).astype(_BF16)
        cpow_ref[0, d * STATE_W:(d + 1) * STATE_W, :] = cpow(pick_l(e_c)).astype(_BF16)

        dec_ref[0, 2 * d:2 * d + 1, :] = pw_re[CHUNK:CHUNK + 1]
        dec_ref[0, 2 * d + 1:2 * d + 2, :] = pw_im[CHUNK:CHUNK + 1] * sgn_row

    zeros = jnp.zeros((SSM_GROUP, CHUNK_W), _F32)
    strip = (jnp.concatenate([strips[1], zeros], axis=1)
             + pltpu.roll(jnp.concatenate([strips[0], zeros], axis=1), CHUNK_W - SSM_GROUP, axis=1))
    for i in range(CHUNK):
        off = (CHUNK - 1 - i) * SSM_GROUP
        win = strip if off == 0 else pltpu.roll(strip, 2 * CHUNK_W - off, axis=1)
        toep_ref[0, i * SSM_GROUP:(i + 1) * SSM_GROUP, :] = win[:, 0:CHUNK_W].astype(_BF16)


def _s5_params(prow, ct2, bt1):
    g_spec = lambda shape: pl.BlockSpec(shape, lambda g: (0, g, 0, 0))
    o_spec = lambda shape: pl.BlockSpec(shape, lambda g: (g, 0, 0))
    return pl.pallas_call(
        _s5_param_kernel,
        grid=(SSM_GROUPS,),
        in_specs=[g_spec((2, 1, 8, STATE_W)),
                  g_spec((2, 1, STATE_W, SSM_GROUP)), g_spec((2, 1, SSM_GROUP, STATE_W))],
        out_specs=[o_spec((1, CHUNK_W, CHUNK_W)), o_spec((1, CHUNK_W, 2 * STATE_W)),
                   o_spec((1, 2 * STATE_W, CHUNK_W)), o_spec((1, 4, STATE_W))],
        out_shape=[jax.ShapeDtypeStruct((SSM_GROUPS, CHUNK_W, CHUNK_W), _BF16),
                   jax.ShapeDtypeStruct((SSM_GROUPS, CHUNK_W, 2 * STATE_W), _BF16),
                   jax.ShapeDtypeStruct((SSM_GROUPS, 2 * STATE_W, CHUNK_W), _BF16),
                   jax.ShapeDtypeStruct((SSM_GROUPS, 4, STATE_W), _F32)],
        compiler_params=_cparams("arbitrary"),
        name="s5_params",
    )(prow, ct2, bt1)


def _s5_kernel(ux_ref, uc_ref, toep_ref, bpow_ref, cpow_ref, dec_ref, y_ref, u_ref, s_ref, h_ref, y2_ref):
    ng = S5_GROUPS_PER_STEP
    gw = SSM_GROUP
    x0 = N_CTX_CHUNKS * BATCH

    half = CHUNK // 2
    blk = 32
    lane_blk = lax.broadcasted_iota(jnp.int32, (1, 128), 1) // gw

    def to_chunk_layout(src_ref, tok0, n_chunks, row0):
        for c0 in range(0, n_chunks, blk):
            nb = min(blk, n_chunks - c0)
            for hf in range(2):
                acc = [None] * ng
                for il in range(half):
                    p = src_ref[pl.ds(tok0 + c0 * CHUNK + hf * half + il, nb, stride=CHUNK), :]
                    for g in range(ng):
                        r = p if il == g else pltpu.roll(p, ((il - g) * gw) % 128, axis=1)
                        acc[g] = r if il == 0 else jnp.where(lane_blk == il, r, acc[g])
                for g in range(ng):
                    u_ref[g, hf, pl.ds(row0 + c0 * BATCH, nb, stride=BATCH), :] = acc[g]

    for b in range(BATCH):
        to_chunk_layout(uc_ref, b * CTX_LEN, N_CTX_CHUNKS, b)
        to_chunk_layout(ux_ref, b * SEQ, N_X_CHUNKS, x0 + b)

    chunk_rows = lambda ref, g, lo: jnp.concatenate([ref[g, 0, lo:, :], ref[g, 1, lo:, :]], axis=1)
    for g in range(ng):
        s_ref[g] = jnp.dot(chunk_rows(u_ref, g, 0).astype(_BF16), bpow_ref[g], preferred_element_type=_F32)

    pair = 2 * BATCH
    n_ctx_pairs = N_CTX_CHUNKS // 2
    n_pairs = N_SEQ_CHUNKS // 2
    lower = lax.broadcasted_iota(jnp.int32, (pair, STATE_W), 0) < BATCH
    zero = jnp.zeros((pair, STATE_W), _F32)

    def step(t, carry):
        tb = jnp.where(t < n_ctx_pairs, n_ctx_pairs - 1 - t, n_pairs + n_ctx_pairs - 1 - t)
        rf = pl.multiple_of(t * pair, pair)
        rb = pl.multiple_of(tb * pair, pair)
        out = []
        for g in range(ng):
            hf, hb = carry[2 * g], carry[2 * g + 1]
            dg = dec_ref[g]
            mul_f = lambda v: dg[0:1] * v + dg[1:2] * pltpu.roll(v, SSM_STATE, axis=1)
            mul_b = lambda v: dg[2:3] * v + dg[3:4] * pltpu.roll(v, SSM_STATE, axis=1)

            sf = s_ref[g, pl.ds(rf, pair), 0:STATE_W]
            mid = mul_f(hf) + jnp.where(lower, sf, pltpu.roll(sf, BATCH, axis=0))
            h_ref[g, pl.ds(rf, pair), 0:STATE_W] = jnp.where(lower, hf, mid)
            end = mul_f(mid) + sf
            hf = jnp.where(lower, pltpu.roll(end, BATCH, axis=0), end)

            sb = s_ref[g, pl.ds(rb, pair), STATE_W:2 * STATE_W]
            mid = mul_b(hb) + jnp.where(lower, pltpu.roll(sb, BATCH, axis=0), sb)
            h_ref[g, pl.ds(rb, pair), STATE_W:2 * STATE_W] = jnp.where(lower, mid, hb)
            end = mul_b(mid) + sb
            hb = jnp.where(lower, end, pltpu.roll(end, BATCH, axis=0))
            out += [hf, hb]
        return tuple(out)

    lax.fori_loop(0, n_pairs, step, (zero,) * (2 * ng))

    for g in range(ng):
        ux = chunk_rows(u_ref, g, x0).astype(_BF16)
        hx = h_ref[g, x0:SCAN_ROWS, :].astype(_BF16)
        y2 = (jnp.dot(ux, toep_ref[g], preferred_element_type=_F32)
              + jnp.dot(hx, cpow_ref[g], preferred_element_type=_F32))
        y2_ref[g, 0] = y2[:, 0:128]
        y2_ref[g, 1] = y2[:, 128:256]

    for b in range(BATCH):
        for c0 in range(0, N_X_CHUNKS, blk):
            for hf in range(2):
                pieces = [y2_ref[g, hf, pl.ds(b + c0 * BATCH, blk, stride=BATCH), :] for g in range(ng)]
                for jl in range(half):
                    acc = None
                    for g in range(ng):
                        r = pieces[g] if jl == g else pltpu.roll(pieces[g], ((g - jl) * gw) % 128, axis=1)
                        acc = r if g == 0 else jnp.where(lane_blk == g, r, acc)
                    y_ref[pl.ds(b * SEQ + c0 * CHUNK + hf * half + jl, blk, stride=CHUNK), :] = acc


def _s5_scan(u_x, u_c, toep, bpow, cpow, dec):
    ng = S5_GROUPS_PER_STEP
    n_x_rows = N_X_CHUNKS * BATCH
    lanes = ng * SSM_GROUP
    spec = lambda r, c: pl.BlockSpec((ng, r, c), lambda i: (i, 0, 0))
    col = lambda rows: pl.BlockSpec((rows, lanes), lambda i: (0, i))
    return pl.pallas_call(
        _s5_kernel,
        grid=(SSM_GROUPS // ng,),
        in_specs=[col(N_TOK), col(N_CTX_TOK), spec(CHUNK_W, CHUNK_W), spec(CHUNK_W, 2 * STATE_W),
                  spec(2 * STATE_W, CHUNK_W), spec(4, STATE_W)],
        out_specs=col(N_TOK),
        out_shape=jax.ShapeDtypeStruct((N_TOK, D_SSM), _F32),
        scratch_shapes=[pltpu.VMEM((ng, 2, SCAN_ROWS, 128), _F32),
                        pltpu.VMEM((ng, SCAN_ROWS, 2 * STATE_W), _F32),
                        pltpu.VMEM((ng, SCAN_ROWS, 2 * STATE_W), _F32),
                        pltpu.VMEM((ng, 2, n_x_rows, 128), _F32)],
        compiler_params=_cparams("arbitrary"),
        name="s5_scan",
    )(u_x, u_c, toep, bpow, cpow, dec)


def _merge_kernel(ys_ref, u_ref, conv_ref, x_ref, mod_ref, dskip_ref, mixg_ref, n2g_ref,
                  wglu_ref, wout_ref, rwt_ref, x1_ref, h2_ref, logit_ref):
    m = mod_ref[0]
    yx = dskip_ref[...] * u_ref[...] + ys_ref[...]
    c0 = math.sqrt(2.0 / math.pi)
    ge = 0.5 * yx * (1.0 + jnp.tanh(c0 * (yx + 0.044715 * (yx * yx * yx))))
    z = jnp.dot(ge.astype(_BF16), wglu_ref[...], preferred_element_type=_F32)
    ssm_y = z[:, 0:D_SSM] * jax.nn.sigmoid(z[:, D_SSM:2 * D_SSM])
    conv_y = conv_ref[...]
    mixg = mixg_ref[...]
    heads_a = (ssm_y * _rms_scale(ssm_y) * mixg[:, 0:D_SSM]).astype(_BF16)
    heads_b = (conv_y * _rms_scale(conv_y) * mixg[:, D_SSM:]).astype(_BF16)
    mix = (jnp.dot(heads_a, wout_ref[0:D_SSM, :], preferred_element_type=_F32)
           + jnp.dot(heads_b, wout_ref[D_SSM:, :], preferred_element_type=_F32))
    x1 = x_ref[...] + m[2:3] * mix
    x1_ref[...] = x1
    h2 = x1 * _rms_scale(x1) * n2g_ref[...] * (1.0 + m[4:5]) + m[3:4]
    h2_ref[...] = h2
    h_hi = h2.astype(_BF16)
    h_lo = (h2 - h_hi.astype(_F32)).astype(_BF16)
    nt = (((1,), (1,)), ((), ()))
    p = lax.dot_general(rwt_ref[...], h_hi, nt, preferred_element_type=_F32)
    q = lax.dot_general(rwt_ref[0:N_EXPERTS, :], h_lo, nt, preferred_element_type=_F32)
    logit_ref[...] = p[0:N_EXPERTS] + p[N_EXPERTS:2 * N_EXPERTS] + q


def _merge(ys, u, conv, x2, mod3, dskip, mixg, n2g, wglu_bf, wout_bf, rwt):
    tiles_per_batch = SEQ // TM
    tok = lambda w: pl.BlockSpec((TM, w), lambda i: (i, 0))
    return pl.pallas_call(
        _merge_kernel,
        grid=(N_TOK // TM,),
        in_specs=[tok(D_SSM), tok(D_SSM), tok(D_CONV), tok(D_MODEL),
                  pl.BlockSpec((1, N_MOD, D_MODEL), lambda i: (i // tiles_per_batch, 0, 0)),
                  _const_spec((1, D_SSM)), _const_spec((1, D_MODEL)), _const_spec((1, D_MODEL)),
                  _const_spec((D_SSM, 2 * D_SSM)), _const_spec((D_MODEL, D_MODEL)),
                  _const_spec((2 * N_EXPERTS, D_MODEL))],
        out_specs=[tok(D_MODEL), tok(D_MODEL),
                   pl.BlockSpec((N_EXPERTS, TM), lambda i: (0, i))],
        out_shape=[jax.ShapeDtypeStruct((N_TOK, D_MODEL), _F32),
                   jax.ShapeDtypeStruct((N_TOK, D_MODEL), _F32),
                   jax.ShapeDtypeStruct((N_EXPERTS, N_TOK), _F32)],
        compiler_params=_cparams("arbitrary"),
        name="merge_heads",
    )(ys, u, conv, x2, mod3, dskip, mixg, n2g, wglu_bf, wout_bf, rwt)


def _route_kernel(logit_ref, bias_ref, eid_ref, w_ref, rank_ref, cnt_ref, carry_ref):
    @pl.when(pl.program_id(0) == 0)
    def _():
        carry_ref[...] = jnp.zeros_like(carry_ref)

    tm = logit_ref.shape[1]
    neg = jnp.float32(-jnp.inf)
    scores = jax.nn.sigmoid(logit_ref[...])
    biased = scores + bias_ref[:, 0:1]
    sub = lax.broadcasted_iota(jnp.int32, (GROUP_SIZE, tm), 0)
    rows = lambda a, g: a[g * GROUP_SIZE:(g + 1) * GROUP_SIZE]
    ngrp = N_EXPERT_GROUPS

    gscore = []
    for g in range(ngrp):
        bg = rows(biased, g)
        m1 = jnp.max(bg, axis=0, keepdims=True)
        first = jnp.min(jnp.where(bg == m1, sub, GROUP_SIZE), axis=0, keepdims=True)
        m2 = jnp.max(jnp.where(sub == first, neg, bg), axis=0, keepdims=True)
        gscore.append(m1 + m2)
    v = []
    for g in range(ngrp):
        beaten = jnp.zeros((1, tm), jnp.int32)
        for o in range(ngrp):
            if o != g:
                beats = (gscore[o] >= gscore[g]) if o < g else (gscore[o] > gscore[g])
                beaten = beaten + beats.astype(jnp.int32)
        v.append(jnp.where(beaten < TOPK_GROUPS, rows(biased, g), neg))
    eids = [sub + g * GROUP_SIZE for g in range(ngrp)]

    pick_ids, pick_masks = [], []
    for _ in range(TOP_K):
        m = functools.reduce(jnp.maximum, v)
        m = jnp.max(m, axis=0, keepdims=True)
        cand = functools.reduce(jnp.minimum, [jnp.where(v[g] == m, eids[g], N_EXPERTS) for g in range(ngrp)])
        pick_id = jnp.min(cand, axis=0, keepdims=True)
        masks = [eids[g] == pick_id for g in range(ngrp)]
        v = [jnp.where(masks[g], neg, v[g]) for g in range(ngrp)]
        pick_ids.append(pick_id)
        pick_masks.append(masks)

    sel = jnp.concatenate(
        [functools.reduce(jnp.logical_or, [pick_masks[k][g] for k in range(TOP_K)]).astype(_F32)
         for g in range(ngrp)], axis=0)
    before = (lax.broadcasted_iota(jnp.int32, (tm, tm), 0)
              < lax.broadcasted_iota(jnp.int32, (tm, tm), 1)).astype(_BF16)
    base = jnp.dot(sel.astype(_BF16), before, preferred_element_type=_F32) + carry_ref[:, 0:1]

    def gather_pick(a, k):
        parts = [jnp.sum(jnp.where(pick_masks[k][g], rows(a, g), 0.0), axis=0, keepdims=True)
                 for g in range(ngrp)]
        return functools.reduce(jnp.add, parts)

    picked = [gather_pick(scores, k) for k in range(TOP_K)]
    denom = functools.reduce(jnp.add, picked)
    for k in range(TOP_K):
        eid_ref[k:k + 1, :] = pick_ids[k]
        w_ref[k:k + 1, :] = picked[k] / denom * ROUTED_SCALE
        rank_ref[k:k + 1, :] = gather_pick(base, k).astype(jnp.int32)
    carry_ref[...] = carry_ref[...] + jnp.sum(sel, axis=1, keepdims=True)
    cnt_ref[...] = carry_ref[...]


def _route(logits_t, bias_col):
    tok = pl.BlockSpec((TOP_K, TM), lambda i: (0, i))
    return pl.pallas_call(
        _route_kernel,
        grid=(N_TOK // TM,),
        in_specs=[pl.BlockSpec((N_EXPERTS, TM), lambda i: (0, i)), _const_spec((N_EXPERTS, 128))],
        out_specs=[tok, tok, tok, pl.BlockSpec((N_EXPERTS, 128), lambda i: (0, 0))],
        out_shape=[jax.ShapeDtypeStruct((TOP_K, N_TOK), jnp.int32),
                   jax.ShapeDtypeStruct((TOP_K, N_TOK), _F32),
                   jax.ShapeDtypeStruct((TOP_K, N_TOK), jnp.int32),
                   jax.ShapeDtypeStruct((N_EXPERTS, 128), _F32)],
        scratch_shapes=[pltpu.VMEM((N_EXPERTS, 128), _F32)],
        compiler_params=_cparams("arbitrary"),
        name="route",
    )(logits_t, bias_col)


SUBLANES = 8


def _pow2_pieces():
    bit = EXPERT_BM // 2
    while bit >= SUBLANES:
        yield bit
        bit //= 2


def _dispatch_kernel(dest_ref, padstart_ref, padn_ref, h_ref, xs_ref, zero_ref, sem, zsem):
    i = pl.program_id(0)
    tm = h_ref.shape[0]
    base = i * tm

    def issue(t, c):
        src = h_ref.at[pl.ds(t, 1), :]
        for k in range(TOP_K):
            pltpu.make_async_copy(src, xs_ref.at[pl.ds(dest_ref[k, base + t], 1), :], sem).start()
        return c

    lax.fori_loop(0, tm, issue, 0)

    @pl.when(i == 0)
    def _():
        zero_ref[...] = jnp.zeros_like(zero_ref)

        def pieces(e, act):
            n = padn_ref[e]
            start = padstart_ref[e]
            head = jnp.minimum(n, (SUBLANES - start % SUBLANES) % SUBLANES)
            for r in range(SUBLANES - 1):
                cp = pltpu.make_async_copy(zero_ref.at[pl.ds(0, 1), :], xs_ref.at[pl.ds(start + r, 1), :], zsem)
                pl.when(r < head)(functools.partial(act, cp))
            start = start + head
            n = n - head
            for bit in _pow2_pieces():
                cp = pltpu.make_async_copy(zero_ref.at[pl.ds(0, bit), :],
                                           xs_ref.at[pl.ds(pl.multiple_of(start, SUBLANES), bit), :], zsem)
                pl.when((n & bit) != 0)(functools.partial(act, cp))
                start = start + (n & bit)

        def zstart(e, c):
            pieces(e, lambda cp: cp.start())
            return c

        def zwait(e, c):
            pieces(e, lambda cp: cp.wait())
            return c

        lax.fori_loop(0, N_EXPERTS, zstart, 0)
        lax.fori_loop(0, N_EXPERTS, zwait, 0)

    for _ in range(TOP_K):
        pltpu.make_async_copy(h_ref, xs_ref.at[pl.ds(0, tm), :], sem).wait()


def _dispatch(dest, pad_start, pad_n, h2):
    tm = TM
    grid_spec = pltpu.PrefetchScalarGridSpec(
        num_scalar_prefetch=3,
        grid=(N_TOK // tm,),
        in_specs=[pl.BlockSpec((tm, D_MODEL), lambda i, *_: (i, 0))],
        out_specs=pl.BlockSpec(memory_space=pl.ANY),
        scratch_shapes=[pltpu.VMEM((EXPERT_BM // 2, D_MODEL), _F32),
                        pltpu.SemaphoreType.DMA(()), pltpu.SemaphoreType.DMA(())],
    )
    return pl.pallas_call(
        _dispatch_kernel,
        grid_spec=grid_spec,
        out_shape=jax.ShapeDtypeStruct((N_ROWS, D_MODEL), _F32),
        compiler_params=_cparams("arbitrary"),
        name="dispatch",
    )(dest, pad_start, pad_n, h2)


def _experts_kernel(be_ref, nact_ref, xs_ref, wg_ref, wu_ref, wd_ref, ys_ref, wg_bf, wu_bf, wd_bf):
    i = pl.program_id(0)

    @pl.when(i < nact_ref[0])
    def _():
        @pl.when((i == 0) | (be_ref[i] != be_ref[jnp.maximum(i - 1, 0)]))
        def _():
            wg_bf[...] = wg_ref[0].astype(_BF16)
            wu_bf[...] = wu_ref[0].astype(_BF16)
            wd_bf[...] = wd_ref[0].astype(_BF16)

        x = xs_ref[...].astype(_BF16)
        a = jnp.dot(x, wg_bf[...], preferred_element_type=_F32)
        b = jnp.dot(x, wu_bf[...], preferred_element_type=_F32)
        ys_ref[...] = jnp.dot((_silu(a) * b).astype(_BF16), wd_bf[...], preferred_element_type=_F32)


def _experts(block_e, n_active, xs, w_gate, w_up, w_down):
    bm = EXPERT_BM
    row_blk = lambda i, be, na: (jnp.minimum(i, na[0] - 1), 0)
    grid_spec = pltpu.PrefetchScalarGridSpec(
        num_scalar_prefetch=2,
        grid=(N_BLOCKS,),
        in_specs=[pl.BlockSpec((bm, D_MODEL), row_blk),
                  pl.BlockSpec((1, D_MODEL, D_EXPERT), lambda i, be, na: (be[i], 0, 0)),
                  pl.BlockSpec((1, D_MODEL, D_EXPERT), lambda i, be, na: (be[i], 0, 0)),
                  pl.BlockSpec((1, D_EXPERT, D_MODEL), lambda i, be, na: (be[i], 0, 0))],
        out_specs=pl.BlockSpec((bm, D_MODEL), row_blk),
        scratch_shapes=[pltpu.VMEM((D_MODEL, D_EXPERT), _BF16), pltpu.VMEM((D_MODEL, D_EXPERT), _BF16),
                        pltpu.VMEM((D_EXPERT, D_MODEL), _BF16)],
    )
    return pl.pallas_call(
        _experts_kernel,
        grid_spec=grid_spec,
        out_shape=jax.ShapeDtypeStruct((N_ROWS, D_MODEL), _F32),
        compiler_params=_cparams("arbitrary"),
        name="experts",
    )(block_e, n_active, xs, w_gate, w_up, w_down)


def _final_kernel(dest_ref, x1_ref, h_ref, wt_ref, mod_ref, wsg_ref, wsu_ref, wsd_ref, fg_ref, ys_ref,
                  o_ref, buf_ref, sem):
    i = pl.program_id(0)
    tm = x1_ref.shape[0]
    base = i * tm

    def issue(t, c):
        for k in range(TOP_K):
            pltpu.make_async_copy(ys_ref.at[pl.ds(dest_ref[k, base + t], 1), :],
                                  buf_ref.at[pl.ds(k * tm + t, 1), :], sem).start()
        return c

    lax.fori_loop(0, tm, issue, 0)

    m = mod_ref[0]
    h = h_ref[...].astype(_BF16)
    a = jnp.dot(h, wsg_ref[...], preferred_element_type=_F32)
    b = jnp.dot(h, wsu_ref[...], preferred_element_type=_F32)
    shared = jnp.dot((_silu(a) * b).astype(_BF16), wsd_ref[...], preferred_element_type=_F32)

    for k in range(TOP_K):
        pltpu.make_async_copy(ys_ref.at[pl.ds(0, tm), :], buf_ref.at[pl.ds(k * tm, tm), :], sem).wait()

    wt = wt_ref[...]
    routed = wt[:, 0:1] * buf_ref[0:tm, :]
    for k in range(1, TOP_K):
        routed = routed + wt[:, k:k + 1] * buf_ref[k * tm:(k + 1) * tm, :]
    y = x1_ref[...] + m[5:6] * (routed + shared)
    o_ref[...] = y * _rms_scale(y) * fg_ref[...]


def _final(dest, x1, h2, w_tok, mod3, wsg_bf, wsu_bf, wsd_bf, final_g, ys):
    tm = TM_COMBINE
    tiles_per_batch = SEQ // tm
    tok = lambda w: pl.BlockSpec((tm, w), lambda i, *_: (i, 0))
    const = lambda shape: pl.BlockSpec(shape, lambda i, *_: (0,) * len(shape), pipeline_mode=pl.Buffered(1))
    grid_spec = pltpu.PrefetchScalarGridSpec(
        num_scalar_prefetch=1,
        grid=(N_TOK // tm,),
        in_specs=[tok(D_MODEL), tok(D_MODEL), tok(TOP_K),
                  pl.BlockSpec((1, N_MOD, D_MODEL), lambda i, *_: (i // tiles_per_batch, 0, 0)),
                  const((D_MODEL, D_SHARED)), const((D_MODEL, D_SHARED)), const((D_SHARED, D_MODEL)),
                  const((1, D_MODEL)),
                  pl.BlockSpec(memory_space=pl.ANY)],
        out_specs=tok(D_MODEL),
        scratch_shapes=[pltpu.VMEM((TOP_K * tm, D_MODEL), _F32), pltpu.SemaphoreType.DMA(())],
    )
    return pl.pallas_call(
        _final_kernel,
        grid_spec=grid_spec,
        out_shape=jax.ShapeDtypeStruct((N_TOK, D_MODEL), _F32),
        compiler_params=_cparams("arbitrary"),
        name="final",
    )(dest, x1, h2, w_tok, mod3, wsg_bf, wsu_bf, wsd_bf, final_g, ys)


def _s5_param_layouts(lam_re, lam_im, b_re, b_im, c_re, c_im, log_dt):
    two = lambda a: jnp.concatenate([a, a], axis=-1)
    lr, li = two(lam_re), two(lam_im)
    dtb = jnp.broadcast_to(log_dt.astype(_F32)[:, :, None], lr.shape)
    zeros = jnp.zeros_like(lr)
    prow = jnp.stack([lr, li, dtb] + [zeros] * 5, axis=2)
    ct2 = jnp.concatenate([jnp.swapaxes(c_re, -1, -2), jnp.swapaxes(c_im, -1, -2)], axis=2)
    bt1 = jnp.concatenate([jnp.swapaxes(b_re, -1, -2), jnp.swapaxes(b_im, -1, -2)], axis=3)
    return prow, ct2, bt1


def kernel(x, c, ctx, c_ctx, norm1_g, norm2_g, w_ada, b_ada, w_in, ssm_lam_re, ssm_lam_im, ssm_b_re, ssm_b_im, ssm_c_re, ssm_c_im, ssm_log_dt, ssm_d, ssm_w_glu, conv_w, conv_b, mix_norm_g, w_out, router_w, router_bias, exp_w_gate, exp_w_up, exp_w_down, shared_w_gate, shared_w_up, shared_w_down, final_g):
    layer = 0
    x2 = x.reshape(N_TOK, D_MODEL)
    ctx2 = ctx.reshape(N_CTX_TOK, D_MODEL)

    c8 = jnp.concatenate([c, c_ctx[None, :], jnp.zeros((8 - BATCH - 1, D_MODEL), _F32)], axis=0)
    mod = _ada_mod(c8, w_ada[layer], b_ada[layer][None, :])
    mod3 = mod.reshape(8, N_MOD, D_MODEL)

    w_in_bf = w_in[layer].astype(_BF16)
    conv_w8 = jnp.concatenate([conv_w[layer], jnp.zeros((8 - conv_w.shape[1], D_CONV), _F32)], axis=0)
    u_x, conv_x = _inproj(x2, mod3, norm1_g[layer][None, :], w_in_bf, conv_w8, conv_b[layer][None, :])
    u_c = _ctxproj(ctx2, mod3, norm1_g[layer][None, :], w_in_bf[:, :D_SSM])

    prow, ct2, bt1 = _s5_param_layouts(ssm_lam_re[layer], ssm_lam_im[layer], ssm_b_re[layer],
                                       ssm_b_im[layer], ssm_c_re[layer], ssm_c_im[layer], ssm_log_dt[layer])
    toep, bpow, cpow, dec = _s5_params(prow, ct2, bt1)
    ys = _s5_scan(u_x, u_c, toep, bpow, cpow, dec)

    rw = router_w[layer]
    rw_hi = rw.astype(_BF16)
    rw_lo = (rw - rw_hi.astype(_F32)).astype(_BF16)
    rwt = jnp.concatenate([rw_hi.T, rw_lo.T], axis=0)
    x1, h2, logits_t = _merge(ys, u_x, conv_x, x2, mod3, ssm_d[layer][None, :],
                              mix_norm_g[layer][None, :], norm2_g[layer][None, :],
                              ssm_w_glu[layer].astype(_BF16), w_out[layer].astype(_BF16), rwt)

    bias_col = jnp.broadcast_to(router_bias[layer][:, None], (N_EXPERTS, 128))
    eid, w_k, rank, cnt = _route(logits_t, bias_col)

    counts = cnt[:, 0].astype(jnp.int32)
    padded = (counts + EXPERT_BM - 1) // EXPERT_BM * EXPERT_BM
    pend = jnp.cumsum(padded)
    pstart = pend - padded
    is_e = eid[:, :, None] == jnp.arange(N_EXPERTS, dtype=jnp.int32)
    dest = jnp.sum(jnp.where(is_e, pstart, 0), axis=-1) + rank
    n_active = (pend[-1] // EXPERT_BM).astype(jnp.int32)
    blk = jnp.minimum(jnp.arange(N_BLOCKS, dtype=jnp.int32), n_active - 1)
    ends_before = (pend[None, :] <= (blk * EXPERT_BM)[:, None]).astype(jnp.int32)
    block_e = jnp.minimum(jnp.sum(ends_before, axis=1), N_EXPERTS - 1)

    xs = _dispatch(dest, pstart + counts, padded - counts, h2)
    ys_rows = _experts(block_e, n_active[None], xs, exp_w_gate[layer], exp_w_up[layer], exp_w_down[layer])

    out = _final(dest, x1, h2, w_k.T, mod3, shared_w_gate[layer].astype(_BF16),
                 shared_w_up[layer].astype(_BF16), shared_w_down[layer].astype(_BF16), final_g[None, :], ys_rows)
    return out.reshape(BATCH, SEQ, D_MODEL)
```

```python
import functools
import math

import jax
import jax.numpy as jnp
from jax import lax
from jax.experimental import pallas as pl
from jax.experimental.pallas import tpu as pltpu

D_MODEL = 2048
BATCH = 4
SEQ = 2048
CTX_LEN = 256
GRID_W = 64
D_SSM = 1024
D_CONV = 1024
SSM_GROUP = 16
SSM_GROUPS = 64
SSM_STATE = 64
N_EXPERTS = 64
N_EXPERT_GROUPS = 8
GROUP_SIZE = N_EXPERTS // N_EXPERT_GROUPS
TOPK_GROUPS = 4
TOP_K = 8
D_EXPERT = 512
D_SHARED = 512
ROUTED_SCALE = 2.5
N_MOD = 6
EPS = 1e-6

N_TOK = BATCH * SEQ
N_CTX_TOK = BATCH * CTX_LEN

CHUNK = 16
CHUNK_W = CHUNK * SSM_GROUP
STATE_W = 2 * SSM_STATE
N_CTX_CHUNKS = CTX_LEN // CHUNK
N_X_CHUNKS = SEQ // CHUNK
N_SEQ_CHUNKS = N_CTX_CHUNKS + N_X_CHUNKS
SCAN_ROWS = N_SEQ_CHUNKS * BATCH
S5_GROUPS_PER_STEP = 8

TM = 256
EXPERT_BM = 256
N_ASSIGN = N_TOK * TOP_K
N_BLOCKS = N_ASSIGN // EXPERT_BM + N_EXPERTS
N_ROWS = N_BLOCKS * EXPERT_BM
TM_COMBINE = 128
VMEM_LIMIT = 56 * 1024 * 1024

_F32 = jnp.float32
_BF16 = jnp.bfloat16


def _cparams(*sem):
    return pltpu.CompilerParams(dimension_semantics=sem, vmem_limit_bytes=VMEM_LIMIT)


def _const_spec(shape):
    nd = len(shape)
    return pl.BlockSpec(shape, lambda *_: (0,) * nd, pipeline_mode=pl.Buffered(1))


def _rms_scale(xf):
    return lax.rsqrt(jnp.mean(xf * xf, axis=-1, keepdims=True) + EPS)


def _silu(x):
    return x * jax.nn.sigmoid(x)


def _ada_kernel(c_ref, w_ref, b_ref, o_ref):
    s = _silu(c_ref[...])
    o_ref[...] = jnp.dot(s, w_ref[...], preferred_element_type=_F32) + b_ref[...]


def _ada_mod(c8, w_ada, b_ada):
    n = w_ada.shape[1]
    tn = 1024
    return pl.pallas_call(
        _ada_kernel,
        grid=(n // tn,),
        in_specs=[pl.BlockSpec((8, D_MODEL), lambda j: (0, 0)),
                  pl.BlockSpec((D_MODEL, tn), lambda j: (0, j)),
                  pl.BlockSpec((1, tn), lambda j: (0, j))],
        out_specs=pl.BlockSpec((8, tn), lambda j: (0, j)),
        out_shape=jax.ShapeDtypeStruct((8, n), _F32),
        compiler_params=_cparams("arbitrary"),
        name="ada_mod",
    )(c8, w_ada, b_ada)


def _modulated_norm(x_ref, mod_ref, g_ref, shift_row, scale_row):
    xf = x_ref[...]
    m = mod_ref[0]
    h = xf * _rms_scale(xf) * g_ref[...]
    return h * (1.0 + m[scale_row:scale_row + 1]) + m[shift_row:shift_row + 1]


def _inproj_kernel(x_ref, mod_ref, g_ref, w_ref, cw_ref, cb_ref, u_ref, conv_ref):
    h = _modulated_norm(x_ref, mod_ref, g_ref, 0, 1).astype(_BF16)
    u_ref[...] = jnp.dot(h, w_ref[:, 0:D_SSM], preferred_element_type=_F32)
    tm = x_ref.shape[0]
    pos = lax.broadcasted_iota(jnp.int32, (tm, 1), 0) % GRID_W
    not_first = (pos != 0).astype(_F32)
    not_last = (pos != GRID_W - 1).astype(_F32)
    cw = cw_ref[...]
    nc = 256
    for j in range(D_CONV // nc):
        lo = j * nc
        bg = jnp.dot(h, w_ref[:, D_SSM + lo:D_SSM + lo + nc], preferred_element_type=_F32)
        cg = jnp.dot(h, w_ref[:, D_SSM + D_CONV + lo:D_SSM + D_CONV + lo + nc], preferred_element_type=_F32)
        v = jnp.dot(h, w_ref[:, D_SSM + 2 * D_CONV + lo:D_SSM + 2 * D_CONV + lo + nc],
                    preferred_element_type=_F32)
        z = cg * v
        z_prev = pltpu.roll(z, 1, axis=0) * not_first
        z_next = pltpu.roll(z, tm - 1, axis=0) * not_last
        y = (cb_ref[:, lo:lo + nc] + z_prev * cw[0:1, lo:lo + nc] + z * cw[1:2, lo:lo + nc]
             + z_next * cw[2:3, lo:lo + nc])
        conv_ref[:, lo:lo + nc] = bg * y


def _inproj(x2, mod3, norm_g, w_in_bf, conv_w, conv_b):
    d_in = w_in_bf.shape[1]
    tiles_per_batch = SEQ // TM
    return pl.pallas_call(
        _inproj_kernel,
        grid=(N_TOK // TM,),
        in_specs=[pl.BlockSpec((TM, D_MODEL), lambda i: (i, 0)),
                  pl.BlockSpec((1, N_MOD, D_MODEL), lambda i: (i // tiles_per_batch, 0, 0)),
                  _const_spec((1, D_MODEL)),
                  _const_spec((D_MODEL, d_in)),
                  _const_spec((8, D_CONV)),
                  _const_spec((1, D_CONV))],
        out_specs=[pl.BlockSpec((TM, D_SSM), lambda i: (i, 0)),
                   pl.BlockSpec((TM, D_CONV), lambda i: (i, 0))],
        out_shape=[jax.ShapeDtypeStruct((N_TOK, D_SSM), _F32),
                   jax.ShapeDtypeStruct((N_TOK, D_CONV), _F32)],
        compiler_params=_cparams("arbitrary"),
        name="in_proj",
    )(x2, mod3, norm_g, w_in_bf, conv_w, conv_b)


def _ctxproj_kernel(x_ref, mod_ref, g_ref, w_ref, u_ref):
    h = _modulated_norm(x_ref, mod_ref, g_ref, 0, 1).astype(_BF16)
    u_ref[...] = jnp.dot(h, w_ref[...], preferred_element_type=_F32)


def _ctxproj(ctx2, mod3, norm_g, w_u_bf):
    return pl.pallas_call(
        _ctxproj_kernel,
        grid=(N_CTX_TOK // TM,),
        in_specs=[pl.BlockSpec((TM, D_MODEL), lambda i: (i, 0)),
                  pl.BlockSpec((1, N_MOD, D_MODEL), lambda i: (BATCH, 0, 0)),
                  _const_spec((1, D_MODEL)),
                  _const_spec((D_MODEL, D_SSM))],
        out_specs=pl.BlockSpec((TM, D_SSM), lambda i: (i, 0)),
        out_shape=jax.ShapeDtypeStruct((N_CTX_TOK, D_SSM), _F32),
        compiler_params=_cparams("arbitrary"),
        name="ctx_proj",
    )(ctx2, mod3, norm_g, w_u_bf)


N_POW = 32


def _s5_param_kernel(prow_ref, ct_ref, bt_ref, toep_ref, bpow_ref, cpow_ref, dec_ref):
    lane = lambda shape: lax.broadcasted_iota(jnp.int32, shape, 1)
    sub = lambda shape: lax.broadcasted_iota(jnp.int32, shape, 0)

    def split2(a):
        a_hi = a.astype(_BF16)
        return a_hi, (a - a_hi.astype(_F32)).astype(_BF16)

    rep = (lane((SSM_GROUP, CHUNK_W)) % SSM_GROUP == sub((SSM_GROUP, CHUNK_W))).astype(_BF16)
    sgn_col = jnp.where(sub((STATE_W, 1)) < SSM_STATE, 1.0, -1.0).astype(_F32)
    sgn_row = jnp.where(lane((1, STATE_W)) < SSM_STATE, -1.0, 1.0).astype(_F32)
    tau_col = sub((N_POW, 1)).astype(_F32)
    blk_l = lane((N_POW, CHUNK_W)) // SSM_GROUP
    tau_s = sub((N_POW, CHUNK_W))
    blk_r = sub((CHUNK_W, N_POW)) // SSM_GROUP
    tau_l = lane((CHUNK_W, N_POW))
    pick_l = lambda e: (tau_s == e).astype(_BF16)
    pick_r = lambda e: (tau_l == e).astype(_BF16)
    tn = (((0,), (0,)), ((), ()))

    strips = []
    for d in range(2):
        pr = prow_ref[d, 0]
        lam_re, lam_im, dt = pr[0:1], pr[1:2], jnp.exp(pr[2:3])
        mag = jnp.exp(tau_col * (lam_re * dt))
        ang = tau_col * (lam_im * dt)
        pw_re = mag * jnp.cos(ang)
        pw_im = mag * jnp.sin(ang)
        pw4 = jnp.concatenate(split2(pw_re) + split2(pw_im), axis=1)

        c_hi, c_lo = split2(ct_ref[d, 0])
        ct = (jnp.dot(c_hi, rep, preferred_element_type=_F32)
              + jnp.dot(c_lo, rep, preferred_element_type=_F32))
        ca = ct * sgn_col
        cb = -pltpu.roll(ct, SSM_STATE, axis=0)

        def cpow(sel):
            o = lax.dot_general(pw4, sel, tn, preferred_element_type=_F32)
            x = o[0:STATE_W] + o[STATE_W:2 * STATE_W]
            y = o[2 * STATE_W:3 * STATE_W] + o[3 * STATE_W:4 * STATE_W]
            return x * ca + y * cb

        nr = pw_re[1:2] - 1.0
        ni = pw_im[1:2]
        den = lam_re * lam_re + lam_im * lam_im
        kr = (nr * lam_re + ni * lam_im) / den
        ki = (ni * lam_re - nr * lam_im) / den
        b1 = bt_ref[d, 0]
        b2 = pltpu.roll(b1, SSM_STATE, axis=1) * sgn_row
        u1 = kr * b1 + ki * b2
        u2 = kr * b2 - ki * b1
        u1t = jnp.concatenate([u1] * CHUNK, axis=0)
        u2t = jnp.concatenate([u2] * CHUNK, axis=0)

        def bpow(sel):
            o = jnp.dot(sel, pw4, preferred_element_type=_F32)
            x = o[:, 0:STATE_W] + o[:, STATE_W:2 * STATE_W]
            y = o[:, 2 * STATE_W:3 * STATE_W] + o[:, 3 * STATE_W:4 * STATE_W]
            return x * u1t + y * u2t

        if d == 0:
            e_strip, e_b, e_c = blk_l, (CHUNK - 1) - blk_r, blk_l + 1
        else:
            e_strip, e_b, e_c = (CHUNK - 1) - blk_l, blk_r, CHUNK - blk_l
        u_hi, u_lo = split2(u1)
        k_hi, k_lo = split2(cpow(pick_l(e_strip)))
        strips.append(jnp.dot(u_hi, k_hi, preferred_element_type=_F32)
                      + jnp.dot(u_hi, k_lo, preferred_element_type=_F32)
                      + jnp.dot(u_lo, k_hi, preferred_element_type=_F32))
        bpow_ref[0, :, d * STATE_W:(d + 1) * STATE_W] = bpow(pick_r(e_b)).astype(_BF16)
        cpow_ref[0, d * STATE_W:(d + 1) * STATE_W, :] = cpow(pick_l(e_c)).astype(_BF16)

        dec_ref[0, 2 * d:2 * d + 1, :] = pw_re[CHUNK:CHUNK + 1]
        dec_ref[0, 2 * d + 1:2 * d + 2, :] = pw_im[CHUNK:CHUNK + 1] * sgn_row

    zeros = jnp.zeros((SSM_GROUP, CHUNK_W), _F32)
    strip = (jnp.concatenate([strips[1], zeros], axis=1)
             + pltpu.roll(jnp.concatenate([strips[0], zeros], axis=1), CHUNK_W - SSM_GROUP, axis=1))
    for i in range(CHUNK):
        off = (CHUNK - 1 - i) * SSM_GROUP
        win = strip if off == 0 else pltpu.roll(strip, 2 * CHUNK_W - off, axis=1)
        toep_ref[0, i * SSM_GROUP:(i + 1) * SSM_GROUP, :] = win[:, 0:CHUNK_W].astype(_BF16)


def _s5_params(prow, ct2, bt1):
    g_spec = lambda shape: pl.BlockSpec(shape, lambda g: (0, g, 0, 0))
    o_spec = lambda shape: pl.BlockSpec(shape, lambda g: (g, 0, 0))
    return pl.pallas_call(
        _s5_param_kernel,
        grid=(SSM_GROUPS,),
        in_specs=[g_spec((2, 1, 8, STATE_W)),
                  g_spec((2, 1, STATE_W, SSM_GROUP)), g_spec((2, 1, SSM_GROUP, STATE_W))],
        out_specs=[o_spec((1, CHUNK_W, CHUNK_W)), o_spec((1, CHUNK_W, 2 * STATE_W)),
                   o_spec((1, 2 * STATE_W, CHUNK_W)), o_spec((1, 4, STATE_W))],
        out_shape=[jax.ShapeDtypeStruct((SSM_GROUPS, CHUNK_W, CHUNK_W), _BF16),
                   jax.ShapeDtypeStruct((SSM_GROUPS, CHUNK_W, 2 * STATE_W), _BF16),
                   jax.ShapeDtypeStruct((SSM_GROUPS, 2 * STATE_W, CHUNK_W), _BF16),
                   jax.ShapeDtypeStruct((SSM_GROUPS, 4, STATE_W), _F32)],
        compiler_params=_cparams("arbitrary"),
        name="s5_params",
    )(prow, ct2, bt1)


def _s5_kernel(ux_ref, uc_ref, toep_ref, bpow_ref, cpow_ref, dec_ref, y_ref, u_ref, s_ref, h_ref, y2_ref):
    ng = S5_GROUPS_PER_STEP
    gw = SSM_GROUP
    x0 = N_CTX_CHUNKS * BATCH

    half = CHUNK // 2
    blk = 32
    lane_blk = lax.broadcasted_iota(jnp.int32, (1, 128), 1) // gw

    def to_chunk_layout(src_ref, tok0, n_chunks, row0):
        for c0 in range(0, n_chunks, blk):
            nb = min(blk, n_chunks - c0)
            for hf in range(2):
                acc = [None] * ng
                for il in range(half):
                    p = src_ref[pl.ds(tok0 + c0 * CHUNK + hf * half + il, nb, stride=CHUNK), :]
                    for g in range(ng):
                        r = p if il == g else pltpu.roll(p, ((il - g) * gw) % 128, axis=1)
                        acc[g] = r if il == 0 else jnp.where(lane_blk == il, r, acc[g])
                for g in range(ng):
                    u_ref[g, hf, pl.ds(row0 + c0 * BATCH, nb, stride=BATCH), :] = acc[g]

    for b in range(BATCH):
        to_chunk_layout(uc_ref, b * CTX_LEN, N_CTX_CHUNKS, b)
        to_chunk_layout(ux_ref, b * SEQ, N_X_CHUNKS, x0 + b)

    chunk_rows = lambda ref, g, lo: jnp.concatenate([ref[g, 0, lo:, :], ref[g, 1, lo:, :]], axis=1)
    for g in range(ng):
        s_ref[g] = jnp.dot(chunk_rows(u_ref, g, 0).astype(_BF16), bpow_ref[g], preferred_element_type=_F32)

    pair = 2 * BATCH
    n_ctx_pairs = N_CTX_CHUNKS // 2
    n_pairs = N_SEQ_CHUNKS // 2
    lower = lax.broadcasted_iota(jnp.int32, (pair, STATE_W), 0) < BATCH
    zero = jnp.zeros((pair, STATE_W), _F32)

    def step(t, carry):
        tb = jnp.where(t < n_ctx_pairs, n_ctx_pairs - 1 - t, n_pairs + n_ctx_pairs - 1 - t)
        rf = pl.multiple_of(t * pair, pair)
        rb = pl.multiple_of(tb * pair, pair)
        out = []
        for g in range(ng):
            hf, hb = carry[2 * g], carry[2 * g + 1]
            dg = dec_ref[g]
            mul_f = lambda v: dg[0:1] * v + dg[1:2] * pltpu.roll(v, SSM_STATE, axis=1)
            mul_b = lambda v: dg[2:3] * v + dg[3:4] * pltpu.roll(v, SSM_STATE, axis=1)

            sf = s_ref[g, pl.ds(rf, pair), 0:STATE_W]
            mid = mul_f(hf) + jnp.where(lower, sf, pltpu.roll(sf, BATCH, axis=0))
            h_ref[g, pl.ds(rf, pair), 0:STATE_W] = jnp.where(lower, hf, mid)
            end = mul_f(mid) + sf
            hf = jnp.where(lower, pltpu.roll(end, BATCH, axis=0), end)

            sb = s_ref[g, pl.ds(rb, pair), STATE_W:2 * STATE_W]
            mid = mul_b(hb) + jnp.where(lower, pltpu.roll(sb, BATCH, axis=0), sb)
            h_ref[g, pl.ds(rb, pair), STATE_W:2 * STATE_W] = jnp.where(lower, mid, hb)
            end = mul_b(mid) + sb
            hb = jnp.where(lower, end, pltpu.roll(end, BATCH, axis=0))
            out += [hf, hb]
        return tuple(out)

    lax.fori_loop(0, n_pairs, step, (zero,) * (2 * ng))

    for g in range(ng):
        ux = chunk_rows(u_ref, g, x0).astype(_BF16)
        hx = h_ref[g, x0:SCAN_ROWS, :].astype(_BF16)
        y2 = (jnp.dot(ux, toep_ref[g], preferred_element_type=_F32)
              + jnp.dot(hx, cpow_ref[g], preferred_element_type=_F32))
        y2_ref[g, 0] = y2[:, 0:128]
        y2_ref[g, 1] = y2[:, 128:256]

    for b in range(BATCH):
        for c0 in range(0, N_X_CHUNKS, blk):
            for hf in range(2):
                pieces = [y2_ref[g, hf, pl.ds(b + c0 * BATCH, blk, stride=BATCH), :] for g in range(ng)]
                for jl in range(half):
                    acc = None
                    for g in range(ng):
                        r = pieces[g] if jl == g else pltpu.roll(pieces[g], ((g - jl) * gw) % 128, axis=1)
                        acc = r if g == 0 else jnp.where(lane_blk == g, r, acc)
                    y_ref[pl.ds(b * SEQ + c0 * CHUNK + hf * half + jl, blk, stride=CHUNK), :] = acc


def _s5_scan(u_x, u_c, toep, bpow, cpow, dec):
    ng = S5_GROUPS_PER_STEP
    n_x_rows = N_X_CHUNKS * BATCH
    lanes = ng * SSM_GROUP
    spec = lambda r, c: pl.BlockSpec((ng, r, c), lambda i: (i, 0, 0))
    col = lambda rows: pl.BlockSpec((rows, lanes), lambda i: (0, i))
    return pl.pallas_call(
        _s5_kernel,
        grid=(SSM_GROUPS // ng,),
        in_specs=[col(N_TOK), col(N_CTX_TOK), spec(CHUNK_W, CHUNK_W), spec(CHUNK_W, 2 * STATE_W),
                  spec(2 * STATE_W, CHUNK_W), spec(4, STATE_W)],
        out_specs=col(N_TOK),
        out_shape=jax.ShapeDtypeStruct((N_TOK, D_SSM), _F32),
        scratch_shapes=[pltpu.VMEM((ng, 2, SCAN_ROWS, 128), _F32),
                        pltpu.VMEM((ng, SCAN_ROWS, 2 * STATE_W), _F32),
                        pltpu.VMEM((ng, SCAN_ROWS, 2 * STATE_W), _F32),
                        pltpu.VMEM((ng, 2, n_x_rows, 128), _F32)],
        compiler_params=_cparams("arbitrary"),
        name="s5_scan",
    )(u_x, u_c, toep, bpow, cpow, dec)


def _merge_kernel(ys_ref, u_ref, conv_ref, x_ref, mod_ref, dskip_ref, mixg_ref, n2g_ref,
                  wglu_ref, wout_ref, rwt_ref, x1_ref, h2_ref, logit_ref):
    m = mod_ref[0]
    yx = dskip_ref[...] * u_ref[...] + ys_ref[...]
    c0 = math.sqrt(2.0 / math.pi)
    ge = 0.5 * yx * (1.0 + jnp.tanh(c0 * (yx + 0.044715 * (yx * yx * yx))))
    z = jnp.dot(ge.astype(_BF16), wglu_ref[...], preferred_element_type=_F32)
    ssm_y = z[:, 0:D_SSM] * jax.nn.sigmoid(z[:, D_SSM:2 * D_SSM])
    conv_y = conv_ref[...]
    mixg = mixg_ref[...]
    heads_a = (ssm_y * _rms_scale(ssm_y) * mixg[:, 0:D_SSM]).astype(_BF16)
    heads_b = (conv_y * _rms_scale(conv_y) * mixg[:, D_SSM:]).astype(_BF16)
    mix = (jnp.dot(heads_a, wout_ref[0:D_SSM, :], preferred_element_type=_F32)
           + jnp.dot(heads_b, wout_ref[D_SSM:, :], preferred_element_type=_F32))
    x1 = x_ref[...] + m[2:3] * mix
    x1_ref[...] = x1
    h2 = x1 * _rms_scale(x1) * n2g_ref[...] * (1.0 + m[4:5]) + m[3:4]
    h2_ref[...] = h2
    h_hi = h2.astype(_BF16)
    h_lo = (h2 - h_hi.astype(_F32)).astype(_BF16)
    nt = (((1,), (1,)), ((), ()))
    p = lax.dot_general(rwt_ref[...], h_hi, nt, preferred_element_type=_F32)
    q = lax.dot_general(rwt_ref[0:N_EXPERTS, :], h_lo, nt, preferred_element_type=_F32)
    logit_ref[...] = p[0:N_EXPERTS] + p[N_EXPERTS:2 * N_EXPERTS] + q


def _merge(ys, u, conv, x2, mod3, dskip, mixg, n2g, wglu_bf, wout_bf, rwt):
    tiles_per_batch = SEQ // TM
    tok = lambda w: pl.BlockSpec((TM, w), lambda i: (i, 0))
    return pl.pallas_call(
        _merge_kernel,
        grid=(N_TOK // TM,),
        in_specs=[tok(D_SSM), tok(D_SSM), tok(D_CONV), tok(D_MODEL),
                  pl.BlockSpec((1, N_MOD, D_MODEL), lambda i: (i // tiles_per_batch, 0, 0)),
                  _const_spec((1, D_SSM)), _const_spec((1, D_MODEL)), _const_spec((1, D_MODEL)),
                  _const_spec((D_SSM, 2 * D_SSM)), _const_spec((D_MODEL, D_MODEL)),
                  _const_spec((2 * N_EXPERTS, D_MODEL))],
        out_specs=[tok(D_MODEL), tok(D_MODEL),
                   pl.BlockSpec((N_EXPERTS, TM), lambda i: (0, i))],
        out_shape=[jax.ShapeDtypeStruct((N_TOK, D_MODEL), _F32),
                   jax.ShapeDtypeStruct((N_TOK, D_MODEL), _F32),
                   jax.ShapeDtypeStruct((N_EXPERTS, N_TOK), _F32)],
        compiler_params=_cparams("arbitrary"),
        name="merge_heads",
    )(ys, u, conv, x2, mod3, dskip, mixg, n2g, wglu_bf, wout_bf, rwt)


def _route_kernel(logit_ref, bias_ref, eid_ref, w_ref, rank_ref, cnt_ref, carry_ref):
    @pl.when(pl.program_id(0) == 0)
    def _():
        carry_ref[...] = jnp.zeros_like(carry_ref)

    tm = logit_ref.shape[1]
    neg = jnp.float32(-jnp.inf)
    scores = jax.nn.sigmoid(logit_ref[...])
    biased = scores + bias_ref[:, 0:1]
    sub = lax.broadcasted_iota(jnp.int32, (GROUP_SIZE, tm), 0)
    rows = lambda a, g: a[g * GROUP_SIZE:(g + 1) * GROUP_SIZE]
    ngrp = N_EXPERT_GROUPS

    gscore = []
    for g in range(ngrp):
        bg = rows(biased, g)
        m1 = jnp.max(bg, axis=0, keepdims=True)
        first = jnp.min(jnp.where(bg == m1, sub, GROUP_SIZE), axis=0, keepdims=True)
        m2 = jnp.max(jnp.where(sub == first, neg, bg), axis=0, keepdims=True)
        gscore.append(m1 + m2)
    v = []
    for g in range(ngrp):
        beaten = jnp.zeros((1, tm), jnp.int32)
        for o in range(ngrp):
            if o != g:
                beats = (gscore[o] >= gscore[g]) if o < g else (gscore[o] > gscore[g])
                beaten = beaten + beats.astype(jnp.int32)
        v.append(jnp.where(beaten < TOPK_GROUPS, rows(biased, g), neg))
    eids = [sub + g * GROUP_SIZE for g in range(ngrp)]

    pick_ids, pick_masks = [], []
    for _ in range(TOP_K):
        m = functools.reduce(jnp.maximum, v)
        m = jnp.max(m, axis=0, keepdims=True)
        cand = functools.reduce(jnp.minimum, [jnp.where(v[g] == m, eids[g], N_EXPERTS) for g in range(ngrp)])
        pick_id = jnp.min(cand, axis=0, keepdims=True)
        masks = [eids[g] == pick_id for g in range(ngrp)]
        v = [jnp.where(masks[g], neg, v[g]) for g in range(ngrp)]
        pick_ids.append(pick_id)
        pick_masks.append(masks)

    sel = jnp.concatenate(
        [functools.reduce(jnp.logical_or, [pick_masks[k][g] for k in range(TOP_K)]).astype(_F32)
         for g in range(ngrp)], axis=0)
    before = (lax.broadcasted_iota(jnp.int32, (tm, tm), 0)
              < lax.broadcasted_iota(jnp.int32, (tm, tm), 1)).astype(_BF16)
    base = jnp.dot(sel.astype(_BF16), before, preferred_element_type=_F32) + carry_ref[:, 0:1]

    def gather_pick(a, k):
        parts = [jnp.sum(jnp.where(pick_masks[k][g], rows(a, g), 0.0), axis=0, keepdims=True)
                 for g in range(ngrp)]
        return functools.reduce(jnp.add, parts)

    picked = [gather_pick(scores, k) for k in range(TOP_K)]
    denom = functools.reduce(jnp.add, picked)
    for k in range(TOP_K):
        eid_ref[k:k + 1, :] = pick_ids[k]
        w_ref[k:k + 1, :] = picked[k] / denom * ROUTED_SCALE
        rank_ref[k:k + 1, :] = gather_pick(base, k).astype(jnp.int32)
    carry_ref[...] = carry_ref[...] + jnp.sum(sel, axis=1, keepdims=True)
    cnt_ref[...] = carry_ref[...]


def _route(logits_t, bias_col):
    tok = pl.BlockSpec((TOP_K, TM), lambda i: (0, i))
    return pl.pallas_call(
        _route_kernel,
        grid=(N_TOK // TM,),
        in_specs=[pl.BlockSpec((N_EXPERTS, TM), lambda i: (0, i)), _const_spec((N_EXPERTS, 128))],
        out_specs=[tok, tok, tok, pl.BlockSpec((N_EXPERTS, 128), lambda i: (0, 0))],
        out_shape=[jax.ShapeDtypeStruct((TOP_K, N_TOK), jnp.int32),
                   jax.ShapeDtypeStruct((TOP_K, N_TOK), _F32),
                   jax.ShapeDtypeStruct((TOP_K, N_TOK), jnp.int32),
                   jax.ShapeDtypeStruct((N_EXPERTS, 128), _F32)],
        scratch_shapes=[pltpu.VMEM((N_EXPERTS, 128), _F32)],
        compiler_params=_cparams("arbitrary"),
        name="route",
    )(logits_t, bias_col)


SUBLANES = 8


def _pow2_pieces():
    bit = EXPERT_BM // 2
    while bit >= SUBLANES:
        yield bit
        bit //= 2


def _dispatch_kernel(dest_ref, padstart_ref, padn_ref, h_ref, x1_ref, mod_ref, wsg_ref, wsu_ref, wsd_ref,
                     xs_ref, xmid_ref, zero_ref, sem, zsem):
    i = pl.program_id(0)
    tm = h_ref.shape[0]
    base = i * tm

    @pl.when(i == 0)
    def _():
        zero_ref[...] = jnp.zeros_like(zero_ref)

        def pieces(e, act):
            n = padn_ref[e]
            start = padstart_ref[e]
            head = jnp.minimum(n, (SUBLANES - start % SUBLANES) % SUBLANES)
            for r in range(SUBLANES - 1):
                cp = pltpu.make_async_copy(zero_ref.at[pl.ds(0, 1), :], xs_ref.at[pl.ds(start + r, 1), :], zsem)
                pl.when(r < head)(functools.partial(act, cp))
            start = start + head
            n = n - head
            for bit in _pow2_pieces():
                cp = pltpu.make_async_copy(zero_ref.at[pl.ds(0, bit), :],
                                           xs_ref.at[pl.ds(pl.multiple_of(start, SUBLANES), bit), :], zsem)
                pl.when((n & bit) != 0)(functools.partial(act, cp))
                start = start + (n & bit)

        def zstart(e, c):
            pieces(e, lambda cp: cp.start())
            return c

        def zwait(e, c):
            pieces(e, lambda cp: cp.wait())
            return c

        lax.fori_loop(0, N_EXPERTS, zstart, 0)
        lax.fori_loop(0, N_EXPERTS, zwait, 0)

    for t in range(tm):
        src = h_ref.at[pl.ds(t, 1), :]
        for k in range(TOP_K):
            pltpu.make_async_copy(src, xs_ref.at[pl.ds(dest_ref[k, base + t], 1), :],
                                  sem).start(priority=k % 2)

    h = h_ref[...].astype(_BF16)
    a = jnp.dot(h, wsg_ref[...], preferred_element_type=_F32)
    b = jnp.dot(h, wsu_ref[...], preferred_element_type=_F32)
    shared = jnp.dot((_silu(a) * b).astype(_BF16), wsd_ref[...], preferred_element_type=_F32)
    xmid_ref[...] = x1_ref[...] + mod_ref[0][5:6] * shared

    for _ in range(TOP_K):
        pltpu.make_async_copy(h_ref, xs_ref.at[pl.ds(0, tm), :], sem).wait()


def _dispatch(dest, pad_start, pad_n, h2, x1, mod3, wsg_bf, wsu_bf, wsd_bf):
    tm = TM
    tiles_per_batch = SEQ // tm
    tok = pl.BlockSpec((tm, D_MODEL), lambda i, *_: (i, 0))
    const = lambda shape: pl.BlockSpec(shape, lambda i, *_: (0,) * len(shape), pipeline_mode=pl.Buffered(1))
    grid_spec = pltpu.PrefetchScalarGridSpec(
        num_scalar_prefetch=3,
        grid=(N_TOK // tm,),
        in_specs=[tok, tok,
                  pl.BlockSpec((1, N_MOD, D_MODEL), lambda i, *_: (i // tiles_per_batch, 0, 0)),
                  const((D_MODEL, D_SHARED)), const((D_MODEL, D_SHARED)), const((D_SHARED, D_MODEL))],
        out_specs=[pl.BlockSpec(memory_space=pl.ANY), tok],
        scratch_shapes=[pltpu.VMEM((EXPERT_BM // 2, D_MODEL), _F32),
                        pltpu.SemaphoreType.DMA(()), pltpu.SemaphoreType.DMA(())],
    )
    return pl.pallas_call(
        _dispatch_kernel,
        grid_spec=grid_spec,
        out_shape=[jax.ShapeDtypeStruct((N_ROWS, D_MODEL), _F32),
                   jax.ShapeDtypeStruct((N_TOK, D_MODEL), _F32)],
        compiler_params=_cparams("arbitrary"),
        name="dispatch",
    )(dest, pad_start, pad_n, h2, x1, mod3, wsg_bf, wsu_bf, wsd_bf)


def _experts_kernel(be_ref, nact_ref, xs_ref, wg_ref, wu_ref, wd_ref, ys_ref, wg_bf, wu_bf, wd_bf):
    i = pl.program_id(0)

    @pl.when(i < nact_ref[0])
    def _():
        @pl.when((i == 0) | (be_ref[i] != be_ref[jnp.maximum(i - 1, 0)]))
        def _():
            wg_bf[...] = wg_ref[0].astype(_BF16)
            wu_bf[...] = wu_ref[0].astype(_BF16)
            wd_bf[...] = wd_ref[0].astype(_BF16)

        x = xs_ref[...].astype(_BF16)
        a = jnp.dot(x, wg_bf[...], preferred_element_type=_F32)
        b = jnp.dot(x, wu_bf[...], preferred_element_type=_F32)
        ys_ref[...] = jnp.dot((_silu(a) * b).astype(_BF16), wd_bf[...], preferred_element_type=_F32)


def _experts(block_e, n_active, xs, w_gate, w_up, w_down):
    bm = EXPERT_BM
    row_blk = lambda i, be, na: (jnp.minimum(i, na[0] - 1), 0)
    grid_spec = pltpu.PrefetchScalarGridSpec(
        num_scalar_prefetch=2,
        grid=(N_BLOCKS,),
        in_specs=[pl.BlockSpec((bm, D_MODEL), row_blk),
                  pl.BlockSpec((1, D_MODEL, D_EXPERT), lambda i, be, na: (be[i], 0, 0)),
                  pl.BlockSpec((1, D_MODEL, D_EXPERT), lambda i, be, na: (be[i], 0, 0)),
                  pl.BlockSpec((1, D_EXPERT, D_MODEL), lambda i, be, na: (be[i], 0, 0))],
        out_specs=pl.BlockSpec((bm, D_MODEL), row_blk),
        scratch_shapes=[pltpu.VMEM((D_MODEL, D_EXPERT), _BF16), pltpu.VMEM((D_MODEL, D_EXPERT), _BF16),
                        pltpu.VMEM((D_EXPERT, D_MODEL), _BF16)],
    )
    return pl.pallas_call(
        _experts_kernel,
        grid_spec=grid_spec,
        out_shape=jax.ShapeDtypeStruct((N_ROWS, D_MODEL), _F32),
        compiler_params=_cparams("arbitrary"),
        name="experts",
    )(block_e, n_active, xs, w_gate, w_up, w_down)


def _final_kernel(dest_ref, xmid_ref, wt_ref, mod_ref, fg_ref, ys_ref, o_ref, buf_ref, sem):
    i = pl.program_id(0)
    n_tiles = pl.num_programs(0)
    tm = xmid_ref.shape[0]

    def gather(tile, slot):
        base = tile * tm
        for t in range(tm):
            for k in range(TOP_K):
                pltpu.make_async_copy(ys_ref.at[pl.ds(dest_ref[k, base + t], 1), :],
                                      buf_ref.at[slot, pl.ds(k * tm + t, 1), :],
                                      sem.at[slot]).start(priority=k % 2)

    def drain(slot):
        for k in range(TOP_K):
            pltpu.make_async_copy(ys_ref.at[pl.ds(0, tm), :], buf_ref.at[slot, pl.ds(k * tm, tm), :],
                                  sem.at[slot]).wait()

    @pl.when(i == 0)
    def _():
        gather(0, 0)

    slot = i % 2
    drain(slot)
    gather(jnp.minimum(i + 1, n_tiles - 1), 1 - slot)

    wt = wt_ref[...]
    routed = wt[:, 0:1] * buf_ref[slot, 0:tm, :]
    for k in range(1, TOP_K):
        routed = routed + wt[:, k:k + 1] * buf_ref[slot, k * tm:(k + 1) * tm, :]
    y = xmid_ref[...] + mod_ref[0][5:6] * routed
    o_ref[...] = y * _rms_scale(y) * fg_ref[...]

    @pl.when(i == n_tiles - 1)
    def _():
        drain(1 - slot)


def _final(dest, xmid, w_tok, mod3, final_g, ys):
    tm = TM_COMBINE
    tiles_per_batch = SEQ // tm
    tok = lambda w: pl.BlockSpec((tm, w), lambda i, *_: (i, 0))
    grid_spec = pltpu.PrefetchScalarGridSpec(
        num_scalar_prefetch=1,
        grid=(N_TOK // tm,),
        in_specs=[tok(D_MODEL), tok(TOP_K),
                  pl.BlockSpec((1, N_MOD, D_MODEL), lambda i, *_: (i // tiles_per_batch, 0, 0)),
                  pl.BlockSpec((1, D_MODEL), lambda i, *_: (0, 0), pipeline_mode=pl.Buffered(1)),
                  pl.BlockSpec(memory_space=pl.ANY)],
        out_specs=tok(D_MODEL),
        scratch_shapes=[pltpu.VMEM((2, TOP_K * tm, D_MODEL), _F32), pltpu.SemaphoreType.DMA((2,))],
    )
    return pl.pallas_call(
        _final_kernel,
        grid_spec=grid_spec,
        out_shape=jax.ShapeDtypeStruct((N_TOK, D_MODEL), _F32),
        compiler_params=_cparams("arbitrary"),
        name="final",
    )(dest, xmid, w_tok, mod3, final_g, ys)


def _s5_param_layouts(lam_re, lam_im, b_re, b_im, c_re, c_im, log_dt):
    two = lambda a: jnp.concatenate([a, a], axis=-1)
    lr, li = two(lam_re), two(lam_im)
    dtb = jnp.broadcast_to(log_dt.astype(_F32)[:, :, None], lr.shape)
    zeros = jnp.zeros_like(lr)
    prow = jnp.stack([lr, li, dtb] + [zeros] * 5, axis=2)
    ct2 = jnp.concatenate([jnp.swapaxes(c_re, -1, -2), jnp.swapaxes(c_im, -1, -2)], axis=2)
    bt1 = jnp.concatenate([jnp.swapaxes(b_re, -1, -2), jnp.swapaxes(b_im, -1, -2)], axis=3)
    return prow, ct2, bt1


def kernel(x, c, ctx, c_ctx, norm1_g, norm2_g, w_ada, b_ada, w_in, ssm_lam_re, ssm_lam_im, ssm_b_re, ssm_b_im, ssm_c_re, ssm_c_im, ssm_log_dt, ssm_d, ssm_w_glu, conv_w, conv_b, mix_norm_g, w_out, router_w, router_bias, exp_w_gate, exp_w_up, exp_w_down, shared_w_gate, shared_w_up, shared_w_down, final_g):
    layer = 0
    x2 = x.reshape(N_TOK, D_MODEL)
    ctx2 = ctx.reshape(N_CTX_TOK, D_MODEL)

    c8 = jnp.concatenate([c, c_ctx[None, :], jnp.zeros((8 - BATCH - 1, D_MODEL), _F32)], axis=0)
    mod = _ada_mod(c8, w_ada[layer], b_ada[layer][None, :])
    mod3 = mod.reshape(8, N_MOD, D_MODEL)

    w_in_bf = w_in[layer].astype(_BF16)
    conv_w8 = jnp.concatenate([conv_w[layer], jnp.zeros((8 - conv_w.shape[1], D_CONV), _F32)], axis=0)
    u_x, conv_x = _inproj(x2, mod3, norm1_g[layer][None, :], w_in_bf, conv_w8, conv_b[layer][None, :])
    u_c = _ctxproj(ctx2, mod3, norm1_g[layer][None, :], w_in_bf[:, :D_SSM])

    prow, ct2, bt1 = _s5_param_layouts(ssm_lam_re[layer], ssm_lam_im[layer], ssm_b_re[layer],
                                       ssm_b_im[layer], ssm_c_re[layer], ssm_c_im[layer], ssm_log_dt[layer])
    toep, bpow, cpow, dec = _s5_params(prow, ct2, bt1)
    ys = _s5_scan(u_x, u_c, toep, bpow, cpow, dec)

    rw = router_w[layer]
    rw_hi = rw.astype(_BF16)
    rw_lo = (rw - rw_hi.astype(_F32)).astype(_BF16)
    rwt = jnp.concatenate([rw_hi.T, rw_lo.T], axis=0)
    x1, h2, logits_t = _merge(ys, u_x, conv_x, x2, mod3, ssm_d[layer][None, :],
                              mix_norm_g[layer][None, :], norm2_g[layer][None, :],
                              ssm_w_glu[layer].astype(_BF16), w_out[layer].astype(_BF16), rwt)

    bias_col = jnp.broadcast_to(router_bias[layer][:, None], (N_EXPERTS, 128))
    eid, w_k, rank, cnt = _route(logits_t, bias_col)

    counts = cnt[:, 0].astype(jnp.int32)
    padded = (counts + EXPERT_BM - 1) // EXPERT_BM * EXPERT_BM
    pend = jnp.cumsum(padded)
    pstart = pend - padded
    is_e = eid[:, :, None] == jnp.arange(N_EXPERTS, dtype=jnp.int32)
    dest = jnp.sum(jnp.where(is_e, pstart, 0), axis=-1) + rank
    n_active = (pend[-1] // EXPERT_BM).astype(jnp.int32)
    blk = jnp.minimum(jnp.arange(N_BLOCKS, dtype=jnp.int32), n_active - 1)
    ends_before = (pend[None, :] <= (blk * EXPERT_BM)[:, None]).astype(jnp.int32)
    block_e = jnp.minimum(jnp.sum(ends_before, axis=1), N_EXPERTS - 1)

    xs, xmid = _dispatch(dest, pstart + counts, padded - counts, h2, x1, mod3,
                         shared_w_gate[layer].astype(_BF16), shared_w_up[layer].astype(_BF16),
                         shared_w_down[layer].astype(_BF16))
    ys_rows = _experts(block_e, n_active[None], xs, exp_w_gate[layer], exp_w_up[layer], exp_w_down[layer])
    out = _final(dest, xmid, w_k.T, mod3, final_g[None, :], ys_rows)
    return out.reshape(BATCH, SEQ, D_MODEL)
```

```python
import functools
import math

import jax
import jax.numpy as jnp
from jax import lax
from jax.experimental import pallas as pl
from jax.experimental.pallas import tpu as pltpu

D_MODEL = 2048
BATCH = 4
SEQ = 2048
CTX_LEN = 256
GRID_W = 64
D_SSM = 1024
D_CONV = 1024
SSM_GROUP = 16
SSM_GROUPS = 64
SSM_STATE = 64
N_EXPERTS = 64
N_EXPERT_GROUPS = 8
GROUP_SIZE = N_EXPERTS // N_EXPERT_GROUPS
TOPK_GROUPS = 4
TOP_K = 8
D_EXPERT = 512
D_SHARED = 512
ROUTED_SCALE = 2.5
N_MOD = 6
EPS = 1e-6

N_TOK = BATCH * SEQ
N_CTX_TOK = BATCH * CTX_LEN

CHUNK = 16
CHUNK_W = CHUNK * SSM_GROUP
STATE_W = 2 * SSM_STATE
N_CTX_CHUNKS = CTX_LEN // CHUNK
N_X_CHUNKS = SEQ // CHUNK
N_SEQ_CHUNKS = N_CTX_CHUNKS + N_X_CHUNKS
SCAN_ROWS = N_SEQ_CHUNKS * BATCH
S5_GROUPS_PER_STEP = 8

TM = 256
EXPERT_BM = 256
N_ASSIGN = N_TOK * TOP_K
N_BLOCKS = N_ASSIGN // EXPERT_BM + N_EXPERTS
N_ROWS = N_BLOCKS * EXPERT_BM
TM_COMBINE = 128
VMEM_LIMIT = 56 * 1024 * 1024

_F32 = jnp.float32
_BF16 = jnp.bfloat16


def _cparams(*sem):
    return pltpu.CompilerParams(dimension_semantics=sem, vmem_limit_bytes=VMEM_LIMIT)


def _const_spec(shape):
    nd = len(shape)
    return pl.BlockSpec(shape, lambda *_: (0,) * nd, pipeline_mode=pl.Buffered(1))


def _rms_scale(xf):
    return lax.rsqrt(jnp.mean(xf * xf, axis=-1, keepdims=True) + EPS)


def _silu(x):
    return x * jax.nn.sigmoid(x)


def _ada_kernel(c_ref, w_ref, b_ref, o_ref):
    s = _silu(c_ref[...])
    o_ref[...] = jnp.dot(s, w_ref[...], preferred_element_type=_F32) + b_ref[...]


def _ada_mod(c8, w_ada, b_ada):
    n = w_ada.shape[1]
    tn = 1024
    return pl.pallas_call(
        _ada_kernel,
        grid=(n // tn,),
        in_specs=[pl.BlockSpec((8, D_MODEL), lambda j: (0, 0)),
                  pl.BlockSpec((D_MODEL, tn), lambda j: (0, j)),
                  pl.BlockSpec((1, tn), lambda j: (0, j))],
        out_specs=pl.BlockSpec((8, tn), lambda j: (0, j)),
        out_shape=jax.ShapeDtypeStruct((8, n), _F32),
        compiler_params=_cparams("arbitrary"),
        name="ada_mod",
    )(c8, w_ada, b_ada)


def _modulated_norm(x_ref, mod_ref, g_ref, shift_row, scale_row):
    xf = x_ref[...]
    m = mod_ref[0]
    h = xf * _rms_scale(xf) * g_ref[...]
    return h * (1.0 + m[scale_row:scale_row + 1]) + m[shift_row:shift_row + 1]


def _inproj_kernel(x_ref, mod_ref, g_ref, w_ref, cw_ref, cb_ref, u_ref, conv_ref):
    h = _modulated_norm(x_ref, mod_ref, g_ref, 0, 1).astype(_BF16)
    u_ref[...] = jnp.dot(h, w_ref[:, 0:D_SSM], preferred_element_type=_F32)
    tm = x_ref.shape[0]
    pos = lax.broadcasted_iota(jnp.int32, (tm, 1), 0) % GRID_W
    not_first = (pos != 0).astype(_F32)
    not_last = (pos != GRID_W - 1).astype(_F32)
    cw = cw_ref[...]
    nc = 256
    for j in range(D_CONV // nc):
        lo = j * nc
        bg = jnp.dot(h, w_ref[:, D_SSM + lo:D_SSM + lo + nc], preferred_element_type=_F32)
        cg = jnp.dot(h, w_ref[:, D_SSM + D_CONV + lo:D_SSM + D_CONV + lo + nc], preferred_element_type=_F32)
        v = jnp.dot(h, w_ref[:, D_SSM + 2 * D_CONV + lo:D_SSM + 2 * D_CONV + lo + nc],
                    preferred_element_type=_F32)
        z = cg * v
        z_prev = pltpu.roll(z, 1, axis=0) * not_first
        z_next = pltpu.roll(z, tm - 1, axis=0) * not_last
        y = (cb_ref[:, lo:lo + nc] + z_prev * cw[0:1, lo:lo + nc] + z * cw[1:2, lo:lo + nc]
             + z_next * cw[2:3, lo:lo + nc])
        conv_ref[:, lo:lo + nc] = bg * y


def _inproj(x2, mod3, norm_g, w_in_bf, conv_w, conv_b):
    d_in = w_in_bf.shape[1]
    tiles_per_batch = SEQ // TM
    return pl.pallas_call(
        _inproj_kernel,
        grid=(N_TOK // TM,),
        in_specs=[pl.BlockSpec((TM, D_MODEL), lambda i: (i, 0)),
                  pl.BlockSpec((1, N_MOD, D_MODEL), lambda i: (i // tiles_per_batch, 0, 0)),
                  _const_spec((1, D_MODEL)),
                  _const_spec((D_MODEL, d_in)),
                  _const_spec((8, D_CONV)),
                  _const_spec((1, D_CONV))],
        out_specs=[pl.BlockSpec((TM, D_SSM), lambda i: (i, 0)),
                   pl.BlockSpec((TM, D_CONV), lambda i: (i, 0))],
        out_shape=[jax.ShapeDtypeStruct((N_TOK, D_SSM), _F32),
                   jax.ShapeDtypeStruct((N_TOK, D_CONV), _F32)],
        compiler_params=_cparams("arbitrary"),
        name="in_proj",
    )(x2, mod3, norm_g, w_in_bf, conv_w, conv_b)


def _ctxproj_kernel(x_ref, mod_ref, g_ref, w_ref, u_ref):
    h = _modulated_norm(x_ref, mod_ref, g_ref, 0, 1).astype(_BF16)
    u_ref[...] = jnp.dot(h, w_ref[...], preferred_element_type=_F32)


def _ctxproj(ctx2, mod3, norm_g, w_u_bf):
    return pl.pallas_call(
        _ctxproj_kernel,
        grid=(N_CTX_TOK // TM,),
        in_specs=[pl.BlockSpec((TM, D_MODEL), lambda i: (i, 0)),
                  pl.BlockSpec((1, N_MOD, D_MODEL), lambda i: (BATCH, 0, 0)),
                  _const_spec((1, D_MODEL)),
                  _const_spec((D_MODEL, D_SSM))],
        out_specs=pl.BlockSpec((TM, D_SSM), lambda i: (i, 0)),
        out_shape=jax.ShapeDtypeStruct((N_CTX_TOK, D_SSM), _F32),
        compiler_params=_cparams("arbitrary"),
        name="ctx_proj",
    )(ctx2, mod3, norm_g, w_u_bf)


N_POW = 32


def _s5_param_kernel(prow_ref, ct_ref, bt_ref, toep_ref, bpow_ref, cpow_ref, dec_ref):
    lane = lambda shape: lax.broadcasted_iota(jnp.int32, shape, 1)
    sub = lambda shape: lax.broadcasted_iota(jnp.int32, shape, 0)

    def split2(a):
        a_hi = a.astype(_BF16)
        return a_hi, (a - a_hi.astype(_F32)).astype(_BF16)

    rep = (lane((SSM_GROUP, CHUNK_W)) % SSM_GROUP == sub((SSM_GROUP, CHUNK_W))).astype(_BF16)
    sgn_col = jnp.where(sub((STATE_W, 1)) < SSM_STATE, 1.0, -1.0).astype(_F32)
    sgn_row = jnp.where(lane((1, STATE_W)) < SSM_STATE, -1.0, 1.0).astype(_F32)
    tau_col = sub((N_POW, 1)).astype(_F32)
    blk_l = lane((N_POW, CHUNK_W)) // SSM_GROUP
    tau_s = sub((N_POW, CHUNK_W))
    blk_r = sub((CHUNK_W, N_POW)) // SSM_GROUP
    tau_l = lane((CHUNK_W, N_POW))
    pick_l = lambda e: (tau_s == e).astype(_BF16)
    pick_r = lambda e: (tau_l == e).astype(_BF16)
    tn = (((0,), (0,)), ((), ()))

    strips = []
    for d in range(2):
        pr = prow_ref[d, 0]
        lam_re, lam_im, dt = pr[0:1], pr[1:2], jnp.exp(pr[2:3])
        mag = jnp.exp(tau_col * (lam_re * dt))
        ang = tau_col * (lam_im * dt)
        pw_re = mag * jnp.cos(ang)
        pw_im = mag * jnp.sin(ang)
        pw4 = jnp.concatenate(split2(pw_re) + split2(pw_im), axis=1)

        c_hi, c_lo = split2(ct_ref[d, 0])
        ct = (jnp.dot(c_hi, rep, preferred_element_type=_F32)
              + jnp.dot(c_lo, rep, preferred_element_type=_F32))
        ca = ct * sgn_col
        cb = -pltpu.roll(ct, SSM_STATE, axis=0)

        def cpow(sel):
            o = lax.dot_general(pw4, sel, tn, preferred_element_type=_F32)
            x = o[0:STATE_W] + o[STATE_W:2 * STATE_W]
            y = o[2 * STATE_W:3 * STATE_W] + o[3 * STATE_W:4 * STATE_W]
            return x * ca + y * cb

        nr = pw_re[1:2] - 1.0
        ni = pw_im[1:2]
        den = lam_re * lam_re + lam_im * lam_im
        kr = (nr * lam_re + ni * lam_im) / den
        ki = (ni * lam_re - nr * lam_im) / den
        b1 = bt_ref[d, 0]
        b2 = pltpu.roll(b1, SSM_STATE, axis=1) * sgn_row
        u1 = kr * b1 + ki * b2
        u2 = kr * b2 - ki * b1
        u1t = jnp.concatenate([u1] * CHUNK, axis=0)
        u2t = jnp.concatenate([u2] * CHUNK, axis=0)

        def bpow(sel):
            o = jnp.dot(sel, pw4, preferred_element_type=_F32)
            x = o[:, 0:STATE_W] + o[:, STATE_W:2 * STATE_W]
            y = o[:, 2 * STATE_W:3 * STATE_W] + o[:, 3 * STATE_W:4 * STATE_W]
            return x * u1t + y * u2t

        if d == 0:
            e_strip, e_b, e_c = blk_l, (CHUNK - 1) - blk_r, blk_l + 1
        else:
            e_strip, e_b, e_c = (CHUNK - 1) - blk_l, blk_r, CHUNK - blk_l
        u_hi, u_lo = split2(u1)
        k_hi, k_lo = split2(cpow(pick_l(e_strip)))
        strips.append(jnp.dot(u_hi, k_hi, preferred_element_type=_F32)
                      + jnp.dot(u_hi, k_lo, preferred_element_type=_F32)
                      + jnp.dot(u_lo, k_hi, preferred_element_type=_F32))
        bpow_ref[0, :, d * STATE_W:(d + 1) * STATE_W] = bpow(pick_r(e_b)).astype(_BF16)
        cpow_ref[0, d * STATE_W:(d + 1) * STATE_W, :] = cpow(pick_l(e_c)).astype(_BF16)

        dec_ref[0, 2 * d:2 * d + 1, :] = pw_re[CHUNK:CHUNK + 1]
        dec_ref[0, 2 * d + 1:2 * d + 2, :] = pw_im[CHUNK:CHUNK + 1] * sgn_row

    zeros = jnp.zeros((SSM_GROUP, CHUNK_W), _F32)
    strip = (jnp.concatenate([strips[1], zeros], axis=1)
             + pltpu.roll(jnp.concatenate([strips[0], zeros], axis=1), CHUNK_W - SSM_GROUP, axis=1))
    for i in range(CHUNK):
        off = (CHUNK - 1 - i) * SSM_GROUP
        win = strip if off == 0 else pltpu.roll(strip, 2 * CHUNK_W - off, axis=1)
        toep_ref[0, i * SSM_GROUP:(i + 1) * SSM_GROUP, :] = win[:, 0:CHUNK_W].astype(_BF16)


def _s5_params(prow, ct2, bt1):
    g_spec = lambda shape: pl.BlockSpec(shape, lambda g: (0, g, 0, 0))
    o_spec = lambda shape: pl.BlockSpec(shape, lambda g: (g, 0, 0))
    return pl.pallas_call(
        _s5_param_kernel,
        grid=(SSM_GROUPS,),
        in_specs=[g_spec((2, 1, 8, STATE_W)),
                  g_spec((2, 1, STATE_W, SSM_GROUP)), g_spec((2, 1, SSM_GROUP, STATE_W))],
        out_specs=[o_spec((1, CHUNK_W, CHUNK_W)), o_spec((1, CHUNK_W, 2 * STATE_W)),
                   o_spec((1, 2 * STATE_W, CHUNK_W)), o_spec((1, 4, STATE_W))],
        out_shape=[jax.ShapeDtypeStruct((SSM_GROUPS, CHUNK_W, CHUNK_W), _BF16),
                   jax.ShapeDtypeStruct((SSM_GROUPS, CHUNK_W, 2 * STATE_W), _BF16),
                   jax.ShapeDtypeStruct((SSM_GROUPS, 2 * STATE_W, CHUNK_W), _BF16),
                   jax.ShapeDtypeStruct((SSM_GROUPS, 4, STATE_W), _F32)],
        compiler_params=_cparams("arbitrary"),
        name="s5_params",
    )(prow, ct2, bt1)


def _s5_kernel(ux_ref, uc_ref, toep_ref, bpow_ref, cpow_ref, dec_ref, y_ref, u_ref, s_ref, h_ref, y2_ref):
    ng = S5_GROUPS_PER_STEP
    gw = SSM_GROUP
    x0 = N_CTX_CHUNKS * BATCH

    half = CHUNK // 2
    blk = 32
    lane_blk = lax.broadcasted_iota(jnp.int32, (1, 128), 1) // gw

    def to_chunk_layout(src_ref, tok0, n_chunks, row0):
        for c0 in range(0, n_chunks, blk):
            nb = min(blk, n_chunks - c0)
            for hf in range(2):
                acc = [None] * ng
                for il in range(half):
                    p = src_ref[pl.ds(tok0 + c0 * CHUNK + hf * half + il, nb, stride=CHUNK), :]
                    for g in range(ng):
                        r = p if il == g else pltpu.roll(p, ((il - g) * gw) % 128, axis=1)
                        acc[g] = r if il == 0 else jnp.where(lane_blk == il, r, acc[g])
                for g in range(ng):
                    u_ref[g, hf, pl.ds(row0 + c0 * BATCH, nb, stride=BATCH), :] = acc[g]

    for b in range(BATCH):
        to_chunk_layout(uc_ref, b * CTX_LEN, N_CTX_CHUNKS, b)
        to_chunk_layout(ux_ref, b * SEQ, N_X_CHUNKS, x0 + b)

    chunk_rows = lambda ref, g, lo: jnp.concatenate([ref[g, 0, lo:, :], ref[g, 1, lo:, :]], axis=1)
    for g in range(ng):
        s_ref[g] = jnp.dot(chunk_rows(u_ref, g, 0).astype(_BF16), bpow_ref[g], preferred_element_type=_F32)

    pair = 2 * BATCH
    n_ctx_pairs = N_CTX_CHUNKS // 2
    n_pairs = N_SEQ_CHUNKS // 2
    lower = lax.broadcasted_iota(jnp.int32, (pair, STATE_W), 0) < BATCH
    zero = jnp.zeros((pair, STATE_W), _F32)

    def step(t, carry):
        tb = jnp.where(t < n_ctx_pairs, n_ctx_pairs - 1 - t, n_pairs + n_ctx_pairs - 1 - t)
        rf = pl.multiple_of(t * pair, pair)
        rb = pl.multiple_of(tb * pair, pair)
        out = []
        for g in range(ng):
            hf, hb = carry[2 * g], carry[2 * g + 1]
            dg = dec_ref[g]
            mul_f = lambda v: dg[0:1] * v + dg[1:2] * pltpu.roll(v, SSM_STATE, axis=1)
            mul_b = lambda v: dg[2:3] * v + dg[3:4] * pltpu.roll(v, SSM_STATE, axis=1)

            sf = s_ref[g, pl.ds(rf, pair), 0:STATE_W]
            mid = mul_f(hf) + jnp.where(lower, sf, pltpu.roll(sf, BATCH, axis=0))
            h_ref[g, pl.ds(rf, pair), 0:STATE_W] = jnp.where(lower, hf, mid)
            end = mul_f(mid) + sf
            hf = jnp.where(lower, pltpu.roll(end, BATCH, axis=0), end)

            sb = s_ref[g, pl.ds(rb, pair), STATE_W:2 * STATE_W]
            mid = mul_b(hb) + jnp.where(lower, pltpu.roll(sb, BATCH, axis=0), sb)
            h_ref[g, pl.ds(rb, pair), STATE_W:2 * STATE_W] = jnp.where(lower, mid, hb)
            end = mul_b(mid) + sb
            hb = jnp.where(lower, end, pltpu.roll(end, BATCH, axis=0))
            out += [hf, hb]
        return tuple(out)

    lax.fori_loop(0, n_pairs, step, (zero,) * (2 * ng))

    for g in range(ng):
        ux = chunk_rows(u_ref, g, x0).astype(_BF16)
        hx = h_ref[g, x0:SCAN_ROWS, :].astype(_BF16)
        y2 = (jnp.dot(ux, toep_ref[g], preferred_element_type=_F32)
              + jnp.dot(hx, cpow_ref[g], preferred_element_type=_F32))
        y2_ref[g, 0] = y2[:, 0:128]
        y2_ref[g, 1] = y2[:, 128:256]

    for b in range(BATCH):
        for c0 in range(0, N_X_CHUNKS, blk):
            for hf in range(2):
                pieces = [y2_ref[g, hf, pl.ds(b + c0 * BATCH, blk, stride=BATCH), :] for g in range(ng)]
                for jl in range(half):
                    acc = None
                    for g in range(ng):
                        r = pieces[g] if jl == g else pltpu.roll(pieces[g], ((g - jl) * gw) % 128, axis=1)
                        acc = r if g == 0 else jnp.where(lane_blk == g, r, acc)
                    y_ref[pl.ds(b * SEQ + c0 * CHUNK + hf * half + jl, blk, stride=CHUNK), :] = acc


def _s5_scan(u_x, u_c, toep, bpow, cpow, dec):
    ng = S5_GROUPS_PER_STEP
    n_x_rows = N_X_CHUNKS * BATCH
    lanes = ng * SSM_GROUP
    spec = lambda r, c: pl.BlockSpec((ng, r, c), lambda i: (i, 0, 0))
    col = lambda rows: pl.BlockSpec((rows, lanes), lambda i: (0, i))
    return pl.pallas_call(
        _s5_kernel,
        grid=(SSM_GROUPS // ng,),
        in_specs=[col(N_TOK), col(N_CTX_TOK), spec(CHUNK_W, CHUNK_W), spec(CHUNK_W, 2 * STATE_W),
                  spec(2 * STATE_W, CHUNK_W), spec(4, STATE_W)],
        out_specs=col(N_TOK),
        out_shape=jax.ShapeDtypeStruct((N_TOK, D_SSM), _F32),
        scratch_shapes=[pltpu.VMEM((ng, 2, SCAN_ROWS, 128), _F32),
                        pltpu.VMEM((ng, SCAN_ROWS, 2 * STATE_W), _F32),
                        pltpu.VMEM((ng, SCAN_ROWS, 2 * STATE_W), _F32),
                        pltpu.VMEM((ng, 2, n_x_rows, 128), _F32)],
        compiler_params=_cparams("arbitrary"),
        name="s5_scan",
    )(u_x, u_c, toep, bpow, cpow, dec)


def _merge_kernel(ys_ref, u_ref, conv_ref, x_ref, mod_ref, dskip_ref, mixg_ref, n2g_ref,
                  wglu_ref, wout_ref, rwt_ref, x1_ref, h2_ref, logit_ref):
    m = mod_ref[0]
    yx = dskip_ref[...] * u_ref[...] + ys_ref[...]
    c0 = math.sqrt(2.0 / math.pi)
    ge = 0.5 * yx * (1.0 + jnp.tanh(c0 * (yx + 0.044715 * (yx * yx * yx))))
    z = jnp.dot(ge.astype(_BF16), wglu_ref[...], preferred_element_type=_F32)
    ssm_y = z[:, 0:D_SSM] * jax.nn.sigmoid(z[:, D_SSM:2 * D_SSM])
    conv_y = conv_ref[...]
    mixg = mixg_ref[...]
    heads_a = (ssm_y * _rms_scale(ssm_y) * mixg[:, 0:D_SSM]).astype(_BF16)
    heads_b = (conv_y * _rms_scale(conv_y) * mixg[:, D_SSM:]).astype(_BF16)
    mix = (jnp.dot(heads_a, wout_ref[0:D_SSM, :], preferred_element_type=_F32)
           + jnp.dot(heads_b, wout_ref[D_SSM:, :], preferred_element_type=_F32))
    x1 = x_ref[...] + m[2:3] * mix
    x1_ref[...] = x1
    h2 = x1 * _rms_scale(x1) * n2g_ref[...] * (1.0 + m[4:5]) + m[3:4]
    h2_ref[...] = h2
    h_hi = h2.astype(_BF16)
    h_lo = (h2 - h_hi.astype(_F32)).astype(_BF16)
    nt = (((1,), (1,)), ((), ()))
    p = lax.dot_general(rwt_ref[...], h_hi, nt, preferred_element_type=_F32)
    q = lax.dot_general(rwt_ref[0:N_EXPERTS, :], h_lo, nt, preferred_element_type=_F32)
    logit_ref[...] = p[0:N_EXPERTS] + p[N_EXPERTS:2 * N_EXPERTS] + q


def _merge(ys, u, conv, x2, mod3, dskip, mixg, n2g, wglu_bf, wout_bf, rwt):
    tiles_per_batch = SEQ // TM
    tok = lambda w: pl.BlockSpec((TM, w), lambda i: (i, 0))
    return pl.pallas_call(
        _merge_kernel,
        grid=(N_TOK // TM,),
        in_specs=[tok(D_SSM), tok(D_SSM), tok(D_CONV), tok(D_MODEL),
                  pl.BlockSpec((1, N_MOD, D_MODEL), lambda i: (i // tiles_per_batch, 0, 0)),
                  _const_spec((1, D_SSM)), _const_spec((1, D_MODEL)), _const_spec((1, D_MODEL)),
                  _const_spec((D_SSM, 2 * D_SSM)), _const_spec((D_MODEL, D_MODEL)),
                  _const_spec((2 * N_EXPERTS, D_MODEL))],
        out_specs=[tok(D_MODEL), tok(D_MODEL),
                   pl.BlockSpec((N_EXPERTS, TM), lambda i: (0, i))],
        out_shape=[jax.ShapeDtypeStruct((N_TOK, D_MODEL), _F32),
                   jax.ShapeDtypeStruct((N_TOK, D_MODEL), _F32),
                   jax.ShapeDtypeStruct((N_EXPERTS, N_TOK), _F32)],
        compiler_params=_cparams("arbitrary"),
        name="merge_heads",
    )(ys, u, conv, x2, mod3, dskip, mixg, n2g, wglu_bf, wout_bf, rwt)


def _route_kernel(logit_ref, bias_ref, eid_ref, w_ref, rank_ref, cnt_ref, carry_ref):
    @pl.when(pl.program_id(0) == 0)
    def _():
        carry_ref[...] = jnp.zeros_like(carry_ref)

    tm = logit_ref.shape[1]
    neg = jnp.float32(-jnp.inf)
    scores = jax.nn.sigmoid(logit_ref[...])
    biased = scores + bias_ref[:, 0:1]
    sub = lax.broadcasted_iota(jnp.int32, (GROUP_SIZE, tm), 0)
    rows = lambda a, g: a[g * GROUP_SIZE:(g + 1) * GROUP_SIZE]
    ngrp = N_EXPERT_GROUPS

    gscore = []
    for g in range(ngrp):
        bg = rows(biased, g)
        m1 = jnp.max(bg, axis=0, keepdims=True)
        first = jnp.min(jnp.where(bg == m1, sub, GROUP_SIZE), axis=0, keepdims=True)
        m2 = jnp.max(jnp.where(sub == first, neg, bg), axis=0, keepdims=True)
        gscore.append(m1 + m2)
    v = []
    for g in range(ngrp):
        beaten = jnp.zeros((1, tm), jnp.int32)
        for o in range(ngrp):
            if o != g:
                beats = (gscore[o] >= gscore[g]) if o < g else (gscore[o] > gscore[g])
                beaten = beaten + beats.astype(jnp.int32)
        v.append(jnp.where(beaten < TOPK_GROUPS, rows(biased, g), neg))
    eids = [sub + g * GROUP_SIZE for g in range(ngrp)]

    pick_ids, pick_masks = [], []
    for _ in range(TOP_K):
        m = functools.reduce(jnp.maximum, v)
        m = jnp.max(m, axis=0, keepdims=True)
        cand = functools.reduce(jnp.minimum, [jnp.where(v[g] == m, eids[g], N_EXPERTS) for g in range(ngrp)])
        pick_id = jnp.min(cand, axis=0, keepdims=True)
        masks = [eids[g] == pick_id for g in range(ngrp)]
        v = [jnp.where(masks[g], neg, v[g]) for g in range(ngrp)]
        pick_ids.append(pick_id)
        pick_masks.append(masks)

    sel = jnp.concatenate(
        [functools.reduce(jnp.logical_or, [pick_masks[k][g] for k in range(TOP_K)]).astype(_F32)
         for g in range(ngrp)], axis=0)
    before = (lax.broadcasted_iota(jnp.int32, (tm, tm), 0)
              < lax.broadcasted_iota(jnp.int32, (tm, tm), 1)).astype(_BF16)
    base = jnp.dot(sel.astype(_BF16), before, preferred_element_type=_F32) + carry_ref[:, 0:1]

    def gather_pick(a, k):
        parts = [jnp.sum(jnp.where(pick_masks[k][g], rows(a, g), 0.0), axis=0, keepdims=True)
                 for g in range(ngrp)]
        return functools.reduce(jnp.add, parts)

    picked = [gather_pick(scores, k) for k in range(TOP_K)]
    denom = functools.reduce(jnp.add, picked)
    for k in range(TOP_K):
        eid_ref[k:k + 1, :] = pick_ids[k]
        w_ref[k:k + 1, :] = picked[k] / denom * ROUTED_SCALE
        rank_ref[k:k + 1, :] = gather_pick(base, k).astype(jnp.int32)
    carry_ref[...] = carry_ref[...] + jnp.sum(sel, axis=1, keepdims=True)
    cnt_ref[...] = carry_ref[...]


def _route(logits_t, bias_col):
    tok = pl.BlockSpec((TOP_K, TM), lambda i: (0, i))
    return pl.pallas_call(
        _route_kernel,
        grid=(N_TOK // TM,),
        in_specs=[pl.BlockSpec((N_EXPERTS, TM), lambda i: (0, i)), _const_spec((N_EXPERTS, 128))],
        out_specs=[tok, tok, tok, pl.BlockSpec((N_EXPERTS, 128), lambda i: (0, 0))],
        out_shape=[jax.ShapeDtypeStruct((TOP_K, N_TOK), jnp.int32),
                   jax.ShapeDtypeStruct((TOP_K, N_TOK), _F32),
                   jax.ShapeDtypeStruct((TOP_K, N_TOK), jnp.int32),
                   jax.ShapeDtypeStruct((N_EXPERTS, 128), _F32)],
        scratch_shapes=[pltpu.VMEM((N_EXPERTS, 128), _F32)],
        compiler_params=_cparams("arbitrary"),
        name="route",
    )(logits_t, bias_col)


SUBLANES = 8


def _pow2_pieces():
    bit = EXPERT_BM // 2
    while bit >= SUBLANES:
        yield bit
        bit //= 2


def _dispatch_kernel(dest_ref, padstart_ref, padn_ref, h_ref, x1_ref, mod_ref, wsg_ref, wsu_ref, wsd_ref,
                     xs_ref, xmid_ref, zero_ref, sem, zsem):
    i = pl.program_id(0)
    tm = h_ref.shape[0]
    base = i * tm

    @pl.when(i == 0)
    def _():
        zero_ref[...] = jnp.zeros_like(zero_ref)

        def pieces(e, act):
            n = padn_ref[e]
            start = padstart_ref[e]
            head = jnp.minimum(n, (SUBLANES - start % SUBLANES) % SUBLANES)
            for r in range(SUBLANES - 1):
                cp = pltpu.make_async_copy(zero_ref.at[pl.ds(0, 1), :], xs_ref.at[pl.ds(start + r, 1), :], zsem)
                pl.when(r < head)(functools.partial(act, cp))
            start = start + head
            n = n - head
            for bit in _pow2_pieces():
                cp = pltpu.make_async_copy(zero_ref.at[pl.ds(0, bit), :],
                                           xs_ref.at[pl.ds(pl.multiple_of(start, SUBLANES), bit), :], zsem)
                pl.when((n & bit) != 0)(functools.partial(act, cp))
                start = start + (n & bit)

        def zstart(e, c):
            pieces(e, lambda cp: cp.start())
            return c

        def zwait(e, c):
            pieces(e, lambda cp: cp.wait())
            return c

        lax.fori_loop(0, N_EXPERTS, zstart, 0)
        lax.fori_loop(0, N_EXPERTS, zwait, 0)

    for t in range(tm):
        src = h_ref.at[pl.ds(t, 1), :]
        for k in range(TOP_K):
            pltpu.make_async_copy(src, xs_ref.at[pl.ds(dest_ref[k, base + t], 1), :],
                                  sem).start(priority=k % 2)

    h = h_ref[...].astype(_BF16)
    a = jnp.dot(h, wsg_ref[...], preferred_element_type=_F32)
    b = jnp.dot(h, wsu_ref[...], preferred_element_type=_F32)
    shared = jnp.dot((_silu(a) * b).astype(_BF16), wsd_ref[...], preferred_element_type=_F32)
    xmid_ref[...] = x1_ref[...] + mod_ref[0][5:6] * shared

    for _ in range(TOP_K):
        pltpu.make_async_copy(h_ref, xs_ref.at[pl.ds(0, tm), :], sem).wait()


def _dispatch(dest, pad_start, pad_n, h2, x1, mod3, wsg_bf, wsu_bf, wsd_bf):
    tm = TM
    tiles_per_batch = SEQ // tm
    tok = pl.BlockSpec((tm, D_MODEL), lambda i, *_: (i, 0))
    const = lambda shape: pl.BlockSpec(shape, lambda i, *_: (0,) * len(shape), pipeline_mode=pl.Buffered(1))
    grid_spec = pltpu.PrefetchScalarGridSpec(
        num_scalar_prefetch=3,
        grid=(N_TOK // tm,),
        in_specs=[tok, tok,
                  pl.BlockSpec((1, N_MOD, D_MODEL), lambda i, *_: (i // tiles_per_batch, 0, 0)),
                  const((D_MODEL, D_SHARED)), const((D_MODEL, D_SHARED)), const((D_SHARED, D_MODEL))],
        out_specs=[pl.BlockSpec(memory_space=pl.ANY), tok],
        scratch_shapes=[pltpu.VMEM((EXPERT_BM // 2, D_MODEL), _F32),
                        pltpu.SemaphoreType.DMA(()), pltpu.SemaphoreType.DMA(())],
    )
    return pl.pallas_call(
        _dispatch_kernel,
        grid_spec=grid_spec,
        out_shape=[jax.ShapeDtypeStruct((N_ROWS, D_MODEL), _F32),
                   jax.ShapeDtypeStruct((N_TOK, D_MODEL), _F32)],
        compiler_params=_cparams("arbitrary"),
        name="dispatch",
    )(dest, pad_start, pad_n, h2, x1, mod3, wsg_bf, wsu_bf, wsd_bf)


def _experts_kernel(be_ref, nact_ref, first_ref, slot_ref, nxt_ref, xs_ref, wg_hbm, wu_hbm, wd_hbm, ys_ref,
                    wg_f, wu_f, wd_f, wg_bf, wu_bf, wd_bf, sem):
    i = pl.program_id(0)

    def fetch(e, s):
        return (pltpu.make_async_copy(wg_hbm.at[e], wg_f.at[s], sem.at[s, 0]),
                pltpu.make_async_copy(wu_hbm.at[e], wu_f.at[s], sem.at[s, 1]),
                pltpu.make_async_copy(wd_hbm.at[e], wd_f.at[s], sem.at[s, 2]))

    @pl.when(i < nact_ref[0])
    def _():
        @pl.when(i == 0)
        def _():
            for cp in fetch(be_ref[0], 0):
                cp.start()

        @pl.when(first_ref[i] == 1)
        def _():
            s = slot_ref[i]
            for cp in fetch(be_ref[i], s):
                cp.wait()

            @pl.when(nxt_ref[i] >= 0)
            def _():
                for cp in fetch(nxt_ref[i], 1 - s):
                    cp.start()

            wg_bf[...] = wg_f[s].astype(_BF16)
            wu_bf[...] = wu_f[s].astype(_BF16)
            wd_bf[...] = wd_f[s].astype(_BF16)

        x = xs_ref[...].astype(_BF16)
        a = jnp.dot(x, wg_bf[...], preferred_element_type=_F32)
        b = jnp.dot(x, wu_bf[...], preferred_element_type=_F32)
        ys_ref[...] = jnp.dot((_silu(a) * b).astype(_BF16), wd_bf[...], preferred_element_type=_F32)


def _experts(block_e, n_active, first, slot, nxt, xs, w_gate, w_up, w_down):
    bm = EXPERT_BM
    row_blk = lambda i, be, na, *_: (jnp.minimum(i, na[0] - 1), 0)
    hbm = pl.BlockSpec(memory_space=pl.ANY)
    grid_spec = pltpu.PrefetchScalarGridSpec(
        num_scalar_prefetch=5,
        grid=(N_BLOCKS,),
        in_specs=[pl.BlockSpec((bm, D_MODEL), row_blk), hbm, hbm, hbm],
        out_specs=pl.BlockSpec((bm, D_MODEL), row_blk),
        scratch_shapes=[pltpu.VMEM((2, D_MODEL, D_EXPERT), _F32), pltpu.VMEM((2, D_MODEL, D_EXPERT), _F32),
                        pltpu.VMEM((2, D_EXPERT, D_MODEL), _F32),
                        pltpu.VMEM((D_MODEL, D_EXPERT), _BF16), pltpu.VMEM((D_MODEL, D_EXPERT), _BF16),
                        pltpu.VMEM((D_EXPERT, D_MODEL), _BF16),
                        pltpu.SemaphoreType.DMA((2, 3))],
    )
    return pl.pallas_call(
        _experts_kernel,
        grid_spec=grid_spec,
        out_shape=jax.ShapeDtypeStruct((N_ROWS, D_MODEL), _F32),
        compiler_params=_cparams("arbitrary"),
        name="experts",
    )(block_e, n_active, first, slot, nxt, xs, w_gate, w_up, w_down)


def _final_kernel(dest_ref, xmid_ref, wt_ref, mod_ref, fg_ref, ys_ref, o_ref, buf_ref, sem):
    i = pl.program_id(0)
    n_tiles = pl.num_programs(0)
    tm = xmid_ref.shape[0]

    def gather(tile, slot):
        base = tile * tm
        for t in range(tm):
            for k in range(TOP_K):
                pltpu.make_async_copy(ys_ref.at[pl.ds(dest_ref[k, base + t], 1), :],
                                      buf_ref.at[slot, pl.ds(k * tm + t, 1), :],
                                      sem.at[slot]).start(priority=k % 2)

    def drain(slot):
        for k in range(TOP_K):
            pltpu.make_async_copy(ys_ref.at[pl.ds(0, tm), :], buf_ref.at[slot, pl.ds(k * tm, tm), :],
                                  sem.at[slot]).wait()

    @pl.when(i == 0)
    def _():
        gather(0, 0)

    slot = i % 2
    drain(slot)
    gather(jnp.minimum(i + 1, n_tiles - 1), 1 - slot)

    wt = wt_ref[...]
    routed = wt[:, 0:1] * buf_ref[slot, 0:tm, :]
    for k in range(1, TOP_K):
        routed = routed + wt[:, k:k + 1] * buf_ref[slot, k * tm:(k + 1) * tm, :]
    y = xmid_ref[...] + mod_ref[0][5:6] * routed
    o_ref[...] = y * _rms_scale(y) * fg_ref[...]

    @pl.when(i == n_tiles - 1)
    def _():
        drain(1 - slot)


def _final(dest, xmid, w_tok, mod3, final_g, ys):
    tm = TM_COMBINE
    tiles_per_batch = SEQ // tm
    tok = lambda w: pl.BlockSpec((tm, w), lambda i, *_: (i, 0))
    grid_spec = pltpu.PrefetchScalarGridSpec(
        num_scalar_prefetch=1,
        grid=(N_TOK // tm,),
        in_specs=[tok(D_MODEL), tok(TOP_K),
                  pl.BlockSpec((1, N_MOD, D_MODEL), lambda i, *_: (i // tiles_per_batch, 0, 0)),
                  pl.BlockSpec((1, D_MODEL), lambda i, *_: (0, 0), pipeline_mode=pl.Buffered(1)),
                  pl.BlockSpec(memory_space=pl.ANY)],
        out_specs=tok(D_MODEL),
        scratch_shapes=[pltpu.VMEM((2, TOP_K * tm, D_MODEL), _F32), pltpu.SemaphoreType.DMA((2,))],
    )
    return pl.pallas_call(
        _final_kernel,
        grid_spec=grid_spec,
        out_shape=jax.ShapeDtypeStruct((N_TOK, D_MODEL), _F32),
        compiler_params=_cparams("arbitrary"),
        name="final",
    )(dest, xmid, w_tok, mod3, final_g, ys)


def _s5_param_layouts(lam_re, lam_im, b_re, b_im, c_re, c_im, log_dt):
    two = lambda a: jnp.concatenate([a, a], axis=-1)
    lr, li = two(lam_re), two(lam_im)
    dtb = jnp.broadcast_to(log_dt.astype(_F32)[:, :, None], lr.shape)
    zeros = jnp.zeros_like(lr)
    prow = jnp.stack([lr, li, dtb] + [zeros] * 5, axis=2)
    ct2 = jnp.concatenate([jnp.swapaxes(c_re, -1, -2), jnp.swapaxes(c_im, -1, -2)], axis=2)
    bt1 = jnp.concatenate([jnp.swapaxes(b_re, -1, -2), jnp.swapaxes(b_im, -1, -2)], axis=3)
    return prow, ct2, bt1


def kernel(x, c, ctx, c_ctx, norm1_g, norm2_g, w_ada, b_ada, w_in, ssm_lam_re, ssm_lam_im, ssm_b_re, ssm_b_im, ssm_c_re, ssm_c_im, ssm_log_dt, ssm_d, ssm_w_glu, conv_w, conv_b, mix_norm_g, w_out, router_w, router_bias, exp_w_gate, exp_w_up, exp_w_down, shared_w_gate, shared_w_up, shared_w_down, final_g):
    layer = 0
    x2 = x.reshape(N_TOK, D_MODEL)
    ctx2 = ctx.reshape(N_CTX_TOK, D_MODEL)

    c8 = jnp.concatenate([c, c_ctx[None, :], jnp.zeros((8 - BATCH - 1, D_MODEL), _F32)], axis=0)
    mod = _ada_mod(c8, w_ada[layer], b_ada[layer][None, :])
    mod3 = mod.reshape(8, N_MOD, D_MODEL)

    w_in_bf = w_in[layer].astype(_BF16)
    conv_w8 = jnp.concatenate([conv_w[layer], jnp.zeros((8 - conv_w.shape[1], D_CONV), _F32)], axis=0)
    u_x, conv_x = _inproj(x2, mod3, norm1_g[layer][None, :], w_in_bf, conv_w8, conv_b[layer][None, :])
    u_c = _ctxproj(ctx2, mod3, norm1_g[layer][None, :], w_in_bf[:, :D_SSM])

    prow, ct2, bt1 = _s5_param_layouts(ssm_lam_re[layer], ssm_lam_im[layer], ssm_b_re[layer],
                                       ssm_b_im[layer], ssm_c_re[layer], ssm_c_im[layer], ssm_log_dt[layer])
    toep, bpow, cpow, dec = _s5_params(prow, ct2, bt1)
    ys = _s5_scan(u_x, u_c, toep, bpow, cpow, dec)

    rw = router_w[layer]
    rw_hi = rw.astype(_BF16)
    rw_lo = (rw - rw_hi.astype(_F32)).astype(_BF16)
    rwt = jnp.concatenate([rw_hi.T, rw_lo.T], axis=0)
    x1, h2, logits_t = _merge(ys, u_x, conv_x, x2, mod3, ssm_d[layer][None, :],
                              mix_norm_g[layer][None, :], norm2_g[layer][None, :],
                              ssm_w_glu[layer].astype(_BF16), w_out[layer].astype(_BF16), rwt)

    bias_col = jnp.broadcast_to(router_bias[layer][:, None], (N_EXPERTS, 128))
    eid, w_k, rank, cnt = _route(logits_t, bias_col)

    counts = cnt[:, 0].astype(jnp.int32)
    padded = (counts + EXPERT_BM - 1) // EXPERT_BM * EXPERT_BM
    pend = jnp.cumsum(padded)
    pstart = pend - padded
    is_e = eid[:, :, None] == jnp.arange(N_EXPERTS, dtype=jnp.int32)
    dest = jnp.sum(jnp.where(is_e, pstart, 0), axis=-1) + rank
    n_active = (pend[-1] // EXPERT_BM).astype(jnp.int32)
    blk = jnp.minimum(jnp.arange(N_BLOCKS, dtype=jnp.int32), n_active - 1)
    ends_before = (pend[None, :] <= (blk * EXPERT_BM)[:, None]).astype(jnp.int32)
    block_e = jnp.minimum(jnp.sum(ends_before, axis=1), N_EXPERTS - 1)

    xs, xmid = _dispatch(dest, pstart + counts, padded - counts, h2, x1, mod3,
                         shared_w_gate[layer].astype(_BF16), shared_w_up[layer].astype(_BF16),
                         shared_w_down[layer].astype(_BF16))
    first = jnp.concatenate([jnp.ones((1,), jnp.int32), (block_e[1:] != block_e[:-1]).astype(jnp.int32)])
    slot = (jnp.cumsum(first) - 1) % 2
    e_ids = jnp.arange(N_EXPERTS, dtype=jnp.int32)
    owner = jnp.where(padded > 0, e_ids, N_EXPERTS)
    later = jnp.min(jnp.where(e_ids[None, :] > e_ids[:, None], owner[None, :], N_EXPERTS), axis=1)
    nxt_e = jnp.where(later == N_EXPERTS, -1, later)
    nxt = jnp.sum(jnp.where(block_e[:, None] == e_ids[None, :], nxt_e[None, :], 0), axis=1)
    ys_rows = _experts(block_e, n_active[None], first, slot, nxt, xs,
                       exp_w_gate[layer], exp_w_up[layer], exp_w_down[layer])
    out = _final(dest, xmid, w_k.T, mod3, final_g[None, :], ys_rows)
    return out.reshape(BATCH, SEQ, D_MODEL)
```

```python
import functools
import math

import jax
import jax.numpy as jnp
from jax import lax
from jax.experimental import pallas as pl
from jax.experimental.pallas import tpu as pltpu

D_MODEL = 2048
BATCH = 4
SEQ = 2048
CTX_LEN = 256
GRID_W = 64
D_SSM = 1024
D_CONV = 1024
SSM_GROUP = 16
SSM_GROUPS = 64
SSM_STATE = 64
N_EXPERTS = 64
N_EXPERT_GROUPS = 8
GROUP_SIZE = N_EXPERTS // N_EXPERT_GROUPS
TOPK_GROUPS = 4
TOP_K = 8
D_EXPERT = 512
D_SHARED = 512
ROUTED_SCALE = 2.5
N_MOD = 6
EPS = 1e-6

N_TOK = BATCH * SEQ
N_CTX_TOK = BATCH * CTX_LEN

CHUNK = 16
CHUNK_W = CHUNK * SSM_GROUP
STATE_W = 2 * SSM_STATE
N_CTX_CHUNKS = CTX_LEN // CHUNK
N_X_CHUNKS = SEQ // CHUNK
N_SEQ_CHUNKS = N_CTX_CHUNKS + N_X_CHUNKS
SCAN_ROWS = N_SEQ_CHUNKS * BATCH
S5_GROUPS_PER_STEP = 8

TM = 256
EXPERT_BM = 256
N_ASSIGN = N_TOK * TOP_K
N_BLOCKS = N_ASSIGN // EXPERT_BM + N_EXPERTS
N_ROWS = N_BLOCKS * EXPERT_BM
TM_COMBINE = 128
VMEM_LIMIT = 56 * 1024 * 1024

_F32 = jnp.float32
_BF16 = jnp.bfloat16


def _cparams(*sem):
    return pltpu.CompilerParams(dimension_semantics=sem, vmem_limit_bytes=VMEM_LIMIT)


def _const_spec(shape):
    nd = len(shape)
    return pl.BlockSpec(shape, lambda *_: (0,) * nd, pipeline_mode=pl.Buffered(1))


def _rms_scale(xf):
    return lax.rsqrt(jnp.mean(xf * xf, axis=-1, keepdims=True) + EPS)


def _silu(x):
    return x * jax.nn.sigmoid(x)


def _ada_kernel(c_ref, w_ref, b_ref, o_ref):
    s = _silu(c_ref[...])
    o_ref[...] = jnp.dot(s, w_ref[...], preferred_element_type=_F32) + b_ref[...]


def _ada_mod(c8, w_ada, b_ada):
    n = w_ada.shape[1]
    tn = 1024
    return pl.pallas_call(
        _ada_kernel,
        grid=(n // tn,),
        in_specs=[pl.BlockSpec((8, D_MODEL), lambda j: (0, 0)),
                  pl.BlockSpec((D_MODEL, tn), lambda j: (0, j)),
                  pl.BlockSpec((1, tn), lambda j: (0, j))],
        out_specs=pl.BlockSpec((8, tn), lambda j: (0, j)),
        out_shape=jax.ShapeDtypeStruct((8, n), _F32),
        compiler_params=_cparams("arbitrary"),
        name="ada_mod",
    )(c8, w_ada, b_ada)


def _modulated_norm(x_ref, mod_ref, g_ref, shift_row, scale_row):
    xf = x_ref[...]
    m = mod_ref[0]
    h = xf * _rms_scale(xf) * g_ref[...]
    return h * (1.0 + m[scale_row:scale_row + 1]) + m[shift_row:shift_row + 1]


def _inproj_kernel(x_ref, mod_ref, g_ref, w_ref, cw_ref, cb_ref, u_ref, conv_ref):
    h = _modulated_norm(x_ref, mod_ref, g_ref, 0, 1).astype(_BF16)
    u_ref[...] = jnp.dot(h, w_ref[:, 0:D_SSM], preferred_element_type=_F32)
    tm = x_ref.shape[0]
    pos = lax.broadcasted_iota(jnp.int32, (tm, 1), 0) % GRID_W
    not_first = (pos != 0).astype(_F32)
    not_last = (pos != GRID_W - 1).astype(_F32)
    cw = cw_ref[...]
    nc = 256
    for j in range(D_CONV // nc):
        lo = j * nc
        bg = jnp.dot(h, w_ref[:, D_SSM + lo:D_SSM + lo + nc], preferred_element_type=_F32)
        cg = jnp.dot(h, w_ref[:, D_SSM + D_CONV + lo:D_SSM + D_CONV + lo + nc], preferred_element_type=_F32)
        v = jnp.dot(h, w_ref[:, D_SSM + 2 * D_CONV + lo:D_SSM + 2 * D_CONV + lo + nc],
                    preferred_element_type=_F32)
        z = cg * v
        z_prev = pltpu.roll(z, 1, axis=0) * not_first
        z_next = pltpu.roll(z, tm - 1, axis=0) * not_last
        y = (cb_ref[:, lo:lo + nc] + z_prev * cw[0:1, lo:lo + nc] + z * cw[1:2, lo:lo + nc]
             + z_next * cw[2:3, lo:lo + nc])
        conv_ref[:, lo:lo + nc] = bg * y


def _inproj(x2, mod3, norm_g, w_in_bf, conv_w, conv_b):
    d_in = w_in_bf.shape[1]
    tiles_per_batch = SEQ // TM
    return pl.pallas_call(
        _inproj_kernel,
        grid=(N_TOK // TM,),
        in_specs=[pl.BlockSpec((TM, D_MODEL), lambda i: (i, 0)),
                  pl.BlockSpec((1, N_MOD, D_MODEL), lambda i: (i // tiles_per_batch, 0, 0)),
                  _const_spec((1, D_MODEL)),
                  _const_spec((D_MODEL, d_in)),
                  _const_spec((8, D_CONV)),
                  _const_spec((1, D_CONV))],
        out_specs=[pl.BlockSpec((TM, D_SSM), lambda i: (i, 0)),
                   pl.BlockSpec((TM, D_CONV), lambda i: (i, 0))],
        out_shape=[jax.ShapeDtypeStruct((N_TOK, D_SSM), _F32),
                   jax.ShapeDtypeStruct((N_TOK, D_CONV), _F32)],
        compiler_params=_cparams("arbitrary"),
        name="in_proj",
    )(x2, mod3, norm_g, w_in_bf, conv_w, conv_b)


def _ctxproj_kernel(x_ref, mod_ref, g_ref, w_ref, u_ref):
    h = _modulated_norm(x_ref, mod_ref, g_ref, 0, 1).astype(_BF16)
    u_ref[...] = jnp.dot(h, w_ref[...], preferred_element_type=_F32)


def _ctxproj(ctx2, mod3, norm_g, w_u_bf):
    return pl.pallas_call(
        _ctxproj_kernel,
        grid=(N_CTX_TOK // TM,),
        in_specs=[pl.BlockSpec((TM, D_MODEL), lambda i: (i, 0)),
                  pl.BlockSpec((1, N_MOD, D_MODEL), lambda i: (BATCH, 0, 0)),
                  _const_spec((1, D_MODEL)),
                  _const_spec((D_MODEL, D_SSM))],
        out_specs=pl.BlockSpec((TM, D_SSM), lambda i: (i, 0)),
        out_shape=jax.ShapeDtypeStruct((N_CTX_TOK, D_SSM), _F32),
        compiler_params=_cparams("arbitrary"),
        name="ctx_proj",
    )(ctx2, mod3, norm_g, w_u_bf)


N_POW = 32


def _s5_param_kernel(prow_ref, ct_ref, bt_ref, toep_ref, bpow_ref, cpow_ref, dec_ref):
    lane = lambda shape: lax.broadcasted_iota(jnp.int32, shape, 1)
    sub = lambda shape: lax.broadcasted_iota(jnp.int32, shape, 0)

    def split2(a):
        a_hi = a.astype(_BF16)
        return a_hi, (a - a_hi.astype(_F32)).astype(_BF16)

    rep = (lane((SSM_GROUP, CHUNK_W)) % SSM_GROUP == sub((SSM_GROUP, CHUNK_W))).astype(_BF16)
    sgn_col = jnp.where(sub((STATE_W, 1)) < SSM_STATE, 1.0, -1.0).astype(_F32)
    sgn_row = jnp.where(lane((1, STATE_W)) < SSM_STATE, -1.0, 1.0).astype(_F32)
    tau_col = sub((N_POW, 1)).astype(_F32)
    blk_l = lane((N_POW, CHUNK_W)) // SSM_GROUP
    tau_s = sub((N_POW, CHUNK_W))
    blk_r = sub((CHUNK_W, N_POW)) // SSM_GROUP
    tau_l = lane((CHUNK_W, N_POW))
    pick_l = lambda e: (tau_s == e).astype(_BF16)
    pick_r = lambda e: (tau_l == e).astype(_BF16)
    tn = (((0,), (0,)), ((), ()))

    strips = []
    for d in range(2):
        pr = prow_ref[d, 0]
        lam_re, lam_im, dt = pr[0:1], pr[1:2], jnp.exp(pr[2:3])
        mag = jnp.exp(tau_col * (lam_re * dt))
        ang = tau_col * (lam_im * dt)
        pw_re = mag * jnp.cos(ang)
        pw_im = mag * jnp.sin(ang)
        pw4 = jnp.concatenate(split2(pw_re) + split2(pw_im), axis=1)

        c_hi, c_lo = split2(ct_ref[d, 0])
        ct = (jnp.dot(c_hi, rep, preferred_element_type=_F32)
              + jnp.dot(c_lo, rep, preferred_element_type=_F32))
        ca = ct * sgn_col
        cb = -pltpu.roll(ct, SSM_STATE, axis=0)

        def cpow(sel):
            o = lax.dot_general(pw4, sel, tn, preferred_element_type=_F32)
            x = o[0:STATE_W] + o[STATE_W:2 * STATE_W]
            y = o[2 * STATE_W:3 * STATE_W] + o[3 * STATE_W:4 * STATE_W]
            return x * ca + y * cb

        nr = pw_re[1:2] - 1.0
        ni = pw_im[1:2]
        den = lam_re * lam_re + lam_im * lam_im
        kr = (nr * lam_re + ni * lam_im) / den
        ki = (ni * lam_re - nr * lam_im) / den
        b1 = bt_ref[d, 0]
        b2 = pltpu.roll(b1, SSM_STATE, axis=1) * sgn_row
        u1 = kr * b1 + ki * b2
        u2 = kr * b2 - ki * b1
        u1t = jnp.concatenate([u1] * CHUNK, axis=0)
        u2t = jnp.concatenate([u2] * CHUNK, axis=0)

        def bpow(sel):
            o = jnp.dot(sel, pw4, preferred_element_type=_F32)
            x = o[:, 0:STATE_W] + o[:, STATE_W:2 * STATE_W]
            y = o[:, 2 * STATE_W:3 * STATE_W] + o[:, 3 * STATE_W:4 * STATE_W]
            return x * u1t + y * u2t

        if d == 0:
            e_strip, e_b, e_c = blk_l, (CHUNK - 1) - blk_r, blk_l + 1
        else:
            e_strip, e_b, e_c = (CHUNK - 1) - blk_l, blk_r, CHUNK - blk_l
        u_hi, u_lo = split2(u1)
        k_hi, k_lo = split2(cpow(pick_l(e_strip)))
        strips.append(jnp.dot(u_hi, k_hi, preferred_element_type=_F32)
                      + jnp.dot(u_hi, k_lo, preferred_element_type=_F32)
                      + jnp.dot(u_lo, k_hi, preferred_element_type=_F32))
        bp = bpow(pick_r(e_b))
        bpow_ref[0, :, d * STATE_W:(d + 1) * STATE_W] = bp.astype(_BF16)
        bpow_ref[0, :, (2 + d) * STATE_W:(3 + d) * STATE_W] = pltpu.roll(bp, SSM_STATE, axis=1).astype(_BF16)
        cpow_ref[0, d * STATE_W:(d + 1) * STATE_W, :] = cpow(pick_l(e_c)).astype(_BF16)

        dec_ref[0, 2 * d:2 * d + 1, :] = pw_re[CHUNK:CHUNK + 1]
        dec_ref[0, 2 * d + 1:2 * d + 2, :] = pw_im[CHUNK:CHUNK + 1] * sgn_row

    zeros = jnp.zeros((SSM_GROUP, CHUNK_W), _F32)
    strip = (jnp.concatenate([strips[1], zeros], axis=1)
             + pltpu.roll(jnp.concatenate([strips[0], zeros], axis=1), CHUNK_W - SSM_GROUP, axis=1))
    for i in range(CHUNK):
        off = (CHUNK - 1 - i) * SSM_GROUP
        win = strip if off == 0 else pltpu.roll(strip, 2 * CHUNK_W - off, axis=1)
        toep_ref[0, i * SSM_GROUP:(i + 1) * SSM_GROUP, :] = win[:, 0:CHUNK_W].astype(_BF16)


def _s5_params(prow, ct2, bt1):
    g_spec = lambda shape: pl.BlockSpec(shape, lambda g: (0, g, 0, 0))
    o_spec = lambda shape: pl.BlockSpec(shape, lambda g: (g, 0, 0))
    return pl.pallas_call(
        _s5_param_kernel,
        grid=(SSM_GROUPS,),
        in_specs=[g_spec((2, 1, 8, STATE_W)),
                  g_spec((2, 1, STATE_W, SSM_GROUP)), g_spec((2, 1, SSM_GROUP, STATE_W))],
        out_specs=[o_spec((1, CHUNK_W, CHUNK_W)), o_spec((1, CHUNK_W, 4 * STATE_W)),
                   o_spec((1, 2 * STATE_W, CHUNK_W)), o_spec((1, 4, STATE_W))],
        out_shape=[jax.ShapeDtypeStruct((SSM_GROUPS, CHUNK_W, CHUNK_W), _BF16),
                   jax.ShapeDtypeStruct((SSM_GROUPS, CHUNK_W, 4 * STATE_W), _BF16),
                   jax.ShapeDtypeStruct((SSM_GROUPS, 2 * STATE_W, CHUNK_W), _BF16),
                   jax.ShapeDtypeStruct((SSM_GROUPS, 4, STATE_W), _F32)],
        compiler_params=_cparams("arbitrary"),
        name="s5_params",
    )(prow, ct2, bt1)


def _s5_kernel(ux_ref, uc_ref, toep_ref, bpow_ref, cpow_ref, dec_ref, y_ref, u_ref, s_ref, h_ref, y2_ref):
    ng = S5_GROUPS_PER_STEP
    gw = SSM_GROUP
    x0 = N_CTX_CHUNKS * BATCH

    half = CHUNK // 2
    assert half == ng
    lanes = ng * 128

    r = lax.broadcasted_iota(jnp.int32, (lanes, 1), 0)
    swapped = ((r // gw) % ng) * 128 + (r // 128) * gw + r % gw
    perm = (lax.broadcasted_iota(jnp.int32, (lanes, lanes), 1) == swapped).astype(_BF16)

    def to_chunk_layout(src_ref, seq_len, n_chunks, row0):
        for hf in range(2):
            p = jnp.concatenate(
                [jnp.concatenate([src_ref[pl.ds(b * seq_len + hf * half + il, n_chunks, stride=CHUNK), :]
                                  for il in range(half)], axis=1) for b in range(BATCH)], axis=0).astype(_BF16)
            q = jnp.dot(p, perm, preferred_element_type=_F32)
            for b in range(BATCH):
                for g in range(ng):
                    u_ref[g, hf, pl.ds(row0 + b, n_chunks, stride=BATCH), :] = (
                        q[b * n_chunks:(b + 1) * n_chunks, g * 128:(g + 1) * 128])

    to_chunk_layout(uc_ref, CTX_LEN, N_CTX_CHUNKS, 0)
    to_chunk_layout(ux_ref, SEQ, N_X_CHUNKS, x0)

    chunk_rows = lambda ref, g, lo: jnp.concatenate([ref[g, 0, lo:, :], ref[g, 1, lo:, :]], axis=1)
    for g in range(ng):
        s_ref[g] = jnp.dot(chunk_rows(u_ref, g, 0).astype(_BF16), bpow_ref[g], preferred_element_type=_F32)

    pair = 2 * BATCH
    n_ctx_pairs = N_CTX_CHUNKS // 2
    n_pairs = N_SEQ_CHUNKS // 2
    lower = lax.broadcasted_iota(jnp.int32, (pair, STATE_W), 0) < BATCH
    zero = jnp.zeros((pair, STATE_W), _F32)

    def step(t, carry):
        tb = jnp.where(t < n_ctx_pairs, n_ctx_pairs - 1 - t, n_pairs + n_ctx_pairs - 1 - t)
        rf = pl.multiple_of(t * pair, pair)
        rb = pl.multiple_of(tb * pair, pair)
        out = []
        swap = lambda v: pltpu.roll(v, BATCH, axis=0)
        for g in range(ng):
            hf, hfs, hb, hbs = carry[4 * g:4 * g + 4]
            dg = dec_ref[g]

            def advance(h, hs, a_re, a_im, s, ss):
                return a_re * h + a_im * hs + s, a_re * hs - a_im * h + ss

            col = lambda r, k: s_ref[g, pl.ds(r, pair), k * STATE_W:(k + 1) * STATE_W]
            sf, sfs = col(rf, 0), col(rf, 2)
            mid, mids = advance(hf, hfs, dg[0:1], dg[1:2], jnp.where(lower, sf, swap(sf)),
                                jnp.where(lower, sfs, swap(sfs)))
            h_ref[g, pl.ds(rf, pair), 0:STATE_W] = jnp.where(lower, hf, mid)
            end, ends = advance(mid, mids, dg[0:1], dg[1:2], sf, sfs)
            hf, hfs = jnp.where(lower, swap(end), end), jnp.where(lower, swap(ends), ends)

            sb, sbs = col(rb, 1), col(rb, 3)
            mid, mids = advance(hb, hbs, dg[2:3], dg[3:4], jnp.where(lower, swap(sb), sb),
                                jnp.where(lower, swap(sbs), sbs))
            h_ref[g, pl.ds(rb, pair), STATE_W:2 * STATE_W] = jnp.where(lower, mid, hb)
            end, ends = advance(mid, mids, dg[2:3], dg[3:4], sb, sbs)
            hb, hbs = jnp.where(lower, end, swap(end)), jnp.where(lower, ends, swap(ends))
            out += [hf, hfs, hb, hbs]
        return tuple(out)

    lax.fori_loop(0, n_pairs, step, (zero,) * (4 * ng))

    for g in range(ng):
        ux = chunk_rows(u_ref, g, x0).astype(_BF16)
        hx = h_ref[g, x0:SCAN_ROWS, :].astype(_BF16)
        y2 = (jnp.dot(ux, toep_ref[g], preferred_element_type=_F32)
              + jnp.dot(hx, cpow_ref[g], preferred_element_type=_F32))
        y2_ref[g, 0] = y2[:, 0:128]
        y2_ref[g, 1] = y2[:, 128:256]

    for hf in range(2):
        yin = jnp.concatenate(
            [jnp.concatenate([y2_ref[g, hf, pl.ds(b, N_X_CHUNKS, stride=BATCH), :] for g in range(ng)], axis=1)
             for b in range(BATCH)], axis=0)
        y_hi = yin.astype(_BF16)
        y_lo = (yin - y_hi.astype(_F32)).astype(_BF16)
        q = (jnp.dot(y_hi, perm, preferred_element_type=_F32)
             + jnp.dot(y_lo, perm, preferred_element_type=_F32))
        for b in range(BATCH):
            for jl in range(half):
                y_ref[pl.ds(b * SEQ + hf * half + jl, N_X_CHUNKS, stride=CHUNK), :] = (
                    q[b * N_X_CHUNKS:(b + 1) * N_X_CHUNKS, jl * 128:(jl + 1) * 128])


def _s5_scan(u_x, u_c, toep, bpow, cpow, dec):
    ng = S5_GROUPS_PER_STEP
    n_x_rows = N_X_CHUNKS * BATCH
    lanes = ng * SSM_GROUP
    spec = lambda r, c: pl.BlockSpec((ng, r, c), lambda i: (i, 0, 0))
    col = lambda rows: pl.BlockSpec((rows, lanes), lambda i: (0, i))
    return pl.pallas_call(
        _s5_kernel,
        grid=(SSM_GROUPS // ng,),
        in_specs=[col(N_TOK), col(N_CTX_TOK), spec(CHUNK_W, CHUNK_W), spec(CHUNK_W, 4 * STATE_W),
                  spec(2 * STATE_W, CHUNK_W), spec(4, STATE_W)],
        out_specs=col(N_TOK),
        out_shape=jax.ShapeDtypeStruct((N_TOK, D_SSM), _F32),
        scratch_shapes=[pltpu.VMEM((ng, 2, SCAN_ROWS, 128), _F32),
                        pltpu.VMEM((ng, SCAN_ROWS, 4 * STATE_W), _F32),
                        pltpu.VMEM((ng, SCAN_ROWS, 2 * STATE_W), _F32),
                        pltpu.VMEM((ng, 2, n_x_rows, 128), _F32)],
        compiler_params=_cparams("arbitrary"),
        name="s5_scan",
    )(u_x, u_c, toep, bpow, cpow, dec)


def _merge_kernel(ys_ref, u_ref, conv_ref, x_ref, mod_ref, dskip_ref, mixg_ref, n2g_ref,
                  wglu_ref, wout_ref, rwt_ref, x1_ref, h2_ref, logit_ref):
    m = mod_ref[0]
    yx = dskip_ref[...] * u_ref[...] + ys_ref[...]
    c0 = math.sqrt(2.0 / math.pi)
    ge = 0.5 * yx * (1.0 + jnp.tanh(c0 * (yx + 0.044715 * (yx * yx * yx))))
    z = jnp.dot(ge.astype(_BF16), wglu_ref[...], preferred_element_type=_F32)
    ssm_y = z[:, 0:D_SSM] * jax.nn.sigmoid(z[:, D_SSM:2 * D_SSM])
    conv_y = conv_ref[...]
    mixg = mixg_ref[...]
    heads_a = (ssm_y * _rms_scale(ssm_y) * mixg[:, 0:D_SSM]).astype(_BF16)
    heads_b = (conv_y * _rms_scale(conv_y) * mixg[:, D_SSM:]).astype(_BF16)
    mix = (jnp.dot(heads_a, wout_ref[0:D_SSM, :], preferred_element_type=_F32)
           + jnp.dot(heads_b, wout_ref[D_SSM:, :], preferred_element_type=_F32))
    x1 = x_ref[...] + m[2:3] * mix
    x1_ref[...] = x1
    h2 = x1 * _rms_scale(x1) * n2g_ref[...] * (1.0 + m[4:5]) + m[3:4]
    h2_ref[...] = h2
    h_hi = h2.astype(_BF16)
    h_lo = (h2 - h_hi.astype(_F32)).astype(_BF16)
    nt = (((1,), (1,)), ((), ()))
    p = lax.dot_general(rwt_ref[...], h_hi, nt, preferred_element_type=_F32)
    q = lax.dot_general(rwt_ref[0:N_EXPERTS, :], h_lo, nt, preferred_element_type=_F32)
    logit_ref[...] = p[0:N_EXPERTS] + p[N_EXPERTS:2 * N_EXPERTS] + q


def _merge(ys, u, conv, x2, mod3, dskip, mixg, n2g, wglu_bf, wout_bf, rwt):
    tiles_per_batch = SEQ // TM
    tok = lambda w: pl.BlockSpec((TM, w), lambda i: (i, 0))
    return pl.pallas_call(
        _merge_kernel,
        grid=(N_TOK // TM,),
        in_specs=[tok(D_SSM), tok(D_SSM), tok(D_CONV), tok(D_MODEL),
                  pl.BlockSpec((1, N_MOD, D_MODEL), lambda i: (i // tiles_per_batch, 0, 0)),
                  _const_spec((1, D_SSM)), _const_spec((1, D_MODEL)), _const_spec((1, D_MODEL)),
                  _const_spec((D_SSM, 2 * D_SSM)), _const_spec((D_MODEL, D_MODEL)),
                  _const_spec((2 * N_EXPERTS, D_MODEL))],
        out_specs=[tok(D_MODEL), tok(D_MODEL),
                   pl.BlockSpec((N_EXPERTS, TM), lambda i: (0, i))],
        out_shape=[jax.ShapeDtypeStruct((N_TOK, D_MODEL), _F32),
                   jax.ShapeDtypeStruct((N_TOK, D_MODEL), _F32),
                   jax.ShapeDtypeStruct((N_EXPERTS, N_TOK), _F32)],
        compiler_params=_cparams("arbitrary"),
        name="merge_heads",
    )(ys, u, conv, x2, mod3, dskip, mixg, n2g, wglu_bf, wout_bf, rwt)


def _route_kernel(logit_ref, bias_ref, eid_ref, w_ref, rank_ref, cnt_ref, carry_ref):
    @pl.when(pl.program_id(0) == 0)
    def _():
        carry_ref[...] = jnp.zeros_like(carry_ref)

    tm = logit_ref.shape[1]
    neg = jnp.float32(-jnp.inf)
    scores = jax.nn.sigmoid(logit_ref[...])
    biased = scores + bias_ref[:, 0:1]
    sub = lax.broadcasted_iota(jnp.int32, (GROUP_SIZE, tm), 0)
    rows = lambda a, g: a[g * GROUP_SIZE:(g + 1) * GROUP_SIZE]
    ngrp = N_EXPERT_GROUPS

    gscore = []
    for g in range(ngrp):
        bg = rows(biased, g)
        m1 = jnp.max(bg, axis=0, keepdims=True)
        first = jnp.min(jnp.where(bg == m1, sub, GROUP_SIZE), axis=0, keepdims=True)
        m2 = jnp.max(jnp.where(sub == first, neg, bg), axis=0, keepdims=True)
        gscore.append(m1 + m2)
    v = []
    for g in range(ngrp):
        beaten = jnp.zeros((1, tm), jnp.int32)
        for o in range(ngrp):
            if o != g:
                beats = (gscore[o] >= gscore[g]) if o < g else (gscore[o] > gscore[g])
                beaten = beaten + beats.astype(jnp.int32)
        v.append(jnp.where(beaten < TOPK_GROUPS, rows(biased, g), neg))
    eids = [sub + g * GROUP_SIZE for g in range(ngrp)]

    pick_ids, pick_masks = [], []
    for _ in range(TOP_K):
        m = functools.reduce(jnp.maximum, v)
        m = jnp.max(m, axis=0, keepdims=True)
        cand = functools.reduce(jnp.minimum, [jnp.where(v[g] == m, eids[g], N_EXPERTS) for g in range(ngrp)])
        pick_id = jnp.min(cand, axis=0, keepdims=True)
        masks = [eids[g] == pick_id for g in range(ngrp)]
        v = [jnp.where(masks[g], neg, v[g]) for g in range(ngrp)]
        pick_ids.append(pick_id)
        pick_masks.append(masks)

    sel = jnp.concatenate(
        [functools.reduce(jnp.logical_or, [pick_masks[k][g] for k in range(TOP_K)]).astype(_F32)
         for g in range(ngrp)], axis=0)
    before = (lax.broadcasted_iota(jnp.int32, (tm, tm), 0)
              < lax.broadcasted_iota(jnp.int32, (tm, tm), 1)).astype(_BF16)
    base = jnp.dot(sel.astype(_BF16), before, preferred_element_type=_F32) + carry_ref[:, 0:1]

    def gather_pick(a, k):
        parts = [jnp.sum(jnp.where(pick_masks[k][g], rows(a, g), 0.0), axis=0, keepdims=True)
                 for g in range(ngrp)]
        return functools.reduce(jnp.add, parts)

    picked = [gather_pick(scores, k) for k in range(TOP_K)]
    denom = functools.reduce(jnp.add, picked)
    for k in range(TOP_K):
        eid_ref[k:k + 1, :] = pick_ids[k]
        w_ref[k:k + 1, :] = picked[k] / denom * ROUTED_SCALE
        rank_ref[k:k + 1, :] = gather_pick(base, k).astype(jnp.int32)
    carry_ref[...] = carry_ref[...] + jnp.sum(sel, axis=1, keepdims=True)
    cnt_ref[...] = carry_ref[...]


def _route(logits_t, bias_col):
    tok = pl.BlockSpec((TOP_K, TM), lambda i: (0, i))
    return pl.pallas_call(
        _route_kernel,
        grid=(N_TOK // TM,),
        in_specs=[pl.BlockSpec((N_EXPERTS, TM), lambda i: (0, i)), _const_spec((N_EXPERTS, 128))],
        out_specs=[tok, tok, tok, pl.BlockSpec((N_EXPERTS, 128), lambda i: (0, 0))],
        out_shape=[jax.ShapeDtypeStruct((TOP_K, N_TOK), jnp.int32),
                   jax.ShapeDtypeStruct((TOP_K, N_TOK), _F32),
                   jax.ShapeDtypeStruct((TOP_K, N_TOK), jnp.int32),
                   jax.ShapeDtypeStruct((N_EXPERTS, 128), _F32)],
        scratch_shapes=[pltpu.VMEM((N_EXPERTS, 128), _F32)],
        compiler_params=_cparams("arbitrary"),
        name="route",
    )(logits_t, bias_col)


SUBLANES = 8


def _pow2_pieces():
    bit = EXPERT_BM // 2
    while bit >= SUBLANES:
        yield bit
        bit //= 2


def _dispatch_kernel(dest_ref, padstart_ref, padn_ref, h_ref, x1_ref, mod_ref, wsg_ref, wsu_ref, wsd_ref,
                     xs_ref, xmid_ref, zero_ref, sem, zsem):
    i = pl.program_id(0)
    tm = h_ref.shape[0]
    base = i * tm

    @pl.when(i == 0)
    def _():
        zero_ref[...] = jnp.zeros_like(zero_ref)

        def pieces(e, act):
            n = padn_ref[e]
            start = padstart_ref[e]
            head = jnp.minimum(n, (SUBLANES - start % SUBLANES) % SUBLANES)
            for r in range(SUBLANES - 1):
                cp = pltpu.make_async_copy(zero_ref.at[pl.ds(0, 1), :], xs_ref.at[pl.ds(start + r, 1), :], zsem)
                pl.when(r < head)(functools.partial(act, cp))
            start = start + head
            n = n - head
            for bit in _pow2_pieces():
                cp = pltpu.make_async_copy(zero_ref.at[pl.ds(0, bit), :],
                                           xs_ref.at[pl.ds(pl.multiple_of(start, SUBLANES), bit), :], zsem)
                pl.when((n & bit) != 0)(functools.partial(act, cp))
                start = start + (n & bit)

        def zstart(e, c):
            pieces(e, lambda cp: cp.start())
            return c

        def zwait(e, c):
            pieces(e, lambda cp: cp.wait())
            return c

        lax.fori_loop(0, N_EXPERTS, zstart, 0)
        lax.fori_loop(0, N_EXPERTS, zwait, 0)

    for t in range(tm):
        src = h_ref.at[pl.ds(t, 1), :]
        for k in range(TOP_K):
            pltpu.make_async_copy(src, xs_ref.at[pl.ds(dest_ref[k, base + t], 1), :],
                                  sem).start(priority=k % 2)

    h = h_ref[...].astype(_BF16)
    a = jnp.dot(h, wsg_ref[...], preferred_element_type=_F32)
    b = jnp.dot(h, wsu_ref[...], preferred_element_type=_F32)
    shared = jnp.dot((_silu(a) * b).astype(_BF16), wsd_ref[...], preferred_element_type=_F32)
    xmid_ref[...] = x1_ref[...] + mod_ref[0][5:6] * shared

    for _ in range(TOP_K):
        pltpu.make_async_copy(h_ref, xs_ref.at[pl.ds(0, tm), :], sem).wait()


def _dispatch(dest, pad_start, pad_n, h2, x1, mod3, wsg_bf, wsu_bf, wsd_bf):
    tm = TM
    tiles_per_batch = SEQ // tm
    tok = pl.BlockSpec((tm, D_MODEL), lambda i, *_: (i, 0))
    const = lambda shape: pl.BlockSpec(shape, lambda i, *_: (0,) * len(shape), pipeline_mode=pl.Buffered(1))
    grid_spec = pltpu.PrefetchScalarGridSpec(
        num_scalar_prefetch=3,
        grid=(N_TOK // tm,),
        in_specs=[tok, tok,
                  pl.BlockSpec((1, N_MOD, D_MODEL), lambda i, *_: (i // tiles_per_batch, 0, 0)),
                  const((D_MODEL, D_SHARED)), const((D_MODEL, D_SHARED)), const((D_SHARED, D_MODEL))],
        out_specs=[pl.BlockSpec(memory_space=pl.ANY), tok],
        scratch_shapes=[pltpu.VMEM((EXPERT_BM // 2, D_MODEL), _F32),
                        pltpu.SemaphoreType.DMA(()), pltpu.SemaphoreType.DMA(())],
    )
    return pl.pallas_call(
        _dispatch_kernel,
        grid_spec=grid_spec,
        out_shape=[jax.ShapeDtypeStruct((N_ROWS, D_MODEL), _F32),
                   jax.ShapeDtypeStruct((N_TOK, D_MODEL), _F32)],
        compiler_params=_cparams("arbitrary"),
        name="dispatch",
    )(dest, pad_start, pad_n, h2, x1, mod3, wsg_bf, wsu_bf, wsd_bf)


def _experts_kernel(be_ref, nact_ref, first_ref, slot_ref, nxt_ref, xs_ref, wg_hbm, wu_hbm, wd_hbm, ys_ref,
                    wg_f, wu_f, wd_f, wg_bf, wu_bf, wd_bf, sem):
    i = pl.program_id(0)

    def fetch(e, s):
        return (pltpu.make_async_copy(wg_hbm.at[e], wg_f.at[s], sem.at[s, 0]),
                pltpu.make_async_copy(wu_hbm.at[e], wu_f.at[s], sem.at[s, 1]),
                pltpu.make_async_copy(wd_hbm.at[e], wd_f.at[s], sem.at[s, 2]))

    @pl.when(i < nact_ref[0])
    def _():
        @pl.when(i == 0)
        def _():
            for cp in fetch(be_ref[0], 0):
                cp.start()

        @pl.when(first_ref[i] == 1)
        def _():
            s = slot_ref[i]
            for cp in fetch(be_ref[i], s):
                cp.wait()

            @pl.when(nxt_ref[i] >= 0)
            def _():
                for cp in fetch(nxt_ref[i], 1 - s):
                    cp.start()

            wg_bf[...] = wg_f[s].astype(_BF16)
            wu_bf[...] = wu_f[s].astype(_BF16)
            wd_bf[...] = wd_f[s].astype(_BF16)

        x = xs_ref[...].astype(_BF16)
        a = jnp.dot(x, wg_bf[...], preferred_element_type=_F32)
        b = jnp.dot(x, wu_bf[...], preferred_element_type=_F32)
        ys_ref[...] = jnp.dot((_silu(a) * b).astype(_BF16), wd_bf[...], preferred_element_type=_F32)


def _experts(block_e, n_active, first, slot, nxt, xs, w_gate, w_up, w_down):
    bm = EXPERT_BM
    row_blk = lambda i, be, na, *_: (jnp.minimum(i, na[0] - 1), 0)
    hbm = pl.BlockSpec(memory_space=pl.ANY)
    grid_spec = pltpu.PrefetchScalarGridSpec(
        num_scalar_prefetch=5,
        grid=(N_BLOCKS,),
        in_specs=[pl.BlockSpec((bm, D_MODEL), row_blk), hbm, hbm, hbm],
        out_specs=pl.BlockSpec((bm, D_MODEL), row_blk),
        scratch_shapes=[pltpu.VMEM((2, D_MODEL, D_EXPERT), _F32), pltpu.VMEM((2, D_MODEL, D_EXPERT), _F32),
                        pltpu.VMEM((2, D_EXPERT, D_MODEL), _F32),
                        pltpu.VMEM((D_MODEL, D_EXPERT), _BF16), pltpu.VMEM((D_MODEL, D_EXPERT), _BF16),
                        pltpu.VMEM((D_EXPERT, D_MODEL), _BF16),
                        pltpu.SemaphoreType.DMA((2, 3))],
    )
    return pl.pallas_call(
        _experts_kernel,
        grid_spec=grid_spec,
        out_shape=jax.ShapeDtypeStruct((N_ROWS, D_MODEL), _F32),
        compiler_params=_cparams("arbitrary"),
        name="experts",
    )(block_e, n_active, first, slot, nxt, xs, w_gate, w_up, w_down)


def _final_kernel(dest_ref, xmid_ref, wt_ref, mod_ref, fg_ref, ys_ref, o_ref, buf_ref, sem):
    i = pl.program_id(0)
    n_tiles = pl.num_programs(0)
    tm = xmid_ref.shape[0]

    def gather(tile, slot):
        base = tile * tm
        for t in range(tm):
            for k in range(TOP_K):
                pltpu.make_async_copy(ys_ref.at[pl.ds(dest_ref[k, base + t], 1), :],
                                      buf_ref.at[slot, pl.ds(k * tm + t, 1), :],
                                      sem.at[slot]).start(priority=k % 2)

    def drain(slot):
        for k in range(TOP_K):
            pltpu.make_async_copy(ys_ref.at[pl.ds(0, tm), :], buf_ref.at[slot, pl.ds(k * tm, tm), :],
                                  sem.at[slot]).wait()

    @pl.when(i == 0)
    def _():
        gather(0, 0)

    slot = i % 2
    drain(slot)
    gather(jnp.minimum(i + 1, n_tiles - 1), 1 - slot)

    wt = wt_ref[...]
    routed = wt[:, 0:1] * buf_ref[slot, 0:tm, :]
    for k in range(1, TOP_K):
        routed = routed + wt[:, k:k + 1] * buf_ref[slot, k * tm:(k + 1) * tm, :]
    y = xmid_ref[...] + mod_ref[0][5:6] * routed
    o_ref[...] = y * _rms_scale(y) * fg_ref[...]

    @pl.when(i == n_tiles - 1)
    def _():
        drain(1 - slot)


def _final(dest, xmid, w_tok, mod3, final_g, ys):
    tm = TM_COMBINE
    tiles_per_batch = SEQ // tm
    tok = lambda w: pl.BlockSpec((tm, w), lambda i, *_: (i, 0))
    grid_spec = pltpu.PrefetchScalarGridSpec(
        num_scalar_prefetch=1,
        grid=(N_TOK // tm,),
        in_specs=[tok(D_MODEL), tok(TOP_K),
                  pl.BlockSpec((1, N_MOD, D_MODEL), lambda i, *_: (i // tiles_per_batch, 0, 0)),
                  pl.BlockSpec((1, D_MODEL), lambda i, *_: (0, 0), pipeline_mode=pl.Buffered(1)),
                  pl.BlockSpec(memory_space=pl.ANY)],
        out_specs=tok(D_MODEL),
        scratch_shapes=[pltpu.VMEM((2, TOP_K * tm, D_MODEL), _F32), pltpu.SemaphoreType.DMA((2,))],
    )
    return pl.pallas_call(
        _final_kernel,
        grid_spec=grid_spec,
        out_shape=jax.ShapeDtypeStruct((N_TOK, D_MODEL), _F32),
        compiler_params=_cparams("arbitrary"),
        name="final",
    )(dest, xmid, w_tok, mod3, final_g, ys)


def _s5_param_layouts(lam_re, lam_im, b_re, b_im, c_re, c_im, log_dt):
    two = lambda a: jnp.concatenate([a, a], axis=-1)
    lr, li = two(lam_re), two(lam_im)
    dtb = jnp.broadcast_to(log_dt.astype(_F32)[:, :, None], lr.shape)
    zeros = jnp.zeros_like(lr)
    prow = jnp.stack([lr, li, dtb] + [zeros] * 5, axis=2)
    ct2 = jnp.concatenate([jnp.swapaxes(c_re, -1, -2), jnp.swapaxes(c_im, -1, -2)], axis=2)
    bt1 = jnp.concatenate([jnp.swapaxes(b_re, -1, -2), jnp.swapaxes(b_im, -1, -2)], axis=3)
    return prow, ct2, bt1


def kernel(x, c, ctx, c_ctx, norm1_g, norm2_g, w_ada, b_ada, w_in, ssm_lam_re, ssm_lam_im, ssm_b_re, ssm_b_im, ssm_c_re, ssm_c_im, ssm_log_dt, ssm_d, ssm_w_glu, conv_w, conv_b, mix_norm_g, w_out, router_w, router_bias, exp_w_gate, exp_w_up, exp_w_down, shared_w_gate, shared_w_up, shared_w_down, final_g):
    layer = 0
    x2 = x.reshape(N_TOK, D_MODEL)
    ctx2 = ctx.reshape(N_CTX_TOK, D_MODEL)

    c8 = jnp.concatenate([c, c_ctx[None, :], jnp.zeros((8 - BATCH - 1, D_MODEL), _F32)], axis=0)
    mod = _ada_mod(c8, w_ada[layer], b_ada[layer][None, :])
    mod3 = mod.reshape(8, N_MOD, D_MODEL)

    w_in_bf = w_in[layer].astype(_BF16)
    conv_w8 = jnp.concatenate([conv_w[layer], jnp.zeros((8 - conv_w.shape[1], D_CONV), _F32)], axis=0)
    u_x, conv_x = _inproj(x2, mod3, norm1_g[layer][None, :], w_in_bf, conv_w8, conv_b[layer][None, :])
    u_c = _ctxproj(ctx2, mod3, norm1_g[layer][None, :], w_in_bf[:, :D_SSM])

    prow, ct2, bt1 = _s5_param_layouts(ssm_lam_re[layer], ssm_lam_im[layer], ssm_b_re[layer],
                                       ssm_b_im[layer], ssm_c_re[layer], ssm_c_im[layer], ssm_log_dt[layer])
    toep, bpow, cpow, dec = _s5_params(prow, ct2, bt1)
    ys = _s5_scan(u_x, u_c, toep, bpow, cpow, dec)

    rw = router_w[layer]
    rw_hi = rw.astype(_BF16)
    rw_lo = (rw - rw_hi.astype(_F32)).astype(_BF16)
    rwt = jnp.concatenate([rw_hi.T, rw_lo.T], axis=0)
    x1, h2, logits_t = _merge(ys, u_x, conv_x, x2, mod3, ssm_d[layer][None, :],
                              mix_norm_g[layer][None, :], norm2_g[layer][None, :],
                              ssm_w_glu[layer].astype(_BF16), w_out[layer].astype(_BF16), rwt)

    bias_col = jnp.broadcast_to(router_bias[layer][:, None], (N_EXPERTS, 128))
    eid, w_k, rank, cnt = _route(logits_t, bias_col)

    counts = cnt[:, 0].astype(jnp.int32)
    padded = (counts + EXPERT_BM - 1) // EXPERT_BM * EXPERT_BM
    pend = jnp.cumsum(padded)
    pstart = pend - padded
    is_e = eid[:, :, None] == jnp.arange(N_EXPERTS, dtype=jnp.int32)
    dest = jnp.sum(jnp.where(is_e, pstart, 0), axis=-1) + rank
    n_active = (pend[-1] // EXPERT_BM).astype(jnp.int32)
    blk = jnp.minimum(jnp.arange(N_BLOCKS, dtype=jnp.int32), n_active - 1)
    ends_before = (pend[None, :] <= (blk * EXPERT_BM)[:, None]).astype(jnp.int32)
    block_e = jnp.minimum(jnp.sum(ends_before, axis=1), N_EXPERTS - 1)

    xs, xmid = _dispatch(dest, pstart + counts, padded - counts, h2, x1, mod3,
                         shared_w_gate[layer].astype(_BF16), shared_w_up[layer].astype(_BF16),
                         shared_w_down[layer].astype(_BF16))
    first = jnp.concatenate([jnp.ones((1,), jnp.int32), (block_e[1:] != block_e[:-1]).astype(jnp.int32)])
    slot = (jnp.cumsum(first) - 1) % 2
    e_ids = jnp.arange(N_EXPERTS, dtype=jnp.int32)
    owner = jnp.where(padded > 0, e_ids, N_EXPERTS)
    later = jnp.min(jnp.where(e_ids[None, :] > e_ids[:, None], owner[None, :], N_EXPERTS), axis=1)
    nxt_e = jnp.where(later == N_EXPERTS, -1, later)
    nxt = jnp.sum(jnp.where(block_e[:, None] == e_ids[None, :], nxt_e[None, :], 0), axis=1)
    ys_rows = _experts(block_e, n_active[None], first, slot, nxt, xs,
                       exp_w_gate[layer], exp_w_up[layer], exp_w_down[layer])
    out = _final(dest, xmid, w_k.T, mod3, final_g[None, :], ys_rows)
    return out.reshape(BATCH, SEQ, D_MODEL)
```

```python
import functools
import math

import jax
import jax.numpy as jnp
from jax import lax
from jax.experimental import pallas as pl
from jax.experimental.pallas import tpu as pltpu

D_MODEL = 2048
BATCH = 4
SEQ = 2048
CTX_LEN = 256
GRID_W = 64
D_SSM = 1024
D_CONV = 1024
SSM_GROUP = 16
SSM_GROUPS = 64
SSM_STATE = 64
N_EXPERTS = 64
N_EXPERT_GROUPS = 8
GROUP_SIZE = N_EXPERTS // N_EXPERT_GROUPS
TOPK_GROUPS = 4
TOP_K = 8
D_EXPERT = 512
D_SHARED = 512
ROUTED_SCALE = 2.5
N_MOD = 6
EPS = 1e-6

N_TOK = BATCH * SEQ
N_CTX_TOK = BATCH * CTX_LEN

CHUNK = 16
CHUNK_W = CHUNK * SSM_GROUP
STATE_W = 2 * SSM_STATE
N_CTX_CHUNKS = CTX_LEN // CHUNK
N_X_CHUNKS = SEQ // CHUNK
N_SEQ_CHUNKS = N_CTX_CHUNKS + N_X_CHUNKS
SCAN_ROWS = N_SEQ_CHUNKS * BATCH
S5_GROUPS_PER_STEP = 8

TM = 256
EXPERT_BM = 256
N_ASSIGN = N_TOK * TOP_K
N_BLOCKS = N_ASSIGN // EXPERT_BM + N_EXPERTS
N_ROWS = N_BLOCKS * EXPERT_BM
TM_COMBINE = 128
VMEM_LIMIT = 56 * 1024 * 1024

_F32 = jnp.float32
_BF16 = jnp.bfloat16


def _cparams(*sem):
    return pltpu.CompilerParams(dimension_semantics=sem, vmem_limit_bytes=VMEM_LIMIT)


def _const_spec(shape):
    nd = len(shape)
    return pl.BlockSpec(shape, lambda *_: (0,) * nd, pipeline_mode=pl.Buffered(1))


def _rms_scale(xf):
    return lax.rsqrt(jnp.mean(xf * xf, axis=-1, keepdims=True) + EPS)


def _silu(x):
    return x * jax.nn.sigmoid(x)


def _ada_kernel(c_ref, w_ref, b_ref, o_ref):
    s = _silu(c_ref[...])
    o_ref[...] = jnp.dot(s, w_ref[...], preferred_element_type=_F32) + b_ref[...]


def _ada_mod(c8, w_ada, b_ada):
    n = w_ada.shape[1]
    tn = 1024
    return pl.pallas_call(
        _ada_kernel,
        grid=(n // tn,),
        in_specs=[pl.BlockSpec((8, D_MODEL), lambda j: (0, 0)),
                  pl.BlockSpec((D_MODEL, tn), lambda j: (0, j)),
                  pl.BlockSpec((1, tn), lambda j: (0, j))],
        out_specs=pl.BlockSpec((8, tn), lambda j: (0, j)),
        out_shape=jax.ShapeDtypeStruct((8, n), _F32),
        compiler_params=_cparams("arbitrary"),
        name="ada_mod",
    )(c8, w_ada, b_ada)


def _modulated_norm(x_ref, mod_ref, g_ref, shift_row, scale_row):
    xf = x_ref[...]
    m = mod_ref[0]
    h = xf * _rms_scale(xf) * g_ref[...]
    return h * (1.0 + m[scale_row:scale_row + 1]) + m[shift_row:shift_row + 1]


def _inproj_kernel(x_ref, mod_ref, g_ref, w_ref, cw_ref, cb_ref, u_ref, conv_ref):
    h = _modulated_norm(x_ref, mod_ref, g_ref, 0, 1).astype(_BF16)
    u_ref[...] = jnp.dot(h, w_ref[:, 0:D_SSM], preferred_element_type=_F32)
    tm = x_ref.shape[0]
    pos = lax.broadcasted_iota(jnp.int32, (tm, 1), 0) % GRID_W
    not_first = (pos != 0).astype(_F32)
    not_last = (pos != GRID_W - 1).astype(_F32)
    cw = cw_ref[...]
    nc = 256
    for j in range(D_CONV // nc):
        lo = j * nc
        bg = jnp.dot(h, w_ref[:, D_SSM + lo:D_SSM + lo + nc], preferred_element_type=_F32)
        cg = jnp.dot(h, w_ref[:, D_SSM + D_CONV + lo:D_SSM + D_CONV + lo + nc], preferred_element_type=_F32)
        v = jnp.dot(h, w_ref[:, D_SSM + 2 * D_CONV + lo:D_SSM + 2 * D_CONV + lo + nc],
                    preferred_element_type=_F32)
        z = cg * v
        z_prev = pltpu.roll(z, 1, axis=0) * not_first
        z_next = pltpu.roll(z, tm - 1, axis=0) * not_last
        y = (cb_ref[:, lo:lo + nc] + z_prev * cw[0:1, lo:lo + nc] + z * cw[1:2, lo:lo + nc]
             + z_next * cw[2:3, lo:lo + nc])
        conv_ref[:, lo:lo + nc] = bg * y


def _inproj(x2, mod3, norm_g, w_in_bf, conv_w, conv_b):
    d_in = w_in_bf.shape[1]
    tiles_per_batch = SEQ // TM
    return pl.pallas_call(
        _inproj_kernel,
        grid=(N_TOK // TM,),
        in_specs=[pl.BlockSpec((TM, D_MODEL), lambda i: (i, 0)),
                  pl.BlockSpec((1, N_MOD, D_MODEL), lambda i: (i // tiles_per_batch, 0, 0)),
                  _const_spec((1, D_MODEL)),
                  _const_spec((D_MODEL, d_in)),
                  _const_spec((8, D_CONV)),
                  _const_spec((1, D_CONV))],
        out_specs=[pl.BlockSpec((TM, D_SSM), lambda i: (i, 0)),
                   pl.BlockSpec((TM, D_CONV), lambda i: (i, 0))],
        out_shape=[jax.ShapeDtypeStruct((N_TOK, D_SSM), _F32),
                   jax.ShapeDtypeStruct((N_TOK, D_CONV), _F32)],
        compiler_params=_cparams("arbitrary"),
        name="in_proj",
    )(x2, mod3, norm_g, w_in_bf, conv_w, conv_b)


def _ctxproj_kernel(x_ref, mod_ref, g_ref, w_ref, u_ref):
    h = _modulated_norm(x_ref, mod_ref, g_ref, 0, 1).astype(_BF16)
    u_ref[...] = jnp.dot(h, w_ref[...], preferred_element_type=_F32)


def _ctxproj(ctx2, mod3, norm_g, w_u_bf):
    return pl.pallas_call(
        _ctxproj_kernel,
        grid=(N_CTX_TOK // TM,),
        in_specs=[pl.BlockSpec((TM, D_MODEL), lambda i: (i, 0)),
                  pl.BlockSpec((1, N_MOD, D_MODEL), lambda i: (BATCH, 0, 0)),
                  _const_spec((1, D_MODEL)),
                  _const_spec((D_MODEL, D_SSM))],
        out_specs=pl.BlockSpec((TM, D_SSM), lambda i: (i, 0)),
        out_shape=jax.ShapeDtypeStruct((N_CTX_TOK, D_SSM), _F32),
        compiler_params=_cparams("arbitrary"),
        name="ctx_proj",
    )(ctx2, mod3, norm_g, w_u_bf)


N_POW = 32


def _s5_param_kernel(prow_ref, ct_ref, bt_ref, toep_ref, bpow_ref, cpow_ref, dec_ref):
    lane = lambda shape: lax.broadcasted_iota(jnp.int32, shape, 1)
    sub = lambda shape: lax.broadcasted_iota(jnp.int32, shape, 0)

    def split2(a):
        a_hi = a.astype(_BF16)
        return a_hi, (a - a_hi.astype(_F32)).astype(_BF16)

    rep = (lane((SSM_GROUP, CHUNK_W)) % SSM_GROUP == sub((SSM_GROUP, CHUNK_W))).astype(_BF16)
    sgn_col = jnp.where(sub((STATE_W, 1)) < SSM_STATE, 1.0, -1.0).astype(_F32)
    sgn_row = jnp.where(lane((1, STATE_W)) < SSM_STATE, -1.0, 1.0).astype(_F32)
    tau_col = sub((N_POW, 1)).astype(_F32)
    blk_l = lane((N_POW, CHUNK_W)) // SSM_GROUP
    tau_s = sub((N_POW, CHUNK_W))
    blk_r = sub((CHUNK_W, N_POW)) // SSM_GROUP
    tau_l = lane((CHUNK_W, N_POW))
    pick_l = lambda e: (tau_s == e).astype(_BF16)
    pick_r = lambda e: (tau_l == e).astype(_BF16)
    tn = (((0,), (0,)), ((), ()))

    strips = []
    for d in range(2):
        pr = prow_ref[d, 0]
        lam_re, lam_im, dt = pr[0:1], pr[1:2], jnp.exp(pr[2:3])
        mag = jnp.exp(tau_col * (lam_re * dt))
        ang = tau_col * (lam_im * dt)
        pw_re = mag * jnp.cos(ang)
        pw_im = mag * jnp.sin(ang)
        pw4 = jnp.concatenate(split2(pw_re) + split2(pw_im), axis=1)

        c_hi, c_lo = split2(ct_ref[d, 0])
        ct = (jnp.dot(c_hi, rep, preferred_element_type=_F32)
              + jnp.dot(c_lo, rep, preferred_element_type=_F32))
        ca = ct * sgn_col
        cb = -pltpu.roll(ct, SSM_STATE, axis=0)

        def cpow(sel):
            o = lax.dot_general(pw4, sel, tn, preferred_element_type=_F32)
            x = o[0:STATE_W] + o[STATE_W:2 * STATE_W]
            y = o[2 * STATE_W:3 * STATE_W] + o[3 * STATE_W:4 * STATE_W]
            return x * ca + y * cb

        nr = pw_re[1:2] - 1.0
        ni = pw_im[1:2]
        den = lam_re * lam_re + lam_im * lam_im
        kr = (nr * lam_re + ni * lam_im) / den
        ki = (ni * lam_re - nr * lam_im) / den
        b1 = bt_ref[d, 0]
        b2 = pltpu.roll(b1, SSM_STATE, axis=1) * sgn_row
        u1 = kr * b1 + ki * b2
        u2 = kr * b2 - ki * b1
        u1t = jnp.concatenate([u1] * CHUNK, axis=0)
        u2t = jnp.concatenate([u2] * CHUNK, axis=0)

        def bpow(sel):
            o = jnp.dot(sel, pw4, preferred_element_type=_F32)
            x = o[:, 0:STATE_W] + o[:, STATE_W:2 * STATE_W]
            y = o[:, 2 * STATE_W:3 * STATE_W] + o[:, 3 * STATE_W:4 * STATE_W]
            return x * u1t + y * u2t

        if d == 0:
            e_strip, e_b, e_c = blk_l, (CHUNK - 1) - blk_r, blk_l + 1
        else:
            e_strip, e_b, e_c = (CHUNK - 1) - blk_l, blk_r, CHUNK - blk_l
        u_hi, u_lo = split2(u1)
        k_hi, k_lo = split2(cpow(pick_l(e_strip)))
        strips.append(jnp.dot(u_hi, k_hi, preferred_element_type=_F32)
                      + jnp.dot(u_hi, k_lo, preferred_element_type=_F32)
                      + jnp.dot(u_lo, k_hi, preferred_element_type=_F32))
        bp = bpow(pick_r(e_b))
        bpow_ref[0, :, d * STATE_W:(d + 1) * STATE_W] = bp.astype(_BF16)
        bpow_ref[0, :, (2 + d) * STATE_W:(3 + d) * STATE_W] = pltpu.roll(bp, SSM_STATE, axis=1).astype(_BF16)
        cpow_ref[0, d * STATE_W:(d + 1) * STATE_W, :] = cpow(pick_l(e_c)).astype(_BF16)

        dec_ref[0, 2 * d:2 * d + 1, :] = pw_re[CHUNK:CHUNK + 1]
        dec_ref[0, 2 * d + 1:2 * d + 2, :] = pw_im[CHUNK:CHUNK + 1] * sgn_row

    zeros = jnp.zeros((SSM_GROUP, CHUNK_W), _F32)
    strip = (jnp.concatenate([strips[1], zeros], axis=1)
             + pltpu.roll(jnp.concatenate([strips[0], zeros], axis=1), CHUNK_W - SSM_GROUP, axis=1))
    for i in range(CHUNK):
        off = (CHUNK - 1 - i) * SSM_GROUP
        win = strip if off == 0 else pltpu.roll(strip, 2 * CHUNK_W - off, axis=1)
        toep_ref[0, i * SSM_GROUP:(i + 1) * SSM_GROUP, :] = win[:, 0:CHUNK_W].astype(_BF16)


def _s5_params(prow, ct2, bt1):
    g_spec = lambda shape: pl.BlockSpec(shape, lambda g: (0, g, 0, 0))
    o_spec = lambda shape: pl.BlockSpec(shape, lambda g: (g, 0, 0))
    return pl.pallas_call(
        _s5_param_kernel,
        grid=(SSM_GROUPS,),
        in_specs=[g_spec((2, 1, 8, STATE_W)),
                  g_spec((2, 1, STATE_W, SSM_GROUP)), g_spec((2, 1, SSM_GROUP, STATE_W))],
        out_specs=[o_spec((1, CHUNK_W, CHUNK_W)), o_spec((1, CHUNK_W, 4 * STATE_W)),
                   o_spec((1, 2 * STATE_W, CHUNK_W)), o_spec((1, 4, STATE_W))],
        out_shape=[jax.ShapeDtypeStruct((SSM_GROUPS, CHUNK_W, CHUNK_W), _BF16),
                   jax.ShapeDtypeStruct((SSM_GROUPS, CHUNK_W, 4 * STATE_W), _BF16),
                   jax.ShapeDtypeStruct((SSM_GROUPS, 2 * STATE_W, CHUNK_W), _BF16),
                   jax.ShapeDtypeStruct((SSM_GROUPS, 4, STATE_W), _F32)],
        compiler_params=_cparams("arbitrary"),
        name="s5_params",
    )(prow, ct2, bt1)


def _s5_kernel(ux_ref, uc_ref, toep_ref, bpow_ref, cpow_ref, dec_ref, y_ref, u_ref, s_ref, h_ref, y2_ref):
    ng = S5_GROUPS_PER_STEP
    gw = SSM_GROUP
    x0 = N_CTX_CHUNKS * BATCH

    half = CHUNK // 2
    assert half == ng
    lanes = ng * 128

    r = lax.broadcasted_iota(jnp.int32, (lanes, 1), 0)
    swapped = ((r // gw) % ng) * 128 + (r // 128) * gw + r % gw
    perm = (lax.broadcasted_iota(jnp.int32, (lanes, lanes), 1) == swapped).astype(_BF16)

    def to_chunk_layout(src_ref, seq_len, n_chunks, row0):
        for hf in range(2):
            p = jnp.concatenate(
                [jnp.concatenate([src_ref[pl.ds(b * seq_len + hf * half + il, n_chunks, stride=CHUNK), :]
                                  for il in range(half)], axis=1) for b in range(BATCH)], axis=0).astype(_BF16)
            q = jnp.dot(p, perm, preferred_element_type=_F32)
            for b in range(BATCH):
                for g in range(ng):
                    u_ref[g, hf, pl.ds(row0 + b, n_chunks, stride=BATCH), :] = (
                        q[b * n_chunks:(b + 1) * n_chunks, g * 128:(g + 1) * 128])

    to_chunk_layout(uc_ref, CTX_LEN, N_CTX_CHUNKS, 0)
    to_chunk_layout(ux_ref, SEQ, N_X_CHUNKS, x0)

    chunk_rows = lambda ref, g, lo: jnp.concatenate([ref[g, 0, lo:, :], ref[g, 1, lo:, :]], axis=1)
    for g in range(ng):
        s_ref[g] = jnp.dot(chunk_rows(u_ref, g, 0).astype(_BF16), bpow_ref[g], preferred_element_type=_F32)

    pair = 2 * BATCH
    n_ctx_pairs = N_CTX_CHUNKS // 2
    n_pairs = N_SEQ_CHUNKS // 2
    lower = lax.broadcasted_iota(jnp.int32, (pair, STATE_W), 0) < BATCH
    zero = jnp.zeros((pair, STATE_W), _F32)

    def step(t, carry):
        tb = jnp.where(t < n_ctx_pairs, n_ctx_pairs - 1 - t, n_pairs + n_ctx_pairs - 1 - t)
        rf = pl.multiple_of(t * pair, pair)
        rb = pl.multiple_of(tb * pair, pair)
        out = []
        swap = lambda v: pltpu.roll(v, BATCH, axis=0)
        for g in range(ng):
            hf, hfs, hb, hbs = carry[4 * g:4 * g + 4]
            dg = dec_ref[g]

            def advance(h, hs, a_re, a_im, s, ss):
                return a_re * h + a_im * hs + s, a_re * hs - a_im * h + ss

            col = lambda r, k: s_ref[g, pl.ds(r, pair), k * STATE_W:(k + 1) * STATE_W]
            sf, sfs = col(rf, 0), col(rf, 2)
            mid, mids = advance(hf, hfs, dg[0:1], dg[1:2], jnp.where(lower, sf, swap(sf)),
                                jnp.where(lower, sfs, swap(sfs)))
            h_ref[g, pl.ds(rf, pair), 0:STATE_W] = jnp.where(lower, hf, mid)
            end, ends = advance(mid, mids, dg[0:1], dg[1:2], sf, sfs)
            hf, hfs = jnp.where(lower, swap(end), end), jnp.where(lower, swap(ends), ends)

            sb, sbs = col(rb, 1), col(rb, 3)
            mid, mids = advance(hb, hbs, dg[2:3], dg[3:4], jnp.where(lower, swap(sb), sb),
                                jnp.where(lower, swap(sbs), sbs))
            h_ref[g, pl.ds(rb, pair), STATE_W:2 * STATE_W] = jnp.where(lower, mid, hb)
            end, ends = advance(mid, mids, dg[2:3], dg[3:4], sb, sbs)
            hb, hbs = jnp.where(lower, end, swap(end)), jnp.where(lower, ends, swap(ends))
            out += [hf, hfs, hb, hbs]
        return tuple(out)

    lax.fori_loop(0, n_pairs, step, (zero,) * (4 * ng))

    for g in range(ng):
        ux = chunk_rows(u_ref, g, x0).astype(_BF16)
        hx = h_ref[g, x0:SCAN_ROWS, :].astype(_BF16)
        y2 = (jnp.dot(ux, toep_ref[g], preferred_element_type=_F32)
              + jnp.dot(hx, cpow_ref[g], preferred_element_type=_F32))
        y2_ref[g, 0] = y2[:, 0:128]
        y2_ref[g, 1] = y2[:, 128:256]

    for hf in range(2):
        yin = jnp.concatenate(
            [jnp.concatenate([y2_ref[g, hf, pl.ds(b, N_X_CHUNKS, stride=BATCH), :] for g in range(ng)], axis=1)
             for b in range(BATCH)], axis=0)
        y_hi = yin.astype(_BF16)
        y_lo = (yin - y_hi.astype(_F32)).astype(_BF16)
        q = (jnp.dot(y_hi, perm, preferred_element_type=_F32)
             + jnp.dot(y_lo, perm, preferred_element_type=_F32))
        for b in range(BATCH):
            for jl in range(half):
                y_ref[pl.ds(b * SEQ + hf * half + jl, N_X_CHUNKS, stride=CHUNK), :] = (
                    q[b * N_X_CHUNKS:(b + 1) * N_X_CHUNKS, jl * 128:(jl + 1) * 128])


def _s5_scan(u_x, u_c, toep, bpow, cpow, dec):
    ng = S5_GROUPS_PER_STEP
    n_x_rows = N_X_CHUNKS * BATCH
    lanes = ng * SSM_GROUP
    spec = lambda r, c: pl.BlockSpec((ng, r, c), lambda i: (i, 0, 0))
    col = lambda rows: pl.BlockSpec((rows, lanes), lambda i: (0, i))
    return pl.pallas_call(
        _s5_kernel,
        grid=(SSM_GROUPS // ng,),
        in_specs=[col(N_TOK), col(N_CTX_TOK), spec(CHUNK_W, CHUNK_W), spec(CHUNK_W, 4 * STATE_W),
                  spec(2 * STATE_W, CHUNK_W), spec(4, STATE_W)],
        out_specs=col(N_TOK),
        out_shape=jax.ShapeDtypeStruct((N_TOK, D_SSM), _F32),
        scratch_shapes=[pltpu.VMEM((ng, 2, SCAN_ROWS, 128), _F32),
                        pltpu.VMEM((ng, SCAN_ROWS, 4 * STATE_W), _F32),
                        pltpu.VMEM((ng, SCAN_ROWS, 2 * STATE_W), _F32),
                        pltpu.VMEM((ng, 2, n_x_rows, 128), _F32)],
        compiler_params=_cparams("arbitrary"),
        name="s5_scan",
    )(u_x, u_c, toep, bpow, cpow, dec)


def _merge_kernel(ys_ref, u_ref, conv_ref, x_ref, mod_ref, dskip_ref, mixg_ref, n2g_ref,
                  wglu_ref, wout_ref, rwt_ref, x1_ref, h2_ref, logit_ref):
    m = mod_ref[0]
    yx = dskip_ref[...] * u_ref[...] + ys_ref[...]
    c0 = math.sqrt(2.0 / math.pi)
    ge = 0.5 * yx * (1.0 + jnp.tanh(c0 * (yx + 0.044715 * (yx * yx * yx))))
    z = jnp.dot(ge.astype(_BF16), wglu_ref[...], preferred_element_type=_F32)
    ssm_y = z[:, 0:D_SSM] * jax.nn.sigmoid(z[:, D_SSM:2 * D_SSM])
    conv_y = conv_ref[...]
    mixg = mixg_ref[...]
    heads_a = (ssm_y * _rms_scale(ssm_y) * mixg[:, 0:D_SSM]).astype(_BF16)
    heads_b = (conv_y * _rms_scale(conv_y) * mixg[:, D_SSM:]).astype(_BF16)
    mix = (jnp.dot(heads_a, wout_ref[0:D_SSM, :], preferred_element_type=_F32)
           + jnp.dot(heads_b, wout_ref[D_SSM:, :], preferred_element_type=_F32))
    x1 = x_ref[...] + m[2:3] * mix
    x1_ref[...] = x1
    h2 = x1 * _rms_scale(x1) * n2g_ref[...] * (1.0 + m[4:5]) + m[3:4]
    h2_ref[...] = h2
    h_hi = h2.astype(_BF16)
    h_lo = (h2 - h_hi.astype(_F32)).astype(_BF16)
    nt = (((1,), (1,)), ((), ()))
    p = lax.dot_general(rwt_ref[...], h_hi, nt, preferred_element_type=_F32)
    q = lax.dot_general(rwt_ref[0:N_EXPERTS, :], h_lo, nt, preferred_element_type=_F32)
    logit_ref[...] = p[0:N_EXPERTS] + p[N_EXPERTS:2 * N_EXPERTS] + q


def _merge(ys, u, conv, x2, mod3, dskip, mixg, n2g, wglu_bf, wout_bf, rwt):
    tiles_per_batch = SEQ // TM
    tok = lambda w: pl.BlockSpec((TM, w), lambda i: (i, 0))
    return pl.pallas_call(
        _merge_kernel,
        grid=(N_TOK // TM,),
        in_specs=[tok(D_SSM), tok(D_SSM), tok(D_CONV), tok(D_MODEL),
                  pl.BlockSpec((1, N_MOD, D_MODEL), lambda i: (i // tiles_per_batch, 0, 0)),
                  _const_spec((1, D_SSM)), _const_spec((1, D_MODEL)), _const_spec((1, D_MODEL)),
                  _const_spec((D_SSM, 2 * D_SSM)), _const_spec((D_MODEL, D_MODEL)),
                  _const_spec((2 * N_EXPERTS, D_MODEL))],
        out_specs=[tok(D_MODEL), tok(D_MODEL),
                   pl.BlockSpec((N_EXPERTS, TM), lambda i: (0, i))],
        out_shape=[jax.ShapeDtypeStruct((N_TOK, D_MODEL), _F32),
                   jax.ShapeDtypeStruct((N_TOK, D_MODEL), _F32),
                   jax.ShapeDtypeStruct((N_EXPERTS, N_TOK), _F32)],
        compiler_params=_cparams("arbitrary"),
        name="merge_heads",
    )(ys, u, conv, x2, mod3, dskip, mixg, n2g, wglu_bf, wout_bf, rwt)


def _route_kernel(logit_ref, bias_ref, eid_ref, w_ref, rank_ref, cnt_ref, carry_ref):
    @pl.when(pl.program_id(0) == 0)
    def _():
        carry_ref[...] = jnp.zeros_like(carry_ref)

    tm = logit_ref.shape[1]
    neg = jnp.float32(-jnp.inf)
    scores = jax.nn.sigmoid(logit_ref[...])
    biased = scores + bias_ref[:, 0:1]
    sub = lax.broadcasted_iota(jnp.int32, (GROUP_SIZE, tm), 0)
    rows = lambda a, g: a[g * GROUP_SIZE:(g + 1) * GROUP_SIZE]
    ngrp = N_EXPERT_GROUPS

    gscore = []
    for g in range(ngrp):
        bg = rows(biased, g)
        m1 = jnp.max(bg, axis=0, keepdims=True)
        first = jnp.min(jnp.where(bg == m1, sub, GROUP_SIZE), axis=0, keepdims=True)
        m2 = jnp.max(jnp.where(sub == first, neg, bg), axis=0, keepdims=True)
        gscore.append(m1 + m2)
    v = []
    for g in range(ngrp):
        beaten = jnp.zeros((1, tm), jnp.int32)
        for o in range(ngrp):
            if o != g:
                beats = (gscore[o] >= gscore[g]) if o < g else (gscore[o] > gscore[g])
                beaten = beaten + beats.astype(jnp.int32)
        v.append(jnp.where(beaten < TOPK_GROUPS, rows(biased, g), neg))
    eids = [sub + g * GROUP_SIZE for g in range(ngrp)]

    pick_ids, pick_masks = [], []
    for _ in range(TOP_K):
        m = functools.reduce(jnp.maximum, v)
        m = jnp.max(m, axis=0, keepdims=True)
        cand = functools.reduce(jnp.minimum, [jnp.where(v[g] == m, eids[g], N_EXPERTS) for g in range(ngrp)])
        pick_id = jnp.min(cand, axis=0, keepdims=True)
        masks = [eids[g] == pick_id for g in range(ngrp)]
        v = [jnp.where(masks[g], neg, v[g]) for g in range(ngrp)]
        pick_ids.append(pick_id)
        pick_masks.append(masks)

    sel = jnp.concatenate(
        [functools.reduce(jnp.logical_or, [pick_masks[k][g] for k in range(TOP_K)]).astype(_F32)
         for g in range(ngrp)], axis=0)
    before = (lax.broadcasted_iota(jnp.int32, (tm, tm), 0)
              < lax.broadcasted_iota(jnp.int32, (tm, tm), 1)).astype(_BF16)
    base = jnp.dot(sel.astype(_BF16), before, preferred_element_type=_F32) + carry_ref[:, 0:1]

    def gather_pick(a, k):
        parts = [jnp.sum(jnp.where(pick_masks[k][g], rows(a, g), 0.0), axis=0, keepdims=True)
                 for g in range(ngrp)]
        return functools.reduce(jnp.add, parts)

    picked = [gather_pick(scores, k) for k in range(TOP_K)]
    denom = functools.reduce(jnp.add, picked)
    for k in range(TOP_K):
        eid_ref[k:k + 1, :] = pick_ids[k]
        w_ref[k:k + 1, :] = picked[k] / denom * ROUTED_SCALE
        rank_ref[k:k + 1, :] = gather_pick(base, k).astype(jnp.int32)
    carry_ref[...] = carry_ref[...] + jnp.sum(sel, axis=1, keepdims=True)
    cnt_ref[...] = carry_ref[...]


def _route(logits_t, bias_col):
    tok = pl.BlockSpec((TOP_K, TM), lambda i: (0, i))
    return pl.pallas_call(
        _route_kernel,
        grid=(N_TOK // TM,),
        in_specs=[pl.BlockSpec((N_EXPERTS, TM), lambda i: (0, i)), _const_spec((N_EXPERTS, 128))],
        out_specs=[tok, tok, tok, pl.BlockSpec((N_EXPERTS, 128), lambda i: (0, 0))],
        out_shape=[jax.ShapeDtypeStruct((TOP_K, N_TOK), jnp.int32),
                   jax.ShapeDtypeStruct((TOP_K, N_TOK), _F32),
                   jax.ShapeDtypeStruct((TOP_K, N_TOK), jnp.int32),
                   jax.ShapeDtypeStruct((N_EXPERTS, 128), _F32)],
        scratch_shapes=[pltpu.VMEM((N_EXPERTS, 128), _F32)],
        compiler_params=_cparams("arbitrary"),
        name="route",
    )(logits_t, bias_col)


def _rowmap_kernel(dest_ref, padstart_ref, padn_ref, h_ref, x1_ref, mod_ref, wsg_ref, wsu_ref, wsd_ref,
                   xmid_ref, codes_ref):
    i = pl.program_id(0)
    tm = h_ref.shape[0]
    base = i * tm

    @pl.when(i == 0)
    def _():
        def fill(e, done):
            def one(r, c):
                codes_ref[padstart_ref[e] + r] = N_ASSIGN + done + r
                return c
            lax.fori_loop(0, padn_ref[e], one, 0)
            return done + padn_ref[e]
        lax.fori_loop(0, N_EXPERTS, fill, 0)
        for r in range(EXPERT_BM):
            codes_ref[N_ROWS + r] = N_ROWS + r

    for t in range(tm):
        ds = [dest_ref[k * N_TOK + base + t] for k in range(TOP_K)]
        for k in range(TOP_K):
            codes_ref[ds[k]] = k * N_TOK + base + t

    h = h_ref[...].astype(_BF16)
    a = jnp.dot(h, wsg_ref[...], preferred_element_type=_F32)
    b = jnp.dot(h, wsu_ref[...], preferred_element_type=_F32)
    shared = jnp.dot((_silu(a) * b).astype(_BF16), wsd_ref[...], preferred_element_type=_F32)
    xmid_ref[...] = x1_ref[...] + mod_ref[0][5:6] * shared


def _rowmap(dest, pad_start, pad_n, h2, x1, mod3, wsg_bf, wsu_bf, wsd_bf):
    tm = TM
    tiles_per_batch = SEQ // tm
    tok = pl.BlockSpec((tm, D_MODEL), lambda i, *_: (i, 0))
    const = lambda shape: pl.BlockSpec(shape, lambda i, *_: (0,) * len(shape), pipeline_mode=pl.Buffered(1))
    grid_spec = pltpu.PrefetchScalarGridSpec(
        num_scalar_prefetch=3,
        grid=(N_TOK // tm,),
        in_specs=[tok, tok,
                  pl.BlockSpec((1, N_MOD, D_MODEL), lambda i, *_: (i // tiles_per_batch, 0, 0)),
                  const((D_MODEL, D_SHARED)), const((D_MODEL, D_SHARED)), const((D_SHARED, D_MODEL))],
        out_specs=[tok, pl.BlockSpec(memory_space=pltpu.SMEM)],
    )
    return pl.pallas_call(
        _rowmap_kernel,
        grid_spec=grid_spec,
        out_shape=[jax.ShapeDtypeStruct((N_TOK, D_MODEL), _F32),
                   jax.ShapeDtypeStruct((N_ROWS + EXPERT_BM,), jnp.int32)],
        compiler_params=_cparams("arbitrary"),
        name="rowmap",
    )(dest.reshape(N_ASSIGN), pad_start, pad_n, h2, x1, mod3, wsg_bf, wsu_bf, wsd_bf)


def _experts_kernel(be_ref, nact_ref, first_ref, slot_ref, nxt_ref, codes_ref, h_hbm, wg_hbm, wu_hbm, wd_hbm,
                    out_hbm, x0, x1, y0, y1, wg_f, wu_f, wd_f, wg_bf, wu_bf, wd_bf, sem, gsem, ssem):
    n_act = nact_ref[0]
    bm = EXPERT_BM
    xbuf, ybuf = (x0, x1), (y0, y1)

    def fetch(e, s):
        return (pltpu.make_async_copy(wg_hbm.at[e], wg_f.at[s], sem.at[s, 0]),
                pltpu.make_async_copy(wu_hbm.at[e], wu_f.at[s], sem.at[s, 1]),
                pltpu.make_async_copy(wd_hbm.at[e], wd_f.at[s], sem.at[s, 2]))

    def gather(blk, p):
        for r in range(bm):
            tok = codes_ref[blk * bm + r] & (N_TOK - 1)
            pltpu.make_async_copy(h_hbm.at[pl.ds(tok, 1), :], xbuf[p].at[pl.ds(r, 1), :],
                                  gsem.at[p]).start(priority=r % 2)

    def gather_wait(p):
        pltpu.make_async_copy(h_hbm.at[pl.ds(0, bm), :], xbuf[p], gsem.at[p]).wait()

    def scatter(blk, p):
        for r in range(bm):
            pltpu.make_async_copy(ybuf[p].at[pl.ds(r, 1), :], out_hbm.at[pl.ds(codes_ref[blk * bm + r], 1), :],
                                  ssem.at[p]).start(priority=r % 2)

    def scatter_wait(p):
        pltpu.make_async_copy(ybuf[p], out_hbm.at[pl.ds(0, bm), :], ssem.at[p]).wait()

    def block(blk, p):
        @pl.when(blk < n_act)
        def _():
            @pl.when(blk == 0)
            def _():
                for cp in fetch(be_ref[0], 0):
                    cp.start()
                gather(0, 0)
                y1[...] = jnp.zeros_like(y1)

            @pl.when(first_ref[blk] == 1)
            def _():
                s = slot_ref[blk]
                for cp in fetch(be_ref[blk], s):
                    cp.wait()

                @pl.when(nxt_ref[blk] >= 0)
                def _():
                    for cp in fetch(nxt_ref[blk], 1 - s):
                        cp.start()

                wg_bf[...] = wg_f[s].astype(_BF16)
                wu_bf[...] = wu_f[s].astype(_BF16)
                wd_bf[...] = wd_f[s].astype(_BF16)

            gather_wait(p)

            @pl.when(blk >= 1)
            def _():
                scatter_wait(p)

            gather(jnp.minimum(blk + 1, n_act - 1), 1 - p)
            scatter(jnp.where(blk == 0, N_BLOCKS, blk - 1), 1 - p)
            x = xbuf[p][...].astype(_BF16)
            a = jnp.dot(x, wg_bf[...], preferred_element_type=_F32)
            b = jnp.dot(x, wu_bf[...], preferred_element_type=_F32)
            ybuf[p][...] = jnp.dot((_silu(a) * b).astype(_BF16), wd_bf[...], preferred_element_type=_F32)

            @pl.when(blk == n_act - 1)
            def _():
                scatter(blk, p)
                gather_wait(1 - p)
                scatter_wait(1 - p)
                scatter_wait(p)

    i = pl.program_id(0)
    block(2 * i, 0)
    block(2 * i + 1, 1)


def _experts(block_e, n_active, first, slot, nxt, codes, h2, w_gate, w_up, w_down):
    bm = EXPERT_BM
    assert N_BLOCKS % 2 == 0
    hbm = pl.BlockSpec(memory_space=pl.ANY)
    grid_spec = pltpu.PrefetchScalarGridSpec(
        num_scalar_prefetch=6,
        grid=(N_BLOCKS // 2,),
        in_specs=[hbm, hbm, hbm, hbm],
        out_specs=hbm,
        scratch_shapes=[pltpu.VMEM((bm, D_MODEL), _F32), pltpu.VMEM((bm, D_MODEL), _F32),
                        pltpu.VMEM((bm, D_MODEL), _F32), pltpu.VMEM((bm, D_MODEL), _F32),
                        pltpu.VMEM((2, D_MODEL, D_EXPERT), _F32), pltpu.VMEM((2, D_MODEL, D_EXPERT), _F32),
                        pltpu.VMEM((2, D_EXPERT, D_MODEL), _F32),
                        pltpu.VMEM((D_MODEL, D_EXPERT), _BF16), pltpu.VMEM((D_MODEL, D_EXPERT), _BF16),
                        pltpu.VMEM((D_EXPERT, D_MODEL), _BF16),
                        pltpu.SemaphoreType.DMA((2, 3)), pltpu.SemaphoreType.DMA((2,)),
                        pltpu.SemaphoreType.DMA((2,))],
    )
    return pl.pallas_call(
        _experts_kernel,
        grid_spec=grid_spec,
        out_shape=jax.ShapeDtypeStruct((N_ROWS + bm, D_MODEL), _F32),
        compiler_params=_cparams("arbitrary"),
        name="experts",
    )(block_e, n_active, first, slot, nxt, codes, h2, w_gate, w_up, w_down)


def _final_kernel(xmid_ref, wt_ref, mod_ref, fg_ref, *refs):
    picks, o_ref = refs[:TOP_K], refs[TOP_K]
    wt = wt_ref[...]
    routed = wt[:, 0:1] * picks[0][...]
    for k in range(1, TOP_K):
        routed = routed + wt[:, k:k + 1] * picks[k][...]
    y = xmid_ref[...] + mod_ref[0][5:6] * routed
    o_ref[...] = y * _rms_scale(y) * fg_ref[...]


def _final(xmid, w_tok, mod3, final_g, ys):
    tm = TM_COMBINE
    tiles = N_TOK // tm
    tiles_per_batch = SEQ // tm
    tok = lambda w: pl.BlockSpec((tm, w), lambda i: (i, 0))
    pick = lambda k: pl.BlockSpec((tm, D_MODEL), lambda i: (k * tiles + i, 0))
    return pl.pallas_call(
        _final_kernel,
        grid=(tiles,),
        in_specs=[tok(D_MODEL), tok(TOP_K),
                  pl.BlockSpec((1, N_MOD, D_MODEL), lambda i: (i // tiles_per_batch, 0, 0)),
                  _const_spec((1, D_MODEL))] + [pick(k) for k in range(TOP_K)],
        out_specs=tok(D_MODEL),
        out_shape=jax.ShapeDtypeStruct((N_TOK, D_MODEL), _F32),
        compiler_params=_cparams("arbitrary"),
        name="final",
    )(xmid, w_tok, mod3, final_g, *([ys] * TOP_K))


def _s5_param_layouts(lam_re, lam_im, b_re, b_im, c_re, c_im, log_dt):
    two = lambda a: jnp.concatenate([a, a], axis=-1)
    lr, li = two(lam_re), two(lam_im)
    dtb = jnp.broadcast_to(log_dt.astype(_F32)[:, :, None], lr.shape)
    zeros = jnp.zeros_like(lr)
    prow = jnp.stack([lr, li, dtb] + [zeros] * 5, axis=2)
    ct2 = jnp.concatenate([jnp.swapaxes(c_re, -1, -2), jnp.swapaxes(c_im, -1, -2)], axis=2)
    bt1 = jnp.concatenate([jnp.swapaxes(b_re, -1, -2), jnp.swapaxes(b_im, -1, -2)], axis=3)
    return prow, ct2, bt1


def kernel(x, c, ctx, c_ctx, norm1_g, norm2_g, w_ada, b_ada, w_in, ssm_lam_re, ssm_lam_im, ssm_b_re, ssm_b_im, ssm_c_re, ssm_c_im, ssm_log_dt, ssm_d, ssm_w_glu, conv_w, conv_b, mix_norm_g, w_out, router_w, router_bias, exp_w_gate, exp_w_up, exp_w_down, shared_w_gate, shared_w_up, shared_w_down, final_g):
    layer = 0
    x2 = x.reshape(N_TOK, D_MODEL)
    ctx2 = ctx.reshape(N_CTX_TOK, D_MODEL)

    c8 = jnp.concatenate([c, c_ctx[None, :], jnp.zeros((8 - BATCH - 1, D_MODEL), _F32)], axis=0)
    mod = _ada_mod(c8, w_ada[layer], b_ada[layer][None, :])
    mod3 = mod.reshape(8, N_MOD, D_MODEL)

    w_in_bf = w_in[layer].astype(_BF16)
    conv_w8 = jnp.concatenate([conv_w[layer], jnp.zeros((8 - conv_w.shape[1], D_CONV), _F32)], axis=0)
    u_x, conv_x = _inproj(x2, mod3, norm1_g[layer][None, :], w_in_bf, conv_w8, conv_b[layer][None, :])
    u_c = _ctxproj(ctx2, mod3, norm1_g[layer][None, :], w_in_bf[:, :D_SSM])

    prow, ct2, bt1 = _s5_param_layouts(ssm_lam_re[layer], ssm_lam_im[layer], ssm_b_re[layer],
                                       ssm_b_im[layer], ssm_c_re[layer], ssm_c_im[layer], ssm_log_dt[layer])
    toep, bpow, cpow, dec = _s5_params(prow, ct2, bt1)
    ys = _s5_scan(u_x, u_c, toep, bpow, cpow, dec)

    rw = router_w[layer]
    rw_hi = rw.astype(_BF16)
    rw_lo = (rw - rw_hi.astype(_F32)).astype(_BF16)
    rwt = jnp.concatenate([rw_hi.T, rw_lo.T], axis=0)
    x1, h2, logits_t = _merge(ys, u_x, conv_x, x2, mod3, ssm_d[layer][None, :],
                              mix_norm_g[layer][None, :], norm2_g[layer][None, :],
                              ssm_w_glu[layer].astype(_BF16), w_out[layer].astype(_BF16), rwt)

    bias_col = jnp.broadcast_to(router_bias[layer][:, None], (N_EXPERTS, 128))
    eid, w_k, rank, cnt = _route(logits_t, bias_col)

    counts = cnt[:, 0].astype(jnp.int32)
    padded = (counts + EXPERT_BM - 1) // EXPERT_BM * EXPERT_BM
    pend = jnp.cumsum(padded)
    pstart = pend - padded
    is_e = eid[:, :, None] == jnp.arange(N_EXPERTS, dtype=jnp.int32)
    dest = jnp.sum(jnp.where(is_e, pstart, 0), axis=-1) + rank
    n_active = (pend[-1] // EXPERT_BM).astype(jnp.int32)
    blk = jnp.minimum(jnp.arange(N_BLOCKS, dtype=jnp.int32), n_active - 1)
    ends_before = (pend[None, :] <= (blk * EXPERT_BM)[:, None]).astype(jnp.int32)
    block_e = jnp.minimum(jnp.sum(ends_before, axis=1), N_EXPERTS - 1)

    xmid, codes = _rowmap(dest, pstart + counts, padded - counts, h2, x1, mod3,
                          shared_w_gate[layer].astype(_BF16), shared_w_up[layer].astype(_BF16),
                          shared_w_down[layer].astype(_BF16))
    first = jnp.concatenate([jnp.ones((1,), jnp.int32), (block_e[1:] != block_e[:-1]).astype(jnp.int32)])
    slot = (jnp.cumsum(first) - 1) % 2
    e_ids = jnp.arange(N_EXPERTS, dtype=jnp.int32)
    owner = jnp.where(padded > 0, e_ids, N_EXPERTS)
    later = jnp.min(jnp.where(e_ids[None, :] > e_ids[:, None], owner[None, :], N_EXPERTS), axis=1)
    nxt_e = jnp.where(later == N_EXPERTS, -1, later)
    nxt = jnp.sum(jnp.where(block_e[:, None] == e_ids[None, :], nxt_e[None, :], 0), axis=1)
    ys_rows = _experts(block_e, n_active[None], first, slot, nxt, codes, h2,
                       exp_w_gate[layer], exp_w_up[layer], exp_w_down[layer])
    out = _final(xmid, w_k.T, mod3, final_g[None, :], ys_rows)
    return out.reshape(BATCH, SEQ, D_MODEL)
```

```python
import functools
import math

import jax
import jax.numpy as jnp
from jax import lax
from jax.experimental import pallas as pl
from jax.experimental.pallas import tpu as pltpu

D_MODEL = 2048
BATCH = 4
SEQ = 2048
CTX_LEN = 256
GRID_W = 64
D_SSM = 1024
D_CONV = 1024
SSM_GROUP = 16
SSM_GROUPS = 64
SSM_STATE = 64
N_EXPERTS = 64
N_EXPERT_GROUPS = 8
GROUP_SIZE = N_EXPERTS // N_EXPERT_GROUPS
TOPK_GROUPS = 4
TOP_K = 8
D_EXPERT = 512
D_SHARED = 512
ROUTED_SCALE = 2.5
N_MOD = 6
EPS = 1e-6

N_TOK = BATCH * SEQ
N_CTX_TOK = BATCH * CTX_LEN

CHUNK = 16
CHUNK_W = CHUNK * SSM_GROUP
STATE_W = 2 * SSM_STATE
N_CTX_CHUNKS = CTX_LEN // CHUNK
N_X_CHUNKS = SEQ // CHUNK
N_SEQ_CHUNKS = N_CTX_CHUNKS + N_X_CHUNKS
SCAN_ROWS = N_SEQ_CHUNKS * BATCH
S5_GROUPS_PER_STEP = 8

TM = 256
EXPERT_BM = 256
N_ASSIGN = N_TOK * TOP_K
N_BLOCKS = N_ASSIGN // EXPERT_BM + N_EXPERTS
N_ROWS = N_BLOCKS * EXPERT_BM
TM_COMBINE = 128
VMEM_LIMIT = 56 * 1024 * 1024

_F32 = jnp.float32
_BF16 = jnp.bfloat16


def _cparams(*sem):
    return pltpu.CompilerParams(dimension_semantics=sem, vmem_limit_bytes=VMEM_LIMIT)


def _const_spec(shape):
    nd = len(shape)
    return pl.BlockSpec(shape, lambda *_: (0,) * nd, pipeline_mode=pl.Buffered(1))


def _rms_scale(xf):
    return lax.rsqrt(jnp.mean(xf * xf, axis=-1, keepdims=True) + EPS)


def _silu(x):
    return x * jax.nn.sigmoid(x)


def _ada_kernel(c_ref, w_ref, b_ref, o_ref):
    s = _silu(c_ref[...])
    o_ref[...] = jnp.dot(s, w_ref[...], preferred_element_type=_F32) + b_ref[...]


def _ada_mod(c8, w_ada, b_ada):
    n = w_ada.shape[1]
    tn = 1024
    return pl.pallas_call(
        _ada_kernel,
        grid=(n // tn,),
        in_specs=[pl.BlockSpec((8, D_MODEL), lambda j: (0, 0)),
                  pl.BlockSpec((D_MODEL, tn), lambda j: (0, j)),
                  pl.BlockSpec((1, tn), lambda j: (0, j))],
        out_specs=pl.BlockSpec((8, tn), lambda j: (0, j)),
        out_shape=jax.ShapeDtypeStruct((8, n), _F32),
        compiler_params=_cparams("arbitrary"),
        name="ada_mod",
    )(c8, w_ada, b_ada)


def _modulated_norm(x_ref, mod_ref, g_ref, shift_row, scale_row):
    xf = x_ref[...]
    m = mod_ref[0]
    h = xf * _rms_scale(xf) * g_ref[...]
    return h * (1.0 + m[scale_row:scale_row + 1]) + m[shift_row:shift_row + 1]


def _inproj_kernel(x_ref, mod_ref, g_ref, w_ref, cw_ref, cb_ref, u_ref, conv_ref):
    h = _modulated_norm(x_ref, mod_ref, g_ref, 0, 1).astype(_BF16)
    u_ref[...] = jnp.dot(h, w_ref[:, 0:D_SSM], preferred_element_type=_F32)
    tm = x_ref.shape[0]
    pos = lax.broadcasted_iota(jnp.int32, (tm, 1), 0) % GRID_W
    not_first = (pos != 0).astype(_F32)
    not_last = (pos != GRID_W - 1).astype(_F32)
    cw = cw_ref[...]
    nc = 256
    for j in range(D_CONV // nc):
        lo = j * nc
        bg = jnp.dot(h, w_ref[:, D_SSM + lo:D_SSM + lo + nc], preferred_element_type=_F32)
        cg = jnp.dot(h, w_ref[:, D_SSM + D_CONV + lo:D_SSM + D_CONV + lo + nc], preferred_element_type=_F32)
        v = jnp.dot(h, w_ref[:, D_SSM + 2 * D_CONV + lo:D_SSM + 2 * D_CONV + lo + nc],
                    preferred_element_type=_F32)
        z = cg * v
        z_prev = pltpu.roll(z, 1, axis=0) * not_first
        z_next = pltpu.roll(z, tm - 1, axis=0) * not_last
        y = (cb_ref[:, lo:lo + nc] + z_prev * cw[0:1, lo:lo + nc] + z * cw[1:2, lo:lo + nc]
             + z_next * cw[2:3, lo:lo + nc])
        conv_ref[:, lo:lo + nc] = bg * y


def _inproj(x2, mod3, norm_g, w_in_bf, conv_w, conv_b):
    d_in = w_in_bf.shape[1]
    tiles_per_batch = SEQ // TM
    return pl.pallas_call(
        _inproj_kernel,
        grid=(N_TOK // TM,),
        in_specs=[pl.BlockSpec((TM, D_MODEL), lambda i: (i, 0)),
                  pl.BlockSpec((1, N_MOD, D_MODEL), lambda i: (i // tiles_per_batch, 0, 0)),
                  _const_spec((1, D_MODEL)),
                  _const_spec((D_MODEL, d_in)),
                  _const_spec((8, D_CONV)),
                  _const_spec((1, D_CONV))],
        out_specs=[pl.BlockSpec((TM, D_SSM), lambda i: (i, 0)),
                   pl.BlockSpec((TM, D_CONV), lambda i: (i, 0))],
        out_shape=[jax.ShapeDtypeStruct((N_TOK, D_SSM), _F32),
                   jax.ShapeDtypeStruct((N_TOK, D_CONV), _F32)],
        compiler_params=_cparams("arbitrary"),
        name="in_proj",
    )(x2, mod3, norm_g, w_in_bf, conv_w, conv_b)


def _ctxproj_kernel(x_ref, mod_ref, g_ref, w_ref, u_ref):
    h = _modulated_norm(x_ref, mod_ref, g_ref, 0, 1).astype(_BF16)
    u_ref[...] = jnp.dot(h, w_ref[...], preferred_element_type=_F32)


def _ctxproj(ctx2, mod3, norm_g, w_u_bf):
    return pl.pallas_call(
        _ctxproj_kernel,
        grid=(N_CTX_TOK // TM,),
        in_specs=[pl.BlockSpec((TM, D_MODEL), lambda i: (i, 0)),
                  pl.BlockSpec((1, N_MOD, D_MODEL), lambda i: (BATCH, 0, 0)),
                  _const_spec((1, D_MODEL)),
                  _const_spec((D_MODEL, D_SSM))],
        out_specs=pl.BlockSpec((TM, D_SSM), lambda i: (i, 0)),
        out_shape=jax.ShapeDtypeStruct((N_CTX_TOK, D_SSM), _F32),
        compiler_params=_cparams("arbitrary"),
        name="ctx_proj",
    )(ctx2, mod3, norm_g, w_u_bf)


N_POW = 32


def _s5_param_kernel(prow_ref, ct_ref, bt_ref, toep_ref, bpow_ref, cpow_ref, dec_ref):
    lane = lambda shape: lax.broadcasted_iota(jnp.int32, shape, 1)
    sub = lambda shape: lax.broadcasted_iota(jnp.int32, shape, 0)

    def split2(a):
        a_hi = a.astype(_BF16)
        return a_hi, (a - a_hi.astype(_F32)).astype(_BF16)

    rep = (lane((SSM_GROUP, CHUNK_W)) % SSM_GROUP == sub((SSM_GROUP, CHUNK_W))).astype(_BF16)
    sgn_col = jnp.where(sub((STATE_W, 1)) < SSM_STATE, 1.0, -1.0).astype(_F32)
    sgn_row = jnp.where(lane((1, STATE_W)) < SSM_STATE, -1.0, 1.0).astype(_F32)
    tau_col = sub((N_POW, 1)).astype(_F32)
    blk_l = lane((N_POW, CHUNK_W)) // SSM_GROUP
    tau_s = sub((N_POW, CHUNK_W))
    blk_r = sub((CHUNK_W, N_POW)) // SSM_GROUP
    tau_l = lane((CHUNK_W, N_POW))
    pick_l = lambda e: (tau_s == e).astype(_BF16)
    pick_r = lambda e: (tau_l == e).astype(_BF16)
    tn = (((0,), (0,)), ((), ()))

    strips = []
    for d in range(2):
        pr = prow_ref[d, 0]
        lam_re, lam_im, dt = pr[0:1], pr[1:2], jnp.exp(pr[2:3])
        mag = jnp.exp(tau_col * (lam_re * dt))
        ang = tau_col * (lam_im * dt)
        pw_re = mag * jnp.cos(ang)
        pw_im = mag * jnp.sin(ang)
        pw4 = jnp.concatenate(split2(pw_re) + split2(pw_im), axis=1)

        c_hi, c_lo = split2(ct_ref[d, 0])
        ct = (jnp.dot(c_hi, rep, preferred_element_type=_F32)
              + jnp.dot(c_lo, rep, preferred_element_type=_F32))
        ca = ct * sgn_col
        cb = -pltpu.roll(ct, SSM_STATE, axis=0)

        def cpow(sel):
            o = lax.dot_general(pw4, sel, tn, preferred_element_type=_F32)
            x = o[0:STATE_W] + o[STATE_W:2 * STATE_W]
            y = o[2 * STATE_W:3 * STATE_W] + o[3 * STATE_W:4 * STATE_W]
            return x * ca + y * cb

        nr = pw_re[1:2] - 1.0
        ni = pw_im[1:2]
        den = lam_re * lam_re + lam_im * lam_im
        kr = (nr * lam_re + ni * lam_im) / den
        ki = (ni * lam_re - nr * lam_im) / den
        b1 = bt_ref[d, 0]
        b2 = pltpu.roll(b1, SSM_STATE, axis=1) * sgn_row
        u1 = kr * b1 + ki * b2
        u2 = kr * b2 - ki * b1
        u1t = jnp.concatenate([u1] * CHUNK, axis=0)
        u2t = jnp.concatenate([u2] * CHUNK, axis=0)

        def bpow(sel):
            o = jnp.dot(sel, pw4, preferred_element_type=_F32)
            x = o[:, 0:STATE_W] + o[:, STATE_W:2 * STATE_W]
            y = o[:, 2 * STATE_W:3 * STATE_W] + o[:, 3 * STATE_W:4 * STATE_W]
            return x * u1t + y * u2t

        if d == 0:
            e_strip, e_b, e_c = blk_l, (CHUNK - 1) - blk_r, blk_l + 1
        else:
            e_strip, e_b, e_c = (CHUNK - 1) - blk_l, blk_r, CHUNK - blk_l
        u_hi, u_lo = split2(u1)
        k_hi, k_lo = split2(cpow(pick_l(e_strip)))
        strips.append(jnp.dot(u_hi, k_hi, preferred_element_type=_F32)
                      + jnp.dot(u_hi, k_lo, preferred_element_type=_F32)
                      + jnp.dot(u_lo, k_hi, preferred_element_type=_F32))
        bp = bpow(pick_r(e_b))
        bpow_ref[0, :, d * STATE_W:(d + 1) * STATE_W] = bp.astype(_BF16)
        bpow_ref[0, :, (2 + d) * STATE_W:(3 + d) * STATE_W] = pltpu.roll(bp, SSM_STATE, axis=1).astype(_BF16)
        cpow_ref[0, d * STATE_W:(d + 1) * STATE_W, :] = cpow(pick_l(e_c)).astype(_BF16)

        dec_ref[0, 2 * d:2 * d + 1, :] = pw_re[CHUNK:CHUNK + 1]
        dec_ref[0, 2 * d + 1:2 * d + 2, :] = pw_im[CHUNK:CHUNK + 1] * sgn_row

    zeros = jnp.zeros((SSM_GROUP, CHUNK_W), _F32)
    strip = (jnp.concatenate([strips[1], zeros], axis=1)
             + pltpu.roll(jnp.concatenate([strips[0], zeros], axis=1), CHUNK_W - SSM_GROUP, axis=1))
    for i in range(CHUNK):
        off = (CHUNK - 1 - i) * SSM_GROUP
        win = strip if off == 0 else pltpu.roll(strip, 2 * CHUNK_W - off, axis=1)
        toep_ref[0, i * SSM_GROUP:(i + 1) * SSM_GROUP, :] = win[:, 0:CHUNK_W].astype(_BF16)


def _s5_params(prow, ct2, bt1):
    g_spec = lambda shape: pl.BlockSpec(shape, lambda g: (0, g, 0, 0))
    o_spec = lambda shape: pl.BlockSpec(shape, lambda g: (g, 0, 0))
    return pl.pallas_call(
        _s5_param_kernel,
        grid=(SSM_GROUPS,),
        in_specs=[g_spec((2, 1, 8, STATE_W)),
                  g_spec((2, 1, STATE_W, SSM_GROUP)), g_spec((2, 1, SSM_GROUP, STATE_W))],
        out_specs=[o_spec((1, CHUNK_W, CHUNK_W)), o_spec((1, CHUNK_W, 4 * STATE_W)),
                   o_spec((1, 2 * STATE_W, CHUNK_W)), o_spec((1, 4, STATE_W))],
        out_shape=[jax.ShapeDtypeStruct((SSM_GROUPS, CHUNK_W, CHUNK_W), _BF16),
                   jax.ShapeDtypeStruct((SSM_GROUPS, CHUNK_W, 4 * STATE_W), _BF16),
                   jax.ShapeDtypeStruct((SSM_GROUPS, 2 * STATE_W, CHUNK_W), _BF16),
                   jax.ShapeDtypeStruct((SSM_GROUPS, 4, STATE_W), _F32)],
        compiler_params=_cparams("arbitrary"),
        name="s5_params",
    )(prow, ct2, bt1)


def _s5_kernel(ux_ref, uc_ref, toep_ref, bpow_ref, cpow_ref, dec_ref, y_ref, u_ref, s_ref, h_ref, y2_ref):
    ng = S5_GROUPS_PER_STEP
    gw = SSM_GROUP
    x0 = N_CTX_CHUNKS * BATCH

    half = CHUNK // 2
    assert half == ng
    lanes = ng * 128

    r = lax.broadcasted_iota(jnp.int32, (lanes, 1), 0)
    swapped = ((r // gw) % ng) * 128 + (r // 128) * gw + r % gw
    perm = (lax.broadcasted_iota(jnp.int32, (lanes, lanes), 1) == swapped).astype(_BF16)

    def to_chunk_layout(src_ref, seq_len, n_chunks, row0):
        for hf in range(2):
            p = jnp.concatenate(
                [jnp.concatenate([src_ref[pl.ds(b * seq_len + hf * half + il, n_chunks, stride=CHUNK), :]
                                  for il in range(half)], axis=1) for b in range(BATCH)], axis=0).astype(_BF16)
            q = jnp.dot(p, perm, preferred_element_type=_F32)
            for b in range(BATCH):
                for g in range(ng):
                    u_ref[g, hf, pl.ds(row0 + b, n_chunks, stride=BATCH), :] = (
                        q[b * n_chunks:(b + 1) * n_chunks, g * 128:(g + 1) * 128])

    to_chunk_layout(uc_ref, CTX_LEN, N_CTX_CHUNKS, 0)
    to_chunk_layout(ux_ref, SEQ, N_X_CHUNKS, x0)

    chunk_rows = lambda ref, g, lo: jnp.concatenate([ref[g, 0, lo:, :], ref[g, 1, lo:, :]], axis=1)
    for g in range(ng):
        s_ref[g] = jnp.dot(chunk_rows(u_ref, g, 0).astype(_BF16), bpow_ref[g], preferred_element_type=_F32)

    pair = 2 * BATCH
    n_ctx_pairs = N_CTX_CHUNKS // 2
    n_pairs = N_SEQ_CHUNKS // 2
    lower = lax.broadcasted_iota(jnp.int32, (pair, STATE_W), 0) < BATCH
    zero = jnp.zeros((pair, STATE_W), _F32)

    def step(t, carry):
        tb = jnp.where(t < n_ctx_pairs, n_ctx_pairs - 1 - t, n_pairs + n_ctx_pairs - 1 - t)
        rf = pl.multiple_of(t * pair, pair)
        rb = pl.multiple_of(tb * pair, pair)
        out = []
        swap = lambda v: pltpu.roll(v, BATCH, axis=0)
        for g in range(ng):
            hf, hfs, hb, hbs = carry[4 * g:4 * g + 4]
            dg = dec_ref[g]

            def advance(h, hs, a_re, a_im, s, ss):
                return a_re * h + a_im * hs + s, a_re * hs - a_im * h + ss

            col = lambda r, k: s_ref[g, pl.ds(r, pair), k * STATE_W:(k + 1) * STATE_W]
            sf, sfs = col(rf, 0), col(rf, 2)
            mid, mids = advance(hf, hfs, dg[0:1], dg[1:2], jnp.where(lower, sf, swap(sf)),
                                jnp.where(lower, sfs, swap(sfs)))
            h_ref[g, pl.ds(rf, pair), 0:STATE_W] = jnp.where(lower, hf, mid)
            end, ends = advance(mid, mids, dg[0:1], dg[1:2], sf, sfs)
            hf, hfs = jnp.where(lower, swap(end), end), jnp.where(lower, swap(ends), ends)

            sb, sbs = col(rb, 1), col(rb, 3)
            mid, mids = advance(hb, hbs, dg[2:3], dg[3:4], jnp.where(lower, swap(sb), sb),
                                jnp.where(lower, swap(sbs), sbs))
            h_ref[g, pl.ds(rb, pair), STATE_W:2 * STATE_W] = jnp.where(lower, mid, hb)
            end, ends = advance(mid, mids, dg[2:3], dg[3:4], sb, sbs)
            hb, hbs = jnp.where(lower, end, swap(end)), jnp.where(lower, ends, swap(ends))
            out += [hf, hfs, hb, hbs]
        return tuple(out)

    lax.fori_loop(0, n_pairs, step, (zero,) * (4 * ng))

    for g in range(ng):
        ux = chunk_rows(u_ref, g, x0).astype(_BF16)
        hx = h_ref[g, x0:SCAN_ROWS, :].astype(_BF16)
        y2 = (jnp.dot(ux, toep_ref[g], preferred_element_type=_F32)
              + jnp.dot(hx, cpow_ref[g], preferred_element_type=_F32))
        y2_ref[g, 0] = y2[:, 0:128]
        y2_ref[g, 1] = y2[:, 128:256]

    for hf in range(2):
        yin = jnp.concatenate(
            [jnp.concatenate([y2_ref[g, hf, pl.ds(b, N_X_CHUNKS, stride=BATCH), :] for g in range(ng)], axis=1)
             for b in range(BATCH)], axis=0)
        y_hi = yin.astype(_BF16)
        y_lo = (yin - y_hi.astype(_F32)).astype(_BF16)
        q = (jnp.dot(y_hi, perm, preferred_element_type=_F32)
             + jnp.dot(y_lo, perm, preferred_element_type=_F32))
        for b in range(BATCH):
            for jl in range(half):
                y_ref[pl.ds(b * SEQ + hf * half + jl, N_X_CHUNKS, stride=CHUNK), :] = (
                    q[b * N_X_CHUNKS:(b + 1) * N_X_CHUNKS, jl * 128:(jl + 1) * 128])


def _s5_scan(u_x, u_c, toep, bpow, cpow, dec):
    ng = S5_GROUPS_PER_STEP
    n_x_rows = N_X_CHUNKS * BATCH
    lanes = ng * SSM_GROUP
    spec = lambda r, c: pl.BlockSpec((ng, r, c), lambda i: (i, 0, 0))
    col = lambda rows: pl.BlockSpec((rows, lanes), lambda i: (0, i))
    return pl.pallas_call(
        _s5_kernel,
        grid=(SSM_GROUPS // ng,),
        in_specs=[col(N_TOK), col(N_CTX_TOK), spec(CHUNK_W, CHUNK_W), spec(CHUNK_W, 4 * STATE_W),
                  spec(2 * STATE_W, CHUNK_W), spec(4, STATE_W)],
        out_specs=col(N_TOK),
        out_shape=jax.ShapeDtypeStruct((N_TOK, D_SSM), _F32),
        scratch_shapes=[pltpu.VMEM((ng, 2, SCAN_ROWS, 128), _F32),
                        pltpu.VMEM((ng, SCAN_ROWS, 4 * STATE_W), _F32),
                        pltpu.VMEM((ng, SCAN_ROWS, 2 * STATE_W), _F32),
                        pltpu.VMEM((ng, 2, n_x_rows, 128), _F32)],
        compiler_params=_cparams("arbitrary"),
        name="s5_scan",
    )(u_x, u_c, toep, bpow, cpow, dec)


def _merge_kernel(ys_ref, u_ref, conv_ref, x_ref, mod_ref, dskip_ref, mixg_ref, n2g_ref,
                  wglu_ref, wout_ref, rwt_ref, x1_ref, h2_ref, logit_ref):
    m = mod_ref[0]
    yx = dskip_ref[...] * u_ref[...] + ys_ref[...]
    c0 = math.sqrt(2.0 / math.pi)
    ge = 0.5 * yx * (1.0 + jnp.tanh(c0 * (yx + 0.044715 * (yx * yx * yx))))
    z = jnp.dot(ge.astype(_BF16), wglu_ref[...], preferred_element_type=_F32)
    ssm_y = z[:, 0:D_SSM] * jax.nn.sigmoid(z[:, D_SSM:2 * D_SSM])
    conv_y = conv_ref[...]
    mixg = mixg_ref[...]
    heads_a = (ssm_y * _rms_scale(ssm_y) * mixg[:, 0:D_SSM]).astype(_BF16)
    heads_b = (conv_y * _rms_scale(conv_y) * mixg[:, D_SSM:]).astype(_BF16)
    mix = (jnp.dot(heads_a, wout_ref[0:D_SSM, :], preferred_element_type=_F32)
           + jnp.dot(heads_b, wout_ref[D_SSM:, :], preferred_element_type=_F32))
    x1 = x_ref[...] + m[2:3] * mix
    x1_ref[...] = x1
    h2 = x1 * _rms_scale(x1) * n2g_ref[...] * (1.0 + m[4:5]) + m[3:4]
    h2_ref[...] = h2
    h_hi = h2.astype(_BF16)
    h_lo = (h2 - h_hi.astype(_F32)).astype(_BF16)
    nt = (((1,), (1,)), ((), ()))
    p = lax.dot_general(rwt_ref[...], h_hi, nt, preferred_element_type=_F32)
    q = lax.dot_general(rwt_ref[0:N_EXPERTS, :], h_lo, nt, preferred_element_type=_F32)
    logit_ref[...] = p[0:N_EXPERTS] + p[N_EXPERTS:2 * N_EXPERTS] + q


def _merge(ys, u, conv, x2, mod3, dskip, mixg, n2g, wglu_bf, wout_bf, rwt):
    tiles_per_batch = SEQ // TM
    tok = lambda w: pl.BlockSpec((TM, w), lambda i: (i, 0))
    return pl.pallas_call(
        _merge_kernel,
        grid=(N_TOK // TM,),
        in_specs=[tok(D_SSM), tok(D_SSM), tok(D_CONV), tok(D_MODEL),
                  pl.BlockSpec((1, N_MOD, D_MODEL), lambda i: (i // tiles_per_batch, 0, 0)),
                  _const_spec((1, D_SSM)), _const_spec((1, D_MODEL)), _const_spec((1, D_MODEL)),
                  _const_spec((D_SSM, 2 * D_SSM)), _const_spec((D_MODEL, D_MODEL)),
                  _const_spec((2 * N_EXPERTS, D_MODEL))],
        out_specs=[tok(D_MODEL), tok(D_MODEL),
                   pl.BlockSpec((N_EXPERTS, TM), lambda i: (0, i))],
        out_shape=[jax.ShapeDtypeStruct((N_TOK, D_MODEL), _F32),
                   jax.ShapeDtypeStruct((N_TOK, D_MODEL), _F32),
                   jax.ShapeDtypeStruct((N_EXPERTS, N_TOK), _F32)],
        compiler_params=_cparams("arbitrary"),
        name="merge_heads",
    )(ys, u, conv, x2, mod3, dskip, mixg, n2g, wglu_bf, wout_bf, rwt)


def _route_kernel(logit_ref, bias_ref, eid_ref, w_ref, rank_ref, cnt_ref, carry_ref):
    @pl.when(pl.program_id(0) == 0)
    def _():
        carry_ref[...] = jnp.zeros_like(carry_ref)

    tm = logit_ref.shape[1]
    neg = jnp.float32(-jnp.inf)
    scores = jax.nn.sigmoid(logit_ref[...])
    biased = scores + bias_ref[:, 0:1]
    sub = lax.broadcasted_iota(jnp.int32, (GROUP_SIZE, tm), 0)
    rows = lambda a, g: a[g * GROUP_SIZE:(g + 1) * GROUP_SIZE]
    ngrp = N_EXPERT_GROUPS

    gscore = []
    for g in range(ngrp):
        bg = rows(biased, g)
        m1 = jnp.max(bg, axis=0, keepdims=True)
        first = jnp.min(jnp.where(bg == m1, sub, GROUP_SIZE), axis=0, keepdims=True)
        m2 = jnp.max(jnp.where(sub == first, neg, bg), axis=0, keepdims=True)
        gscore.append(m1 + m2)
    v = []
    for g in range(ngrp):
        beaten = jnp.zeros((1, tm), jnp.int32)
        for o in range(ngrp):
            if o != g:
                beats = (gscore[o] >= gscore[g]) if o < g else (gscore[o] > gscore[g])
                beaten = beaten + beats.astype(jnp.int32)
        v.append(jnp.where(beaten < TOPK_GROUPS, rows(biased, g), neg))
    eids = [sub + g * GROUP_SIZE for g in range(ngrp)]

    pick_ids, pick_masks = [], []
    for _ in range(TOP_K):
        m = functools.reduce(jnp.maximum, v)
        m = jnp.max(m, axis=0, keepdims=True)
        cand = functools.reduce(jnp.minimum, [jnp.where(v[g] == m, eids[g], N_EXPERTS) for g in range(ngrp)])
        pick_id = jnp.min(cand, axis=0, keepdims=True)
        masks = [eids[g] == pick_id for g in range(ngrp)]
        v = [jnp.where(masks[g], neg, v[g]) for g in range(ngrp)]
        pick_ids.append(pick_id)
        pick_masks.append(masks)

    sel = jnp.concatenate(
        [functools.reduce(jnp.logical_or, [pick_masks[k][g] for k in range(TOP_K)]).astype(_F32)
         for g in range(ngrp)], axis=0)
    before = (lax.broadcasted_iota(jnp.int32, (tm, tm), 0)
              < lax.broadcasted_iota(jnp.int32, (tm, tm), 1)).astype(_BF16)
    base = jnp.dot(sel.astype(_BF16), before, preferred_element_type=_F32) + carry_ref[:, 0:1]

    def gather_pick(a, k):
        parts = [jnp.sum(jnp.where(pick_masks[k][g], rows(a, g), 0.0), axis=0, keepdims=True)
                 for g in range(ngrp)]
        return functools.reduce(jnp.add, parts)

    picked = [gather_pick(scores, k) for k in range(TOP_K)]
    denom = functools.reduce(jnp.add, picked)
    for k in range(TOP_K):
        eid_ref[k:k + 1, :] = pick_ids[k]
        w_ref[k:k + 1, :] = picked[k] / denom * ROUTED_SCALE
        rank_ref[k:k + 1, :] = gather_pick(base, k).astype(jnp.int32)
    carry_ref[...] = carry_ref[...] + jnp.sum(sel, axis=1, keepdims=True)
    cnt_ref[...] = carry_ref[...]


def _route(logits_t, bias_col):
    tok = pl.BlockSpec((TOP_K, TM), lambda i: (0, i))
    return pl.pallas_call(
        _route_kernel,
        grid=(N_TOK // TM,),
        in_specs=[pl.BlockSpec((N_EXPERTS, TM), lambda i: (0, i)), _const_spec((N_EXPERTS, 128))],
        out_specs=[tok, tok, tok, pl.BlockSpec((N_EXPERTS, 128), lambda i: (0, 0))],
        out_shape=[jax.ShapeDtypeStruct((TOP_K, N_TOK), jnp.int32),
                   jax.ShapeDtypeStruct((TOP_K, N_TOK), _F32),
                   jax.ShapeDtypeStruct((TOP_K, N_TOK), jnp.int32),
                   jax.ShapeDtypeStruct((N_EXPERTS, 128), _F32)],
        scratch_shapes=[pltpu.VMEM((N_EXPERTS, 128), _F32)],
        compiler_params=_cparams("arbitrary"),
        name="route",
    )(logits_t, bias_col)


def _shared_kernel(h_ref, x1_ref, mod_ref, wsg_ref, wsu_ref, wsd_ref, xmid_ref):
    h = h_ref[...].astype(_BF16)
    a = jnp.dot(h, wsg_ref[...], preferred_element_type=_F32)
    b = jnp.dot(h, wsu_ref[...], preferred_element_type=_F32)
    shared = jnp.dot((_silu(a) * b).astype(_BF16), wsd_ref[...], preferred_element_type=_F32)
    xmid_ref[...] = x1_ref[...] + mod_ref[0][5:6] * shared


def _shared(h2, x1, mod3, wsg_bf, wsu_bf, wsd_bf):
    tm = 512
    tiles_per_batch = SEQ // tm
    tok = pl.BlockSpec((tm, D_MODEL), lambda i: (i, 0))
    return pl.pallas_call(
        _shared_kernel,
        grid=(N_TOK // tm,),
        in_specs=[tok, tok,
                  pl.BlockSpec((1, N_MOD, D_MODEL), lambda i: (i // tiles_per_batch, 0, 0)),
                  _const_spec((D_MODEL, D_SHARED)), _const_spec((D_MODEL, D_SHARED)),
                  _const_spec((D_SHARED, D_MODEL))],
        out_specs=tok,
        out_shape=jax.ShapeDtypeStruct((N_TOK, D_MODEL), _F32),
        compiler_params=_cparams("arbitrary"),
        name="shared",
    )(h2, x1, mod3, wsg_bf, wsu_bf, wsd_bf)


def _experts_kernel(be_ref, nact_ref, first_ref, slot_ref, nxt_ref, codes_ref, h_hbm, wg_hbm, wu_hbm, wd_hbm,
                    out_hbm, x0, x1, y0, y1, wg_f, wu_f, wd_f, wg_bf, wu_bf, wd_bf, sem, gsem, ssem):
    n_act = nact_ref[0]
    bm = EXPERT_BM
    xbuf, ybuf = (x0, x1), (y0, y1)

    def fetch(e, s):
        return (pltpu.make_async_copy(wg_hbm.at[e], wg_f.at[s], sem.at[s, 0]),
                pltpu.make_async_copy(wu_hbm.at[e], wu_f.at[s], sem.at[s, 1]),
                pltpu.make_async_copy(wd_hbm.at[e], wd_f.at[s], sem.at[s, 2]))

    def gather(blk, p):
        for r in range(bm):
            tok = codes_ref[blk * bm + r] & (N_TOK - 1)
            pltpu.make_async_copy(h_hbm.at[pl.ds(tok, 1), :], xbuf[p].at[pl.ds(r, 1), :],
                                  gsem.at[p]).start(priority=r % 2)

    def gather_wait(p):
        pltpu.make_async_copy(h_hbm.at[pl.ds(0, bm), :], xbuf[p], gsem.at[p]).wait()

    def scatter(blk, p):
        for r in range(bm):
            pltpu.make_async_copy(ybuf[p].at[pl.ds(r, 1), :], out_hbm.at[pl.ds(codes_ref[blk * bm + r], 1), :],
                                  ssem.at[p]).start(priority=r % 2)

    def scatter_wait(p):
        pltpu.make_async_copy(ybuf[p], out_hbm.at[pl.ds(0, bm), :], ssem.at[p]).wait()

    def block(blk, p):
        @pl.when(blk < n_act)
        def _():
            @pl.when(blk == 0)
            def _():
                for cp in fetch(be_ref[0], 0):
                    cp.start()
                gather(0, 0)
                y1[...] = jnp.zeros_like(y1)

            @pl.when(first_ref[blk] == 1)
            def _():
                s = slot_ref[blk]
                for cp in fetch(be_ref[blk], s):
                    cp.wait()

                @pl.when(nxt_ref[blk] >= 0)
                def _():
                    for cp in fetch(nxt_ref[blk], 1 - s):
                        cp.start()

                wg_bf[...] = wg_f[s].astype(_BF16)
                wu_bf[...] = wu_f[s].astype(_BF16)
                wd_bf[...] = wd_f[s].astype(_BF16)

            gather_wait(p)

            @pl.when(blk >= 1)
            def _():
                scatter_wait(p)

            gather(jnp.minimum(blk + 1, n_act - 1), 1 - p)
            scatter(jnp.where(blk == 0, N_BLOCKS, blk - 1), 1 - p)
            x = xbuf[p][...].astype(_BF16)
            a = jnp.dot(x, wg_bf[...], preferred_element_type=_F32)
            b = jnp.dot(x, wu_bf[...], preferred_element_type=_F32)
            ybuf[p][...] = jnp.dot((_silu(a) * b).astype(_BF16), wd_bf[...], preferred_element_type=_F32)

            @pl.when(blk == n_act - 1)
            def _():
                scatter(blk, p)
                gather_wait(1 - p)
                scatter_wait(1 - p)
                scatter_wait(p)

    i = pl.program_id(0)
    block(2 * i, 0)
    block(2 * i + 1, 1)


def _experts(block_e, n_active, first, slot, nxt, codes, h2, w_gate, w_up, w_down):
    bm = EXPERT_BM
    assert N_BLOCKS % 2 == 0
    hbm = pl.BlockSpec(memory_space=pl.ANY)
    grid_spec = pltpu.PrefetchScalarGridSpec(
        num_scalar_prefetch=6,
        grid=(N_BLOCKS // 2,),
        in_specs=[hbm, hbm, hbm, hbm],
        out_specs=hbm,
        scratch_shapes=[pltpu.VMEM((bm, D_MODEL), _F32), pltpu.VMEM((bm, D_MODEL), _F32),
                        pltpu.VMEM((bm, D_MODEL), _F32), pltpu.VMEM((bm, D_MODEL), _F32),
                        pltpu.VMEM((2, D_MODEL, D_EXPERT), _F32), pltpu.VMEM((2, D_MODEL, D_EXPERT), _F32),
                        pltpu.VMEM((2, D_EXPERT, D_MODEL), _F32),
                        pltpu.VMEM((D_MODEL, D_EXPERT), _BF16), pltpu.VMEM((D_MODEL, D_EXPERT), _BF16),
                        pltpu.VMEM((D_EXPERT, D_MODEL), _BF16),
                        pltpu.SemaphoreType.DMA((2, 3)), pltpu.SemaphoreType.DMA((2,)),
                        pltpu.SemaphoreType.DMA((2,))],
    )
    return pl.pallas_call(
        _experts_kernel,
        grid_spec=grid_spec,
        out_shape=jax.ShapeDtypeStruct((N_ASSIGN + N_ROWS + bm, D_MODEL), _F32),
        compiler_params=_cparams("arbitrary"),
        name="experts",
    )(block_e, n_active, first, slot, nxt, codes, h2, w_gate, w_up, w_down)


def _final_kernel(xmid_ref, wt_ref, mod_ref, fg_ref, *refs):
    picks, o_ref = refs[:TOP_K], refs[TOP_K]
    wt = wt_ref[...]
    routed = wt[:, 0:1] * picks[0][...]
    for k in range(1, TOP_K):
        routed = routed + wt[:, k:k + 1] * picks[k][...]
    y = xmid_ref[...] + mod_ref[0][5:6] * routed
    o_ref[...] = y * _rms_scale(y) * fg_ref[...]


def _final(xmid, w_tok, mod3, final_g, ys):
    tm = TM_COMBINE
    tiles = N_TOK // tm
    tiles_per_batch = SEQ // tm
    tok = lambda w: pl.BlockSpec((tm, w), lambda i: (i, 0))
    pick = lambda k: pl.BlockSpec((tm, D_MODEL), lambda i: (k * tiles + i, 0))
    return pl.pallas_call(
        _final_kernel,
        grid=(tiles,),
        in_specs=[tok(D_MODEL), tok(TOP_K),
                  pl.BlockSpec((1, N_MOD, D_MODEL), lambda i: (i // tiles_per_batch, 0, 0)),
                  _const_spec((1, D_MODEL))] + [pick(k) for k in range(TOP_K)],
        out_specs=tok(D_MODEL),
        out_shape=jax.ShapeDtypeStruct((N_TOK, D_MODEL), _F32),
        compiler_params=_cparams("arbitrary"),
        name="final",
    )(xmid, w_tok, mod3, final_g, *([ys] * TOP_K))


def _s5_param_layouts(lam_re, lam_im, b_re, b_im, c_re, c_im, log_dt):
    two = lambda a: jnp.concatenate([a, a], axis=-1)
    lr, li = two(lam_re), two(lam_im)
    dtb = jnp.broadcast_to(log_dt.astype(_F32)[:, :, None], lr.shape)
    zeros = jnp.zeros_like(lr)
    prow = jnp.stack([lr, li, dtb] + [zeros] * 5, axis=2)
    ct2 = jnp.concatenate([jnp.swapaxes(c_re, -1, -2), jnp.swapaxes(c_im, -1, -2)], axis=2)
    bt1 = jnp.concatenate([jnp.swapaxes(b_re, -1, -2), jnp.swapaxes(b_im, -1, -2)], axis=3)
    return prow, ct2, bt1


def kernel(x, c, ctx, c_ctx, norm1_g, norm2_g, w_ada, b_ada, w_in, ssm_lam_re, ssm_lam_im, ssm_b_re, ssm_b_im, ssm_c_re, ssm_c_im, ssm_log_dt, ssm_d, ssm_w_glu, conv_w, conv_b, mix_norm_g, w_out, router_w, router_bias, exp_w_gate, exp_w_up, exp_w_down, shared_w_gate, shared_w_up, shared_w_down, final_g):
    layer = 0
    x2 = x.reshape(N_TOK, D_MODEL)
    ctx2 = ctx.reshape(N_CTX_TOK, D_MODEL)

    c8 = jnp.concatenate([c, c_ctx[None, :], jnp.zeros((8 - BATCH - 1, D_MODEL), _F32)], axis=0)
    mod = _ada_mod(c8, w_ada[layer], b_ada[layer][None, :])
    mod3 = mod.reshape(8, N_MOD, D_MODEL)

    w_in_bf = w_in[layer].astype(_BF16)
    conv_w8 = jnp.concatenate([conv_w[layer], jnp.zeros((8 - conv_w.shape[1], D_CONV), _F32)], axis=0)
    u_x, conv_x = _inproj(x2, mod3, norm1_g[layer][None, :], w_in_bf, conv_w8, conv_b[layer][None, :])
    u_c = _ctxproj(ctx2, mod3, norm1_g[layer][None, :], w_in_bf[:, :D_SSM])

    prow, ct2, bt1 = _s5_param_layouts(ssm_lam_re[layer], ssm_lam_im[layer], ssm_b_re[layer],
                                       ssm_b_im[layer], ssm_c_re[layer], ssm_c_im[layer], ssm_log_dt[layer])
    toep, bpow, cpow, dec = _s5_params(prow, ct2, bt1)
    ys = _s5_scan(u_x, u_c, toep, bpow, cpow, dec)

    rw = router_w[layer]
    rw_hi = rw.astype(_BF16)
    rw_lo = (rw - rw_hi.astype(_F32)).astype(_BF16)
    rwt = jnp.concatenate([rw_hi.T, rw_lo.T], axis=0)
    x1, h2, logits_t = _merge(ys, u_x, conv_x, x2, mod3, ssm_d[layer][None, :],
                              mix_norm_g[layer][None, :], norm2_g[layer][None, :],
                              ssm_w_glu[layer].astype(_BF16), w_out[layer].astype(_BF16), rwt)

    bias_col = jnp.broadcast_to(router_bias[layer][:, None], (N_EXPERTS, 128))
    eid, w_k, rank, cnt = _route(logits_t, bias_col)

    counts = cnt[:, 0].astype(jnp.int32)
    padded = (counts + EXPERT_BM - 1) // EXPERT_BM * EXPERT_BM
    pend = jnp.cumsum(padded)
    pstart = pend - padded
    is_e = eid[:, :, None] == jnp.arange(N_EXPERTS, dtype=jnp.int32)
    dest = jnp.sum(jnp.where(is_e, pstart, 0), axis=-1) + rank
    n_active = (pend[-1] // EXPERT_BM).astype(jnp.int32)
    blk = jnp.minimum(jnp.arange(N_BLOCKS, dtype=jnp.int32), n_active - 1)
    ends_before = (pend[None, :] <= (blk * EXPERT_BM)[:, None]).astype(jnp.int32)
    block_e = jnp.minimum(jnp.sum(ends_before, axis=1), N_EXPERTS - 1)

    xmid = _shared(h2, x1, mod3, shared_w_gate[layer].astype(_BF16), shared_w_up[layer].astype(_BF16),
                   shared_w_down[layer].astype(_BF16))
    n_codes = N_ROWS + EXPERT_BM
    codes = (N_ASSIGN + jnp.arange(n_codes, dtype=jnp.int32)).at[dest.reshape(N_ASSIGN)].set(
        jnp.arange(N_ASSIGN, dtype=jnp.int32), unique_indices=True)
    first = jnp.concatenate([jnp.ones((1,), jnp.int32), (block_e[1:] != block_e[:-1]).astype(jnp.int32)])
    slot = (jnp.cumsum(first) - 1) % 2
    e_ids = jnp.arange(N_EXPERTS, dtype=jnp.int32)
    owner = jnp.where(padded > 0, e_ids, N_EXPERTS)
    later = jnp.min(jnp.where(e_ids[None, :] > e_ids[:, None], owner[None, :], N_EXPERTS), axis=1)
    nxt_e = jnp.where(later == N_EXPERTS, -1, later)
    nxt = jnp.sum(jnp.where(block_e[:, None] == e_ids[None, :], nxt_e[None, :], 0), axis=1)
    ys_rows = _experts(block_e, n_active[None], first, slot, nxt, codes, h2,
                       exp_w_gate[layer], exp_w_up[layer], exp_w_down[layer])
    out = _final(xmid, w_k.T, mod3, final_g[None, :], ys_rows)
    return out.reshape(BATCH, SEQ, D_MODEL)
```

```python
import functools
import math

import jax
import jax.numpy as jnp
from jax import lax
from jax.experimental import pallas as pl
from jax.experimental.pallas import tpu as pltpu

D_MODEL = 2048
BATCH = 4
SEQ = 2048
CTX_LEN = 256
GRID_W = 64
D_SSM = 1024
D_CONV = 1024
SSM_GROUP = 16
SSM_GROUPS = 64
SSM_STATE = 64
N_EXPERTS = 64
N_EXPERT_GROUPS = 8
GROUP_SIZE = N_EXPERTS // N_EXPERT_GROUPS
TOPK_GROUPS = 4
TOP_K = 8
D_EXPERT = 512
D_SHARED = 512
ROUTED_SCALE = 2.5
N_MOD = 6
EPS = 1e-6

N_TOK = BATCH * SEQ
N_CTX_TOK = BATCH * CTX_LEN

CHUNK = 16
CHUNK_W = CHUNK * SSM_GROUP
STATE_W = 2 * SSM_STATE
N_CTX_CHUNKS = CTX_LEN // CHUNK
N_X_CHUNKS = SEQ // CHUNK
N_SEQ_CHUNKS = N_CTX_CHUNKS + N_X_CHUNKS
SCAN_ROWS = N_SEQ_CHUNKS * BATCH
S5_GROUPS_PER_STEP = 8

TM = 256
EXPERT_BM = 128
N_ASSIGN = N_TOK * TOP_K
N_BLOCKS = N_ASSIGN // EXPERT_BM + N_EXPERTS
N_ROWS = N_BLOCKS * EXPERT_BM
TM_COMBINE = 128
VMEM_LIMIT = 56 * 1024 * 1024

_F32 = jnp.float32
_BF16 = jnp.bfloat16


def _cparams(*sem):
    return pltpu.CompilerParams(dimension_semantics=sem, vmem_limit_bytes=VMEM_LIMIT)


def _const_spec(shape):
    nd = len(shape)
    return pl.BlockSpec(shape, lambda *_: (0,) * nd, pipeline_mode=pl.Buffered(1))


def _rms_scale(xf):
    return lax.rsqrt(jnp.mean(xf * xf, axis=-1, keepdims=True) + EPS)


def _silu(x):
    return x * jax.nn.sigmoid(x)


def _ada_kernel(c_ref, w_ref, b_ref, o_ref):
    s = _silu(c_ref[...])
    o_ref[...] = jnp.dot(s, w_ref[...], preferred_element_type=_F32) + b_ref[...]


def _ada_mod(c8, w_ada, b_ada):
    n = w_ada.shape[1]
    tn = 1024
    return pl.pallas_call(
        _ada_kernel,
        grid=(n // tn,),
        in_specs=[pl.BlockSpec((8, D_MODEL), lambda j: (0, 0)),
                  pl.BlockSpec((D_MODEL, tn), lambda j: (0, j)),
                  pl.BlockSpec((1, tn), lambda j: (0, j))],
        out_specs=pl.BlockSpec((8, tn), lambda j: (0, j)),
        out_shape=jax.ShapeDtypeStruct((8, n), _F32),
        compiler_params=_cparams("arbitrary"),
        name="ada_mod",
    )(c8, w_ada, b_ada)


def _modulated_norm(x_ref, mod_ref, g_ref, shift_row, scale_row):
    xf = x_ref[...]
    m = mod_ref[0]
    h = xf * _rms_scale(xf) * g_ref[...]
    return h * (1.0 + m[scale_row:scale_row + 1]) + m[shift_row:shift_row + 1]


def _inproj_kernel(x_ref, mod_ref, g_ref, w_ref, cw_ref, cb_ref, u_ref, conv_ref):
    h = _modulated_norm(x_ref, mod_ref, g_ref, 0, 1).astype(_BF16)
    u_ref[...] = jnp.dot(h, w_ref[:, 0:D_SSM], preferred_element_type=_F32)
    tm = x_ref.shape[0]
    pos = lax.broadcasted_iota(jnp.int32, (tm, 1), 0) % GRID_W
    not_first = (pos != 0).astype(_F32)
    not_last = (pos != GRID_W - 1).astype(_F32)
    cw = cw_ref[...]
    nc = 256
    for j in range(D_CONV // nc):
        lo = j * nc
        bg = jnp.dot(h, w_ref[:, D_SSM + lo:D_SSM + lo + nc], preferred_element_type=_F32)
        cg = jnp.dot(h, w_ref[:, D_SSM + D_CONV + lo:D_SSM + D_CONV + lo + nc], preferred_element_type=_F32)
        v = jnp.dot(h, w_ref[:, D_SSM + 2 * D_CONV + lo:D_SSM + 2 * D_CONV + lo + nc],
                    preferred_element_type=_F32)
        z = cg * v
        z_prev = pltpu.roll(z, 1, axis=0) * not_first
        z_next = pltpu.roll(z, tm - 1, axis=0) * not_last
        y = (cb_ref[:, lo:lo + nc] + z_prev * cw[0:1, lo:lo + nc] + z * cw[1:2, lo:lo + nc]
             + z_next * cw[2:3, lo:lo + nc])
        conv_ref[:, lo:lo + nc] = bg * y


def _inproj(x2, mod3, norm_g, w_in_bf, conv_w, conv_b):
    d_in = w_in_bf.shape[1]
    tiles_per_batch = SEQ // TM
    return pl.pallas_call(
        _inproj_kernel,
        grid=(N_TOK // TM,),
        in_specs=[pl.BlockSpec((TM, D_MODEL), lambda i: (i, 0)),
                  pl.BlockSpec((1, N_MOD, D_MODEL), lambda i: (i // tiles_per_batch, 0, 0)),
                  _const_spec((1, D_MODEL)),
                  _const_spec((D_MODEL, d_in)),
                  _const_spec((8, D_CONV)),
                  _const_spec((1, D_CONV))],
        out_specs=[pl.BlockSpec((TM, D_SSM), lambda i: (i, 0)),
                   pl.BlockSpec((TM, D_CONV), lambda i: (i, 0))],
        out_shape=[jax.ShapeDtypeStruct((N_TOK, D_SSM), _F32),
                   jax.ShapeDtypeStruct((N_TOK, D_CONV), _F32)],
        compiler_params=_cparams("arbitrary"),
        name="in_proj",
    )(x2, mod3, norm_g, w_in_bf, conv_w, conv_b)


def _ctxproj_kernel(x_ref, mod_ref, g_ref, w_ref, u_ref):
    h = _modulated_norm(x_ref, mod_ref, g_ref, 0, 1).astype(_BF16)
    u_ref[...] = jnp.dot(h, w_ref[...], preferred_element_type=_F32)


def _ctxproj(ctx2, mod3, norm_g, w_u_bf):
    return pl.pallas_call(
        _ctxproj_kernel,
        grid=(N_CTX_TOK // TM,),
        in_specs=[pl.BlockSpec((TM, D_MODEL), lambda i: (i, 0)),
                  pl.BlockSpec((1, N_MOD, D_MODEL), lambda i: (BATCH, 0, 0)),
                  _const_spec((1, D_MODEL)),
                  _const_spec((D_MODEL, D_SSM))],
        out_specs=pl.BlockSpec((TM, D_SSM), lambda i: (i, 0)),
        out_shape=jax.ShapeDtypeStruct((N_CTX_TOK, D_SSM), _F32),
        compiler_params=_cparams("arbitrary"),
        name="ctx_proj",
    )(ctx2, mod3, norm_g, w_u_bf)


N_POW = 32


def _s5_param_kernel(prow_ref, ct_ref, bt_ref, toep_ref, bpow_ref, cpow_ref, dec_ref):
    lane = lambda shape: lax.broadcasted_iota(jnp.int32, shape, 1)
    sub = lambda shape: lax.broadcasted_iota(jnp.int32, shape, 0)

    def split2(a):
        a_hi = a.astype(_BF16)
        return a_hi, (a - a_hi.astype(_F32)).astype(_BF16)

    rep = (lane((SSM_GROUP, CHUNK_W)) % SSM_GROUP == sub((SSM_GROUP, CHUNK_W))).astype(_BF16)
    sgn_col = jnp.where(sub((STATE_W, 1)) < SSM_STATE, 1.0, -1.0).astype(_F32)
    sgn_row = jnp.where(lane((1, STATE_W)) < SSM_STATE, -1.0, 1.0).astype(_F32)
    tau_col = sub((N_POW, 1)).astype(_F32)
    blk_l = lane((N_POW, CHUNK_W)) // SSM_GROUP
    tau_s = sub((N_POW, CHUNK_W))
    blk_r = sub((CHUNK_W, N_POW)) // SSM_GROUP
    tau_l = lane((CHUNK_W, N_POW))
    pick_l = lambda e: (tau_s == e).astype(_BF16)
    pick_r = lambda e: (tau_l == e).astype(_BF16)
    tn = (((0,), (0,)), ((), ()))

    strips = []
    for d in range(2):
        pr = prow_ref[d, 0]
        lam_re, lam_im, dt = pr[0:1], pr[1:2], jnp.exp(pr[2:3])
        mag = jnp.exp(tau_col * (lam_re * dt))
        ang = tau_col * (lam_im * dt)
        pw_re = mag * jnp.cos(ang)
        pw_im = mag * jnp.sin(ang)
        pw4 = jnp.concatenate(split2(pw_re) + split2(pw_im), axis=1)

        c_hi, c_lo = split2(ct_ref[d, 0])
        ct = (jnp.dot(c_hi, rep, preferred_element_type=_F32)
              + jnp.dot(c_lo, rep, preferred_element_type=_F32))
        ca = ct * sgn_col
        cb = -pltpu.roll(ct, SSM_STATE, axis=0)

        def cpow(sel):
            o = lax.dot_general(pw4, sel, tn, preferred_element_type=_F32)
            x = o[0:STATE_W] + o[STATE_W:2 * STATE_W]
            y = o[2 * STATE_W:3 * STATE_W] + o[3 * STATE_W:4 * STATE_W]
            return x * ca + y * cb

        nr = pw_re[1:2] - 1.0
        ni = pw_im[1:2]
        den = lam_re * lam_re + lam_im * lam_im
        kr = (nr * lam_re + ni * lam_im) / den
        ki = (ni * lam_re - nr * lam_im) / den
        b1 = bt_ref[d, 0]
        b2 = pltpu.roll(b1, SSM_STATE, axis=1) * sgn_row
        u1 = kr * b1 + ki * b2
        u2 = kr * b2 - ki * b1
        u1t = jnp.concatenate([u1] * CHUNK, axis=0)
        u2t = jnp.concatenate([u2] * CHUNK, axis=0)

        def bpow(sel):
            o = jnp.dot(sel, pw4, preferred_element_type=_F32)
            x = o[:, 0:STATE_W] + o[:, STATE_W:2 * STATE_W]
            y = o[:, 2 * STATE_W:3 * STATE_W] + o[:, 3 * STATE_W:4 * STATE_W]
            return x * u1t + y * u2t

        if d == 0:
            e_strip, e_b, e_c = blk_l, (CHUNK - 1) - blk_r, blk_l + 1
        else:
            e_strip, e_b, e_c = (CHUNK - 1) - blk_l, blk_r, CHUNK - blk_l
        u_hi, u_lo = split2(u1)
        k_hi, k_lo = split2(cpow(pick_l(e_strip)))
        strips.append(jnp.dot(u_hi, k_hi, preferred_element_type=_F32)
                      + jnp.dot(u_hi, k_lo, preferred_element_type=_F32)
                      + jnp.dot(u_lo, k_hi, preferred_element_type=_F32))
        bp = bpow(pick_r(e_b))
        bpow_ref[0, :, d * STATE_W:(d + 1) * STATE_W] = bp.astype(_BF16)
        bpow_ref[0, :, (2 + d) * STATE_W:(3 + d) * STATE_W] = pltpu.roll(bp, SSM_STATE, axis=1).astype(_BF16)
        cpow_ref[0, d * STATE_W:(d + 1) * STATE_W, :] = cpow(pick_l(e_c)).astype(_BF16)

        dec_ref[0, 2 * d:2 * d + 1, :] = pw_re[CHUNK:CHUNK + 1]
        dec_ref[0, 2 * d + 1:2 * d + 2, :] = pw_im[CHUNK:CHUNK + 1] * sgn_row

    zeros = jnp.zeros((SSM_GROUP, CHUNK_W), _F32)
    strip = (jnp.concatenate([strips[1], zeros], axis=1)
             + pltpu.roll(jnp.concatenate([strips[0], zeros], axis=1), CHUNK_W - SSM_GROUP, axis=1))
    for i in range(CHUNK):
        off = (CHUNK - 1 - i) * SSM_GROUP
        win = strip if off == 0 else pltpu.roll(strip, 2 * CHUNK_W - off, axis=1)
        toep_ref[0, i * SSM_GROUP:(i + 1) * SSM_GROUP, :] = win[:, 0:CHUNK_W].astype(_BF16)


def _s5_params(prow, ct2, bt1):
    g_spec = lambda shape: pl.BlockSpec(shape, lambda g: (0, g, 0, 0))
    o_spec = lambda shape: pl.BlockSpec(shape, lambda g: (g, 0, 0))
    return pl.pallas_call(
        _s5_param_kernel,
        grid=(SSM_GROUPS,),
        in_specs=[g_spec((2, 1, 8, STATE_W)),
                  g_spec((2, 1, STATE_W, SSM_GROUP)), g_spec((2, 1, SSM_GROUP, STATE_W))],
        out_specs=[o_spec((1, CHUNK_W, CHUNK_W)), o_spec((1, CHUNK_W, 4 * STATE_W)),
                   o_spec((1, 2 * STATE_W, CHUNK_W)), o_spec((1, 4, STATE_W))],
        out_shape=[jax.ShapeDtypeStruct((SSM_GROUPS, CHUNK_W, CHUNK_W), _BF16),
                   jax.ShapeDtypeStruct((SSM_GROUPS, CHUNK_W, 4 * STATE_W), _BF16),
                   jax.ShapeDtypeStruct((SSM_GROUPS, 2 * STATE_W, CHUNK_W), _BF16),
                   jax.ShapeDtypeStruct((SSM_GROUPS, 4, STATE_W), _F32)],
        compiler_params=_cparams("arbitrary"),
        name="s5_params",
    )(prow, ct2, bt1)


def _s5_kernel(ux_ref, uc_ref, toep_ref, bpow_ref, cpow_ref, dec_ref, y_ref, u_ref, s_ref, h_ref, y2_ref):
    ng = S5_GROUPS_PER_STEP
    gw = SSM_GROUP
    x0 = N_CTX_CHUNKS * BATCH

    half = CHUNK // 2
    assert half == ng
    lanes = ng * 128

    r = lax.broadcasted_iota(jnp.int32, (lanes, 1), 0)
    swapped = ((r // gw) % ng) * 128 + (r // 128) * gw + r % gw
    perm = (lax.broadcasted_iota(jnp.int32, (lanes, lanes), 1) == swapped).astype(_BF16)

    def to_chunk_layout(src_ref, seq_len, n_chunks, row0):
        for hf in range(2):
            p = jnp.concatenate(
                [jnp.concatenate([src_ref[pl.ds(b * seq_len + hf * half + il, n_chunks, stride=CHUNK), :]
                                  for il in range(half)], axis=1) for b in range(BATCH)], axis=0).astype(_BF16)
            q = jnp.dot(p, perm, preferred_element_type=_F32)
            for b in range(BATCH):
                for g in range(ng):
                    u_ref[g, hf, pl.ds(row0 + b, n_chunks, stride=BATCH), :] = (
                        q[b * n_chunks:(b + 1) * n_chunks, g * 128:(g + 1) * 128])

    to_chunk_layout(uc_ref, CTX_LEN, N_CTX_CHUNKS, 0)
    to_chunk_layout(ux_ref, SEQ, N_X_CHUNKS, x0)

    chunk_rows = lambda ref, g, lo: jnp.concatenate([ref[g, 0, lo:, :], ref[g, 1, lo:, :]], axis=1)
    for g in range(ng):
        s_ref[g] = jnp.dot(chunk_rows(u_ref, g, 0).astype(_BF16), bpow_ref[g], preferred_element_type=_F32)

    pair = 2 * BATCH
    n_ctx_pairs = N_CTX_CHUNKS // 2
    n_pairs = N_SEQ_CHUNKS // 2
    lower = lax.broadcasted_iota(jnp.int32, (pair, STATE_W), 0) < BATCH
    zero = jnp.zeros((pair, STATE_W), _F32)

    def step(t, carry):
        tb = jnp.where(t < n_ctx_pairs, n_ctx_pairs - 1 - t, n_pairs + n_ctx_pairs - 1 - t)
        rf = pl.multiple_of(t * pair, pair)
        rb = pl.multiple_of(tb * pair, pair)
        out = []
        swap = lambda v: pltpu.roll(v, BATCH, axis=0)
        for g in range(ng):
            hf, hfs, hb, hbs = carry[4 * g:4 * g + 4]
            dg = dec_ref[g]

            def advance(h, hs, a_re, a_im, s, ss):
                return a_re * h + a_im * hs + s, a_re * hs - a_im * h + ss

            col = lambda r, k: s_ref[g, pl.ds(r, pair), k * STATE_W:(k + 1) * STATE_W]
            sf, sfs = col(rf, 0), col(rf, 2)
            mid, mids = advance(hf, hfs, dg[0:1], dg[1:2], jnp.where(lower, sf, swap(sf)),
                                jnp.where(lower, sfs, swap(sfs)))
            h_ref[g, pl.ds(rf, pair), 0:STATE_W] = jnp.where(lower, hf, mid)
            end, ends = advance(mid, mids, dg[0:1], dg[1:2], sf, sfs)
            hf, hfs = jnp.where(lower, swap(end), end), jnp.where(lower, swap(ends), ends)

            sb, sbs = col(rb, 1), col(rb, 3)
            mid, mids = advance(hb, hbs, dg[2:3], dg[3:4], jnp.where(lower, swap(sb), sb),
                                jnp.where(lower, swap(sbs), sbs))
            h_ref[g, pl.ds(rb, pair), STATE_W:2 * STATE_W] = jnp.where(lower, mid, hb)
            end, ends = advance(mid, mids, dg[2:3], dg[3:4], sb, sbs)
            hb, hbs = jnp.where(lower, end, swap(end)), jnp.where(lower, ends, swap(ends))
            out += [hf, hfs, hb, hbs]
        return tuple(out)

    lax.fori_loop(0, n_pairs, step, (zero,) * (4 * ng))

    for g in range(ng):
        ux = chunk_rows(u_ref, g, x0).astype(_BF16)
        hx = h_ref[g, x0:SCAN_ROWS, :].astype(_BF16)
        y2 = (jnp.dot(ux, toep_ref[g], preferred_element_type=_F32)
              + jnp.dot(hx, cpow_ref[g], preferred_element_type=_F32))
        y2_ref[g, 0] = y2[:, 0:128]
        y2_ref[g, 1] = y2[:, 128:256]

    for hf in range(2):
        yin = jnp.concatenate(
            [jnp.concatenate([y2_ref[g, hf, pl.ds(b, N_X_CHUNKS, stride=BATCH), :] for g in range(ng)], axis=1)
             for b in range(BATCH)], axis=0)
        y_hi = yin.astype(_BF16)
        y_lo = (yin - y_hi.astype(_F32)).astype(_BF16)
        q = (jnp.dot(y_hi, perm, preferred_element_type=_F32)
             + jnp.dot(y_lo, perm, preferred_element_type=_F32))
        for b in range(BATCH):
            for jl in range(half):
                y_ref[pl.ds(b * SEQ + hf * half + jl, N_X_CHUNKS, stride=CHUNK), :] = (
                    q[b * N_X_CHUNKS:(b + 1) * N_X_CHUNKS, jl * 128:(jl + 1) * 128])


def _s5_scan(u_x, u_c, toep, bpow, cpow, dec):
    ng = S5_GROUPS_PER_STEP
    n_x_rows = N_X_CHUNKS * BATCH
    lanes = ng * SSM_GROUP
    spec = lambda r, c: pl.BlockSpec((ng, r, c), lambda i: (i, 0, 0))
    col = lambda rows: pl.BlockSpec((rows, lanes), lambda i: (0, i))
    return pl.pallas_call(
        _s5_kernel,
        grid=(SSM_GROUPS // ng,),
        in_specs=[col(N_TOK), col(N_CTX_TOK), spec(CHUNK_W, CHUNK_W), spec(CHUNK_W, 4 * STATE_W),
                  spec(2 * STATE_W, CHUNK_W), spec(4, STATE_W)],
        out_specs=col(N_TOK),
        out_shape=jax.ShapeDtypeStruct((N_TOK, D_SSM), _F32),
        scratch_shapes=[pltpu.VMEM((ng, 2, SCAN_ROWS, 128), _F32),
                        pltpu.VMEM((ng, SCAN_ROWS, 4 * STATE_W), _F32),
                        pltpu.VMEM((ng, SCAN_ROWS, 2 * STATE_W), _F32),
                        pltpu.VMEM((ng, 2, n_x_rows, 128), _F32)],
        compiler_params=_cparams("arbitrary"),
        name="s5_scan",
    )(u_x, u_c, toep, bpow, cpow, dec)


def _merge_kernel(ys_ref, u_ref, conv_ref, x_ref, mod_ref, dskip_ref, mixg_ref, n2g_ref,
                  wglu_ref, wout_ref, rwt_ref, x1_ref, h2_ref, logit_ref):
    m = mod_ref[0]
    yx = dskip_ref[...] * u_ref[...] + ys_ref[...]
    c0 = math.sqrt(2.0 / math.pi)
    ge = 0.5 * yx * (1.0 + jnp.tanh(c0 * (yx + 0.044715 * (yx * yx * yx))))
    z = jnp.dot(ge.astype(_BF16), wglu_ref[...], preferred_element_type=_F32)
    ssm_y = z[:, 0:D_SSM] * jax.nn.sigmoid(z[:, D_SSM:2 * D_SSM])
    conv_y = conv_ref[...]
    mixg = mixg_ref[...]
    heads_a = (ssm_y * _rms_scale(ssm_y) * mixg[:, 0:D_SSM]).astype(_BF16)
    heads_b = (conv_y * _rms_scale(conv_y) * mixg[:, D_SSM:]).astype(_BF16)
    mix = (jnp.dot(heads_a, wout_ref[0:D_SSM, :], preferred_element_type=_F32)
           + jnp.dot(heads_b, wout_ref[D_SSM:, :], preferred_element_type=_F32))
    x1 = x_ref[...] + m[2:3] * mix
    x1_ref[...] = x1
    h2 = x1 * _rms_scale(x1) * n2g_ref[...] * (1.0 + m[4:5]) + m[3:4]
    h2_ref[...] = h2
    h_hi = h2.astype(_BF16)
    h_lo = (h2 - h_hi.astype(_F32)).astype(_BF16)
    nt = (((1,), (1,)), ((), ()))
    p = lax.dot_general(rwt_ref[...], h_hi, nt, preferred_element_type=_F32)
    q = lax.dot_general(rwt_ref[0:N_EXPERTS, :], h_lo, nt, preferred_element_type=_F32)
    logit_ref[...] = p[0:N_EXPERTS] + p[N_EXPERTS:2 * N_EXPERTS] + q


def _merge(ys, u, conv, x2, mod3, dskip, mixg, n2g, wglu_bf, wout_bf, rwt):
    tiles_per_batch = SEQ // TM
    tok = lambda w: pl.BlockSpec((TM, w), lambda i: (i, 0))
    return pl.pallas_call(
        _merge_kernel,
        grid=(N_TOK // TM,),
        in_specs=[tok(D_SSM), tok(D_SSM), tok(D_CONV), tok(D_MODEL),
                  pl.BlockSpec((1, N_MOD, D_MODEL), lambda i: (i // tiles_per_batch, 0, 0)),
                  _const_spec((1, D_SSM)), _const_spec((1, D_MODEL)), _const_spec((1, D_MODEL)),
                  _const_spec((D_SSM, 2 * D_SSM)), _const_spec((D_MODEL, D_MODEL)),
                  _const_spec((2 * N_EXPERTS, D_MODEL))],
        out_specs=[tok(D_MODEL), tok(D_MODEL),
                   pl.BlockSpec((N_EXPERTS, TM), lambda i: (0, i))],
        out_shape=[jax.ShapeDtypeStruct((N_TOK, D_MODEL), _F32),
                   jax.ShapeDtypeStruct((N_TOK, D_MODEL), _F32),
                   jax.ShapeDtypeStruct((N_EXPERTS, N_TOK), _F32)],
        compiler_params=_cparams("arbitrary"),
        name="merge_heads",
    )(ys, u, conv, x2, mod3, dskip, mixg, n2g, wglu_bf, wout_bf, rwt)


def _route_kernel(logit_ref, bias_ref, eid_ref, w_ref, rank_ref, cnt_ref, carry_ref):
    @pl.when(pl.program_id(0) == 0)
    def _():
        carry_ref[...] = jnp.zeros_like(carry_ref)

    tm = logit_ref.shape[1]
    neg = jnp.float32(-jnp.inf)
    scores = jax.nn.sigmoid(logit_ref[...])
    biased = scores + bias_ref[:, 0:1]
    sub = lax.broadcasted_iota(jnp.int32, (GROUP_SIZE, tm), 0)
    rows = lambda a, g: a[g * GROUP_SIZE:(g + 1) * GROUP_SIZE]
    ngrp = N_EXPERT_GROUPS

    gscore = []
    for g in range(ngrp):
        bg = rows(biased, g)
        m1 = jnp.max(bg, axis=0, keepdims=True)
        first = jnp.min(jnp.where(bg == m1, sub, GROUP_SIZE), axis=0, keepdims=True)
        m2 = jnp.max(jnp.where(sub == first, neg, bg), axis=0, keepdims=True)
        gscore.append(m1 + m2)
    v = []
    for g in range(ngrp):
        beaten = jnp.zeros((1, tm), jnp.int32)
        for o in range(ngrp):
            if o != g:
                beats = (gscore[o] >= gscore[g]) if o < g else (gscore[o] > gscore[g])
                beaten = beaten + beats.astype(jnp.int32)
        v.append(jnp.where(beaten < TOPK_GROUPS, rows(biased, g), neg))
    eids = [sub + g * GROUP_SIZE for g in range(ngrp)]

    pick_ids, pick_masks = [], []
    for _ in range(TOP_K):
        m = functools.reduce(jnp.maximum, v)
        m = jnp.max(m, axis=0, keepdims=True)
        cand = functools.reduce(jnp.minimum, [jnp.where(v[g] == m, eids[g], N_EXPERTS) for g in range(ngrp)])
        pick_id = jnp.min(cand, axis=0, keepdims=True)
        masks = [eids[g] == pick_id for g in range(ngrp)]
        v = [jnp.where(masks[g], neg, v[g]) for g in range(ngrp)]
        pick_ids.append(pick_id)
        pick_masks.append(masks)

    sel = jnp.concatenate(
        [functools.reduce(jnp.logical_or, [pick_masks[k][g] for k in range(TOP_K)]).astype(_F32)
         for g in range(ngrp)], axis=0)
    before = (lax.broadcasted_iota(jnp.int32, (tm, tm), 0)
              < lax.broadcasted_iota(jnp.int32, (tm, tm), 1)).astype(_BF16)
    base = jnp.dot(sel.astype(_BF16), before, preferred_element_type=_F32) + carry_ref[:, 0:1]

    def gather_pick(a, k):
        parts = [jnp.sum(jnp.where(pick_masks[k][g], rows(a, g), 0.0), axis=0, keepdims=True)
                 for g in range(ngrp)]
        return functools.reduce(jnp.add, parts)

    picked = [gather_pick(scores, k) for k in range(TOP_K)]
    denom = functools.reduce(jnp.add, picked)
    for k in range(TOP_K):
        eid_ref[k:k + 1, :] = pick_ids[k]
        w_ref[k:k + 1, :] = picked[k] / denom * ROUTED_SCALE
        rank_ref[k:k + 1, :] = gather_pick(base, k).astype(jnp.int32)
    carry_ref[...] = carry_ref[...] + jnp.sum(sel, axis=1, keepdims=True)
    cnt_ref[...] = carry_ref[...]


def _route(logits_t, bias_col):
    tok = pl.BlockSpec((TOP_K, TM), lambda i: (0, i))
    return pl.pallas_call(
        _route_kernel,
        grid=(N_TOK // TM,),
        in_specs=[pl.BlockSpec((N_EXPERTS, TM), lambda i: (0, i)), _const_spec((N_EXPERTS, 128))],
        out_specs=[tok, tok, tok, pl.BlockSpec((N_EXPERTS, 128), lambda i: (0, 0))],
        out_shape=[jax.ShapeDtypeStruct((TOP_K, N_TOK), jnp.int32),
                   jax.ShapeDtypeStruct((TOP_K, N_TOK), _F32),
                   jax.ShapeDtypeStruct((TOP_K, N_TOK), jnp.int32),
                   jax.ShapeDtypeStruct((N_EXPERTS, 128), _F32)],
        scratch_shapes=[pltpu.VMEM((N_EXPERTS, 128), _F32)],
        compiler_params=_cparams("arbitrary"),
        name="route",
    )(logits_t, bias_col)


def _rowmap_kernel(dest_ref, padstart_ref, padn_ref, h_ref, x1_ref, mod_ref, wsg_ref, wsu_ref, wsd_ref,
                   xmid_ref, codes_ref):
    i = pl.program_id(0)
    tm = h_ref.shape[0]
    base = i * tm

    @pl.when(i == 0)
    def _():
        def fill(e, done):
            def one(r, c):
                codes_ref[padstart_ref[e] + r] = N_ASSIGN + done + r
                return c
            lax.fori_loop(0, padn_ref[e], one, 0)
            return done + padn_ref[e]
        lax.fori_loop(0, N_EXPERTS, fill, 0)
        for r in range(EXPERT_BM):
            codes_ref[N_ROWS + r] = N_ROWS + r

    for t in range(tm):
        ds = [dest_ref[k * N_TOK + base + t] for k in range(TOP_K)]
        for k in range(TOP_K):
            codes_ref[ds[k]] = k * N_TOK + base + t

    h = h_ref[...].astype(_BF16)
    a = jnp.dot(h, wsg_ref[...], preferred_element_type=_F32)
    b = jnp.dot(h, wsu_ref[...], preferred_element_type=_F32)
    shared = jnp.dot((_silu(a) * b).astype(_BF16), wsd_ref[...], preferred_element_type=_F32)
    xmid_ref[...] = x1_ref[...] + mod_ref[0][5:6] * shared


def _rowmap(dest, pad_start, pad_n, h2, x1, mod3, wsg_bf, wsu_bf, wsd_bf):
    tm = TM
    tiles_per_batch = SEQ // tm
    tok = pl.BlockSpec((tm, D_MODEL), lambda i, *_: (i, 0))
    const = lambda shape: pl.BlockSpec(shape, lambda i, *_: (0,) * len(shape), pipeline_mode=pl.Buffered(1))
    grid_spec = pltpu.PrefetchScalarGridSpec(
        num_scalar_prefetch=3,
        grid=(N_TOK // tm,),
        in_specs=[tok, tok,
                  pl.BlockSpec((1, N_MOD, D_MODEL), lambda i, *_: (i // tiles_per_batch, 0, 0)),
                  const((D_MODEL, D_SHARED)), const((D_MODEL, D_SHARED)), const((D_SHARED, D_MODEL))],
        out_specs=[tok, pl.BlockSpec(memory_space=pltpu.SMEM)],
    )
    return pl.pallas_call(
        _rowmap_kernel,
        grid_spec=grid_spec,
        out_shape=[jax.ShapeDtypeStruct((N_TOK, D_MODEL), _F32),
                   jax.ShapeDtypeStruct((N_ROWS + EXPERT_BM,), jnp.int32)],
        compiler_params=_cparams("arbitrary"),
        name="rowmap",
    )(dest.reshape(N_ASSIGN), pad_start, pad_n, h2, x1, mod3, wsg_bf, wsu_bf, wsd_bf)


def _experts_kernel(be_ref, nact_ref, first_ref, slot_ref, nxt_ref, codes_ref, h_hbm, wg_hbm, wu_hbm, wd_hbm,
                    out_hbm, x0, x1, y0, y1, wg_f, wu_f, wd_f, wg_bf, wu_bf, wd_bf, sem, gsem, ssem):
    n_act = nact_ref[0]
    bm = EXPERT_BM
    xbuf, ybuf = (x0, x1), (y0, y1)

    def fetch(e, s):
        return (pltpu.make_async_copy(wg_hbm.at[e], wg_f.at[s], sem.at[s, 0]),
                pltpu.make_async_copy(wu_hbm.at[e], wu_f.at[s], sem.at[s, 1]),
                pltpu.make_async_copy(wd_hbm.at[e], wd_f.at[s], sem.at[s, 2]))

    def gather(blk, p):
        for r in range(bm):
            tok = codes_ref[blk * bm + r] & (N_TOK - 1)
            pltpu.make_async_copy(h_hbm.at[pl.ds(tok, 1), :], xbuf[p].at[pl.ds(r, 1), :],
                                  gsem.at[p]).start(priority=r % 2)

    def gather_wait(p):
        pltpu.make_async_copy(h_hbm.at[pl.ds(0, bm), :], xbuf[p], gsem.at[p]).wait()

    def scatter(blk, p):
        for r in range(bm):
            pltpu.make_async_copy(ybuf[p].at[pl.ds(r, 1), :], out_hbm.at[pl.ds(codes_ref[blk * bm + r], 1), :],
                                  ssem.at[p]).start(priority=r % 2)

    def scatter_wait(p):
        pltpu.make_async_copy(ybuf[p], out_hbm.at[pl.ds(0, bm), :], ssem.at[p]).wait()

    def block(blk, p):
        @pl.when(blk < n_act)
        def _():
            @pl.when(blk == 0)
            def _():
                for cp in fetch(be_ref[0], 0):
                    cp.start()
                gather(0, 0)
                y1[...] = jnp.zeros_like(y1)

            @pl.when(first_ref[blk] == 1)
            def _():
                s = slot_ref[blk]
                for cp in fetch(be_ref[blk], s):
                    cp.wait()

                @pl.when(nxt_ref[blk] >= 0)
                def _():
                    for cp in fetch(nxt_ref[blk], 1 - s):
                        cp.start()

                wg_bf[...] = wg_f[s].astype(_BF16)
                wu_bf[...] = wu_f[s].astype(_BF16)
                wd_bf[...] = wd_f[s].astype(_BF16)

            gather_wait(p)

            @pl.when(blk >= 1)
            def _():
                scatter_wait(p)

            gather(jnp.minimum(blk + 1, n_act - 1), 1 - p)
            scatter(jnp.where(blk == 0, N_BLOCKS, blk - 1), 1 - p)
            x = xbuf[p][...].astype(_BF16)
            a = jnp.dot(x, wg_bf[...], preferred_element_type=_F32)
            b = jnp.dot(x, wu_bf[...], preferred_element_type=_F32)
            ybuf[p][...] = jnp.dot((_silu(a) * b).astype(_BF16), wd_bf[...], preferred_element_type=_F32)

            @pl.when(blk == n_act - 1)
            def _():
                scatter(blk, p)
                gather_wait(1 - p)
                scatter_wait(1 - p)
                scatter_wait(p)

    i = pl.program_id(0)
    block(2 * i, 0)
    block(2 * i + 1, 1)


def _experts(block_e, n_active, first, slot, nxt, codes, h2, w_gate, w_up, w_down):
    bm = EXPERT_BM
    assert N_BLOCKS % 2 == 0
    hbm = pl.BlockSpec(memory_space=pl.ANY)
    grid_spec = pltpu.PrefetchScalarGridSpec(
        num_scalar_prefetch=6,
        grid=(N_BLOCKS // 2,),
        in_specs=[hbm, hbm, hbm, hbm],
        out_specs=hbm,
        scratch_shapes=[pltpu.VMEM((bm, D_MODEL), _F32), pltpu.VMEM((bm, D_MODEL), _F32),
                        pltpu.VMEM((bm, D_MODEL), _F32), pltpu.VMEM((bm, D_MODEL), _F32),
                        pltpu.VMEM((2, D_MODEL, D_EXPERT), _F32), pltpu.VMEM((2, D_MODEL, D_EXPERT), _F32),
                        pltpu.VMEM((2, D_EXPERT, D_MODEL), _F32),
                        pltpu.VMEM((D_MODEL, D_EXPERT), _BF16), pltpu.VMEM((D_MODEL, D_EXPERT), _BF16),
                        pltpu.VMEM((D_EXPERT, D_MODEL), _BF16),
                        pltpu.SemaphoreType.DMA((2, 3)), pltpu.SemaphoreType.DMA((2,)),
                        pltpu.SemaphoreType.DMA((2,))],
    )
    return pl.pallas_call(
        _experts_kernel,
        grid_spec=grid_spec,
        out_shape=jax.ShapeDtypeStruct((N_ROWS + bm, D_MODEL), _F32),
        compiler_params=_cparams("arbitrary"),
        name="experts",
    )(block_e, n_active, first, slot, nxt, codes, h2, w_gate, w_up, w_down)


def _final_kernel(xmid_ref, wt_ref, mod_ref, fg_ref, *refs):
    picks, o_ref = refs[:TOP_K], refs[TOP_K]
    wt = wt_ref[...]
    routed = wt[:, 0:1] * picks[0][...]
    for k in range(1, TOP_K):
        routed = routed + wt[:, k:k + 1] * picks[k][...]
    y = xmid_ref[...] + mod_ref[0][5:6] * routed
    o_ref[...] = y * _rms_scale(y) * fg_ref[...]


def _final(xmid, w_tok, mod3, final_g, ys):
    tm = TM_COMBINE
    tiles = N_TOK // tm
    tiles_per_batch = SEQ // tm
    tok = lambda w: pl.BlockSpec((tm, w), lambda i: (i, 0))
    pick = lambda k: pl.BlockSpec((tm, D_MODEL), lambda i: (k * tiles + i, 0))
    return pl.pallas_call(
        _final_kernel,
        grid=(tiles,),
        in_specs=[tok(D_MODEL), tok(TOP_K),
                  pl.BlockSpec((1, N_MOD, D_MODEL), lambda i: (i // tiles_per_batch, 0, 0)),
                  _const_spec((1, D_MODEL))] + [pick(k) for k in range(TOP_K)],
        out_specs=tok(D_MODEL),
        out_shape=jax.ShapeDtypeStruct((N_TOK, D_MODEL), _F32),
        compiler_params=_cparams("arbitrary"),
        name="final",
    )(xmid, w_tok, mod3, final_g, *([ys] * TOP_K))


def _s5_param_layouts(lam_re, lam_im, b_re, b_im, c_re, c_im, log_dt):
    two = lambda a: jnp.concatenate([a, a], axis=-1)
    lr, li = two(lam_re), two(lam_im)
    dtb = jnp.broadcast_to(log_dt.astype(_F32)[:, :, None], lr.shape)
    zeros = jnp.zeros_like(lr)
    prow = jnp.stack([lr, li, dtb] + [zeros] * 5, axis=2)
    ct2 = jnp.concatenate([jnp.swapaxes(c_re, -1, -2), jnp.swapaxes(c_im, -1, -2)], axis=2)
    bt1 = jnp.concatenate([jnp.swapaxes(b_re, -1, -2), jnp.swapaxes(b_im, -1, -2)], axis=3)
    return prow, ct2, bt1


def kernel(x, c, ctx, c_ctx, norm1_g, norm2_g, w_ada, b_ada, w_in, ssm_lam_re, ssm_lam_im, ssm_b_re, ssm_b_im, ssm_c_re, ssm_c_im, ssm_log_dt, ssm_d, ssm_w_glu, conv_w, conv_b, mix_norm_g, w_out, router_w, router_bias, exp_w_gate, exp_w_up, exp_w_down, shared_w_gate, shared_w_up, shared_w_down, final_g):
    layer = 0
    x2 = x.reshape(N_TOK, D_MODEL)
    ctx2 = ctx.reshape(N_CTX_TOK, D_MODEL)

    c8 = jnp.concatenate([c, c_ctx[None, :], jnp.zeros((8 - BATCH - 1, D_MODEL), _F32)], axis=0)
    mod = _ada_mod(c8, w_ada[layer], b_ada[layer][None, :])
    mod3 = mod.reshape(8, N_MOD, D_MODEL)

    w_in_bf = w_in[layer].astype(_BF16)
    conv_w8 = jnp.concatenate([conv_w[layer], jnp.zeros((8 - conv_w.shape[1], D_CONV), _F32)], axis=0)
    u_x, conv_x = _inproj(x2, mod3, norm1_g[layer][None, :], w_in_bf, conv_w8, conv_b[layer][None, :])
    u_c = _ctxproj(ctx2, mod3, norm1_g[layer][None, :], w_in_bf[:, :D_SSM])

    prow, ct2, bt1 = _s5_param_layouts(ssm_lam_re[layer], ssm_lam_im[layer], ssm_b_re[layer],
                                       ssm_b_im[layer], ssm_c_re[layer], ssm_c_im[layer], ssm_log_dt[layer])
    toep, bpow, cpow, dec = _s5_params(prow, ct2, bt1)
    ys = _s5_scan(u_x, u_c, toep, bpow, cpow, dec)

    rw = router_w[layer]
    rw_hi = rw.astype(_BF16)
    rw_lo = (rw - rw_hi.astype(_F32)).astype(_BF16)
    rwt = jnp.concatenate([rw_hi.T, rw_lo.T], axis=0)
    x1, h2, logits_t = _merge(ys, u_x, conv_x, x2, mod3, ssm_d[layer][None, :],
                              mix_norm_g[layer][None, :], norm2_g[layer][None, :],
                              ssm_w_glu[layer].astype(_BF16), w_out[layer].astype(_BF16), rwt)

    bias_col = jnp.broadcast_to(router_bias[layer][:, None], (N_EXPERTS, 128))
    eid, w_k, rank, cnt = _route(logits_t, bias_col)

    counts = cnt[:, 0].astype(jnp.int32)
    padded = (counts + EXPERT_BM - 1) // EXPERT_BM * EXPERT_BM
    pend = jnp.cumsum(padded)
    pstart = pend - padded
    is_e = eid[:, :, None] == jnp.arange(N_EXPERTS, dtype=jnp.int32)
    dest = jnp.sum(jnp.where(is_e, pstart, 0), axis=-1) + rank
    n_active = (pend[-1] // EXPERT_BM).astype(jnp.int32)
    blk = jnp.minimum(jnp.arange(N_BLOCKS, dtype=jnp.int32), n_active - 1)
    ends_before = (pend[None, :] <= (blk * EXPERT_BM)[:, None]).astype(jnp.int32)
    block_e = jnp.minimum(jnp.sum(ends_before, axis=1), N_EXPERTS - 1)

    xmid, codes = _rowmap(dest, pstart + counts, padded - counts, h2, x1, mod3,
                          shared_w_gate[layer].astype(_BF16), shared_w_up[layer].astype(_BF16),
                          shared_w_down[layer].astype(_BF16))
    first = jnp.concatenate([jnp.ones((1,), jnp.int32), (block_e[1:] != block_e[:-1]).astype(jnp.int32)])
    slot = (jnp.cumsum(first) - 1) % 2
    e_ids = jnp.arange(N_EXPERTS, dtype=jnp.int32)
    owner = jnp.where(padded > 0, e_ids, N_EXPERTS)
    later = jnp.min(jnp.where(e_ids[None, :] > e_ids[:, None], owner[None, :], N_EXPERTS), axis=1)
    nxt_e = jnp.where(later == N_EXPERTS, -1, later)
    nxt = jnp.sum(jnp.where(block_e[:, None] == e_ids[None, :], nxt_e[None, :], 0), axis=1)
    ys_rows = _experts(block_e, n_active[None], first, slot, nxt, codes, h2,
                       exp_w_gate[layer], exp_w_up[layer], exp_w_down[layer])
    out = _final(xmid, w_k.T, mod3, final_g[None, :], ys_rows)
    return out.reshape(BATCH, SEQ, D_MODEL)
```

```python
import functools
import math

import jax
import jax.numpy as jnp
from jax import lax
from jax.experimental import pallas as pl
from jax.experimental.pallas import tpu as pltpu

D_MODEL = 2048
BATCH = 4
SEQ = 2048
CTX_LEN = 256
GRID_W = 64
D_SSM = 1024
D_CONV = 1024
SSM_GROUP = 16
SSM_GROUPS = 64
SSM_STATE = 64
N_EXPERTS = 64
N_EXPERT_GROUPS = 8
GROUP_SIZE = N_EXPERTS // N_EXPERT_GROUPS
TOPK_GROUPS = 4
TOP_K = 8
D_EXPERT = 512
D_SHARED = 512
ROUTED_SCALE = 2.5
N_MOD = 6
EPS = 1e-6

N_TOK = BATCH * SEQ
N_CTX_TOK = BATCH * CTX_LEN

CHUNK = 16
CHUNK_W = CHUNK * SSM_GROUP
STATE_W = 2 * SSM_STATE
N_CTX_CHUNKS = CTX_LEN // CHUNK
N_X_CHUNKS = SEQ // CHUNK
N_SEQ_CHUNKS = N_CTX_CHUNKS + N_X_CHUNKS
SCAN_ROWS = N_SEQ_CHUNKS * BATCH
S5_GROUPS_PER_STEP = 8

TM = 256
EXPERT_BM = 256
N_ASSIGN = N_TOK * TOP_K
EXPERT_DEPTH = 3
N_BLOCKS = -(-(N_ASSIGN // EXPERT_BM + N_EXPERTS) // EXPERT_DEPTH) * EXPERT_DEPTH
N_ROWS = N_BLOCKS * EXPERT_BM
TM_COMBINE = 128
VMEM_LIMIT = 56 * 1024 * 1024

_F32 = jnp.float32
_BF16 = jnp.bfloat16


def _cparams(*sem):
    return pltpu.CompilerParams(dimension_semantics=sem, vmem_limit_bytes=VMEM_LIMIT)


def _const_spec(shape):
    nd = len(shape)
    return pl.BlockSpec(shape, lambda *_: (0,) * nd, pipeline_mode=pl.Buffered(1))


def _rms_scale(xf):
    return lax.rsqrt(jnp.mean(xf * xf, axis=-1, keepdims=True) + EPS)


def _silu(x):
    return x * jax.nn.sigmoid(x)


def _ada_kernel(c_ref, w_ref, b_ref, o_ref):
    s = _silu(c_ref[...])
    o_ref[...] = jnp.dot(s, w_ref[...], preferred_element_type=_F32) + b_ref[...]


def _ada_mod(c8, w_ada, b_ada):
    n = w_ada.shape[1]
    tn = 1024
    return pl.pallas_call(
        _ada_kernel,
        grid=(n // tn,),
        in_specs=[pl.BlockSpec((8, D_MODEL), lambda j: (0, 0)),
                  pl.BlockSpec((D_MODEL, tn), lambda j: (0, j)),
                  pl.BlockSpec((1, tn), lambda j: (0, j))],
        out_specs=pl.BlockSpec((8, tn), lambda j: (0, j)),
        out_shape=jax.ShapeDtypeStruct((8, n), _F32),
        compiler_params=_cparams("arbitrary"),
        name="ada_mod",
    )(c8, w_ada, b_ada)


def _modulated_norm(x_ref, mod_ref, g_ref, shift_row, scale_row):
    xf = x_ref[...]
    m = mod_ref[0]
    h = xf * _rms_scale(xf) * g_ref[...]
    return h * (1.0 + m[scale_row:scale_row + 1]) + m[shift_row:shift_row + 1]


def _inproj_kernel(x_ref, mod_ref, g_ref, w_ref, cw_ref, cb_ref, u_ref, conv_ref):
    h = _modulated_norm(x_ref, mod_ref, g_ref, 0, 1).astype(_BF16)
    u_ref[...] = jnp.dot(h, w_ref[:, 0:D_SSM], preferred_element_type=_F32)
    tm = x_ref.shape[0]
    pos = lax.broadcasted_iota(jnp.int32, (tm, 1), 0) % GRID_W
    not_first = (pos != 0).astype(_F32)
    not_last = (pos != GRID_W - 1).astype(_F32)
    cw = cw_ref[...]
    nc = 256
    for j in range(D_CONV // nc):
        lo = j * nc
        bg = jnp.dot(h, w_ref[:, D_SSM + lo:D_SSM + lo + nc], preferred_element_type=_F32)
        cg = jnp.dot(h, w_ref[:, D_SSM + D_CONV + lo:D_SSM + D_CONV + lo + nc], preferred_element_type=_F32)
        v = jnp.dot(h, w_ref[:, D_SSM + 2 * D_CONV + lo:D_SSM + 2 * D_CONV + lo + nc],
                    preferred_element_type=_F32)
        z = cg * v
        z_prev = pltpu.roll(z, 1, axis=0) * not_first
        z_next = pltpu.roll(z, tm - 1, axis=0) * not_last
        y = (cb_ref[:, lo:lo + nc] + z_prev * cw[0:1, lo:lo + nc] + z * cw[1:2, lo:lo + nc]
             + z_next * cw[2:3, lo:lo + nc])
        conv_ref[:, lo:lo + nc] = bg * y


def _inproj(x2, mod3, norm_g, w_in_bf, conv_w, conv_b):
    d_in = w_in_bf.shape[1]
    tiles_per_batch = SEQ // TM
    return pl.pallas_call(
        _inproj_kernel,
        grid=(N_TOK // TM,),
        in_specs=[pl.BlockSpec((TM, D_MODEL), lambda i: (i, 0)),
                  pl.BlockSpec((1, N_MOD, D_MODEL), lambda i: (i // tiles_per_batch, 0, 0)),
                  _const_spec((1, D_MODEL)),
                  _const_spec((D_MODEL, d_in)),
                  _const_spec((8, D_CONV)),
                  _const_spec((1, D_CONV))],
        out_specs=[pl.BlockSpec((TM, D_SSM), lambda i: (i, 0)),
                   pl.BlockSpec((TM, D_CONV), lambda i: (i, 0))],
        out_shape=[jax.ShapeDtypeStruct((N_TOK, D_SSM), _F32),
                   jax.ShapeDtypeStruct((N_TOK, D_CONV), _F32)],
        compiler_params=_cparams("arbitrary"),
        name="in_proj",
    )(x2, mod3, norm_g, w_in_bf, conv_w, conv_b)


def _ctxproj_kernel(x_ref, mod_ref, g_ref, w_ref, u_ref):
    h = _modulated_norm(x_ref, mod_ref, g_ref, 0, 1).astype(_BF16)
    u_ref[...] = jnp.dot(h, w_ref[...], preferred_element_type=_F32)


def _ctxproj(ctx2, mod3, norm_g, w_u_bf):
    return pl.pallas_call(
        _ctxproj_kernel,
        grid=(N_CTX_TOK // TM,),
        in_specs=[pl.BlockSpec((TM, D_MODEL), lambda i: (i, 0)),
                  pl.BlockSpec((1, N_MOD, D_MODEL), lambda i: (BATCH, 0, 0)),
                  _const_spec((1, D_MODEL)),
                  _const_spec((D_MODEL, D_SSM))],
        out_specs=pl.BlockSpec((TM, D_SSM), lambda i: (i, 0)),
        out_shape=jax.ShapeDtypeStruct((N_CTX_TOK, D_SSM), _F32),
        compiler_params=_cparams("arbitrary"),
        name="ctx_proj",
    )(ctx2, mod3, norm_g, w_u_bf)


N_POW = 32


def _s5_param_kernel(prow_ref, ct_ref, bt_ref, toep_ref, bpow_ref, cpow_ref, dec_ref):
    lane = lambda shape: lax.broadcasted_iota(jnp.int32, shape, 1)
    sub = lambda shape: lax.broadcasted_iota(jnp.int32, shape, 0)

    def split2(a):
        a_hi = a.astype(_BF16)
        return a_hi, (a - a_hi.astype(_F32)).astype(_BF16)

    rep = (lane((SSM_GROUP, CHUNK_W)) % SSM_GROUP == sub((SSM_GROUP, CHUNK_W))).astype(_BF16)
    sgn_col = jnp.where(sub((STATE_W, 1)) < SSM_STATE, 1.0, -1.0).astype(_F32)
    sgn_row = jnp.where(lane((1, STATE_W)) < SSM_STATE, -1.0, 1.0).astype(_F32)
    tau_col = sub((N_POW, 1)).astype(_F32)
    blk_l = lane((N_POW, CHUNK_W)) // SSM_GROUP
    tau_s = sub((N_POW, CHUNK_W))
    blk_r = sub((CHUNK_W, N_POW)) // SSM_GROUP
    tau_l = lane((CHUNK_W, N_POW))
    pick_l = lambda e: (tau_s == e).astype(_BF16)
    pick_r = lambda e: (tau_l == e).astype(_BF16)
    tn = (((0,), (0,)), ((), ()))

    strips = []
    for d in range(2):
        pr = prow_ref[d, 0]
        lam_re, lam_im, dt = pr[0:1], pr[1:2], jnp.exp(pr[2:3])
        mag = jnp.exp(tau_col * (lam_re * dt))
        ang = tau_col * (lam_im * dt)
        pw_re = mag * jnp.cos(ang)
        pw_im = mag * jnp.sin(ang)
        pw4 = jnp.concatenate(split2(pw_re) + split2(pw_im), axis=1)

        c_hi, c_lo = split2(ct_ref[d, 0])
        ct = (jnp.dot(c_hi, rep, preferred_element_type=_F32)
              + jnp.dot(c_lo, rep, preferred_element_type=_F32))
        ca = ct * sgn_col
        cb = -pltpu.roll(ct, SSM_STATE, axis=0)

        def cpow(sel):
            o = lax.dot_general(pw4, sel, tn, preferred_element_type=_F32)
            x = o[0:STATE_W] + o[STATE_W:2 * STATE_W]
            y = o[2 * STATE_W:3 * STATE_W] + o[3 * STATE_W:4 * STATE_W]
            return x * ca + y * cb

        nr = pw_re[1:2] - 1.0
        ni = pw_im[1:2]
        den = lam_re * lam_re + lam_im * lam_im
        kr = (nr * lam_re + ni * lam_im) / den
        ki = (ni * lam_re - nr * lam_im) / den
        b1 = bt_ref[d, 0]
        b2 = pltpu.roll(b1, SSM_STATE, axis=1) * sgn_row
        u1 = kr * b1 + ki * b2
        u2 = kr * b2 - ki * b1
        u1t = jnp.concatenate([u1] * CHUNK, axis=0)
        u2t = jnp.concatenate([u2] * CHUNK, axis=0)

        def bpow(sel):
            o = jnp.dot(sel, pw4, preferred_element_type=_F32)
            x = o[:, 0:STATE_W] + o[:, STATE_W:2 * STATE_W]
            y = o[:, 2 * STATE_W:3 * STATE_W] + o[:, 3 * STATE_W:4 * STATE_W]
            return x * u1t + y * u2t

        if d == 0:
            e_strip, e_b, e_c = blk_l, (CHUNK - 1) - blk_r, blk_l + 1
        else:
            e_strip, e_b, e_c = (CHUNK - 1) - blk_l, blk_r, CHUNK - blk_l
        u_hi, u_lo = split2(u1)
        k_hi, k_lo = split2(cpow(pick_l(e_strip)))
        strips.append(jnp.dot(u_hi, k_hi, preferred_element_type=_F32)
                      + jnp.dot(u_hi, k_lo, preferred_element_type=_F32)
                      + jnp.dot(u_lo, k_hi, preferred_element_type=_F32))
        bp = bpow(pick_r(e_b))
        bpow_ref[0, :, d * STATE_W:(d + 1) * STATE_W] = bp.astype(_BF16)
        bpow_ref[0, :, (2 + d) * STATE_W:(3 + d) * STATE_W] = pltpu.roll(bp, SSM_STATE, axis=1).astype(_BF16)
        cpow_ref[0, d * STATE_W:(d + 1) * STATE_W, :] = cpow(pick_l(e_c)).astype(_BF16)

        dec_ref[0, 2 * d:2 * d + 1, :] = pw_re[CHUNK:CHUNK + 1]
        dec_ref[0, 2 * d + 1:2 * d + 2, :] = pw_im[CHUNK:CHUNK + 1] * sgn_row

    zeros = jnp.zeros((SSM_GROUP, CHUNK_W), _F32)
    strip = (jnp.concatenate([strips[1], zeros], axis=1)
             + pltpu.roll(jnp.concatenate([strips[0], zeros], axis=1), CHUNK_W - SSM_GROUP, axis=1))
    for i in range(CHUNK):
        off = (CHUNK - 1 - i) * SSM_GROUP
        win = strip if off == 0 else pltpu.roll(strip, 2 * CHUNK_W - off, axis=1)
        toep_ref[0, i * SSM_GROUP:(i + 1) * SSM_GROUP, :] = win[:, 0:CHUNK_W].astype(_BF16)


def _s5_params(prow, ct2, bt1):
    g_spec = lambda shape: pl.BlockSpec(shape, lambda g: (0, g, 0, 0))
    o_spec = lambda shape: pl.BlockSpec(shape, lambda g: (g, 0, 0))
    return pl.pallas_call(
        _s5_param_kernel,
        grid=(SSM_GROUPS,),
        in_specs=[g_spec((2, 1, 8, STATE_W)),
                  g_spec((2, 1, STATE_W, SSM_GROUP)), g_spec((2, 1, SSM_GROUP, STATE_W))],
        out_specs=[o_spec((1, CHUNK_W, CHUNK_W)), o_spec((1, CHUNK_W, 4 * STATE_W)),
                   o_spec((1, 2 * STATE_W, CHUNK_W)), o_spec((1, 4, STATE_W))],
        out_shape=[jax.ShapeDtypeStruct((SSM_GROUPS, CHUNK_W, CHUNK_W), _BF16),
                   jax.ShapeDtypeStruct((SSM_GROUPS, CHUNK_W, 4 * STATE_W), _BF16),
                   jax.ShapeDtypeStruct((SSM_GROUPS, 2 * STATE_W, CHUNK_W), _BF16),
                   jax.ShapeDtypeStruct((SSM_GROUPS, 4, STATE_W), _F32)],
        compiler_params=_cparams("arbitrary"),
        name="s5_params",
    )(prow, ct2, bt1)


def _s5_kernel(ux_ref, uc_ref, toep_ref, bpow_ref, cpow_ref, dec_ref, y_ref, u_ref, s_ref, h_ref, y2_ref):
    ng = S5_GROUPS_PER_STEP
    gw = SSM_GROUP
    x0 = N_CTX_CHUNKS * BATCH

    half = CHUNK // 2
    assert half == ng
    lanes = ng * 128

    r = lax.broadcasted_iota(jnp.int32, (lanes, 1), 0)
    swapped = ((r // gw) % ng) * 128 + (r // 128) * gw + r % gw
    perm = (lax.broadcasted_iota(jnp.int32, (lanes, lanes), 1) == swapped).astype(_BF16)

    def to_chunk_layout(src_ref, seq_len, n_chunks, row0):
        for hf in range(2):
            p = jnp.concatenate(
                [jnp.concatenate([src_ref[pl.ds(b * seq_len + hf * half + il, n_chunks, stride=CHUNK), :]
                                  for il in range(half)], axis=1) for b in range(BATCH)], axis=0).astype(_BF16)
            q = jnp.dot(p, perm, preferred_element_type=_F32)
            for b in range(BATCH):
                for g in range(ng):
                    u_ref[g, hf, pl.ds(row0 + b, n_chunks, stride=BATCH), :] = (
                        q[b * n_chunks:(b + 1) * n_chunks, g * 128:(g + 1) * 128])

    to_chunk_layout(uc_ref, CTX_LEN, N_CTX_CHUNKS, 0)
    to_chunk_layout(ux_ref, SEQ, N_X_CHUNKS, x0)

    chunk_rows = lambda ref, g, lo: jnp.concatenate([ref[g, 0, lo:, :], ref[g, 1, lo:, :]], axis=1)
    for g in range(ng):
        s_ref[g] = jnp.dot(chunk_rows(u_ref, g, 0).astype(_BF16), bpow_ref[g], preferred_element_type=_F32)

    pair = 2 * BATCH
    n_ctx_pairs = N_CTX_CHUNKS // 2
    n_pairs = N_SEQ_CHUNKS // 2
    lower = lax.broadcasted_iota(jnp.int32, (pair, STATE_W), 0) < BATCH
    zero = jnp.zeros((pair, STATE_W), _F32)

    def step(t, carry):
        tb = jnp.where(t < n_ctx_pairs, n_ctx_pairs - 1 - t, n_pairs + n_ctx_pairs - 1 - t)
        rf = pl.multiple_of(t * pair, pair)
        rb = pl.multiple_of(tb * pair, pair)
        out = []
        swap = lambda v: pltpu.roll(v, BATCH, axis=0)
        for g in range(ng):
            hf, hfs, hb, hbs = carry[4 * g:4 * g + 4]
            dg = dec_ref[g]

            def advance(h, hs, a_re, a_im, s, ss):
                return a_re * h + a_im * hs + s, a_re * hs - a_im * h + ss

            col = lambda r, k: s_ref[g, pl.ds(r, pair), k * STATE_W:(k + 1) * STATE_W]
            sf, sfs = col(rf, 0), col(rf, 2)
            mid, mids = advance(hf, hfs, dg[0:1], dg[1:2], jnp.where(lower, sf, swap(sf)),
                                jnp.where(lower, sfs, swap(sfs)))
            h_ref[g, pl.ds(rf, pair), 0:STATE_W] = jnp.where(lower, hf, mid)
            end, ends = advance(mid, mids, dg[0:1], dg[1:2], sf, sfs)
            hf, hfs = jnp.where(lower, swap(end), end), jnp.where(lower, swap(ends), ends)

            sb, sbs = col(rb, 1), col(rb, 3)
            mid, mids = advance(hb, hbs, dg[2:3], dg[3:4], jnp.where(lower, swap(sb), sb),
                                jnp.where(lower, swap(sbs), sbs))
            h_ref[g, pl.ds(rb, pair), STATE_W:2 * STATE_W] = jnp.where(lower, mid, hb)
            end, ends = advance(mid, mids, dg[2:3], dg[3:4], sb, sbs)
            hb, hbs = jnp.where(lower, end, swap(end)), jnp.where(lower, ends, swap(ends))
            out += [hf, hfs, hb, hbs]
        return tuple(out)

    lax.fori_loop(0, n_pairs, step, (zero,) * (4 * ng))

    for g in range(ng):
        ux = chunk_rows(u_ref, g, x0).astype(_BF16)
        hx = h_ref[g, x0:SCAN_ROWS, :].astype(_BF16)
        y2 = (jnp.dot(ux, toep_ref[g], preferred_element_type=_F32)
              + jnp.dot(hx, cpow_ref[g], preferred_element_type=_F32))
        y2_ref[g, 0] = y2[:, 0:128]
        y2_ref[g, 1] = y2[:, 128:256]

    for hf in range(2):
        yin = jnp.concatenate(
            [jnp.concatenate([y2_ref[g, hf, pl.ds(b, N_X_CHUNKS, stride=BATCH), :] for g in range(ng)], axis=1)
             for b in range(BATCH)], axis=0)
        y_hi = yin.astype(_BF16)
        y_lo = (yin - y_hi.astype(_F32)).astype(_BF16)
        q = (jnp.dot(y_hi, perm, preferred_element_type=_F32)
             + jnp.dot(y_lo, perm, preferred_element_type=_F32))
        for b in range(BATCH):
            for jl in range(half):
                y_ref[pl.ds(b * SEQ + hf * half + jl, N_X_CHUNKS, stride=CHUNK), :] = (
                    q[b * N_X_CHUNKS:(b + 1) * N_X_CHUNKS, jl * 128:(jl + 1) * 128])


def _s5_scan(u_x, u_c, toep, bpow, cpow, dec):
    ng = S5_GROUPS_PER_STEP
    n_x_rows = N_X_CHUNKS * BATCH
    lanes = ng * SSM_GROUP
    spec = lambda r, c: pl.BlockSpec((ng, r, c), lambda i: (i, 0, 0))
    col = lambda rows: pl.BlockSpec((rows, lanes), lambda i: (0, i))
    return pl.pallas_call(
        _s5_kernel,
        grid=(SSM_GROUPS // ng,),
        in_specs=[col(N_TOK), col(N_CTX_TOK), spec(CHUNK_W, CHUNK_W), spec(CHUNK_W, 4 * STATE_W),
                  spec(2 * STATE_W, CHUNK_W), spec(4, STATE_W)],
        out_specs=col(N_TOK),
        out_shape=jax.ShapeDtypeStruct((N_TOK, D_SSM), _F32),
        scratch_shapes=[pltpu.VMEM((ng, 2, SCAN_ROWS, 128), _F32),
                        pltpu.VMEM((ng, SCAN_ROWS, 4 * STATE_W), _F32),
                        pltpu.VMEM((ng, SCAN_ROWS, 2 * STATE_W), _F32),
                        pltpu.VMEM((ng, 2, n_x_rows, 128), _F32)],
        compiler_params=_cparams("arbitrary"),
        name="s5_scan",
    )(u_x, u_c, toep, bpow, cpow, dec)


def _merge_kernel(ys_ref, u_ref, conv_ref, x_ref, mod_ref, dskip_ref, mixg_ref, n2g_ref,
                  wglu_ref, wout_ref, rwt_ref, x1_ref, h2_ref, logit_ref):
    m = mod_ref[0]
    yx = dskip_ref[...] * u_ref[...] + ys_ref[...]
    c0 = math.sqrt(2.0 / math.pi)
    ge = 0.5 * yx * (1.0 + jnp.tanh(c0 * (yx + 0.044715 * (yx * yx * yx))))
    z = jnp.dot(ge.astype(_BF16), wglu_ref[...], preferred_element_type=_F32)
    ssm_y = z[:, 0:D_SSM] * jax.nn.sigmoid(z[:, D_SSM:2 * D_SSM])
    conv_y = conv_ref[...]
    mixg = mixg_ref[...]
    heads_a = (ssm_y * _rms_scale(ssm_y) * mixg[:, 0:D_SSM]).astype(_BF16)
    heads_b = (conv_y * _rms_scale(conv_y) * mixg[:, D_SSM:]).astype(_BF16)
    mix = (jnp.dot(heads_a, wout_ref[0:D_SSM, :], preferred_element_type=_F32)
           + jnp.dot(heads_b, wout_ref[D_SSM:, :], preferred_element_type=_F32))
    x1 = x_ref[...] + m[2:3] * mix
    x1_ref[...] = x1
    h2 = x1 * _rms_scale(x1) * n2g_ref[...] * (1.0 + m[4:5]) + m[3:4]
    h2_ref[...] = h2
    h_hi = h2.astype(_BF16)
    h_lo = (h2 - h_hi.astype(_F32)).astype(_BF16)
    nt = (((1,), (1,)), ((), ()))
    p = lax.dot_general(rwt_ref[...], h_hi, nt, preferred_element_type=_F32)
    q = lax.dot_general(rwt_ref[0:N_EXPERTS, :], h_lo, nt, preferred_element_type=_F32)
    logit_ref[...] = p[0:N_EXPERTS] + p[N_EXPERTS:2 * N_EXPERTS] + q


def _merge(ys, u, conv, x2, mod3, dskip, mixg, n2g, wglu_bf, wout_bf, rwt):
    tiles_per_batch = SEQ // TM
    tok = lambda w: pl.BlockSpec((TM, w), lambda i: (i, 0))
    return pl.pallas_call(
        _merge_kernel,
        grid=(N_TOK // TM,),
        in_specs=[tok(D_SSM), tok(D_SSM), tok(D_CONV), tok(D_MODEL),
                  pl.BlockSpec((1, N_MOD, D_MODEL), lambda i: (i // tiles_per_batch, 0, 0)),
                  _const_spec((1, D_SSM)), _const_spec((1, D_MODEL)), _const_spec((1, D_MODEL)),
                  _const_spec((D_SSM, 2 * D_SSM)), _const_spec((D_MODEL, D_MODEL)),
                  _const_spec((2 * N_EXPERTS, D_MODEL))],
        out_specs=[tok(D_MODEL), tok(D_MODEL),
                   pl.BlockSpec((N_EXPERTS, TM), lambda i: (0, i))],
        out_shape=[jax.ShapeDtypeStruct((N_TOK, D_MODEL), _F32),
                   jax.ShapeDtypeStruct((N_TOK, D_MODEL), _F32),
                   jax.ShapeDtypeStruct((N_EXPERTS, N_TOK), _F32)],
        compiler_params=_cparams("arbitrary"),
        name="merge_heads",
    )(ys, u, conv, x2, mod3, dskip, mixg, n2g, wglu_bf, wout_bf, rwt)


def _route_kernel(logit_ref, bias_ref, eid_ref, w_ref, rank_ref, cnt_ref, carry_ref):
    @pl.when(pl.program_id(0) == 0)
    def _():
        carry_ref[...] = jnp.zeros_like(carry_ref)

    tm = logit_ref.shape[1]
    neg = jnp.float32(-jnp.inf)
    scores = jax.nn.sigmoid(logit_ref[...])
    biased = scores + bias_ref[:, 0:1]
    sub = lax.broadcasted_iota(jnp.int32, (GROUP_SIZE, tm), 0)
    rows = lambda a, g: a[g * GROUP_SIZE:(g + 1) * GROUP_SIZE]
    ngrp = N_EXPERT_GROUPS

    gscore = []
    for g in range(ngrp):
        bg = rows(biased, g)
        m1 = jnp.max(bg, axis=0, keepdims=True)
        first = jnp.min(jnp.where(bg == m1, sub, GROUP_SIZE), axis=0, keepdims=True)
        m2 = jnp.max(jnp.where(sub == first, neg, bg), axis=0, keepdims=True)
        gscore.append(m1 + m2)
    v = []
    for g in range(ngrp):
        beaten = jnp.zeros((1, tm), jnp.int32)
        for o in range(ngrp):
            if o != g:
                beats = (gscore[o] >= gscore[g]) if o < g else (gscore[o] > gscore[g])
                beaten = beaten + beats.astype(jnp.int32)
        v.append(jnp.where(beaten < TOPK_GROUPS, rows(biased, g), neg))
    eids = [sub + g * GROUP_SIZE for g in range(ngrp)]

    pick_ids, pick_masks = [], []
    for _ in range(TOP_K):
        m = functools.reduce(jnp.maximum, v)
        m = jnp.max(m, axis=0, keepdims=True)
        cand = functools.reduce(jnp.minimum, [jnp.where(v[g] == m, eids[g], N_EXPERTS) for g in range(ngrp)])
        pick_id = jnp.min(cand, axis=0, keepdims=True)
        masks = [eids[g] == pick_id for g in range(ngrp)]
        v = [jnp.where(masks[g], neg, v[g]) for g in range(ngrp)]
        pick_ids.append(pick_id)
        pick_masks.append(masks)

    sel = jnp.concatenate(
        [functools.reduce(jnp.logical_or, [pick_masks[k][g] for k in range(TOP_K)]).astype(_F32)
         for g in range(ngrp)], axis=0)
    before = (lax.broadcasted_iota(jnp.int32, (tm, tm), 0)
              < lax.broadcasted_iota(jnp.int32, (tm, tm), 1)).astype(_BF16)
    base = jnp.dot(sel.astype(_BF16), before, preferred_element_type=_F32) + carry_ref[:, 0:1]

    def gather_pick(a, k):
        parts = [jnp.sum(jnp.where(pick_masks[k][g], rows(a, g), 0.0), axis=0, keepdims=True)
                 for g in range(ngrp)]
        return functools.reduce(jnp.add, parts)

    picked = [gather_pick(scores, k) for k in range(TOP_K)]
    denom = functools.reduce(jnp.add, picked)
    for k in range(TOP_K):
        eid_ref[k:k + 1, :] = pick_ids[k]
        w_ref[k:k + 1, :] = picked[k] / denom * ROUTED_SCALE
        rank_ref[k:k + 1, :] = gather_pick(base, k).astype(jnp.int32)
    carry_ref[...] = carry_ref[...] + jnp.sum(sel, axis=1, keepdims=True)
    cnt_ref[...] = carry_ref[...]


def _route(logits_t, bias_col):
    tok = pl.BlockSpec((TOP_K, TM), lambda i: (0, i))
    return pl.pallas_call(
        _route_kernel,
        grid=(N_TOK // TM,),
        in_specs=[pl.BlockSpec((N_EXPERTS, TM), lambda i: (0, i)), _const_spec((N_EXPERTS, 128))],
        out_specs=[tok, tok, tok, pl.BlockSpec((N_EXPERTS, 128), lambda i: (0, 0))],
        out_shape=[jax.ShapeDtypeStruct((TOP_K, N_TOK), jnp.int32),
                   jax.ShapeDtypeStruct((TOP_K, N_TOK), _F32),
                   jax.ShapeDtypeStruct((TOP_K, N_TOK), jnp.int32),
                   jax.ShapeDtypeStruct((N_EXPERTS, 128), _F32)],
        scratch_shapes=[pltpu.VMEM((N_EXPERTS, 128), _F32)],
        compiler_params=_cparams("arbitrary"),
        name="route",
    )(logits_t, bias_col)


def _rowmap_kernel(dest_ref, padstart_ref, padn_ref, h_ref, x1_ref, mod_ref, wsg_ref, wsu_ref, wsd_ref,
                   xmid_ref, codes_ref):
    i = pl.program_id(0)
    tm = h_ref.shape[0]
    base = i * tm

    @pl.when(i == 0)
    def _():
        def fill(e, done):
            def one(r, c):
                codes_ref[padstart_ref[e] + r] = N_ASSIGN + done + r
                return c
            lax.fori_loop(0, padn_ref[e], one, 0)
            return done + padn_ref[e]
        lax.fori_loop(0, N_EXPERTS, fill, 0)
        for r in range(EXPERT_BM):
            codes_ref[N_ROWS + r] = N_ROWS + r

    for t in range(tm):
        ds = [dest_ref[k * N_TOK + base + t] for k in range(TOP_K)]
        for k in range(TOP_K):
            codes_ref[ds[k]] = k * N_TOK + base + t

    h = h_ref[...].astype(_BF16)
    a = jnp.dot(h, wsg_ref[...], preferred_element_type=_F32)
    b = jnp.dot(h, wsu_ref[...], preferred_element_type=_F32)
    shared = jnp.dot((_silu(a) * b).astype(_BF16), wsd_ref[...], preferred_element_type=_F32)
    xmid_ref[...] = x1_ref[...] + mod_ref[0][5:6] * shared


def _rowmap(dest, pad_start, pad_n, h2, x1, mod3, wsg_bf, wsu_bf, wsd_bf):
    tm = TM
    tiles_per_batch = SEQ // tm
    tok = pl.BlockSpec((tm, D_MODEL), lambda i, *_: (i, 0))
    const = lambda shape: pl.BlockSpec(shape, lambda i, *_: (0,) * len(shape), pipeline_mode=pl.Buffered(1))
    grid_spec = pltpu.PrefetchScalarGridSpec(
        num_scalar_prefetch=3,
        grid=(N_TOK // tm,),
        in_specs=[tok, tok,
                  pl.BlockSpec((1, N_MOD, D_MODEL), lambda i, *_: (i // tiles_per_batch, 0, 0)),
                  const((D_MODEL, D_SHARED)), const((D_MODEL, D_SHARED)), const((D_SHARED, D_MODEL))],
        out_specs=[tok, pl.BlockSpec(memory_space=pltpu.SMEM)],
    )
    return pl.pallas_call(
        _rowmap_kernel,
        grid_spec=grid_spec,
        out_shape=[jax.ShapeDtypeStruct((N_TOK, D_MODEL), _F32),
                   jax.ShapeDtypeStruct((N_ROWS + EXPERT_BM,), jnp.int32)],
        compiler_params=_cparams("arbitrary"),
        name="rowmap",
    )(dest.reshape(N_ASSIGN), pad_start, pad_n, h2, x1, mod3, wsg_bf, wsu_bf, wsd_bf)


def _experts_kernel(be_ref, nact_ref, first_ref, slot_ref, nxt_ref, codes_ref, h_hbm, wg_hbm, wu_hbm, wd_hbm,
                    out_hbm, x0, x1, x2, y0, y1, y2, wg_f, wu_f, wd_f, wg_bf, wu_bf, wd_bf, sem, gsem, ssem):
    n_act = nact_ref[0]
    bm = EXPERT_BM
    depth = EXPERT_DEPTH
    xbuf, ybuf = (x0, x1, x2), (y0, y1, y2)

    def fetch(e, s):
        return (pltpu.make_async_copy(wg_hbm.at[e], wg_f.at[s], sem.at[s, 0]),
                pltpu.make_async_copy(wu_hbm.at[e], wu_f.at[s], sem.at[s, 1]),
                pltpu.make_async_copy(wd_hbm.at[e], wd_f.at[s], sem.at[s, 2]))

    def gather(blk, p):
        for r in range(bm):
            tok = codes_ref[blk * bm + r] & (N_TOK - 1)
            pltpu.make_async_copy(h_hbm.at[pl.ds(tok, 1), :], xbuf[p].at[pl.ds(r, 1), :],
                                  gsem.at[p]).start(priority=r % 2)

    def gather_wait(p):
        pltpu.make_async_copy(h_hbm.at[pl.ds(0, bm), :], xbuf[p], gsem.at[p]).wait()

    def scatter(blk, p):
        for r in range(bm):
            pltpu.make_async_copy(ybuf[p].at[pl.ds(r, 1), :], out_hbm.at[pl.ds(codes_ref[blk * bm + r], 1), :],
                                  ssem.at[p]).start(priority=r % 2)

    def scatter_wait(p):
        pltpu.make_async_copy(ybuf[p], out_hbm.at[pl.ds(0, bm), :], ssem.at[p]).wait()

    def block(blk, p):
        nxt1, nxt2 = (p + 1) % depth, (p + 2) % depth

        @pl.when(blk < n_act)
        def _():
            @pl.when(blk == 0)
            def _():
                for cp in fetch(be_ref[0], 0):
                    cp.start()
                gather(0, 0)
                gather(jnp.minimum(1, n_act - 1), 1)
                y2[...] = jnp.zeros_like(y2)

            @pl.when(first_ref[blk] == 1)
            def _():
                s = slot_ref[blk]
                for cp in fetch(be_ref[blk], s):
                    cp.wait()

                @pl.when(nxt_ref[blk] >= 0)
                def _():
                    for cp in fetch(nxt_ref[blk], 1 - s):
                        cp.start()

                wg_bf[...] = wg_f[s].astype(_BF16)
                wu_bf[...] = wu_f[s].astype(_BF16)
                wd_bf[...] = wd_f[s].astype(_BF16)

            gather_wait(p)

            @pl.when(blk >= 2)
            def _():
                scatter_wait(p)

            gather(jnp.minimum(blk + 2, n_act - 1), nxt2)
            scatter(jnp.where(blk == 0, N_BLOCKS, blk - 1), nxt2)
            x = xbuf[p][...].astype(_BF16)
            a = jnp.dot(x, wg_bf[...], preferred_element_type=_F32)
            b = jnp.dot(x, wu_bf[...], preferred_element_type=_F32)
            ybuf[p][...] = jnp.dot((_silu(a) * b).astype(_BF16), wd_bf[...], preferred_element_type=_F32)

            @pl.when(blk == n_act - 1)
            def _():
                scatter(blk, p)
                gather_wait(nxt1)
                gather_wait(nxt2)
                scatter_wait(nxt2)

                @pl.when(blk >= 1)
                def _():
                    scatter_wait(nxt1)
                scatter_wait(p)

    i = pl.program_id(0)
    for q in range(depth):
        block(depth * i + q, q)


def _experts(block_e, n_active, first, slot, nxt, codes, h2, w_gate, w_up, w_down):
    bm = EXPERT_BM
    depth = EXPERT_DEPTH
    hbm = pl.BlockSpec(memory_space=pl.ANY)
    grid_spec = pltpu.PrefetchScalarGridSpec(
        num_scalar_prefetch=6,
        grid=(N_BLOCKS // depth,),
        in_specs=[hbm, hbm, hbm, hbm],
        out_specs=hbm,
        scratch_shapes=[pltpu.VMEM((bm, D_MODEL), _F32)] * (2 * depth) + [
                        pltpu.VMEM((2, D_MODEL, D_EXPERT), _F32), pltpu.VMEM((2, D_MODEL, D_EXPERT), _F32),
                        pltpu.VMEM((2, D_EXPERT, D_MODEL), _F32),
                        pltpu.VMEM((D_MODEL, D_EXPERT), _BF16), pltpu.VMEM((D_MODEL, D_EXPERT), _BF16),
                        pltpu.VMEM((D_EXPERT, D_MODEL), _BF16),
                        pltpu.SemaphoreType.DMA((2, 3)), pltpu.SemaphoreType.DMA((depth,)),
                        pltpu.SemaphoreType.DMA((depth,))],
    )
    return pl.pallas_call(
        _experts_kernel,
        grid_spec=grid_spec,
        out_shape=jax.ShapeDtypeStruct((N_ROWS + bm, D_MODEL), _F32),
        compiler_params=_cparams("arbitrary"),
        name="experts",
    )(block_e, n_active, first, slot, nxt, codes, h2, w_gate, w_up, w_down)


def _final_kernel(xmid_ref, wt_ref, mod_ref, fg_ref, *refs):
    picks, o_ref = refs[:TOP_K], refs[TOP_K]
    wt = wt_ref[...]
    routed = wt[:, 0:1] * picks[0][...]
    for k in range(1, TOP_K):
        routed = routed + wt[:, k:k + 1] * picks[k][...]
    y = xmid_ref[...] + mod_ref[0][5:6] * routed
    o_ref[...] = y * _rms_scale(y) * fg_ref[...]


def _final(xmid, w_tok, mod3, final_g, ys):
    tm = TM_COMBINE
    tiles = N_TOK // tm
    tiles_per_batch = SEQ // tm
    tok = lambda w: pl.BlockSpec((tm, w), lambda i: (i, 0))
    pick = lambda k: pl.BlockSpec((tm, D_MODEL), lambda i: (k * tiles + i, 0))
    return pl.pallas_call(
        _final_kernel,
        grid=(tiles,),
        in_specs=[tok(D_MODEL), tok(TOP_K),
                  pl.BlockSpec((1, N_MOD, D_MODEL), lambda i: (i // tiles_per_batch, 0, 0)),
                  _const_spec((1, D_MODEL))] + [pick(k) for k in range(TOP_K)],
        out_specs=tok(D_MODEL),
        out_shape=jax.ShapeDtypeStruct((N_TOK, D_MODEL), _F32),
        compiler_params=_cparams("arbitrary"),
        name="final",
    )(xmid, w_tok, mod3, final_g, *([ys] * TOP_K))


def _s5_param_layouts(lam_re, lam_im, b_re, b_im, c_re, c_im, log_dt):
    two = lambda a: jnp.concatenate([a, a], axis=-1)
    lr, li = two(lam_re), two(lam_im)
    dtb = jnp.broadcast_to(log_dt.astype(_F32)[:, :, None], lr.shape)
    zeros = jnp.zeros_like(lr)
    prow = jnp.stack([lr, li, dtb] + [zeros] * 5, axis=2)
    ct2 = jnp.concatenate([jnp.swapaxes(c_re, -1, -2), jnp.swapaxes(c_im, -1, -2)], axis=2)
    bt1 = jnp.concatenate([jnp.swapaxes(b_re, -1, -2), jnp.swapaxes(b_im, -1, -2)], axis=3)
    return prow, ct2, bt1


def kernel(x, c, ctx, c_ctx, norm1_g, norm2_g, w_ada, b_ada, w_in, ssm_lam_re, ssm_lam_im, ssm_b_re, ssm_b_im, ssm_c_re, ssm_c_im, ssm_log_dt, ssm_d, ssm_w_glu, conv_w, conv_b, mix_norm_g, w_out, router_w, router_bias, exp_w_gate, exp_w_up, exp_w_down, shared_w_gate, shared_w_up, shared_w_down, final_g):
    layer = 0
    x2 = x.reshape(N_TOK, D_MODEL)
    ctx2 = ctx.reshape(N_CTX_TOK, D_MODEL)

    c8 = jnp.concatenate([c, c_ctx[None, :], jnp.zeros((8 - BATCH - 1, D_MODEL), _F32)], axis=0)
    mod = _ada_mod(c8, w_ada[layer], b_ada[layer][None, :])
    mod3 = mod.reshape(8, N_MOD, D_MODEL)

    w_in_bf = w_in[layer].astype(_BF16)
    conv_w8 = jnp.concatenate([conv_w[layer], jnp.zeros((8 - conv_w.shape[1], D_CONV), _F32)], axis=0)
    u_x, conv_x = _inproj(x2, mod3, norm1_g[layer][None, :], w_in_bf, conv_w8, conv_b[layer][None, :])
    u_c = _ctxproj(ctx2, mod3, norm1_g[layer][None, :], w_in_bf[:, :D_SSM])

    prow, ct2, bt1 = _s5_param_layouts(ssm_lam_re[layer], ssm_lam_im[layer], ssm_b_re[layer],
                                       ssm_b_im[layer], ssm_c_re[layer], ssm_c_im[layer], ssm_log_dt[layer])
    toep, bpow, cpow, dec = _s5_params(prow, ct2, bt1)
    ys = _s5_scan(u_x, u_c, toep, bpow, cpow, dec)

    rw = router_w[layer]
    rw_hi = rw.astype(_BF16)
    rw_lo = (rw - rw_hi.astype(_F32)).astype(_BF16)
    rwt = jnp.concatenate([rw_hi.T, rw_lo.T], axis=0)
    x1, h2, logits_t = _merge(ys, u_x, conv_x, x2, mod3, ssm_d[layer][None, :],
                              mix_norm_g[layer][None, :], norm2_g[layer][None, :],
                              ssm_w_glu[layer].astype(_BF16), w_out[layer].astype(_BF16), rwt)

    bias_col = jnp.broadcast_to(router_bias[layer][:, None], (N_EXPERTS, 128))
    eid, w_k, rank, cnt = _route(logits_t, bias_col)

    counts = cnt[:, 0].astype(jnp.int32)
    padded = (counts + EXPERT_BM - 1) // EXPERT_BM * EXPERT_BM
    pend = jnp.cumsum(padded)
    pstart = pend - padded
    is_e = eid[:, :, None] == jnp.arange(N_EXPERTS, dtype=jnp.int32)
    dest = jnp.sum(jnp.where(is_e, pstart, 0), axis=-1) + rank
    n_active = (pend[-1] // EXPERT_BM).astype(jnp.int32)
    blk = jnp.minimum(jnp.arange(N_BLOCKS, dtype=jnp.int32), n_active - 1)
    ends_before = (pend[None, :] <= (blk * EXPERT_BM)[:, None]).astype(jnp.int32)
    block_e = jnp.minimum(jnp.sum(ends_before, axis=1), N_EXPERTS - 1)

    xmid, codes = _rowmap(dest, pstart + counts, padded - counts, h2, x1, mod3,
                          shared_w_gate[layer].astype(_BF16), shared_w_up[layer].astype(_BF16),
                          shared_w_down[layer].astype(_BF16))
    first = jnp.concatenate([jnp.ones((1,), jnp.int32), (block_e[1:] != block_e[:-1]).astype(jnp.int32)])
    slot = (jnp.cumsum(first) - 1) % 2
    e_ids = jnp.arange(N_EXPERTS, dtype=jnp.int32)
    owner = jnp.where(padded > 0, e_ids, N_EXPERTS)
    later = jnp.min(jnp.where(e_ids[None, :] > e_ids[:, None], owner[None, :], N_EXPERTS), axis=1)
    nxt_e = jnp.where(later == N_EXPERTS, -1, later)
    nxt = jnp.sum(jnp.where(block_e[:, None] == e_ids[None, :], nxt_e[None, :], 0), axis=1)
    ys_rows = _experts(block_e, n_active[None], first, slot, nxt, codes, h2,
                       exp_w_gate[layer], exp_w_up[layer], exp_w_down[layer])
    out = _final(xmid, w_k.T, mod3, final_g[None, :], ys_rows)
    return out.reshape(BATCH, SEQ, D_MODEL)
```

```python
import functools
import math

import jax
import jax.numpy as jnp
from jax import lax
from jax.experimental import pallas as pl
from jax.experimental.pallas import tpu as pltpu

D_MODEL = 2048
BATCH = 4
SEQ = 2048
CTX_LEN = 256
GRID_W = 64
D_SSM = 1024
D_CONV = 1024
SSM_GROUP = 16
SSM_GROUPS = 64
SSM_STATE = 64
N_EXPERTS = 64
N_EXPERT_GROUPS = 8
GROUP_SIZE = N_EXPERTS // N_EXPERT_GROUPS
TOPK_GROUPS = 4
TOP_K = 8
D_EXPERT = 512
D_SHARED = 512
ROUTED_SCALE = 2.5
N_MOD = 6
EPS = 1e-6

N_TOK = BATCH * SEQ
N_CTX_TOK = BATCH * CTX_LEN

CHUNK = 16
CHUNK_W = CHUNK * SSM_GROUP
STATE_W = 2 * SSM_STATE
N_CTX_CHUNKS = CTX_LEN // CHUNK
N_X_CHUNKS = SEQ // CHUNK
N_SEQ_CHUNKS = N_CTX_CHUNKS + N_X_CHUNKS
SCAN_ROWS = N_SEQ_CHUNKS * BATCH
S5_GROUPS_PER_STEP = 8

TM = 256
EXPERT_BM = 256
N_ASSIGN = N_TOK * TOP_K
EXPERT_DEPTH = 3
N_BLOCKS = -(-(N_ASSIGN // EXPERT_BM + N_EXPERTS) // EXPERT_DEPTH) * EXPERT_DEPTH
N_ROWS = N_BLOCKS * EXPERT_BM
TM_COMBINE = 128
VMEM_LIMIT = 56 * 1024 * 1024

_F32 = jnp.float32
_BF16 = jnp.bfloat16


def _cparams(*sem):
    return pltpu.CompilerParams(dimension_semantics=sem, vmem_limit_bytes=VMEM_LIMIT)


def _const_spec(shape):
    nd = len(shape)
    return pl.BlockSpec(shape, lambda *_: (0,) * nd, pipeline_mode=pl.Buffered(1))


def _rms_scale(xf):
    return lax.rsqrt(jnp.mean(xf * xf, axis=-1, keepdims=True) + EPS)


def _silu(x):
    return x * jax.nn.sigmoid(x)


def _ada_kernel(c_ref, w_ref, b_ref, o_ref):
    s = _silu(c_ref[...])
    o_ref[...] = jnp.dot(s, w_ref[...], preferred_element_type=_F32) + b_ref[...]


def _ada_mod(c8, w_ada, b_ada):
    n = w_ada.shape[1]
    tn = 1024
    return pl.pallas_call(
        _ada_kernel,
        grid=(n // tn,),
        in_specs=[pl.BlockSpec((8, D_MODEL), lambda j: (0, 0)),
                  pl.BlockSpec((D_MODEL, tn), lambda j: (0, j)),
                  pl.BlockSpec((1, tn), lambda j: (0, j))],
        out_specs=pl.BlockSpec((8, tn), lambda j: (0, j)),
        out_shape=jax.ShapeDtypeStruct((8, n), _F32),
        compiler_params=_cparams("arbitrary"),
        name="ada_mod",
    )(c8, w_ada, b_ada)


def _modulated_norm(x_ref, mod_ref, g_ref, shift_row, scale_row):
    xf = x_ref[...]
    m = mod_ref[0]
    h = xf * _rms_scale(xf) * g_ref[...]
    return h * (1.0 + m[scale_row:scale_row + 1]) + m[shift_row:shift_row + 1]


def _inproj_kernel(x_ref, mod_ref, g_ref, w_ref, cw_ref, cb_ref, u_ref, conv_ref):
    h = _modulated_norm(x_ref, mod_ref, g_ref, 0, 1).astype(_BF16)
    u_ref[...] = jnp.dot(h, w_ref[:, 0:D_SSM], preferred_element_type=_F32)
    tm = x_ref.shape[0]
    pos = lax.broadcasted_iota(jnp.int32, (tm, 1), 0) % GRID_W
    not_first = (pos != 0).astype(_F32)
    not_last = (pos != GRID_W - 1).astype(_F32)
    cw = cw_ref[...]
    nc = 256
    for j in range(D_CONV // nc):
        lo = j * nc
        bg = jnp.dot(h, w_ref[:, D_SSM + lo:D_SSM + lo + nc], preferred_element_type=_F32)
        cg = jnp.dot(h, w_ref[:, D_SSM + D_CONV + lo:D_SSM + D_CONV + lo + nc], preferred_element_type=_F32)
        v = jnp.dot(h, w_ref[:, D_SSM + 2 * D_CONV + lo:D_SSM + 2 * D_CONV + lo + nc],
                    preferred_element_type=_F32)
        z = cg * v
        z_prev = pltpu.roll(z, 1, axis=0) * not_first
        z_next = pltpu.roll(z, tm - 1, axis=0) * not_last
        y = (cb_ref[:, lo:lo + nc] + z_prev * cw[0:1, lo:lo + nc] + z * cw[1:2, lo:lo + nc]
             + z_next * cw[2:3, lo:lo + nc])
        conv_ref[:, lo:lo + nc] = bg * y


def _inproj(x2, mod3, norm_g, w_in_bf, conv_w, conv_b):
    d_in = w_in_bf.shape[1]
    tiles_per_batch = SEQ // TM
    return pl.pallas_call(
        _inproj_kernel,
        grid=(N_TOK // TM,),
        in_specs=[pl.BlockSpec((TM, D_MODEL), lambda i: (i, 0)),
                  pl.BlockSpec((1, N_MOD, D_MODEL), lambda i: (i // tiles_per_batch, 0, 0)),
                  _const_spec((1, D_MODEL)),
                  _const_spec((D_MODEL, d_in)),
                  _const_spec((8, D_CONV)),
                  _const_spec((1, D_CONV))],
        out_specs=[pl.BlockSpec((TM, D_SSM), lambda i: (i, 0)),
                   pl.BlockSpec((TM, D_CONV), lambda i: (i, 0))],
        out_shape=[jax.ShapeDtypeStruct((N_TOK, D_SSM), _F32),
                   jax.ShapeDtypeStruct((N_TOK, D_CONV), _F32)],
        compiler_params=_cparams("arbitrary"),
        name="in_proj",
    )(x2, mod3, norm_g, w_in_bf, conv_w, conv_b)


def _ctxproj_kernel(x_ref, mod_ref, g_ref, w_ref, u_ref):
    h = _modulated_norm(x_ref, mod_ref, g_ref, 0, 1).astype(_BF16)
    u_ref[...] = jnp.dot(h, w_ref[...], preferred_element_type=_F32)


def _ctxproj(ctx2, mod3, norm_g, w_u_bf):
    return pl.pallas_call(
        _ctxproj_kernel,
        grid=(N_CTX_TOK // TM,),
        in_specs=[pl.BlockSpec((TM, D_MODEL), lambda i: (i, 0)),
                  pl.BlockSpec((1, N_MOD, D_MODEL), lambda i: (BATCH, 0, 0)),
                  _const_spec((1, D_MODEL)),
                  _const_spec((D_MODEL, D_SSM))],
        out_specs=pl.BlockSpec((TM, D_SSM), lambda i: (i, 0)),
        out_shape=jax.ShapeDtypeStruct((N_CTX_TOK, D_SSM), _F32),
        compiler_params=_cparams("arbitrary"),
        name="ctx_proj",
    )(ctx2, mod3, norm_g, w_u_bf)


N_POW = 32


def _s5_param_kernel(prow_ref, ct_ref, bt_ref, toep_ref, bpow_ref, cpow_ref, dec_ref):
    lane = lambda shape: lax.broadcasted_iota(jnp.int32, shape, 1)
    sub = lambda shape: lax.broadcasted_iota(jnp.int32, shape, 0)

    def split2(a):
        a_hi = a.astype(_BF16)
        return a_hi, (a - a_hi.astype(_F32)).astype(_BF16)

    rep = (lane((SSM_GROUP, CHUNK_W)) % SSM_GROUP == sub((SSM_GROUP, CHUNK_W))).astype(_BF16)
    sgn_col = jnp.where(sub((STATE_W, 1)) < SSM_STATE, 1.0, -1.0).astype(_F32)
    sgn_row = jnp.where(lane((1, STATE_W)) < SSM_STATE, -1.0, 1.0).astype(_F32)
    tau_col = sub((N_POW, 1)).astype(_F32)
    blk_l = lane((N_POW, CHUNK_W)) // SSM_GROUP
    tau_s = sub((N_POW, CHUNK_W))
    blk_r = sub((CHUNK_W, N_POW)) // SSM_GROUP
    tau_l = lane((CHUNK_W, N_POW))
    pick_l = lambda e: (tau_s == e).astype(_BF16)
    pick_r = lambda e: (tau_l == e).astype(_BF16)
    tn = (((0,), (0,)), ((), ()))

    strips = []
    for d in range(2):
        pr = prow_ref[d, 0]
        lam_re, lam_im, dt = pr[0:1], pr[1:2], jnp.exp(pr[2:3])
        mag = jnp.exp(tau_col * (lam_re * dt))
        ang = tau_col * (lam_im * dt)
        pw_re = mag * jnp.cos(ang)
        pw_im = mag * jnp.sin(ang)
        pw4 = jnp.concatenate(split2(pw_re) + split2(pw_im), axis=1)

        c_hi, c_lo = split2(ct_ref[d, 0])
        ct = (jnp.dot(c_hi, rep, preferred_element_type=_F32)
              + jnp.dot(c_lo, rep, preferred_element_type=_F32))
        ca = ct * sgn_col
        cb = -pltpu.roll(ct, SSM_STATE, axis=0)

        def cpow(sel):
            o = lax.dot_general(pw4, sel, tn, preferred_element_type=_F32)
            x = o[0:STATE_W] + o[STATE_W:2 * STATE_W]
            y = o[2 * STATE_W:3 * STATE_W] + o[3 * STATE_W:4 * STATE_W]
            return x * ca + y * cb

        nr = pw_re[1:2] - 1.0
        ni = pw_im[1:2]
        den = lam_re * lam_re + lam_im * lam_im
        kr = (nr * lam_re + ni * lam_im) / den
        ki = (ni * lam_re - nr * lam_im) / den
        b1 = bt_ref[d, 0]
        b2 = pltpu.roll(b1, SSM_STATE, axis=1) * sgn_row
        u1 = kr * b1 + ki * b2
        u2 = kr * b2 - ki * b1
        u1t = jnp.concatenate([u1] * CHUNK, axis=0)
        u2t = jnp.concatenate([u2] * CHUNK, axis=0)

        def bpow(sel):
            o = jnp.dot(sel, pw4, preferred_element_type=_F32)
            x = o[:, 0:STATE_W] + o[:, STATE_W:2 * STATE_W]
            y = o[:, 2 * STATE_W:3 * STATE_W] + o[:, 3 * STATE_W:4 * STATE_W]
            return x * u1t + y * u2t

        if d == 0:
            e_strip, e_b, e_c = blk_l, (CHUNK - 1) - blk_r, blk_l + 1
        else:
            e_strip, e_b, e_c = (CHUNK - 1) - blk_l, blk_r, CHUNK - blk_l
        u_hi, u_lo = split2(u1)
        k_hi, k_lo = split2(cpow(pick_l(e_strip)))
        strips.append(jnp.dot(u_hi, k_hi, preferred_element_type=_F32)
                      + jnp.dot(u_hi, k_lo, preferred_element_type=_F32)
                      + jnp.dot(u_lo, k_hi, preferred_element_type=_F32))
        bp = bpow(pick_r(e_b))
        bpow_ref[0, :, d * STATE_W:(d + 1) * STATE_W] = bp.astype(_BF16)
        bpow_ref[0, :, (2 + d) * STATE_W:(3 + d) * STATE_W] = pltpu.roll(bp, SSM_STATE, axis=1).astype(_BF16)
        cpow_ref[0, d * STATE_W:(d + 1) * STATE_W, :] = cpow(pick_l(e_c)).astype(_BF16)

        dec_ref[0, 2 * d:2 * d + 1, :] = pw_re[CHUNK:CHUNK + 1]
        dec_ref[0, 2 * d + 1:2 * d + 2, :] = pw_im[CHUNK:CHUNK + 1] * sgn_row

    zeros = jnp.zeros((SSM_GROUP, CHUNK_W), _F32)
    strip = (jnp.concatenate([strips[1], zeros], axis=1)
             + pltpu.roll(jnp.concatenate([strips[0], zeros], axis=1), CHUNK_W - SSM_GROUP, axis=1))
    for i in range(CHUNK):
        off = (CHUNK - 1 - i) * SSM_GROUP
        win = strip if off == 0 else pltpu.roll(strip, 2 * CHUNK_W - off, axis=1)
        toep_ref[0, i * SSM_GROUP:(i + 1) * SSM_GROUP, :] = win[:, 0:CHUNK_W].astype(_BF16)


def _s5_params(prow, ct2, bt1):
    g_spec = lambda shape: pl.BlockSpec(shape, lambda g: (0, g, 0, 0))
    o_spec = lambda shape: pl.BlockSpec(shape, lambda g: (g, 0, 0))
    return pl.pallas_call(
        _s5_param_kernel,
        grid=(SSM_GROUPS,),
        in_specs=[g_spec((2, 1, 8, STATE_W)),
                  g_spec((2, 1, STATE_W, SSM_GROUP)), g_spec((2, 1, SSM_GROUP, STATE_W))],
        out_specs=[o_spec((1, CHUNK_W, CHUNK_W)), o_spec((1, CHUNK_W, 4 * STATE_W)),
                   o_spec((1, 2 * STATE_W, CHUNK_W)), o_spec((1, 4, STATE_W))],
        out_shape=[jax.ShapeDtypeStruct((SSM_GROUPS, CHUNK_W, CHUNK_W), _BF16),
                   jax.ShapeDtypeStruct((SSM_GROUPS, CHUNK_W, 4 * STATE_W), _BF16),
                   jax.ShapeDtypeStruct((SSM_GROUPS, 2 * STATE_W, CHUNK_W), _BF16),
                   jax.ShapeDtypeStruct((SSM_GROUPS, 4, STATE_W), _F32)],
        compiler_params=_cparams("arbitrary"),
        name="s5_params",
    )(prow, ct2, bt1)


def _s5_kernel(ux_ref, uc_ref, toep_ref, bpow_ref, cpow_ref, dec_ref, y_ref, u_ref, s_ref, h_ref, y2_ref):
    ng = S5_GROUPS_PER_STEP
    gw = SSM_GROUP
    x0 = N_CTX_CHUNKS * BATCH

    half = CHUNK // 2
    assert half == ng
    lanes = ng * 128

    r = lax.broadcasted_iota(jnp.int32, (lanes, 1), 0)
    swapped = ((r // gw) % ng) * 128 + (r // 128) * gw + r % gw
    perm = (lax.broadcasted_iota(jnp.int32, (lanes, lanes), 1) == swapped).astype(_BF16)

    def to_chunk_layout(src_ref, seq_len, n_chunks, row0):
        for hf in range(2):
            p = jnp.concatenate(
                [jnp.concatenate([src_ref[pl.ds(b * seq_len + hf * half + il, n_chunks, stride=CHUNK), :]
                                  for il in range(half)], axis=1) for b in range(BATCH)], axis=0).astype(_BF16)
            q = jnp.dot(p, perm, preferred_element_type=_F32)
            for b in range(BATCH):
                for g in range(ng):
                    u_ref[g, hf, pl.ds(row0 + b, n_chunks, stride=BATCH), :] = (
                        q[b * n_chunks:(b + 1) * n_chunks, g * 128:(g + 1) * 128])

    to_chunk_layout(uc_ref, CTX_LEN, N_CTX_CHUNKS, 0)
    to_chunk_layout(ux_ref, SEQ, N_X_CHUNKS, x0)

    chunk_rows = lambda ref, g, lo: jnp.concatenate([ref[g, 0, lo:, :], ref[g, 1, lo:, :]], axis=1)
    for g in range(ng):
        s_ref[g] = jnp.dot(chunk_rows(u_ref, g, 0).astype(_BF16), bpow_ref[g], preferred_element_type=_F32)

    pair = 2 * BATCH
    n_ctx_pairs = N_CTX_CHUNKS // 2
    n_pairs = N_SEQ_CHUNKS // 2
    lower = lax.broadcasted_iota(jnp.int32, (pair, STATE_W), 0) < BATCH
    zero = jnp.zeros((pair, STATE_W), _F32)

    def step(t, carry):
        tb = jnp.where(t < n_ctx_pairs, n_ctx_pairs - 1 - t, n_pairs + n_ctx_pairs - 1 - t)
        rf = pl.multiple_of(t * pair, pair)
        rb = pl.multiple_of(tb * pair, pair)
        out = []
        swap = lambda v: pltpu.roll(v, BATCH, axis=0)
        for g in range(ng):
            hf, hfs, hb, hbs = carry[4 * g:4 * g + 4]
            dg = dec_ref[g]

            def advance(h, hs, a_re, a_im, s, ss):
                return a_re * h + a_im * hs + s, a_re * hs - a_im * h + ss

            col = lambda r, k: s_ref[g, pl.ds(r, pair), k * STATE_W:(k + 1) * STATE_W]
            sf, sfs = col(rf, 0), col(rf, 2)
            mid, mids = advance(hf, hfs, dg[0:1], dg[1:2], jnp.where(lower, sf, swap(sf)),
                                jnp.where(lower, sfs, swap(sfs)))
            h_ref[g, pl.ds(rf, pair), 0:STATE_W] = jnp.where(lower, hf, mid)
            end, ends = advance(mid, mids, dg[0:1], dg[1:2], sf, sfs)
            hf, hfs = jnp.where(lower, swap(end), end), jnp.where(lower, swap(ends), ends)

            sb, sbs = col(rb, 1), col(rb, 3)
            mid, mids = advance(hb, hbs, dg[2:3], dg[3:4], jnp.where(lower, swap(sb), sb),
                                jnp.where(lower, swap(sbs), sbs))
            h_ref[g, pl.ds(rb, pair), STATE_W:2 * STATE_W] = jnp.where(lower, mid, hb)
            end, ends = advance(mid, mids, dg[2:3], dg[3:4], sb, sbs)
            hb, hbs = jnp.where(lower, end, swap(end)), jnp.where(lower, ends, swap(ends))
            out += [hf, hfs, hb, hbs]
        return tuple(out)

    lax.fori_loop(0, n_pairs, step, (zero,) * (4 * ng))

    for g in range(ng):
        ux = chunk_rows(u_ref, g, x0).astype(_BF16)
        hx = h_ref[g, x0:SCAN_ROWS, :].astype(_BF16)
        y2 = (jnp.dot(ux, toep_ref[g], preferred_element_type=_F32)
              + jnp.dot(hx, cpow_ref[g], preferred_element_type=_F32))
        y2_ref[g, 0] = y2[:, 0:128]
        y2_ref[g, 1] = y2[:, 128:256]

    for hf in range(2):
        yin = jnp.concatenate(
            [jnp.concatenate([y2_ref[g, hf, pl.ds(b, N_X_CHUNKS, stride=BATCH), :] for g in range(ng)], axis=1)
             for b in range(BATCH)], axis=0)
        y_hi = yin.astype(_BF16)
        y_lo = (yin - y_hi.astype(_F32)).astype(_BF16)
        q = (jnp.dot(y_hi, perm, preferred_element_type=_F32)
             + jnp.dot(y_lo, perm, preferred_element_type=_F32))
        for b in range(BATCH):
            for jl in range(half):
                y_ref[pl.ds(b * SEQ + hf * half + jl, N_X_CHUNKS, stride=CHUNK), :] = (
                    q[b * N_X_CHUNKS:(b + 1) * N_X_CHUNKS, jl * 128:(jl + 1) * 128])


def _s5_scan(u_x, u_c, toep, bpow, cpow, dec):
    ng = S5_GROUPS_PER_STEP
    n_x_rows = N_X_CHUNKS * BATCH
    lanes = ng * SSM_GROUP
    spec = lambda r, c: pl.BlockSpec((ng, r, c), lambda i: (i, 0, 0))
    col = lambda rows: pl.BlockSpec((rows, lanes), lambda i: (0, i))
    return pl.pallas_call(
        _s5_kernel,
        grid=(SSM_GROUPS // ng,),
        in_specs=[col(N_TOK), col(N_CTX_TOK), spec(CHUNK_W, CHUNK_W), spec(CHUNK_W, 4 * STATE_W),
                  spec(2 * STATE_W, CHUNK_W), spec(4, STATE_W)],
        out_specs=col(N_TOK),
        out_shape=jax.ShapeDtypeStruct((N_TOK, D_SSM), _F32),
        scratch_shapes=[pltpu.VMEM((ng, 2, SCAN_ROWS, 128), _F32),
                        pltpu.VMEM((ng, SCAN_ROWS, 4 * STATE_W), _F32),
                        pltpu.VMEM((ng, SCAN_ROWS, 2 * STATE_W), _F32),
                        pltpu.VMEM((ng, 2, n_x_rows, 128), _F32)],
        compiler_params=_cparams("arbitrary"),
        name="s5_scan",
    )(u_x, u_c, toep, bpow, cpow, dec)


def _merge_kernel(ys_ref, u_ref, conv_ref, x_ref, mod_ref, dskip_ref, mixg_ref, n2g_ref,
                  wglu_ref, wout_ref, rwt_ref, x1_ref, h2_ref, logit_ref):
    m = mod_ref[0]
    yx = dskip_ref[...] * u_ref[...] + ys_ref[...]
    c0 = math.sqrt(2.0 / math.pi)
    ge = 0.5 * yx * (1.0 + jnp.tanh(c0 * (yx + 0.044715 * (yx * yx * yx))))
    z = jnp.dot(ge.astype(_BF16), wglu_ref[...], preferred_element_type=_F32)
    ssm_y = z[:, 0:D_SSM] * jax.nn.sigmoid(z[:, D_SSM:2 * D_SSM])
    conv_y = conv_ref[...]
    mixg = mixg_ref[...]
    heads_a = (ssm_y * _rms_scale(ssm_y) * mixg[:, 0:D_SSM]).astype(_BF16)
    heads_b = (conv_y * _rms_scale(conv_y) * mixg[:, D_SSM:]).astype(_BF16)
    mix = (jnp.dot(heads_a, wout_ref[0:D_SSM, :], preferred_element_type=_F32)
           + jnp.dot(heads_b, wout_ref[D_SSM:, :], preferred_element_type=_F32))
    x1 = x_ref[...] + m[2:3] * mix
    x1_ref[...] = x1
    h2 = x1 * _rms_scale(x1) * n2g_ref[...] * (1.0 + m[4:5]) + m[3:4]
    h2_ref[...] = h2
    h_hi = h2.astype(_BF16)
    h_lo = (h2 - h_hi.astype(_F32)).astype(_BF16)
    nt = (((1,), (1,)), ((), ()))
    p = lax.dot_general(rwt_ref[...], h_hi, nt, preferred_element_type=_F32)
    q = lax.dot_general(rwt_ref[0:N_EXPERTS, :], h_lo, nt, preferred_element_type=_F32)
    logit_ref[...] = p[0:N_EXPERTS] + p[N_EXPERTS:2 * N_EXPERTS] + q


def _merge(ys, u, conv, x2, mod3, dskip, mixg, n2g, wglu_bf, wout_bf, rwt):
    tiles_per_batch = SEQ // TM
    tok = lambda w: pl.BlockSpec((TM, w), lambda i: (i, 0))
    return pl.pallas_call(
        _merge_kernel,
        grid=(N_TOK // TM,),
        in_specs=[tok(D_SSM), tok(D_SSM), tok(D_CONV), tok(D_MODEL),
                  pl.BlockSpec((1, N_MOD, D_MODEL), lambda i: (i // tiles_per_batch, 0, 0)),
                  _const_spec((1, D_SSM)), _const_spec((1, D_MODEL)), _const_spec((1, D_MODEL)),
                  _const_spec((D_SSM, 2 * D_SSM)), _const_spec((D_MODEL, D_MODEL)),
                  _const_spec((2 * N_EXPERTS, D_MODEL))],
        out_specs=[tok(D_MODEL), tok(D_MODEL),
                   pl.BlockSpec((N_EXPERTS, TM), lambda i: (0, i))],
        out_shape=[jax.ShapeDtypeStruct((N_TOK, D_MODEL), _F32),
                   jax.ShapeDtypeStruct((N_TOK, D_MODEL), _F32),
                   jax.ShapeDtypeStruct((N_EXPERTS, N_TOK), _F32)],
        compiler_params=_cparams("arbitrary"),
        name="merge_heads",
    )(ys, u, conv, x2, mod3, dskip, mixg, n2g, wglu_bf, wout_bf, rwt)


def _route_kernel(logit_ref, bias_ref, eid_ref, w_ref, rank_ref, cnt_ref, carry_ref):
    @pl.when(pl.program_id(0) == 0)
    def _():
        carry_ref[...] = jnp.zeros_like(carry_ref)

    tm = logit_ref.shape[1]
    neg = jnp.float32(-jnp.inf)
    scores = jax.nn.sigmoid(logit_ref[...])
    biased = scores + bias_ref[:, 0:1]
    sub = lax.broadcasted_iota(jnp.int32, (GROUP_SIZE, tm), 0)
    rows = lambda a, g: a[g * GROUP_SIZE:(g + 1) * GROUP_SIZE]
    ngrp = N_EXPERT_GROUPS

    gscore = []
    for g in range(ngrp):
        bg = rows(biased, g)
        m1 = jnp.max(bg, axis=0, keepdims=True)
        first = jnp.min(jnp.where(bg == m1, sub, GROUP_SIZE), axis=0, keepdims=True)
        m2 = jnp.max(jnp.where(sub == first, neg, bg), axis=0, keepdims=True)
        gscore.append(m1 + m2)
    v = []
    for g in range(ngrp):
        beaten = jnp.zeros((1, tm), jnp.int32)
        for o in range(ngrp):
            if o != g:
                beats = (gscore[o] >= gscore[g]) if o < g else (gscore[o] > gscore[g])
                beaten = beaten + beats.astype(jnp.int32)
        v.append(jnp.where(beaten < TOPK_GROUPS, rows(biased, g), neg))
    eids = [sub + g * GROUP_SIZE for g in range(ngrp)]

    pick_ids, pick_masks = [], []
    for _ in range(TOP_K):
        m = functools.reduce(jnp.maximum, v)
        m = jnp.max(m, axis=0, keepdims=True)
        cand = functools.reduce(jnp.minimum, [jnp.where(v[g] == m, eids[g], N_EXPERTS) for g in range(ngrp)])
        pick_id = jnp.min(cand, axis=0, keepdims=True)
        masks = [eids[g] == pick_id for g in range(ngrp)]
        v = [jnp.where(masks[g], neg, v[g]) for g in range(ngrp)]
        pick_ids.append(pick_id)
        pick_masks.append(masks)

    sel = jnp.concatenate(
        [functools.reduce(jnp.logical_or, [pick_masks[k][g] for k in range(TOP_K)]).astype(_F32)
         for g in range(ngrp)], axis=0)
    before = (lax.broadcasted_iota(jnp.int32, (tm, tm), 0)
              < lax.broadcasted_iota(jnp.int32, (tm, tm), 1)).astype(_BF16)
    base = jnp.dot(sel.astype(_BF16), before, preferred_element_type=_F32) + carry_ref[:, 0:1]

    def gather_pick(a, k):
        parts = [jnp.sum(jnp.where(pick_masks[k][g], rows(a, g), 0.0), axis=0, keepdims=True)
                 for g in range(ngrp)]
        return functools.reduce(jnp.add, parts)

    picked = [gather_pick(scores, k) for k in range(TOP_K)]
    denom = functools.reduce(jnp.add, picked)
    for k in range(TOP_K):
        eid_ref[k:k + 1, :] = pick_ids[k]
        w_ref[k:k + 1, :] = picked[k] / denom * ROUTED_SCALE
        rank_ref[k:k + 1, :] = gather_pick(base, k).astype(jnp.int32)
    carry_ref[...] = carry_ref[...] + jnp.sum(sel, axis=1, keepdims=True)
    cnt_ref[...] = carry_ref[...]


def _route(logits_t, bias_col):
    tok = pl.BlockSpec((TOP_K, TM), lambda i: (0, i))
    return pl.pallas_call(
        _route_kernel,
        grid=(N_TOK // TM,),
        in_specs=[pl.BlockSpec((N_EXPERTS, TM), lambda i: (0, i)), _const_spec((N_EXPERTS, 128))],
        out_specs=[tok, tok, tok, pl.BlockSpec((N_EXPERTS, 128), lambda i: (0, 0))],
        out_shape=[jax.ShapeDtypeStruct((TOP_K, N_TOK), jnp.int32),
                   jax.ShapeDtypeStruct((TOP_K, N_TOK), _F32),
                   jax.ShapeDtypeStruct((TOP_K, N_TOK), jnp.int32),
                   jax.ShapeDtypeStruct((N_EXPERTS, 128), _F32)],
        scratch_shapes=[pltpu.VMEM((N_EXPERTS, 128), _F32)],
        compiler_params=_cparams("arbitrary"),
        name="route",
    )(logits_t, bias_col)


def _rowmap_kernel(dest_ref, padstart_ref, padn_ref, h_ref, x1_ref, mod_ref, wsg_ref, wsu_ref, wsd_ref,
                   xmid_ref, codes_ref):
    i = pl.program_id(0)
    tm = h_ref.shape[0]
    base = i * tm

    @pl.when(i == 0)
    def _():
        def fill(e, c):
            def one(r, c2):
                codes_ref[padstart_ref[e] + r] = 0
                return c2
            return lax.fori_loop(0, padn_ref[e], one, c)
        lax.fori_loop(0, N_EXPERTS, fill, 0)

    for t in range(tm):
        ds = [dest_ref[k * N_TOK + base + t] for k in range(TOP_K)]
        for k in range(TOP_K):
            codes_ref[ds[k]] = base + t

    h = h_ref[...].astype(_BF16)
    a = jnp.dot(h, wsg_ref[...], preferred_element_type=_F32)
    b = jnp.dot(h, wsu_ref[...], preferred_element_type=_F32)
    shared = jnp.dot((_silu(a) * b).astype(_BF16), wsd_ref[...], preferred_element_type=_F32)
    xmid_ref[...] = x1_ref[...] + mod_ref[0][5:6] * shared


def _rowmap(dest, pad_start, pad_n, h2, x1, mod3, wsg_bf, wsu_bf, wsd_bf):
    tm = TM
    tiles_per_batch = SEQ // tm
    tok = pl.BlockSpec((tm, D_MODEL), lambda i, *_: (i, 0))
    const = lambda shape: pl.BlockSpec(shape, lambda i, *_: (0,) * len(shape), pipeline_mode=pl.Buffered(1))
    grid_spec = pltpu.PrefetchScalarGridSpec(
        num_scalar_prefetch=3,
        grid=(N_TOK // tm,),
        in_specs=[tok, tok,
                  pl.BlockSpec((1, N_MOD, D_MODEL), lambda i, *_: (i // tiles_per_batch, 0, 0)),
                  const((D_MODEL, D_SHARED)), const((D_MODEL, D_SHARED)), const((D_SHARED, D_MODEL))],
        out_specs=[tok, pl.BlockSpec(memory_space=pltpu.SMEM)],
    )
    return pl.pallas_call(
        _rowmap_kernel,
        grid_spec=grid_spec,
        out_shape=[jax.ShapeDtypeStruct((N_TOK, D_MODEL), _F32),
                   jax.ShapeDtypeStruct((N_ROWS,), jnp.int32)],
        compiler_params=_cparams("arbitrary"),
        name="rowmap",
    )(dest.reshape(N_ASSIGN), pad_start, pad_n, h2, x1, mod3, wsg_bf, wsu_bf, wsd_bf)


def _experts_kernel(be_ref, nact_ref, first_ref, slot_ref, nxt_ref, codes_ref, h_hbm, wg_hbm, wu_hbm, wd_hbm,
                    out_hbm, x0, x1, x2, y0, y1, y2, wg_f, wu_f, wd_f, wg_bf, wu_bf, wd_bf, sem, gsem, ssem):
    n_act = nact_ref[0]
    bm = EXPERT_BM
    depth = EXPERT_DEPTH
    xbuf, ybuf = (x0, x1, x2), (y0, y1, y2)

    def fetch(e, s):
        return (pltpu.make_async_copy(wg_hbm.at[e], wg_f.at[s], sem.at[s, 0]),
                pltpu.make_async_copy(wu_hbm.at[e], wu_f.at[s], sem.at[s, 1]),
                pltpu.make_async_copy(wd_hbm.at[e], wd_f.at[s], sem.at[s, 2]))

    def gather(blk, p):
        for r in range(bm):
            tok = codes_ref[blk * bm + r]
            pltpu.make_async_copy(h_hbm.at[pl.ds(tok, 1), :], xbuf[p].at[pl.ds(r, 1), :],
                                  gsem.at[p]).start(priority=r % 2)

    def gather_wait(p):
        pltpu.make_async_copy(h_hbm.at[pl.ds(0, bm), :], xbuf[p], gsem.at[p]).wait()

    def writeback(blk, p):
        return pltpu.make_async_copy(ybuf[p], out_hbm.at[pl.ds(pl.multiple_of(blk * bm, bm), bm), :], ssem.at[p])

    def block(blk, p):
        nxt1, nxt2 = (p + 1) % depth, (p + 2) % depth

        @pl.when(blk < n_act)
        def _():
            @pl.when(blk == 0)
            def _():
                for cp in fetch(be_ref[0], 0):
                    cp.start()
                gather(0, 0)
                gather(jnp.minimum(1, n_act - 1), 1)

            @pl.when(first_ref[blk] == 1)
            def _():
                s = slot_ref[blk]
                for cp in fetch(be_ref[blk], s):
                    cp.wait()

                @pl.when(nxt_ref[blk] >= 0)
                def _():
                    for cp in fetch(nxt_ref[blk], 1 - s):
                        cp.start()

                wg_bf[...] = wg_f[s].astype(_BF16)
                wu_bf[...] = wu_f[s].astype(_BF16)
                wd_bf[...] = wd_f[s].astype(_BF16)

            gather_wait(p)

            @pl.when(blk >= depth)
            def _():
                writeback(blk - depth, p).wait()

            gather(jnp.minimum(blk + 2, n_act - 1), nxt2)
            x = xbuf[p][...].astype(_BF16)
            a = jnp.dot(x, wg_bf[...], preferred_element_type=_F32)
            b = jnp.dot(x, wu_bf[...], preferred_element_type=_F32)
            ybuf[p][...] = jnp.dot((_silu(a) * b).astype(_BF16), wd_bf[...], preferred_element_type=_F32)
            writeback(blk, p).start()

            @pl.when(blk == n_act - 1)
            def _():
                gather_wait(nxt1)
                gather_wait(nxt2)
                writeback(blk, p).wait()

                @pl.when(blk >= 1)
                def _():
                    writeback(blk - 1, nxt2).wait()

                @pl.when(blk >= 2)
                def _():
                    writeback(blk - 2, nxt1).wait()

    i = pl.program_id(0)
    for q in range(depth):
        block(depth * i + q, q)


def _experts(block_e, n_active, first, slot, nxt, codes, h2, w_gate, w_up, w_down):
    bm = EXPERT_BM
    depth = EXPERT_DEPTH
    hbm = pl.BlockSpec(memory_space=pl.ANY)
    grid_spec = pltpu.PrefetchScalarGridSpec(
        num_scalar_prefetch=6,
        grid=(N_BLOCKS // depth,),
        in_specs=[hbm, hbm, hbm, hbm],
        out_specs=hbm,
        scratch_shapes=[pltpu.VMEM((bm, D_MODEL), _F32)] * (2 * depth) + [
                        pltpu.VMEM((2, D_MODEL, D_EXPERT), _F32), pltpu.VMEM((2, D_MODEL, D_EXPERT), _F32),
                        pltpu.VMEM((2, D_EXPERT, D_MODEL), _F32),
                        pltpu.VMEM((D_MODEL, D_EXPERT), _BF16), pltpu.VMEM((D_MODEL, D_EXPERT), _BF16),
                        pltpu.VMEM((D_EXPERT, D_MODEL), _BF16),
                        pltpu.SemaphoreType.DMA((2, 3)), pltpu.SemaphoreType.DMA((depth,)),
                        pltpu.SemaphoreType.DMA((depth,))],
    )
    return pl.pallas_call(
        _experts_kernel,
        grid_spec=grid_spec,
        out_shape=jax.ShapeDtypeStruct((N_ROWS, D_MODEL), _F32),
        compiler_params=_cparams("arbitrary"),
        name="experts",
    )(block_e, n_active, first, slot, nxt, codes, h2, w_gate, w_up, w_down)


def _final_kernel(dest_ref, xmid_ref, wt_ref, mod_ref, fg_ref, ys_ref, o_ref, buf0, buf1, sem):
    i = pl.program_id(0)
    n_tiles = 2 * pl.num_programs(0)
    tm = TM_COMBINE
    bufs = (buf0, buf1)

    def gather(tile, p):
        base = tile * tm
        for t in range(tm):
            for k in range(TOP_K):
                pltpu.make_async_copy(ys_ref.at[pl.ds(dest_ref[k * N_TOK + base + t], 1), :],
                                      bufs[p].at[pl.ds(k * tm + t, 1), :], sem.at[p]).start(priority=k % 2)

    def drain(p):
        pltpu.make_async_copy(ys_ref.at[pl.ds(0, TOP_K * tm), :], bufs[p], sem.at[p]).wait()

    @pl.when(i == 0)
    def _():
        gather(0, 0)

    g2 = mod_ref[0][5:6]
    for p in range(2):
        tile = 2 * i + p
        drain(p)
        gather(jnp.minimum(tile + 1, n_tiles - 1), 1 - p)
        rows = slice(p * tm, (p + 1) * tm)
        wt = wt_ref[rows, :]
        routed = wt[:, 0:1] * bufs[p][0:tm, :]
        for k in range(1, TOP_K):
            routed = routed + wt[:, k:k + 1] * bufs[p][k * tm:(k + 1) * tm, :]
        y = xmid_ref[rows, :] + g2 * routed
        o_ref[rows, :] = y * _rms_scale(y) * fg_ref[...]

    @pl.when(i == pl.num_programs(0) - 1)
    def _():
        drain(0)


def _final(dest, xmid, w_tok, mod3, final_g, ys):
    tm = 2 * TM_COMBINE
    tiles_per_batch = SEQ // tm
    tok = lambda w: pl.BlockSpec((tm, w), lambda i, *_: (i, 0))
    grid_spec = pltpu.PrefetchScalarGridSpec(
        num_scalar_prefetch=1,
        grid=(N_TOK // tm,),
        in_specs=[tok(D_MODEL), tok(TOP_K),
                  pl.BlockSpec((1, N_MOD, D_MODEL), lambda i, *_: (i // tiles_per_batch, 0, 0)),
                  pl.BlockSpec((1, D_MODEL), lambda i, *_: (0, 0), pipeline_mode=pl.Buffered(1)),
                  pl.BlockSpec(memory_space=pl.ANY)],
        out_specs=tok(D_MODEL),
        scratch_shapes=[pltpu.VMEM((TOP_K * TM_COMBINE, D_MODEL), _F32),
                        pltpu.VMEM((TOP_K * TM_COMBINE, D_MODEL), _F32), pltpu.SemaphoreType.DMA((2,))],
    )
    return pl.pallas_call(
        _final_kernel,
        grid_spec=grid_spec,
        out_shape=jax.ShapeDtypeStruct((N_TOK, D_MODEL), _F32),
        compiler_params=_cparams("arbitrary"),
        name="final",
    )(dest.reshape(N_ASSIGN), xmid, w_tok, mod3, final_g, ys)


def _s5_param_layouts(lam_re, lam_im, b_re, b_im, c_re, c_im, log_dt):
    two = lambda a: jnp.concatenate([a, a], axis=-1)
    lr, li = two(lam_re), two(lam_im)
    dtb = jnp.broadcast_to(log_dt.astype(_F32)[:, :, None], lr.shape)
    zeros = jnp.zeros_like(lr)
    prow = jnp.stack([lr, li, dtb] + [zeros] * 5, axis=2)
    ct2 = jnp.concatenate([jnp.swapaxes(c_re, -1, -2), jnp.swapaxes(c_im, -1, -2)], axis=2)
    bt1 = jnp.concatenate([jnp.swapaxes(b_re, -1, -2), jnp.swapaxes(b_im, -1, -2)], axis=3)
    return prow, ct2, bt1


def kernel(x, c, ctx, c_ctx, norm1_g, norm2_g, w_ada, b_ada, w_in, ssm_lam_re, ssm_lam_im, ssm_b_re, ssm_b_im, ssm_c_re, ssm_c_im, ssm_log_dt, ssm_d, ssm_w_glu, conv_w, conv_b, mix_norm_g, w_out, router_w, router_bias, exp_w_gate, exp_w_up, exp_w_down, shared_w_gate, shared_w_up, shared_w_down, final_g):
    layer = 0
    x2 = x.reshape(N_TOK, D_MODEL)
    ctx2 = ctx.reshape(N_CTX_TOK, D_MODEL)

    c8 = jnp.concatenate([c, c_ctx[None, :], jnp.zeros((8 - BATCH - 1, D_MODEL), _F32)], axis=0)
    mod = _ada_mod(c8, w_ada[layer], b_ada[layer][None, :])
    mod3 = mod.reshape(8, N_MOD, D_MODEL)

    w_in_bf = w_in[layer].astype(_BF16)
    conv_w8 = jnp.concatenate([conv_w[layer], jnp.zeros((8 - conv_w.shape[1], D_CONV), _F32)], axis=0)
    u_x, conv_x = _inproj(x2, mod3, norm1_g[layer][None, :], w_in_bf, conv_w8, conv_b[layer][None, :])
    u_c = _ctxproj(ctx2, mod3, norm1_g[layer][None, :], w_in_bf[:, :D_SSM])

    prow, ct2, bt1 = _s5_param_layouts(ssm_lam_re[layer], ssm_lam_im[layer], ssm_b_re[layer],
                                       ssm_b_im[layer], ssm_c_re[layer], ssm_c_im[layer], ssm_log_dt[layer])
    toep, bpow, cpow, dec = _s5_params(prow, ct2, bt1)
    ys = _s5_scan(u_x, u_c, toep, bpow, cpow, dec)

    rw = router_w[layer]
    rw_hi = rw.astype(_BF16)
    rw_lo = (rw - rw_hi.astype(_F32)).astype(_BF16)
    rwt = jnp.concatenate([rw_hi.T, rw_lo.T], axis=0)
    x1, h2, logits_t = _merge(ys, u_x, conv_x, x2, mod3, ssm_d[layer][None, :],
                              mix_norm_g[layer][None, :], norm2_g[layer][None, :],
                              ssm_w_glu[layer].astype(_BF16), w_out[layer].astype(_BF16), rwt)

    bias_col = jnp.broadcast_to(router_bias[layer][:, None], (N_EXPERTS, 128))
    eid, w_k, rank, cnt = _route(logits_t, bias_col)

    counts = cnt[:, 0].astype(jnp.int32)
    padded = (counts + EXPERT_BM - 1) // EXPERT_BM * EXPERT_BM
    pend = jnp.cumsum(padded)
    pstart = pend - padded
    is_e = eid[:, :, None] == jnp.arange(N_EXPERTS, dtype=jnp.int32)
    dest = jnp.sum(jnp.where(is_e, pstart, 0), axis=-1) + rank
    n_active = (pend[-1] // EXPERT_BM).astype(jnp.int32)
    blk = jnp.minimum(jnp.arange(N_BLOCKS, dtype=jnp.int32), n_active - 1)
    ends_before = (pend[None, :] <= (blk * EXPERT_BM)[:, None]).astype(jnp.int32)
    block_e = jnp.minimum(jnp.sum(ends_before, axis=1), N_EXPERTS - 1)

    xmid, codes = _rowmap(dest, pstart + counts, padded - counts, h2, x1, mod3,
                          shared_w_gate[layer].astype(_BF16), shared_w_up[layer].astype(_BF16),
                          shared_w_down[layer].astype(_BF16))
    first = jnp.concatenate([jnp.ones((1,), jnp.int32), (block_e[1:] != block_e[:-1]).astype(jnp.int32)])
    slot = (jnp.cumsum(first) - 1) % 2
    e_ids = jnp.arange(N_EXPERTS, dtype=jnp.int32)
    owner = jnp.where(padded > 0, e_ids, N_EXPERTS)
    later = jnp.min(jnp.where(e_ids[None, :] > e_ids[:, None], owner[None, :], N_EXPERTS), axis=1)
    nxt_e = jnp.where(later == N_EXPERTS, -1, later)
    nxt = jnp.sum(jnp.where(block_e[:, None] == e_ids[None, :], nxt_e[None, :], 0), axis=1)
    ys_rows = _experts(block_e, n_active[None], first, slot, nxt, codes, h2,
                       exp_w_gate[layer], exp_w_up[layer], exp_w_down[layer])
    out = _final(dest, xmid, w_k.T, mod3, final_g[None, :], ys_rows)
    return out.reshape(BATCH, SEQ, D_MODEL)
```

```python
import functools
import math

import jax
import jax.numpy as jnp
from jax import lax
from jax.experimental import pallas as pl
from jax.experimental.pallas import tpu as pltpu

D_MODEL = 2048
BATCH = 4
SEQ = 2048
CTX_LEN = 256
GRID_W = 64
D_SSM = 1024
D_CONV = 1024
SSM_GROUP = 16
SSM_GROUPS = 64
SSM_STATE = 64
N_EXPERTS = 64
N_EXPERT_GROUPS = 8
GROUP_SIZE = N_EXPERTS // N_EXPERT_GROUPS
TOPK_GROUPS = 4
TOP_K = 8
D_EXPERT = 512
D_SHARED = 512
ROUTED_SCALE = 2.5
N_MOD = 6
EPS = 1e-6

N_TOK = BATCH * SEQ
N_CTX_TOK = BATCH * CTX_LEN

CHUNK = 16
CHUNK_W = CHUNK * SSM_GROUP
STATE_W = 2 * SSM_STATE
N_CTX_CHUNKS = CTX_LEN // CHUNK
N_X_CHUNKS = SEQ // CHUNK
N_SEQ_CHUNKS = N_CTX_CHUNKS + N_X_CHUNKS
SCAN_ROWS = N_SEQ_CHUNKS * BATCH
S5_GROUPS_PER_STEP = 8

TM = 256
EXPERT_BM = 256
N_ASSIGN = N_TOK * TOP_K
EXPERT_DEPTH = 3
N_BLOCKS = -(-(N_ASSIGN // EXPERT_BM + N_EXPERTS) // EXPERT_DEPTH) * EXPERT_DEPTH
N_ROWS = N_BLOCKS * EXPERT_BM
TM_COMBINE = 128
VMEM_LIMIT = 56 * 1024 * 1024

_F32 = jnp.float32
_BF16 = jnp.bfloat16


def _cparams(*sem):
    return pltpu.CompilerParams(dimension_semantics=sem, vmem_limit_bytes=VMEM_LIMIT)


def _const_spec(shape):
    nd = len(shape)
    return pl.BlockSpec(shape, lambda *_: (0,) * nd, pipeline_mode=pl.Buffered(1))


def _rms_scale(xf):
    return lax.rsqrt(jnp.mean(xf * xf, axis=-1, keepdims=True) + EPS)


def _silu(x):
    return x * jax.nn.sigmoid(x)


def _ada_kernel(c_ref, w_ref, b_ref, o_ref):
    s = _silu(c_ref[...])
    o_ref[...] = jnp.dot(s, w_ref[...], preferred_element_type=_F32) + b_ref[...]


def _ada_mod(c8, w_ada, b_ada):
    n = w_ada.shape[1]
    tn = 1024
    return pl.pallas_call(
        _ada_kernel,
        grid=(n // tn,),
        in_specs=[pl.BlockSpec((8, D_MODEL), lambda j: (0, 0)),
                  pl.BlockSpec((D_MODEL, tn), lambda j: (0, j)),
                  pl.BlockSpec((1, tn), lambda j: (0, j))],
        out_specs=pl.BlockSpec((8, tn), lambda j: (0, j)),
        out_shape=jax.ShapeDtypeStruct((8, n), _F32),
        compiler_params=_cparams("arbitrary"),
        name="ada_mod",
    )(c8, w_ada, b_ada)


def _modulated_norm(x_ref, mod_ref, g_ref, shift_row, scale_row):
    xf = x_ref[...]
    m = mod_ref[0]
    h = xf * _rms_scale(xf) * g_ref[...]
    return h * (1.0 + m[scale_row:scale_row + 1]) + m[shift_row:shift_row + 1]


def _inproj_kernel(x_ref, mod_ref, g_ref, w_ref, cw_ref, cb_ref, u_ref, conv_ref):
    h = _modulated_norm(x_ref, mod_ref, g_ref, 0, 1).astype(_BF16)
    u_ref[...] = jnp.dot(h, w_ref[:, 0:D_SSM], preferred_element_type=_F32)
    tm = x_ref.shape[0]
    pos = lax.broadcasted_iota(jnp.int32, (tm, 1), 0) % GRID_W
    not_first = (pos != 0).astype(_F32)
    not_last = (pos != GRID_W - 1).astype(_F32)
    cw = cw_ref[...]
    nc = 256
    for j in range(D_CONV // nc):
        lo = j * nc
        bg = jnp.dot(h, w_ref[:, D_SSM + lo:D_SSM + lo + nc], preferred_element_type=_F32)
        cg = jnp.dot(h, w_ref[:, D_SSM + D_CONV + lo:D_SSM + D_CONV + lo + nc], preferred_element_type=_F32)
        v = jnp.dot(h, w_ref[:, D_SSM + 2 * D_CONV + lo:D_SSM + 2 * D_CONV + lo + nc],
                    preferred_element_type=_F32)
        z = cg * v
        z_prev = pltpu.roll(z, 1, axis=0) * not_first
        z_next = pltpu.roll(z, tm - 1, axis=0) * not_last
        y = (cb_ref[:, lo:lo + nc] + z_prev * cw[0:1, lo:lo + nc] + z * cw[1:2, lo:lo + nc]
             + z_next * cw[2:3, lo:lo + nc])
        conv_ref[:, lo:lo + nc] = bg * y


def _inproj(x2, mod3, norm_g, w_in_bf, conv_w, conv_b):
    d_in = w_in_bf.shape[1]
    tiles_per_batch = SEQ // TM
    return pl.pallas_call(
        _inproj_kernel,
        grid=(N_TOK // TM,),
        in_specs=[pl.BlockSpec((TM, D_MODEL), lambda i: (i, 0)),
                  pl.BlockSpec((1, N_MOD, D_MODEL), lambda i: (i // tiles_per_batch, 0, 0)),
                  _const_spec((1, D_MODEL)),
                  _const_spec((D_MODEL, d_in)),
                  _const_spec((8, D_CONV)),
                  _const_spec((1, D_CONV))],
        out_specs=[pl.BlockSpec((TM, D_SSM), lambda i: (i, 0)),
                   pl.BlockSpec((TM, D_CONV), lambda i: (i, 0))],
        out_shape=[jax.ShapeDtypeStruct((N_TOK, D_SSM), _F32),
                   jax.ShapeDtypeStruct((N_TOK, D_CONV), _F32)],
        compiler_params=_cparams("arbitrary"),
        name="in_proj",
    )(x2, mod3, norm_g, w_in_bf, conv_w, conv_b)


def _ctxproj_kernel(x_ref, mod_ref, g_ref, w_ref, u_ref):
    h = _modulated_norm(x_ref, mod_ref, g_ref, 0, 1).astype(_BF16)
    u_ref[...] = jnp.dot(h, w_ref[...], preferred_element_type=_F32)


def _ctxproj(ctx2, mod3, norm_g, w_u_bf):
    return pl.pallas_call(
        _ctxproj_kernel,
        grid=(N_CTX_TOK // TM,),
        in_specs=[pl.BlockSpec((TM, D_MODEL), lambda i: (i, 0)),
                  pl.BlockSpec((1, N_MOD, D_MODEL), lambda i: (BATCH, 0, 0)),
                  _const_spec((1, D_MODEL)),
                  _const_spec((D_MODEL, D_SSM))],
        out_specs=pl.BlockSpec((TM, D_SSM), lambda i: (i, 0)),
        out_shape=jax.ShapeDtypeStruct((N_CTX_TOK, D_SSM), _F32),
        compiler_params=_cparams("arbitrary"),
        name="ctx_proj",
    )(ctx2, mod3, norm_g, w_u_bf)


N_POW = 32


def _s5_param_kernel(prow_ref, ct_ref, bt_ref, toep_ref, bpow_ref, cpow_ref, dec_ref):
    lane = lambda shape: lax.broadcasted_iota(jnp.int32, shape, 1)
    sub = lambda shape: lax.broadcasted_iota(jnp.int32, shape, 0)

    def split2(a):
        a_hi = a.astype(_BF16)
        return a_hi, (a - a_hi.astype(_F32)).astype(_BF16)

    rep = (lane((SSM_GROUP, CHUNK_W)) % SSM_GROUP == sub((SSM_GROUP, CHUNK_W))).astype(_BF16)
    sgn_col = jnp.where(sub((STATE_W, 1)) < SSM_STATE, 1.0, -1.0).astype(_F32)
    sgn_row = jnp.where(lane((1, STATE_W)) < SSM_STATE, -1.0, 1.0).astype(_F32)
    tau_col = sub((N_POW, 1)).astype(_F32)
    blk_l = lane((N_POW, CHUNK_W)) // SSM_GROUP
    tau_s = sub((N_POW, CHUNK_W))
    blk_r = sub((CHUNK_W, N_POW)) // SSM_GROUP
    tau_l = lane((CHUNK_W, N_POW))
    pick_l = lambda e: (tau_s == e).astype(_BF16)
    pick_r = lambda e: (tau_l == e).astype(_BF16)
    tn = (((0,), (0,)), ((), ()))

    strips = []
    for d in range(2):
        pr = prow_ref[d, 0]
        lam_re, lam_im, dt = pr[0:1], pr[1:2], jnp.exp(pr[2:3])
        mag = jnp.exp(tau_col * (lam_re * dt))
        ang = tau_col * (lam_im * dt)
        pw_re = mag * jnp.cos(ang)
        pw_im = mag * jnp.sin(ang)
        pw4 = jnp.concatenate(split2(pw_re) + split2(pw_im), axis=1)

        c_hi, c_lo = split2(ct_ref[d, 0])
        ct = (jnp.dot(c_hi, rep, preferred_element_type=_F32)
              + jnp.dot(c_lo, rep, preferred_element_type=_F32))
        ca = ct * sgn_col
        cb = -pltpu.roll(ct, SSM_STATE, axis=0)

        def cpow(sel):
            o = lax.dot_general(pw4, sel, tn, preferred_element_type=_F32)
            x = o[0:STATE_W] + o[STATE_W:2 * STATE_W]
            y = o[2 * STATE_W:3 * STATE_W] + o[3 * STATE_W:4 * STATE_W]
            return x * ca + y * cb

        nr = pw_re[1:2] - 1.0
        ni = pw_im[1:2]
        den = lam_re * lam_re + lam_im * lam_im
        kr = (nr * lam_re + ni * lam_im) / den
        ki = (ni * lam_re - nr * lam_im) / den
        b1 = bt_ref[d, 0]
        b2 = pltpu.roll(b1, SSM_STATE, axis=1) * sgn_row
        u1 = kr * b1 + ki * b2
        u2 = kr * b2 - ki * b1
        u1t = jnp.concatenate([u1] * CHUNK, axis=0)
        u2t = jnp.concatenate([u2] * CHUNK, axis=0)

        def bpow(sel):
            o = jnp.dot(sel, pw4, preferred_element_type=_F32)
            x = o[:, 0:STATE_W] + o[:, STATE_W:2 * STATE_W]
            y = o[:, 2 * STATE_W:3 * STATE_W] + o[:, 3 * STATE_W:4 * STATE_W]
            return x * u1t + y * u2t

        if d == 0:
            e_strip, e_b, e_c = blk_l, (CHUNK - 1) - blk_r, blk_l + 1
        else:
            e_strip, e_b, e_c = (CHUNK - 1) - blk_l, blk_r, CHUNK - blk_l
        u_hi, u_lo = split2(u1)
        k_hi, k_lo = split2(cpow(pick_l(e_strip)))
        strips.append(jnp.dot(u_hi, k_hi, preferred_element_type=_F32)
                      + jnp.dot(u_hi, k_lo, preferred_element_type=_F32)
                      + jnp.dot(u_lo, k_hi, preferred_element_type=_F32))
        bp = bpow(pick_r(e_b))
        bpow_ref[0, :, d * STATE_W:(d + 1) * STATE_W] = bp.astype(_BF16)
        bpow_ref[0, :, (2 + d) * STATE_W:(3 + d) * STATE_W] = pltpu.roll(bp, SSM_STATE, axis=1).astype(_BF16)
        cpow_ref[0, d * STATE_W:(d + 1) * STATE_W, :] = cpow(pick_l(e_c)).astype(_BF16)

        dec_ref[0, 2 * d:2 * d + 1, :] = pw_re[CHUNK:CHUNK + 1]
        dec_ref[0, 2 * d + 1:2 * d + 2, :] = pw_im[CHUNK:CHUNK + 1] * sgn_row

    zeros = jnp.zeros((SSM_GROUP, CHUNK_W), _F32)
    strip = (jnp.concatenate([strips[1], zeros], axis=1)
             + pltpu.roll(jnp.concatenate([strips[0], zeros], axis=1), CHUNK_W - SSM_GROUP, axis=1))
    for i in range(CHUNK):
        off = (CHUNK - 1 - i) * SSM_GROUP
        win = strip if off == 0 else pltpu.roll(strip, 2 * CHUNK_W - off, axis=1)
        toep_ref[0, i * SSM_GROUP:(i + 1) * SSM_GROUP, :] = win[:, 0:CHUNK_W].astype(_BF16)


def _s5_params(prow, ct2, bt1):
    g_spec = lambda shape: pl.BlockSpec(shape, lambda g: (0, g, 0, 0))
    o_spec = lambda shape: pl.BlockSpec(shape, lambda g: (g, 0, 0))
    return pl.pallas_call(
        _s5_param_kernel,
        grid=(SSM_GROUPS,),
        in_specs=[g_spec((2, 1, 8, STATE_W)),
                  g_spec((2, 1, STATE_W, SSM_GROUP)), g_spec((2, 1, SSM_GROUP, STATE_W))],
        out_specs=[o_spec((1, CHUNK_W, CHUNK_W)), o_spec((1, CHUNK_W, 4 * STATE_W)),
                   o_spec((1, 2 * STATE_W, CHUNK_W)), o_spec((1, 4, STATE_W))],
        out_shape=[jax.ShapeDtypeStruct((SSM_GROUPS, CHUNK_W, CHUNK_W), _BF16),
                   jax.ShapeDtypeStruct((SSM_GROUPS, CHUNK_W, 4 * STATE_W), _BF16),
                   jax.ShapeDtypeStruct((SSM_GROUPS, 2 * STATE_W, CHUNK_W), _BF16),
                   jax.ShapeDtypeStruct((SSM_GROUPS, 4, STATE_W), _F32)],
        compiler_params=_cparams("arbitrary"),
        name="s5_params",
    )(prow, ct2, bt1)


def _s5_kernel(ux_ref, uc_ref, toep_ref, bpow_ref, cpow_ref, dec_ref, y_ref, u_ref, s_ref, h_ref, y2_ref):
    ng = S5_GROUPS_PER_STEP
    gw = SSM_GROUP
    x0 = N_CTX_CHUNKS * BATCH

    half = CHUNK // 2
    assert half == ng
    lanes = ng * 128

    r = lax.broadcasted_iota(jnp.int32, (lanes, 1), 0)
    swapped = ((r // gw) % ng) * 128 + (r // 128) * gw + r % gw
    perm = (lax.broadcasted_iota(jnp.int32, (lanes, lanes), 1) == swapped).astype(_BF16)

    def to_chunk_layout(src_ref, seq_len, n_chunks, row0):
        for hf in range(2):
            p = jnp.concatenate(
                [jnp.concatenate([src_ref[pl.ds(b * seq_len + hf * half + il, n_chunks, stride=CHUNK), :]
                                  for il in range(half)], axis=1) for b in range(BATCH)], axis=0).astype(_BF16)
            q = jnp.dot(p, perm, preferred_element_type=_F32)
            for b in range(BATCH):
                for g in range(ng):
                    u_ref[g, hf, pl.ds(row0 + b, n_chunks, stride=BATCH), :] = (
                        q[b * n_chunks:(b + 1) * n_chunks, g * 128:(g + 1) * 128])

    to_chunk_layout(uc_ref, CTX_LEN, N_CTX_CHUNKS, 0)
    to_chunk_layout(ux_ref, SEQ, N_X_CHUNKS, x0)

    chunk_rows = lambda ref, g, lo: jnp.concatenate([ref[g, 0, lo:, :], ref[g, 1, lo:, :]], axis=1)
    for g in range(ng):
        s_ref[g] = jnp.dot(chunk_rows(u_ref, g, 0).astype(_BF16), bpow_ref[g], preferred_element_type=_F32)

    pair = 2 * BATCH
    n_ctx_pairs = N_CTX_CHUNKS // 2
    n_pairs = N_SEQ_CHUNKS // 2
    lower = lax.broadcasted_iota(jnp.int32, (pair, STATE_W), 0) < BATCH
    zero = jnp.zeros((pair, STATE_W), _F32)

    def step(t, carry):
        tb = jnp.where(t < n_ctx_pairs, n_ctx_pairs - 1 - t, n_pairs + n_ctx_pairs - 1 - t)
        rf = pl.multiple_of(t * pair, pair)
        rb = pl.multiple_of(tb * pair, pair)
        out = []
        swap = lambda v: pltpu.roll(v, BATCH, axis=0)
        for g in range(ng):
            hf, hfs, hb, hbs = carry[4 * g:4 * g + 4]
            dg = dec_ref[g]

            def advance(h, hs, a_re, a_im, s, ss):
                return a_re * h + a_im * hs + s, a_re * hs - a_im * h + ss

            col = lambda r, k: s_ref[g, pl.ds(r, pair), k * STATE_W:(k + 1) * STATE_W]
            sf, sfs = col(rf, 0), col(rf, 2)
            mid, mids = advance(hf, hfs, dg[0:1], dg[1:2], jnp.where(lower, sf, swap(sf)),
                                jnp.where(lower, sfs, swap(sfs)))
            h_ref[g, pl.ds(rf, pair), 0:STATE_W] = jnp.where(lower, hf, mid)
            end, ends = advance(mid, mids, dg[0:1], dg[1:2], sf, sfs)
            hf, hfs = jnp.where(lower, swap(end), end), jnp.where(lower, swap(ends), ends)

            sb, sbs = col(rb, 1), col(rb, 3)
            mid, mids = advance(hb, hbs, dg[2:3], dg[3:4], jnp.where(lower, swap(sb), sb),
                                jnp.where(lower, swap(sbs), sbs))
            h_ref[g, pl.ds(rb, pair), STATE_W:2 * STATE_W] = jnp.where(lower, mid, hb)
            end, ends = advance(mid, mids, dg[2:3], dg[3:4], sb, sbs)
            hb, hbs = jnp.where(lower, end, swap(end)), jnp.where(lower, ends, swap(ends))
            out += [hf, hfs, hb, hbs]
        return tuple(out)

    lax.fori_loop(0, n_pairs, step, (zero,) * (4 * ng))

    for g in range(ng):
        ux = chunk_rows(u_ref, g, x0).astype(_BF16)
        hx = h_ref[g, x0:SCAN_ROWS, :].astype(_BF16)
        y2 = (jnp.dot(ux, toep_ref[g], preferred_element_type=_F32)
              + jnp.dot(hx, cpow_ref[g], preferred_element_type=_F32))
        y2_ref[g, 0] = y2[:, 0:128]
        y2_ref[g, 1] = y2[:, 128:256]

    for hf in range(2):
        yin = jnp.concatenate(
            [jnp.concatenate([y2_ref[g, hf, pl.ds(b, N_X_CHUNKS, stride=BATCH), :] for g in range(ng)], axis=1)
             for b in range(BATCH)], axis=0)
        y_hi = yin.astype(_BF16)
        y_lo = (yin - y_hi.astype(_F32)).astype(_BF16)
        q = (jnp.dot(y_hi, perm, preferred_element_type=_F32)
             + jnp.dot(y_lo, perm, preferred_element_type=_F32))
        for b in range(BATCH):
            for jl in range(half):
                y_ref[pl.ds(b * SEQ + hf * half + jl, N_X_CHUNKS, stride=CHUNK), :] = (
                    q[b * N_X_CHUNKS:(b + 1) * N_X_CHUNKS, jl * 128:(jl + 1) * 128])


def _s5_scan(u_x, u_c, toep, bpow, cpow, dec):
    ng = S5_GROUPS_PER_STEP
    n_x_rows = N_X_CHUNKS * BATCH
    lanes = ng * SSM_GROUP
    spec = lambda r, c: pl.BlockSpec((ng, r, c), lambda i: (i, 0, 0))
    col = lambda rows: pl.BlockSpec((rows, lanes), lambda i: (0, i))
    return pl.pallas_call(
        _s5_kernel,
        grid=(SSM_GROUPS // ng,),
        in_specs=[col(N_TOK), col(N_CTX_TOK), spec(CHUNK_W, CHUNK_W), spec(CHUNK_W, 4 * STATE_W),
                  spec(2 * STATE_W, CHUNK_W), spec(4, STATE_W)],
        out_specs=col(N_TOK),
        out_shape=jax.ShapeDtypeStruct((N_TOK, D_SSM), _F32),
        scratch_shapes=[pltpu.VMEM((ng, 2, SCAN_ROWS, 128), _F32),
                        pltpu.VMEM((ng, SCAN_ROWS, 4 * STATE_W), _F32),
                        pltpu.VMEM((ng, SCAN_ROWS, 2 * STATE_W), _F32),
                        pltpu.VMEM((ng, 2, n_x_rows, 128), _F32)],
        compiler_params=_cparams("arbitrary"),
        name="s5_scan",
    )(u_x, u_c, toep, bpow, cpow, dec)


def _merge_kernel(ys_ref, u_ref, conv_ref, x_ref, mod_ref, dskip_ref, mixg_ref, n2g_ref,
                  wglu_ref, wout_ref, rwt_ref, x1_ref, h2_ref, logit_ref):
    m = mod_ref[0]
    yx = dskip_ref[...] * u_ref[...] + ys_ref[...]
    c0 = math.sqrt(2.0 / math.pi)
    ge = 0.5 * yx * (1.0 + jnp.tanh(c0 * (yx + 0.044715 * (yx * yx * yx))))
    z = jnp.dot(ge.astype(_BF16), wglu_ref[...], preferred_element_type=_F32)
    ssm_y = z[:, 0:D_SSM] * jax.nn.sigmoid(z[:, D_SSM:2 * D_SSM])
    conv_y = conv_ref[...]
    mixg = mixg_ref[...]
    heads_a = (ssm_y * _rms_scale(ssm_y) * mixg[:, 0:D_SSM]).astype(_BF16)
    heads_b = (conv_y * _rms_scale(conv_y) * mixg[:, D_SSM:]).astype(_BF16)
    mix = (jnp.dot(heads_a, wout_ref[0:D_SSM, :], preferred_element_type=_F32)
           + jnp.dot(heads_b, wout_ref[D_SSM:, :], preferred_element_type=_F32))
    x1 = x_ref[...] + m[2:3] * mix
    x1_ref[...] = x1
    h2 = x1 * _rms_scale(x1) * n2g_ref[...] * (1.0 + m[4:5]) + m[3:4]
    h2_ref[...] = h2
    h_hi = h2.astype(_BF16)
    h_lo = (h2 - h_hi.astype(_F32)).astype(_BF16)
    nt = (((1,), (1,)), ((), ()))
    p = lax.dot_general(rwt_ref[...], h_hi, nt, preferred_element_type=_F32)
    q = lax.dot_general(rwt_ref[0:N_EXPERTS, :], h_lo, nt, preferred_element_type=_F32)
    logit_ref[...] = p[0:N_EXPERTS] + p[N_EXPERTS:2 * N_EXPERTS] + q


def _merge(ys, u, conv, x2, mod3, dskip, mixg, n2g, wglu_bf, wout_bf, rwt):
    tiles_per_batch = SEQ // TM
    tok = lambda w: pl.BlockSpec((TM, w), lambda i: (i, 0))
    return pl.pallas_call(
        _merge_kernel,
        grid=(N_TOK // TM,),
        in_specs=[tok(D_SSM), tok(D_SSM), tok(D_CONV), tok(D_MODEL),
                  pl.BlockSpec((1, N_MOD, D_MODEL), lambda i: (i // tiles_per_batch, 0, 0)),
                  _const_spec((1, D_SSM)), _const_spec((1, D_MODEL)), _const_spec((1, D_MODEL)),
                  _const_spec((D_SSM, 2 * D_SSM)), _const_spec((D_MODEL, D_MODEL)),
                  _const_spec((2 * N_EXPERTS, D_MODEL))],
        out_specs=[tok(D_MODEL), tok(D_MODEL),
                   pl.BlockSpec((N_EXPERTS, TM), lambda i: (0, i))],
        out_shape=[jax.ShapeDtypeStruct((N_TOK, D_MODEL), _F32),
                   jax.ShapeDtypeStruct((N_TOK, D_MODEL), _F32),
                   jax.ShapeDtypeStruct((N_EXPERTS, N_TOK), _F32)],
        compiler_params=_cparams("arbitrary"),
        name="merge_heads",
    )(ys, u, conv, x2, mod3, dskip, mixg, n2g, wglu_bf, wout_bf, rwt)


def _route_kernel(logit_ref, bias_ref, eid_ref, w_ref, rank_ref, cnt_ref, carry_ref):
    @pl.when(pl.program_id(0) == 0)
    def _():
        carry_ref[...] = jnp.zeros_like(carry_ref)

    tm = logit_ref.shape[1]
    neg = jnp.float32(-jnp.inf)
    scores = jax.nn.sigmoid(logit_ref[...])
    biased = scores + bias_ref[:, 0:1]
    sub = lax.broadcasted_iota(jnp.int32, (GROUP_SIZE, tm), 0)
    rows = lambda a, g: a[g * GROUP_SIZE:(g + 1) * GROUP_SIZE]
    ngrp = N_EXPERT_GROUPS

    gscore = []
    for g in range(ngrp):
        bg = rows(biased, g)
        m1 = jnp.max(bg, axis=0, keepdims=True)
        first = jnp.min(jnp.where(bg == m1, sub, GROUP_SIZE), axis=0, keepdims=True)
        m2 = jnp.max(jnp.where(sub == first, neg, bg), axis=0, keepdims=True)
        gscore.append(m1 + m2)
    v = []
    for g in range(ngrp):
        beaten = jnp.zeros((1, tm), jnp.int32)
        for o in range(ngrp):
            if o != g:
                beats = (gscore[o] >= gscore[g]) if o < g else (gscore[o] > gscore[g])
                beaten = beaten + beats.astype(jnp.int32)
        v.append(jnp.where(beaten < TOPK_GROUPS, rows(biased, g), neg))
    eids = [sub + g * GROUP_SIZE for g in range(ngrp)]

    pick_ids, pick_masks = [], []
    for _ in range(TOP_K):
        m = functools.reduce(jnp.maximum, v)
        m = jnp.max(m, axis=0, keepdims=True)
        cand = functools.reduce(jnp.minimum, [jnp.where(v[g] == m, eids[g], N_EXPERTS) for g in range(ngrp)])
        pick_id = jnp.min(cand, axis=0, keepdims=True)
        masks = [eids[g] == pick_id for g in range(ngrp)]
        v = [jnp.where(masks[g], neg, v[g]) for g in range(ngrp)]
        pick_ids.append(pick_id)
        pick_masks.append(masks)

    sel = jnp.concatenate(
        [functools.reduce(jnp.logical_or, [pick_masks[k][g] for k in range(TOP_K)]).astype(_F32)
         for g in range(ngrp)], axis=0)
    before = (lax.broadcasted_iota(jnp.int32, (tm, tm), 0)
              < lax.broadcasted_iota(jnp.int32, (tm, tm), 1)).astype(_BF16)
    base = jnp.dot(sel.astype(_BF16), before, preferred_element_type=_F32) + carry_ref[:, 0:1]

    def gather_pick(a, k):
        parts = [jnp.sum(jnp.where(pick_masks[k][g], rows(a, g), 0.0), axis=0, keepdims=True)
                 for g in range(ngrp)]
        return functools.reduce(jnp.add, parts)

    picked = [gather_pick(scores, k) for k in range(TOP_K)]
    denom = functools.reduce(jnp.add, picked)
    for k in range(TOP_K):
        eid_ref[k:k + 1, :] = pick_ids[k]
        w_ref[k:k + 1, :] = picked[k] / denom * ROUTED_SCALE
        rank_ref[k:k + 1, :] = gather_pick(base, k).astype(jnp.int32)
    carry_ref[...] = carry_ref[...] + jnp.sum(sel, axis=1, keepdims=True)
    cnt_ref[...] = carry_ref[...]


def _route(logits_t, bias_col):
    tok = pl.BlockSpec((TOP_K, TM), lambda i: (0, i))
    return pl.pallas_call(
        _route_kernel,
        grid=(N_TOK // TM,),
        in_specs=[pl.BlockSpec((N_EXPERTS, TM), lambda i: (0, i)), _const_spec((N_EXPERTS, 128))],
        out_specs=[tok, tok, tok, pl.BlockSpec((N_EXPERTS, 128), lambda i: (0, 0))],
        out_shape=[jax.ShapeDtypeStruct((TOP_K, N_TOK), jnp.int32),
                   jax.ShapeDtypeStruct((TOP_K, N_TOK), _F32),
                   jax.ShapeDtypeStruct((TOP_K, N_TOK), jnp.int32),
                   jax.ShapeDtypeStruct((N_EXPERTS, 128), _F32)],
        scratch_shapes=[pltpu.VMEM((N_EXPERTS, 128), _F32)],
        compiler_params=_cparams("arbitrary"),
        name="route",
    )(logits_t, bias_col)


def _rowmap_kernel(dest_ref, padstart_ref, padn_ref, h_ref, x1_ref, mod_ref, wsg_ref, wsu_ref, wsd_ref,
                   xmid_ref, codes_ref):
    i = pl.program_id(0)
    tm = h_ref.shape[0]
    base = i * tm

    @pl.when(i == 0)
    def _():
        def fill(e, c):
            def one(r, c2):
                codes_ref[padstart_ref[e] + r] = 0
                return c2
            return lax.fori_loop(0, padn_ref[e], one, c)
        lax.fori_loop(0, N_EXPERTS, fill, 0)

    for t in range(tm):
        ds = [dest_ref[k * N_TOK + base + t] for k in range(TOP_K)]
        for k in range(TOP_K):
            codes_ref[ds[k]] = base + t

    h = h_ref[...].astype(_BF16)
    a = jnp.dot(h, wsg_ref[...], preferred_element_type=_F32)
    b = jnp.dot(h, wsu_ref[...], preferred_element_type=_F32)
    shared = jnp.dot((_silu(a) * b).astype(_BF16), wsd_ref[...], preferred_element_type=_F32)
    xmid_ref[...] = x1_ref[...] + mod_ref[0][5:6] * shared


def _rowmap(dest, pad_start, pad_n, h2, x1, mod3, wsg_bf, wsu_bf, wsd_bf):
    tm = TM
    tiles_per_batch = SEQ // tm
    tok = pl.BlockSpec((tm, D_MODEL), lambda i, *_: (i, 0))
    const = lambda shape: pl.BlockSpec(shape, lambda i, *_: (0,) * len(shape), pipeline_mode=pl.Buffered(1))
    grid_spec = pltpu.PrefetchScalarGridSpec(
        num_scalar_prefetch=3,
        grid=(N_TOK // tm,),
        in_specs=[tok, tok,
                  pl.BlockSpec((1, N_MOD, D_MODEL), lambda i, *_: (i // tiles_per_batch, 0, 0)),
                  const((D_MODEL, D_SHARED)), const((D_MODEL, D_SHARED)), const((D_SHARED, D_MODEL))],
        out_specs=[tok, pl.BlockSpec(memory_space=pltpu.SMEM)],
    )
    return pl.pallas_call(
        _rowmap_kernel,
        grid_spec=grid_spec,
        out_shape=[jax.ShapeDtypeStruct((N_TOK, D_MODEL), _F32),
                   jax.ShapeDtypeStruct((N_ROWS,), jnp.int32)],
        compiler_params=_cparams("arbitrary"),
        name="rowmap",
    )(dest.reshape(N_ASSIGN), pad_start, pad_n, h2, x1, mod3, wsg_bf, wsu_bf, wsd_bf)


def _experts_kernel(be_ref, nact_ref, first_ref, slot_ref, nxt_ref, codes_ref, h_hbm, wg_hbm, wu_hbm, wd_hbm,
                    out_hbm, x0, x1, x2, y0, y1, y2, wg_f, wu_f, wd_f, wg_bf, wu_bf, wd_bf, sem, gsem, ssem):
    n_act = nact_ref[0]
    bm = EXPERT_BM
    depth = EXPERT_DEPTH
    xbuf, ybuf = (x0, x1, x2), (y0, y1, y2)

    def fetch(e, s):
        return (pltpu.make_async_copy(wg_hbm.at[e], wg_f.at[s], sem.at[s, 0]),
                pltpu.make_async_copy(wu_hbm.at[e], wu_f.at[s], sem.at[s, 1]),
                pltpu.make_async_copy(wd_hbm.at[e], wd_f.at[s], sem.at[s, 2]))

    def gather(blk, p):
        for r in range(bm):
            tok = codes_ref[blk * bm + r]
            pltpu.make_async_copy(h_hbm.at[pl.ds(tok, 1), :], xbuf[p].at[pl.ds(r, 1), :],
                                  gsem.at[p]).start(priority=0)

    def gather_wait(p):
        pltpu.make_async_copy(h_hbm.at[pl.ds(0, bm), :], xbuf[p], gsem.at[p]).wait()

    def writeback(blk, p):
        return pltpu.make_async_copy(ybuf[p], out_hbm.at[pl.ds(pl.multiple_of(blk * bm, bm), bm), :], ssem.at[p])

    def block(blk, p):
        nxt1, nxt2 = (p + 1) % depth, (p + 2) % depth

        @pl.when(blk < n_act)
        def _():
            @pl.when(blk == 0)
            def _():
                for cp in fetch(be_ref[0], 0):
                    cp.start(priority=1)
                gather(0, 0)
                gather(jnp.minimum(1, n_act - 1), 1)

            @pl.when(first_ref[blk] == 1)
            def _():
                s = slot_ref[blk]
                for cp in fetch(be_ref[blk], s):
                    cp.wait()

                @pl.when(nxt_ref[blk] >= 0)
                def _():
                    for cp in fetch(nxt_ref[blk], 1 - s):
                        cp.start(priority=1)

                wg_bf[...] = wg_f[s].astype(_BF16)
                wu_bf[...] = wu_f[s].astype(_BF16)
                wd_bf[...] = wd_f[s].astype(_BF16)

            gather_wait(p)

            @pl.when(blk >= depth)
            def _():
                writeback(blk - depth, p).wait()

            gather(jnp.minimum(blk + 2, n_act - 1), nxt2)
            x = xbuf[p][...].astype(_BF16)
            a = jnp.dot(x, wg_bf[...], preferred_element_type=_F32)
            b = jnp.dot(x, wu_bf[...], preferred_element_type=_F32)
            ybuf[p][...] = jnp.dot((_silu(a) * b).astype(_BF16), wd_bf[...], preferred_element_type=_F32)
            writeback(blk, p).start(priority=1)

            @pl.when(blk == n_act - 1)
            def _():
                gather_wait(nxt1)
                gather_wait(nxt2)
                writeback(blk, p).wait()

                @pl.when(blk >= 1)
                def _():
                    writeback(blk - 1, nxt2).wait()

                @pl.when(blk >= 2)
                def _():
                    writeback(blk - 2, nxt1).wait()

    i = pl.program_id(0)
    for q in range(depth):
        block(depth * i + q, q)


def _experts(block_e, n_active, first, slot, nxt, codes, h2, w_gate, w_up, w_down):
    bm = EXPERT_BM
    depth = EXPERT_DEPTH
    hbm = pl.BlockSpec(memory_space=pl.ANY)
    grid_spec = pltpu.PrefetchScalarGridSpec(
        num_scalar_prefetch=6,
        grid=(N_BLOCKS // depth,),
        in_specs=[hbm, hbm, hbm, hbm],
        out_specs=hbm,
        scratch_shapes=[pltpu.VMEM((bm, D_MODEL), _F32)] * (2 * depth) + [
                        pltpu.VMEM((2, D_MODEL, D_EXPERT), _F32), pltpu.VMEM((2, D_MODEL, D_EXPERT), _F32),
                        pltpu.VMEM((2, D_EXPERT, D_MODEL), _F32),
                        pltpu.VMEM((D_MODEL, D_EXPERT), _BF16), pltpu.VMEM((D_MODEL, D_EXPERT), _BF16),
                        pltpu.VMEM((D_EXPERT, D_MODEL), _BF16),
                        pltpu.SemaphoreType.DMA((2, 3)), pltpu.SemaphoreType.DMA((depth,)),
                        pltpu.SemaphoreType.DMA((depth,))],
    )
    return pl.pallas_call(
        _experts_kernel,
        grid_spec=grid_spec,
        out_shape=jax.ShapeDtypeStruct((N_ROWS, D_MODEL), _F32),
        compiler_params=_cparams("arbitrary"),
        name="experts",
    )(block_e, n_active, first, slot, nxt, codes, h2, w_gate, w_up, w_down)


def _final_kernel(dest_ref, xmid_ref, wt_ref, mod_ref, fg_ref, ys_ref, o_ref, buf0, buf1, sem):
    i = pl.program_id(0)
    n_tiles = 2 * pl.num_programs(0)
    tm = TM_COMBINE
    bufs = (buf0, buf1)

    def gather(tile, p):
        base = tile * tm
        for t in range(tm):
            for k in range(TOP_K):
                pltpu.make_async_copy(ys_ref.at[pl.ds(dest_ref[k * N_TOK + base + t], 1), :],
                                      bufs[p].at[pl.ds(k * tm + t, 1), :], sem.at[p]).start(priority=k % 2)

    def drain(p):
        pltpu.make_async_copy(ys_ref.at[pl.ds(0, TOP_K * tm), :], bufs[p], sem.at[p]).wait()

    @pl.when(i == 0)
    def _():
        gather(0, 0)

    g2 = mod_ref[0][5:6]
    for p in range(2):
        tile = 2 * i + p
        drain(p)
        gather(jnp.minimum(tile + 1, n_tiles - 1), 1 - p)
        rows = slice(p * tm, (p + 1) * tm)
        wt = wt_ref[rows, :]
        routed = wt[:, 0:1] * bufs[p][0:tm, :]
        for k in range(1, TOP_K):
            routed = routed + wt[:, k:k + 1] * bufs[p][k * tm:(k + 1) * tm, :]
        y = xmid_ref[rows, :] + g2 * routed
        o_ref[rows, :] = y * _rms_scale(y) * fg_ref[...]

    @pl.when(i == pl.num_programs(0) - 1)
    def _():
        drain(0)


def _final(dest, xmid, w_tok, mod3, final_g, ys):
    tm = 2 * TM_COMBINE
    tiles_per_batch = SEQ // tm
    tok = lambda w: pl.BlockSpec((tm, w), lambda i, *_: (i, 0))
    grid_spec = pltpu.PrefetchScalarGridSpec(
        num_scalar_prefetch=1,
        grid=(N_TOK // tm,),
        in_specs=[tok(D_MODEL), tok(TOP_K),
                  pl.BlockSpec((1, N_MOD, D_MODEL), lambda i, *_: (i // tiles_per_batch, 0, 0)),
                  pl.BlockSpec((1, D_MODEL), lambda i, *_: (0, 0), pipeline_mode=pl.Buffered(1)),
                  pl.BlockSpec(memory_space=pl.ANY)],
        out_specs=tok(D_MODEL),
        scratch_shapes=[pltpu.VMEM((TOP_K * TM_COMBINE, D_MODEL), _F32),
                        pltpu.VMEM((TOP_K * TM_COMBINE, D_MODEL), _F32), pltpu.SemaphoreType.DMA((2,))],
    )
    return pl.pallas_call(
        _final_kernel,
        grid_spec=grid_spec,
        out_shape=jax.ShapeDtypeStruct((N_TOK, D_MODEL), _F32),
        compiler_params=_cparams("arbitrary"),
        name="final",
    )(dest.reshape(N_ASSIGN), xmid, w_tok, mod3, final_g, ys)


def _s5_param_layouts(lam_re, lam_im, b_re, b_im, c_re, c_im, log_dt):
    two = lambda a: jnp.concatenate([a, a], axis=-1)
    lr, li = two(lam_re), two(lam_im)
    dtb = jnp.broadcast_to(log_dt.astype(_F32)[:, :, None], lr.shape)
    zeros = jnp.zeros_like(lr)
    prow = jnp.stack([lr, li, dtb] + [zeros] * 5, axis=2)
    ct2 = jnp.concatenate([jnp.swapaxes(c_re, -1, -2), jnp.swapaxes(c_im, -1, -2)], axis=2)
    bt1 = jnp.concatenate([jnp.swapaxes(b_re, -1, -2), jnp.swapaxes(b_im, -1, -2)], axis=3)
    return prow, ct2, bt1


def kernel(x, c, ctx, c_ctx, norm1_g, norm2_g, w_ada, b_ada, w_in, ssm_lam_re, ssm_lam_im, ssm_b_re, ssm_b_im, ssm_c_re, ssm_c_im, ssm_log_dt, ssm_d, ssm_w_glu, conv_w, conv_b, mix_norm_g, w_out, router_w, router_bias, exp_w_gate, exp_w_up, exp_w_down, shared_w_gate, shared_w_up, shared_w_down, final_g):
    layer = 0
    x2 = x.reshape(N_TOK, D_MODEL)
    ctx2 = ctx.reshape(N_CTX_TOK, D_MODEL)

    c8 = jnp.concatenate([c, c_ctx[None, :], jnp.zeros((8 - BATCH - 1, D_MODEL), _F32)], axis=0)
    mod = _ada_mod(c8, w_ada[layer], b_ada[layer][None, :])
    mod3 = mod.reshape(8, N_MOD, D_MODEL)

    w_in_bf = w_in[layer].astype(_BF16)
    conv_w8 = jnp.concatenate([conv_w[layer], jnp.zeros((8 - conv_w.shape[1], D_CONV), _F32)], axis=0)
    u_x, conv_x = _inproj(x2, mod3, norm1_g[layer][None, :], w_in_bf, conv_w8, conv_b[layer][None, :])
    u_c = _ctxproj(ctx2, mod3, norm1_g[layer][None, :], w_in_bf[:, :D_SSM])

    prow, ct2, bt1 = _s5_param_layouts(ssm_lam_re[layer], ssm_lam_im[layer], ssm_b_re[layer],
                                       ssm_b_im[layer], ssm_c_re[layer], ssm_c_im[layer], ssm_log_dt[layer])
    toep, bpow, cpow, dec = _s5_params(prow, ct2, bt1)
    ys = _s5_scan(u_x, u_c, toep, bpow, cpow, dec)

    rw = router_w[layer]
    rw_hi = rw.astype(_BF16)
    rw_lo = (rw - rw_hi.astype(_F32)).astype(_BF16)
    rwt = jnp.concatenate([rw_hi.T, rw_lo.T], axis=0)
    x1, h2, logits_t = _merge(ys, u_x, conv_x, x2, mod3, ssm_d[layer][None, :],
                              mix_norm_g[layer][None, :], norm2_g[layer][None, :],
                              ssm_w_glu[layer].astype(_BF16), w_out[layer].astype(_BF16), rwt)

    bias_col = jnp.broadcast_to(router_bias[layer][:, None], (N_EXPERTS, 128))
    eid, w_k, rank, cnt = _route(logits_t, bias_col)

    counts = cnt[:, 0].astype(jnp.int32)
    padded = (counts + EXPERT_BM - 1) // EXPERT_BM * EXPERT_BM
    pend = jnp.cumsum(padded)
    pstart = pend - padded
    is_e = eid[:, :, None] == jnp.arange(N_EXPERTS, dtype=jnp.int32)
    dest = jnp.sum(jnp.where(is_e, pstart, 0), axis=-1) + rank
    n_active = (pend[-1] // EXPERT_BM).astype(jnp.int32)
    blk = jnp.minimum(jnp.arange(N_BLOCKS, dtype=jnp.int32), n_active - 1)
    ends_before = (pend[None, :] <= (blk * EXPERT_BM)[:, None]).astype(jnp.int32)
    block_e = jnp.minimum(jnp.sum(ends_before, axis=1), N_EXPERTS - 1)

    xmid, codes = _rowmap(dest, pstart + counts, padded - counts, h2, x1, mod3,
                          shared_w_gate[layer].astype(_BF16), shared_w_up[layer].astype(_BF16),
                          shared_w_down[layer].astype(_BF16))
    first = jnp.concatenate([jnp.ones((1,), jnp.int32), (block_e[1:] != block_e[:-1]).astype(jnp.int32)])
    slot = (jnp.cumsum(first) - 1) % 2
    e_ids = jnp.arange(N_EXPERTS, dtype=jnp.int32)
    owner = jnp.where(padded > 0, e_ids, N_EXPERTS)
    later = jnp.min(jnp.where(e_ids[None, :] > e_ids[:, None], owner[None, :], N_EXPERTS), axis=1)
    nxt_e = jnp.where(later == N_EXPERTS, -1, later)
    nxt = jnp.sum(jnp.where(block_e[:, None] == e_ids[None, :], nxt_e[None, :], 0), axis=1)
    ys_rows = _experts(block_e, n_active[None], first, slot, nxt, codes, h2,
                       exp_w_gate[layer], exp_w_up[layer], exp_w_down[layer])
    out = _final(dest, xmid, w_k.T, mod3, final_g[None, :], ys_rows)
    return out.reshape(BATCH, SEQ, D_MODEL)
```

```python
import functools
import math

import jax
import jax.numpy as jnp
from jax import lax
from jax.experimental import pallas as pl
from jax.experimental.pallas import tpu as pltpu

D_MODEL = 2048
BATCH = 4
SEQ = 2048
CTX_LEN = 256
GRID_W = 64
D_SSM = 1024
D_CONV = 1024
SSM_GROUP = 16
SSM_GROUPS = 64
SSM_STATE = 64
N_EXPERTS = 64
N_EXPERT_GROUPS = 8
GROUP_SIZE = N_EXPERTS // N_EXPERT_GROUPS
TOPK_GROUPS = 4
TOP_K = 8
D_EXPERT = 512
D_SHARED = 512
ROUTED_SCALE = 2.5
N_MOD = 6
EPS = 1e-6

N_TOK = BATCH * SEQ
N_CTX_TOK = BATCH * CTX_LEN

CHUNK = 16
CHUNK_W = CHUNK * SSM_GROUP
STATE_W = 2 * SSM_STATE
N_CTX_CHUNKS = CTX_LEN // CHUNK
N_X_CHUNKS = SEQ // CHUNK
N_SEQ_CHUNKS = N_CTX_CHUNKS + N_X_CHUNKS
SCAN_ROWS = N_SEQ_CHUNKS * BATCH
S5_GROUPS_PER_STEP = 8

TM = 256
EXPERT_BM = 256
N_ASSIGN = N_TOK * TOP_K
EXPERT_DEPTH = 3
N_BLOCKS = -(-(N_ASSIGN // EXPERT_BM + N_EXPERTS) // EXPERT_DEPTH) * EXPERT_DEPTH
N_ROWS = N_BLOCKS * EXPERT_BM
TM_COMBINE = 128
VMEM_LIMIT = 56 * 1024 * 1024

_F32 = jnp.float32
_BF16 = jnp.bfloat16


def _cparams(*sem):
    return pltpu.CompilerParams(dimension_semantics=sem, vmem_limit_bytes=VMEM_LIMIT)


def _const_spec(shape):
    nd = len(shape)
    return pl.BlockSpec(shape, lambda *_: (0,) * nd, pipeline_mode=pl.Buffered(1))


def _rms_scale(xf):
    return lax.rsqrt(jnp.mean(xf * xf, axis=-1, keepdims=True) + EPS)


def _silu(x):
    return x * jax.nn.sigmoid(x)


def _ada_kernel(c_ref, w_ref, b_ref, o_ref):
    s = _silu(c_ref[...])
    o_ref[...] = jnp.dot(s, w_ref[...], preferred_element_type=_F32) + b_ref[...]


def _ada_mod(c8, w_ada, b_ada):
    n = w_ada.shape[1]
    tn = 1024
    return pl.pallas_call(
        _ada_kernel,
        grid=(n // tn,),
        in_specs=[pl.BlockSpec((8, D_MODEL), lambda j: (0, 0)),
                  pl.BlockSpec((D_MODEL, tn), lambda j: (0, j)),
                  pl.BlockSpec((1, tn), lambda j: (0, j))],
        out_specs=pl.BlockSpec((8, tn), lambda j: (0, j)),
        out_shape=jax.ShapeDtypeStruct((8, n), _F32),
        compiler_params=_cparams("arbitrary"),
        name="ada_mod",
    )(c8, w_ada, b_ada)


def _modulated_norm(x_ref, mod_ref, g_ref, shift_row, scale_row):
    xf = x_ref[...]
    m = mod_ref[0]
    h = xf * _rms_scale(xf) * g_ref[...]
    return h * (1.0 + m[scale_row:scale_row + 1]) + m[shift_row:shift_row + 1]


def _inproj_kernel(x_ref, mod_ref, g_ref, w_ref, cw_ref, cb_ref, u_ref, conv_ref):
    h = _modulated_norm(x_ref, mod_ref, g_ref, 0, 1).astype(_BF16)
    u_ref[...] = jnp.dot(h, w_ref[:, 0:D_SSM], preferred_element_type=_F32)
    tm = x_ref.shape[0]
    pos = lax.broadcasted_iota(jnp.int32, (tm, 1), 0) % GRID_W
    not_first = (pos != 0).astype(_F32)
    not_last = (pos != GRID_W - 1).astype(_F32)
    cw = cw_ref[...]
    nc = 256
    for j in range(D_CONV // nc):
        lo = j * nc
        bg = jnp.dot(h, w_ref[:, D_SSM + lo:D_SSM + lo + nc], preferred_element_type=_F32)
        cg = jnp.dot(h, w_ref[:, D_SSM + D_CONV + lo:D_SSM + D_CONV + lo + nc], preferred_element_type=_F32)
        v = jnp.dot(h, w_ref[:, D_SSM + 2 * D_CONV + lo:D_SSM + 2 * D_CONV + lo + nc],
                    preferred_element_type=_F32)
        z = cg * v
        z_prev = pltpu.roll(z, 1, axis=0) * not_first
        z_next = pltpu.roll(z, tm - 1, axis=0) * not_last
        y = (cb_ref[:, lo:lo + nc] + z_prev * cw[0:1, lo:lo + nc] + z * cw[1:2, lo:lo + nc]
             + z_next * cw[2:3, lo:lo + nc])
        conv_ref[:, lo:lo + nc] = bg * y


def _inproj(x2, mod3, norm_g, w_in_bf, conv_w, conv_b):
    d_in = w_in_bf.shape[1]
    tiles_per_batch = SEQ // TM
    return pl.pallas_call(
        _inproj_kernel,
        grid=(N_TOK // TM,),
        in_specs=[pl.BlockSpec((TM, D_MODEL), lambda i: (i, 0)),
                  pl.BlockSpec((1, N_MOD, D_MODEL), lambda i: (i // tiles_per_batch, 0, 0)),
                  _const_spec((1, D_MODEL)),
                  _const_spec((D_MODEL, d_in)),
                  _const_spec((8, D_CONV)),
                  _const_spec((1, D_CONV))],
        out_specs=[pl.BlockSpec((TM, D_SSM), lambda i: (i, 0)),
                   pl.BlockSpec((TM, D_CONV), lambda i: (i, 0))],
        out_shape=[jax.ShapeDtypeStruct((N_TOK, D_SSM), _F32),
                   jax.ShapeDtypeStruct((N_TOK, D_CONV), _F32)],
        compiler_params=_cparams("arbitrary"),
        name="in_proj",
    )(x2, mod3, norm_g, w_in_bf, conv_w, conv_b)


def _ctxproj_kernel(x_ref, mod_ref, g_ref, w_ref, u_ref):
    h = _modulated_norm(x_ref, mod_ref, g_ref, 0, 1).astype(_BF16)
    u_ref[...] = jnp.dot(h, w_ref[...], preferred_element_type=_F32)


def _ctxproj(ctx2, mod3, norm_g, w_u_bf):
    return pl.pallas_call(
        _ctxproj_kernel,
        grid=(N_CTX_TOK // TM,),
        in_specs=[pl.BlockSpec((TM, D_MODEL), lambda i: (i, 0)),
                  pl.BlockSpec((1, N_MOD, D_MODEL), lambda i: (BATCH, 0, 0)),
                  _const_spec((1, D_MODEL)),
                  _const_spec((D_MODEL, D_SSM))],
        out_specs=pl.BlockSpec((TM, D_SSM), lambda i: (i, 0)),
        out_shape=jax.ShapeDtypeStruct((N_CTX_TOK, D_SSM), _F32),
        compiler_params=_cparams("arbitrary"),
        name="ctx_proj",
    )(ctx2, mod3, norm_g, w_u_bf)


N_POW = 32


def _s5_param_kernel(prow_ref, ct_ref, bt_ref, toep_ref, bpow_ref, cpow_ref, dec_ref):
    lane = lambda shape: lax.broadcasted_iota(jnp.int32, shape, 1)
    sub = lambda shape: lax.broadcasted_iota(jnp.int32, shape, 0)

    def split2(a):
        a_hi = a.astype(_BF16)
        return a_hi, (a - a_hi.astype(_F32)).astype(_BF16)

    rep = (lane((SSM_GROUP, CHUNK_W)) % SSM_GROUP == sub((SSM_GROUP, CHUNK_W))).astype(_BF16)
    sgn_col = jnp.where(sub((STATE_W, 1)) < SSM_STATE, 1.0, -1.0).astype(_F32)
    sgn_row = jnp.where(lane((1, STATE_W)) < SSM_STATE, -1.0, 1.0).astype(_F32)
    tau_col = sub((N_POW, 1)).astype(_F32)
    blk_l = lane((N_POW, CHUNK_W)) // SSM_GROUP
    tau_s = sub((N_POW, CHUNK_W))
    blk_r = sub((CHUNK_W, N_POW)) // SSM_GROUP
    tau_l = lane((CHUNK_W, N_POW))
    pick_l = lambda e: (tau_s == e).astype(_BF16)
    pick_r = lambda e: (tau_l == e).astype(_BF16)
    tn = (((0,), (0,)), ((), ()))

    strips = []
    for d in range(2):
        pr = prow_ref[d, 0]
        lam_re, lam_im, dt = pr[0:1], pr[1:2], jnp.exp(pr[2:3])
        mag = jnp.exp(tau_col * (lam_re * dt))
        ang = tau_col * (lam_im * dt)
        pw_re = mag * jnp.cos(ang)
        pw_im = mag * jnp.sin(ang)
        pw4 = jnp.concatenate(split2(pw_re) + split2(pw_im), axis=1)

        c_hi, c_lo = split2(ct_ref[d, 0])
        ct = (jnp.dot(c_hi, rep, preferred_element_type=_F32)
              + jnp.dot(c_lo, rep, preferred_element_type=_F32))
        ca = ct * sgn_col
        cb = -pltpu.roll(ct, SSM_STATE, axis=0)

        def cpow(sel):
            o = lax.dot_general(pw4, sel, tn, preferred_element_type=_F32)
            x = o[0:STATE_W] + o[STATE_W:2 * STATE_W]
            y = o[2 * STATE_W:3 * STATE_W] + o[3 * STATE_W:4 * STATE_W]
            return x * ca + y * cb

        nr = pw_re[1:2] - 1.0
        ni = pw_im[1:2]
        den = lam_re * lam_re + lam_im * lam_im
        kr = (nr * lam_re + ni * lam_im) / den
        ki = (ni * lam_re - nr * lam_im) / den
        b1 = bt_ref[d, 0]
        b2 = pltpu.roll(b1, SSM_STATE, axis=1) * sgn_row
        u1 = kr * b1 + ki * b2
        u2 = kr * b2 - ki * b1
        u1t = jnp.concatenate([u1] * CHUNK, axis=0)
        u2t = jnp.concatenate([u2] * CHUNK, axis=0)

        def bpow(sel):
            o = jnp.dot(sel, pw4, preferred_element_type=_F32)
            x = o[:, 0:STATE_W] + o[:, STATE_W:2 * STATE_W]
            y = o[:, 2 * STATE_W:3 * STATE_W] + o[:, 3 * STATE_W:4 * STATE_W]
            return x * u1t + y * u2t

        if d == 0:
            e_strip, e_b, e_c = blk_l, (CHUNK - 1) - blk_r, blk_l + 1
        else:
            e_strip, e_b, e_c = (CHUNK - 1) - blk_l, blk_r, CHUNK - blk_l
        u_hi, u_lo = split2(u1)
        k_hi, k_lo = split2(cpow(pick_l(e_strip)))
        strips.append(jnp.dot(u_hi, k_hi, preferred_element_type=_F32)
                      + jnp.dot(u_hi, k_lo, preferred_element_type=_F32)
                      + jnp.dot(u_lo, k_hi, preferred_element_type=_F32))
        bp = bpow(pick_r(e_b))
        bpow_ref[0, :, d * STATE_W:(d + 1) * STATE_W] = bp.astype(_BF16)
        bpow_ref[0, :, (2 + d) * STATE_W:(3 + d) * STATE_W] = pltpu.roll(bp, SSM_STATE, axis=1).astype(_BF16)
        cpow_ref[0, d * STATE_W:(d + 1) * STATE_W, :] = cpow(pick_l(e_c)).astype(_BF16)

        dec_ref[0, 2 * d:2 * d + 1, :] = pw_re[CHUNK:CHUNK + 1]
        dec_ref[0, 2 * d + 1:2 * d + 2, :] = pw_im[CHUNK:CHUNK + 1] * sgn_row

    zeros = jnp.zeros((SSM_GROUP, CHUNK_W), _F32)
    strip = (jnp.concatenate([strips[1], zeros], axis=1)
             + pltpu.roll(jnp.concatenate([strips[0], zeros], axis=1), CHUNK_W - SSM_GROUP, axis=1))
    for i in range(CHUNK):
        off = (CHUNK - 1 - i) * SSM_GROUP
        win = strip if off == 0 else pltpu.roll(strip, 2 * CHUNK_W - off, axis=1)
        toep_ref[0, i * SSM_GROUP:(i + 1) * SSM_GROUP, :] = win[:, 0:CHUNK_W].astype(_BF16)


def _s5_params(prow, ct2, bt1):
    g_spec = lambda shape: pl.BlockSpec(shape, lambda g: (0, g, 0, 0))
    o_spec = lambda shape: pl.BlockSpec(shape, lambda g: (g, 0, 0))
    return pl.pallas_call(
        _s5_param_kernel,
        grid=(SSM_GROUPS,),
        in_specs=[g_spec((2, 1, 8, STATE_W)),
                  g_spec((2, 1, STATE_W, SSM_GROUP)), g_spec((2, 1, SSM_GROUP, STATE_W))],
        out_specs=[o_spec((1, CHUNK_W, CHUNK_W)), o_spec((1, CHUNK_W, 4 * STATE_W)),
                   o_spec((1, 2 * STATE_W, CHUNK_W)), o_spec((1, 4, STATE_W))],
        out_shape=[jax.ShapeDtypeStruct((SSM_GROUPS, CHUNK_W, CHUNK_W), _BF16),
                   jax.ShapeDtypeStruct((SSM_GROUPS, CHUNK_W, 4 * STATE_W), _BF16),
                   jax.ShapeDtypeStruct((SSM_GROUPS, 2 * STATE_W, CHUNK_W), _BF16),
                   jax.ShapeDtypeStruct((SSM_GROUPS, 4, STATE_W), _F32)],
        compiler_params=_cparams("arbitrary"),
        name="s5_params",
    )(prow, ct2, bt1)


def _s5_kernel(ux_ref, uc_ref, toep_ref, bpow_ref, cpow_ref, dec_ref, y_ref, u_ref, s_ref, h_ref, y2_ref):
    ng = S5_GROUPS_PER_STEP
    gw = SSM_GROUP
    x0 = N_CTX_CHUNKS * BATCH

    half = CHUNK // 2
    assert half == ng
    lanes = ng * 128

    r = lax.broadcasted_iota(jnp.int32, (lanes, 1), 0)
    swapped = ((r // gw) % ng) * 128 + (r // 128) * gw + r % gw
    perm = (lax.broadcasted_iota(jnp.int32, (lanes, lanes), 1) == swapped).astype(_BF16)

    def to_chunk_layout(src_ref, seq_len, n_chunks, row0):
        for hf in range(2):
            p = jnp.concatenate(
                [jnp.concatenate([src_ref[pl.ds(b * seq_len + hf * half + il, n_chunks, stride=CHUNK), :]
                                  for il in range(half)], axis=1) for b in range(BATCH)], axis=0).astype(_BF16)
            q = jnp.dot(p, perm, preferred_element_type=_F32)
            for b in range(BATCH):
                for g in range(ng):
                    u_ref[g, hf, pl.ds(row0 + b, n_chunks, stride=BATCH), :] = (
                        q[b * n_chunks:(b + 1) * n_chunks, g * 128:(g + 1) * 128])

    to_chunk_layout(uc_ref, CTX_LEN, N_CTX_CHUNKS, 0)
    to_chunk_layout(ux_ref, SEQ, N_X_CHUNKS, x0)

    chunk_rows = lambda ref, g, lo: jnp.concatenate([ref[g, 0, lo:, :], ref[g, 1, lo:, :]], axis=1)
    for g in range(ng):
        s_ref[g] = jnp.dot(chunk_rows(u_ref, g, 0).astype(_BF16), bpow_ref[g], preferred_element_type=_F32)

    pair = 2 * BATCH
    n_ctx_pairs = N_CTX_CHUNKS // 2
    n_pairs = N_SEQ_CHUNKS // 2
    lower = lax.broadcasted_iota(jnp.int32, (pair, STATE_W), 0) < BATCH
    zero = jnp.zeros((pair, STATE_W), _F32)

    def step(t, carry):
        tb = jnp.where(t < n_ctx_pairs, n_ctx_pairs - 1 - t, n_pairs + n_ctx_pairs - 1 - t)
        rf = pl.multiple_of(t * pair, pair)
        rb = pl.multiple_of(tb * pair, pair)
        out = []
        swap = lambda v: pltpu.roll(v, BATCH, axis=0)
        for g in range(ng):
            hf, hfs, hb, hbs = carry[4 * g:4 * g + 4]
            dg = dec_ref[g]

            def advance(h, hs, a_re, a_im, s, ss):
                return a_re * h + a_im * hs + s, a_re * hs - a_im * h + ss

            col = lambda r, k: s_ref[g, pl.ds(r, pair), k * STATE_W:(k + 1) * STATE_W]
            sf, sfs = col(rf, 0), col(rf, 2)
            mid, mids = advance(hf, hfs, dg[0:1], dg[1:2], jnp.where(lower, sf, swap(sf)),
                                jnp.where(lower, sfs, swap(sfs)))
            h_ref[g, pl.ds(rf, pair), 0:STATE_W] = jnp.where(lower, hf, mid)
            end, ends = advance(mid, mids, dg[0:1], dg[1:2], sf, sfs)
            hf, hfs = jnp.where(lower, swap(end), end), jnp.where(lower, swap(ends), ends)

            sb, sbs = col(rb, 1), col(rb, 3)
            mid, mids = advance(hb, hbs, dg[2:3], dg[3:4], jnp.where(lower, swap(sb), sb),
                                jnp.where(lower, swap(sbs), sbs))
            h_ref[g, pl.ds(rb, pair), STATE_W:2 * STATE_W] = jnp.where(lower, mid, hb)
            end, ends = advance(mid, mids, dg[2:3], dg[3:4], sb, sbs)
            hb, hbs = jnp.where(lower, end, swap(end)), jnp.where(lower, ends, swap(ends))
            out += [hf, hfs, hb, hbs]
        return tuple(out)

    lax.fori_loop(0, n_pairs, step, (zero,) * (4 * ng))

    for g in range(ng):
        ux = chunk_rows(u_ref, g, x0).astype(_BF16)
        hx = h_ref[g, x0:SCAN_ROWS, :].astype(_BF16)
        y2 = (jnp.dot(ux, toep_ref[g], preferred_element_type=_F32)
              + jnp.dot(hx, cpow_ref[g], preferred_element_type=_F32))
        y2_ref[g, 0] = y2[:, 0:128]
        y2_ref[g, 1] = y2[:, 128:256]

    for hf in range(2):
        yin = jnp.concatenate(
            [jnp.concatenate([y2_ref[g, hf, pl.ds(b, N_X_CHUNKS, stride=BATCH), :] for g in range(ng)], axis=1)
             for b in range(BATCH)], axis=0)
        y_hi = yin.astype(_BF16)
        y_lo = (yin - y_hi.astype(_F32)).astype(_BF16)
        q = (jnp.dot(y_hi, perm, preferred_element_type=_F32)
             + jnp.dot(y_lo, perm, preferred_element_type=_F32))
        for b in range(BATCH):
            for jl in range(half):
                y_ref[pl.ds(b * SEQ + hf * half + jl, N_X_CHUNKS, stride=CHUNK), :] = (
                    q[b * N_X_CHUNKS:(b + 1) * N_X_CHUNKS, jl * 128:(jl + 1) * 128])


def _s5_scan(u_x, u_c, toep, bpow, cpow, dec):
    ng = S5_GROUPS_PER_STEP
    n_x_rows = N_X_CHUNKS * BATCH
    lanes = ng * SSM_GROUP
    spec = lambda r, c: pl.BlockSpec((ng, r, c), lambda i: (i, 0, 0))
    col = lambda rows: pl.BlockSpec((rows, lanes), lambda i: (0, i))
    return pl.pallas_call(
        _s5_kernel,
        grid=(SSM_GROUPS // ng,),
        in_specs=[col(N_TOK), col(N_CTX_TOK), spec(CHUNK_W, CHUNK_W), spec(CHUNK_W, 4 * STATE_W),
                  spec(2 * STATE_W, CHUNK_W), spec(4, STATE_W)],
        out_specs=col(N_TOK),
        out_shape=jax.ShapeDtypeStruct((N_TOK, D_SSM), _F32),
        scratch_shapes=[pltpu.VMEM((ng, 2, SCAN_ROWS, 128), _F32),
                        pltpu.VMEM((ng, SCAN_ROWS, 4 * STATE_W), _F32),
                        pltpu.VMEM((ng, SCAN_ROWS, 2 * STATE_W), _F32),
                        pltpu.VMEM((ng, 2, n_x_rows, 128), _F32)],
        compiler_params=_cparams("arbitrary"),
        name="s5_scan",
    )(u_x, u_c, toep, bpow, cpow, dec)


def _merge_kernel(ys_ref, u_ref, conv_ref, x_ref, mod_ref, dskip_ref, mixg_ref, n2g_ref,
                  wglu_ref, wout_ref, rwt_ref, x1_ref, h2_ref, logit_ref):
    m = mod_ref[0]
    yx = dskip_ref[...] * u_ref[...] + ys_ref[...]
    c0 = math.sqrt(2.0 / math.pi)
    ge = 0.5 * yx * (1.0 + jnp.tanh(c0 * (yx + 0.044715 * (yx * yx * yx))))
    z = jnp.dot(ge.astype(_BF16), wglu_ref[...], preferred_element_type=_F32)
    ssm_y = z[:, 0:D_SSM] * jax.nn.sigmoid(z[:, D_SSM:2 * D_SSM])
    conv_y = conv_ref[...]
    mixg = mixg_ref[...]
    heads_a = (ssm_y * _rms_scale(ssm_y) * mixg[:, 0:D_SSM]).astype(_BF16)
    heads_b = (conv_y * _rms_scale(conv_y) * mixg[:, D_SSM:]).astype(_BF16)
    mix = (jnp.dot(heads_a, wout_ref[0:D_SSM, :], preferred_element_type=_F32)
           + jnp.dot(heads_b, wout_ref[D_SSM:, :], preferred_element_type=_F32))
    x1 = x_ref[...] + m[2:3] * mix
    x1_ref[...] = x1
    h2 = x1 * _rms_scale(x1) * n2g_ref[...] * (1.0 + m[4:5]) + m[3:4]
    h2_ref[...] = h2
    h_hi = h2.astype(_BF16)
    h_lo = (h2 - h_hi.astype(_F32)).astype(_BF16)
    nt = (((1,), (1,)), ((), ()))
    p = lax.dot_general(rwt_ref[...], h_hi, nt, preferred_element_type=_F32)
    q = lax.dot_general(rwt_ref[0:N_EXPERTS, :], h_lo, nt, preferred_element_type=_F32)
    logit_ref[...] = p[0:N_EXPERTS] + p[N_EXPERTS:2 * N_EXPERTS] + q


def _merge(ys, u, conv, x2, mod3, dskip, mixg, n2g, wglu_bf, wout_bf, rwt):
    tiles_per_batch = SEQ // TM
    tok = lambda w: pl.BlockSpec((TM, w), lambda i: (i, 0))
    return pl.pallas_call(
        _merge_kernel,
        grid=(N_TOK // TM,),
        in_specs=[tok(D_SSM), tok(D_SSM), tok(D_CONV), tok(D_MODEL),
                  pl.BlockSpec((1, N_MOD, D_MODEL), lambda i: (i // tiles_per_batch, 0, 0)),
                  _const_spec((1, D_SSM)), _const_spec((1, D_MODEL)), _const_spec((1, D_MODEL)),
                  _const_spec((D_SSM, 2 * D_SSM)), _const_spec((D_MODEL, D_MODEL)),
                  _const_spec((2 * N_EXPERTS, D_MODEL))],
        out_specs=[tok(D_MODEL), tok(D_MODEL),
                   pl.BlockSpec((N_EXPERTS, TM), lambda i: (0, i))],
        out_shape=[jax.ShapeDtypeStruct((N_TOK, D_MODEL), _F32),
                   jax.ShapeDtypeStruct((N_TOK, D_MODEL), _F32),
                   jax.ShapeDtypeStruct((N_EXPERTS, N_TOK), _F32)],
        compiler_params=_cparams("arbitrary"),
        name="merge_heads",
    )(ys, u, conv, x2, mod3, dskip, mixg, n2g, wglu_bf, wout_bf, rwt)


def _route_kernel(logit_ref, bias_ref, eid_ref, w_ref, rank_ref, cnt_ref, carry_ref):
    @pl.when(pl.program_id(0) == 0)
    def _():
        carry_ref[...] = jnp.zeros_like(carry_ref)

    tm = logit_ref.shape[1]
    neg = jnp.float32(-jnp.inf)
    scores = jax.nn.sigmoid(logit_ref[...])
    biased = scores + bias_ref[:, 0:1]
    sub = lax.broadcasted_iota(jnp.int32, (GROUP_SIZE, tm), 0)
    rows = lambda a, g: a[g * GROUP_SIZE:(g + 1) * GROUP_SIZE]
    ngrp = N_EXPERT_GROUPS

    gscore = []
    for g in range(ngrp):
        bg = rows(biased, g)
        m1 = jnp.max(bg, axis=0, keepdims=True)
        first = jnp.min(jnp.where(bg == m1, sub, GROUP_SIZE), axis=0, keepdims=True)
        m2 = jnp.max(jnp.where(sub == first, neg, bg), axis=0, keepdims=True)
        gscore.append(m1 + m2)
    v = []
    for g in range(ngrp):
        beaten = jnp.zeros((1, tm), jnp.int32)
        for o in range(ngrp):
            if o != g:
                beats = (gscore[o] >= gscore[g]) if o < g else (gscore[o] > gscore[g])
                beaten = beaten + beats.astype(jnp.int32)
        v.append(jnp.where(beaten < TOPK_GROUPS, rows(biased, g), neg))
    eids = [sub + g * GROUP_SIZE for g in range(ngrp)]

    pick_ids, pick_masks = [], []
    for _ in range(TOP_K):
        m = functools.reduce(jnp.maximum, v)
        m = jnp.max(m, axis=0, keepdims=True)
        cand = functools.reduce(jnp.minimum, [jnp.where(v[g] == m, eids[g], N_EXPERTS) for g in range(ngrp)])
        pick_id = jnp.min(cand, axis=0, keepdims=True)
        masks = [eids[g] == pick_id for g in range(ngrp)]
        v = [jnp.where(masks[g], neg, v[g]) for g in range(ngrp)]
        pick_ids.append(pick_id)
        pick_masks.append(masks)

    sel = jnp.concatenate(
        [functools.reduce(jnp.logical_or, [pick_masks[k][g] for k in range(TOP_K)]).astype(_F32)
         for g in range(ngrp)], axis=0)
    before = (lax.broadcasted_iota(jnp.int32, (tm, tm), 0)
              < lax.broadcasted_iota(jnp.int32, (tm, tm), 1)).astype(_BF16)
    base = jnp.dot(sel.astype(_BF16), before, preferred_element_type=_F32) + carry_ref[:, 0:1]

    def gather_pick(a, k):
        parts = [jnp.sum(jnp.where(pick_masks[k][g], rows(a, g), 0.0), axis=0, keepdims=True)
                 for g in range(ngrp)]
        return functools.reduce(jnp.add, parts)

    picked = [gather_pick(scores, k) for k in range(TOP_K)]
    denom = functools.reduce(jnp.add, picked)
    for k in range(TOP_K):
        eid_ref[k:k + 1, :] = pick_ids[k]
        w_ref[k:k + 1, :] = picked[k] / denom * ROUTED_SCALE
        rank_ref[k:k + 1, :] = gather_pick(base, k).astype(jnp.int32)
    carry_ref[...] = carry_ref[...] + jnp.sum(sel, axis=1, keepdims=True)
    cnt_ref[...] = carry_ref[...]


def _route(logits_t, bias_col):
    tok = pl.BlockSpec((TOP_K, TM), lambda i: (0, i))
    return pl.pallas_call(
        _route_kernel,
        grid=(N_TOK // TM,),
        in_specs=[pl.BlockSpec((N_EXPERTS, TM), lambda i: (0, i)), _const_spec((N_EXPERTS, 128))],
        out_specs=[tok, tok, tok, pl.BlockSpec((N_EXPERTS, 128), lambda i: (0, 0))],
        out_shape=[jax.ShapeDtypeStruct((TOP_K, N_TOK), jnp.int32),
                   jax.ShapeDtypeStruct((TOP_K, N_TOK), _F32),
                   jax.ShapeDtypeStruct((TOP_K, N_TOK), jnp.int32),
                   jax.ShapeDtypeStruct((N_EXPERTS, 128), _F32)],
        scratch_shapes=[pltpu.VMEM((N_EXPERTS, 128), _F32)],
        compiler_params=_cparams("arbitrary"),
        name="route",
    )(logits_t, bias_col)


def _rowmap_kernel(dest_ref, padstart_ref, padn_ref, h_ref, x1_ref, mod_ref, wsg_ref, wsu_ref, wsd_ref,
                   xmid_ref, codes_ref):
    i = pl.program_id(0)
    tm = h_ref.shape[0]
    base = i * tm

    @pl.when(i == 0)
    def _():
        def fill(e, done):
            def one(r, c):
                codes_ref[padstart_ref[e] + r] = (done + r) & (N_TOK - 1)
                return c
            lax.fori_loop(0, padn_ref[e], one, 0)
            return done + padn_ref[e]
        lax.fori_loop(0, N_EXPERTS, fill, 0)

    for t in range(tm):
        ds = [dest_ref[k * N_TOK + base + t] for k in range(TOP_K)]
        for k in range(TOP_K):
            codes_ref[ds[k]] = base + t

    h = h_ref[...].astype(_BF16)
    a = jnp.dot(h, wsg_ref[...], preferred_element_type=_F32)
    b = jnp.dot(h, wsu_ref[...], preferred_element_type=_F32)
    shared = jnp.dot((_silu(a) * b).astype(_BF16), wsd_ref[...], preferred_element_type=_F32)
    xmid_ref[...] = x1_ref[...] + mod_ref[0][5:6] * shared


def _rowmap(dest, pad_start, pad_n, h2, x1, mod3, wsg_bf, wsu_bf, wsd_bf):
    tm = TM
    tiles_per_batch = SEQ // tm
    tok = pl.BlockSpec((tm, D_MODEL), lambda i, *_: (i, 0))
    const = lambda shape: pl.BlockSpec(shape, lambda i, *_: (0,) * len(shape), pipeline_mode=pl.Buffered(1))
    grid_spec = pltpu.PrefetchScalarGridSpec(
        num_scalar_prefetch=3,
        grid=(N_TOK // tm,),
        in_specs=[tok, tok,
                  pl.BlockSpec((1, N_MOD, D_MODEL), lambda i, *_: (i // tiles_per_batch, 0, 0)),
                  const((D_MODEL, D_SHARED)), const((D_MODEL, D_SHARED)), const((D_SHARED, D_MODEL))],
        out_specs=[tok, pl.BlockSpec(memory_space=pltpu.SMEM)],
    )
    return pl.pallas_call(
        _rowmap_kernel,
        grid_spec=grid_spec,
        out_shape=[jax.ShapeDtypeStruct((N_TOK, D_MODEL), _F32),
                   jax.ShapeDtypeStruct((N_ROWS,), jnp.int32)],
        compiler_params=_cparams("arbitrary"),
        name="rowmap",
    )(dest.reshape(N_ASSIGN), pad_start, pad_n, h2, x1, mod3, wsg_bf, wsu_bf, wsd_bf)


def _experts_kernel(be_ref, nact_ref, first_ref, slot_ref, nxt_ref, codes_ref, h_hbm, wg_hbm, wu_hbm, wd_hbm,
                    out_hbm, x0, x1, x2, y0, y1, y2, wg_f, wu_f, wd_f, wg_bf, wu_bf, wd_bf, sem, gsem, ssem):
    n_act = nact_ref[0]
    bm = EXPERT_BM
    depth = EXPERT_DEPTH
    xbuf, ybuf = (x0, x1, x2), (y0, y1, y2)

    def fetch(e, s):
        return (pltpu.make_async_copy(wg_hbm.at[e], wg_f.at[s], sem.at[s, 0]),
                pltpu.make_async_copy(wu_hbm.at[e], wu_f.at[s], sem.at[s, 1]),
                pltpu.make_async_copy(wd_hbm.at[e], wd_f.at[s], sem.at[s, 2]))

    def gather(blk, p):
        for r in range(bm):
            tok = codes_ref[blk * bm + r]
            pltpu.make_async_copy(h_hbm.at[pl.ds(tok, 1), :], xbuf[p].at[pl.ds(r, 1), :],
                                  gsem.at[p]).start(priority=0)

    def gather_wait(p):
        pltpu.make_async_copy(h_hbm.at[pl.ds(0, bm), :], xbuf[p], gsem.at[p]).wait()

    def writeback(blk, p):
        return pltpu.make_async_copy(ybuf[p], out_hbm.at[pl.ds(pl.multiple_of(blk * bm, bm), bm), :], ssem.at[p])

    def block(blk, p):
        nxt1, nxt2 = (p + 1) % depth, (p + 2) % depth

        @pl.when(blk < n_act)
        def _():
            @pl.when(blk == 0)
            def _():
                for cp in fetch(be_ref[0], 0):
                    cp.start(priority=1)
                gather(0, 0)
                gather(jnp.minimum(1, n_act - 1), 1)

            @pl.when(first_ref[blk] == 1)
            def _():
                s = slot_ref[blk]
                for cp in fetch(be_ref[blk], s):
                    cp.wait()

                @pl.when(nxt_ref[blk] >= 0)
                def _():
                    for cp in fetch(nxt_ref[blk], 1 - s):
                        cp.start(priority=1)

                wg_bf[...] = wg_f[s].astype(_BF16)
                wu_bf[...] = wu_f[s].astype(_BF16)
                wd_bf[...] = wd_f[s].astype(_BF16)

            gather_wait(p)

            @pl.when(blk >= depth)
            def _():
                writeback(blk - depth, p).wait()

            gather(jnp.minimum(blk + 2, n_act - 1), nxt2)
            x = xbuf[p][...].astype(_BF16)
            a = jnp.dot(x, wg_bf[...], preferred_element_type=_F32)
            b = jnp.dot(x, wu_bf[...], preferred_element_type=_F32)
            ybuf[p][...] = jnp.dot((_silu(a) * b).astype(_BF16), wd_bf[...], preferred_element_type=_F32)
            writeback(blk, p).start(priority=1)

            @pl.when(blk == n_act - 1)
            def _():
                gather_wait(nxt1)
                gather_wait(nxt2)
                writeback(blk, p).wait()

                @pl.when(blk >= 1)
                def _():
                    writeback(blk - 1, nxt2).wait()

                @pl.when(blk >= 2)
                def _():
                    writeback(blk - 2, nxt1).wait()

    i = pl.program_id(0)
    for q in range(depth):
        block(depth * i + q, q)


def _experts(block_e, n_active, first, slot, nxt, codes, h2, w_gate, w_up, w_down):
    bm = EXPERT_BM
    depth = EXPERT_DEPTH
    hbm = pl.BlockSpec(memory_space=pl.ANY)
    grid_spec = pltpu.PrefetchScalarGridSpec(
        num_scalar_prefetch=6,
        grid=(N_BLOCKS // depth,),
        in_specs=[hbm, hbm, hbm, hbm],
        out_specs=hbm,
        scratch_shapes=[pltpu.VMEM((bm, D_MODEL), _F32)] * (2 * depth) + [
                        pltpu.VMEM((2, D_MODEL, D_EXPERT), _F32), pltpu.VMEM((2, D_MODEL, D_EXPERT), _F32),
                        pltpu.VMEM((2, D_EXPERT, D_MODEL), _F32),
                        pltpu.VMEM((D_MODEL, D_EXPERT), _BF16), pltpu.VMEM((D_MODEL, D_EXPERT), _BF16),
                        pltpu.VMEM((D_EXPERT, D_MODEL), _BF16),
                        pltpu.SemaphoreType.DMA((2, 3)), pltpu.SemaphoreType.DMA((depth,)),
                        pltpu.SemaphoreType.DMA((depth,))],
    )
    return pl.pallas_call(
        _experts_kernel,
        grid_spec=grid_spec,
        out_shape=jax.ShapeDtypeStruct((N_ROWS, D_MODEL), _F32),
        compiler_params=_cparams("arbitrary"),
        name="experts",
    )(block_e, n_active, first, slot, nxt, codes, h2, w_gate, w_up, w_down)


def _final_kernel(dest_ref, xmid_ref, wt_ref, mod_ref, fg_ref, ys_ref, o_ref, buf0, buf1, sem):
    i = pl.program_id(0)
    n_tiles = 2 * pl.num_programs(0)
    tm = TM_COMBINE
    bufs = (buf0, buf1)

    def gather(tile, p):
        base = tile * tm
        for t in range(tm):
            for k in range(TOP_K):
                pltpu.make_async_copy(ys_ref.at[pl.ds(dest_ref[k * N_TOK + base + t], 1), :],
                                      bufs[p].at[pl.ds(k * tm + t, 1), :], sem.at[p]).start(priority=k % 2)

    def drain(p):
        pltpu.make_async_copy(ys_ref.at[pl.ds(0, TOP_K * tm), :], bufs[p], sem.at[p]).wait()

    @pl.when(i == 0)
    def _():
        gather(0, 0)

    g2 = mod_ref[0][5:6]
    for p in range(2):
        tile = 2 * i + p
        drain(p)
        gather(jnp.minimum(tile + 1, n_tiles - 1), 1 - p)
        rows = slice(p * tm, (p + 1) * tm)
        wt = wt_ref[rows, :]
        routed = wt[:, 0:1] * bufs[p][0:tm, :]
        for k in range(1, TOP_K):
            routed = routed + wt[:, k:k + 1] * bufs[p][k * tm:(k + 1) * tm, :]
        y = xmid_ref[rows, :] + g2 * routed
        o_ref[rows, :] = y * _rms_scale(y) * fg_ref[...]

    @pl.when(i == pl.num_programs(0) - 1)
    def _():
        drain(0)


def _final(dest, xmid, w_tok, mod3, final_g, ys):
    tm = 2 * TM_COMBINE
    tiles_per_batch = SEQ // tm
    tok = lambda w: pl.BlockSpec((tm, w), lambda i, *_: (i, 0))
    grid_spec = pltpu.PrefetchScalarGridSpec(
        num_scalar_prefetch=1,
        grid=(N_TOK // tm,),
        in_specs=[tok(D_MODEL), tok(TOP_K),
                  pl.BlockSpec((1, N_MOD, D_MODEL), lambda i, *_: (i // tiles_per_batch, 0, 0)),
                  pl.BlockSpec((1, D_MODEL), lambda i, *_: (0, 0), pipeline_mode=pl.Buffered(1)),
                  pl.BlockSpec(memory_space=pl.ANY)],
        out_specs=tok(D_MODEL),
        scratch_shapes=[pltpu.VMEM((TOP_K * TM_COMBINE, D_MODEL), _F32),
                        pltpu.VMEM((TOP_K * TM_COMBINE, D_MODEL), _F32), pltpu.SemaphoreType.DMA((2,))],
    )
    return pl.pallas_call(
        _final_kernel,
        grid_spec=grid_spec,
        out_shape=jax.ShapeDtypeStruct((N_TOK, D_MODEL), _F32),
        compiler_params=_cparams("arbitrary"),
        name="final",
    )(dest.reshape(N_ASSIGN), xmid, w_tok, mod3, final_g, ys)


def _s5_param_layouts(lam_re, lam_im, b_re, b_im, c_re, c_im, log_dt):
    two = lambda a: jnp.concatenate([a, a], axis=-1)
    lr, li = two(lam_re), two(lam_im)
    dtb = jnp.broadcast_to(log_dt.astype(_F32)[:, :, None], lr.shape)
    zeros = jnp.zeros_like(lr)
    prow = jnp.stack([lr, li, dtb] + [zeros] * 5, axis=2)
    ct2 = jnp.concatenate([jnp.swapaxes(c_re, -1, -2), jnp.swapaxes(c_im, -1, -2)], axis=2)
    bt1 = jnp.concatenate([jnp.swapaxes(b_re, -1, -2), jnp.swapaxes(b_im, -1, -2)], axis=3)
    return prow, ct2, bt1


def kernel(x, c, ctx, c_ctx, norm1_g, norm2_g, w_ada, b_ada, w_in, ssm_lam_re, ssm_lam_im, ssm_b_re, ssm_b_im, ssm_c_re, ssm_c_im, ssm_log_dt, ssm_d, ssm_w_glu, conv_w, conv_b, mix_norm_g, w_out, router_w, router_bias, exp_w_gate, exp_w_up, exp_w_down, shared_w_gate, shared_w_up, shared_w_down, final_g):
    layer = 0
    x2 = x.reshape(N_TOK, D_MODEL)
    ctx2 = ctx.reshape(N_CTX_TOK, D_MODEL)

    c8 = jnp.concatenate([c, c_ctx[None, :], jnp.zeros((8 - BATCH - 1, D_MODEL), _F32)], axis=0)
    mod = _ada_mod(c8, w_ada[layer], b_ada[layer][None, :])
    mod3 = mod.reshape(8, N_MOD, D_MODEL)

    w_in_bf = w_in[layer].astype(_BF16)
    conv_w8 = jnp.concatenate([conv_w[layer], jnp.zeros((8 - conv_w.shape[1], D_CONV), _F32)], axis=0)
    u_x, conv_x = _inproj(x2, mod3, norm1_g[layer][None, :], w_in_bf, conv_w8, conv_b[layer][None, :])
    u_c = _ctxproj(ctx2, mod3, norm1_g[layer][None, :], w_in_bf[:, :D_SSM])

    prow, ct2, bt1 = _s5_param_layouts(ssm_lam_re[layer], ssm_lam_im[layer], ssm_b_re[layer],
                                       ssm_b_im[layer], ssm_c_re[layer], ssm_c_im[layer], ssm_log_dt[layer])
    toep, bpow, cpow, dec = _s5_params(prow, ct2, bt1)
    ys = _s5_scan(u_x, u_c, toep, bpow, cpow, dec)

    rw = router_w[layer]
    rw_hi = rw.astype(_BF16)
    rw_lo = (rw - rw_hi.astype(_F32)).astype(_BF16)
    rwt = jnp.concatenate([rw_hi.T, rw_lo.T], axis=0)
    x1, h2, logits_t = _merge(ys, u_x, conv_x, x2, mod3, ssm_d[layer][None, :],
                              mix_norm_g[layer][None, :], norm2_g[layer][None, :],
                              ssm_w_glu[layer].astype(_BF16), w_out[layer].astype(_BF16), rwt)

    bias_col = jnp.broadcast_to(router_bias[layer][:, None], (N_EXPERTS, 128))
    eid, w_k, rank, cnt = _route(logits_t, bias_col)

    counts = cnt[:, 0].astype(jnp.int32)
    padded = (counts + EXPERT_BM - 1) // EXPERT_BM * EXPERT_BM
    pend = jnp.cumsum(padded)
    pstart = pend - padded
    is_e = eid[:, :, None] == jnp.arange(N_EXPERTS, dtype=jnp.int32)
    dest = jnp.sum(jnp.where(is_e, pstart, 0), axis=-1) + rank
    n_active = (pend[-1] // EXPERT_BM).astype(jnp.int32)
    blk = jnp.minimum(jnp.arange(N_BLOCKS, dtype=jnp.int32), n_active - 1)
    ends_before = (pend[None, :] <= (blk * EXPERT_BM)[:, None]).astype(jnp.int32)
    block_e = jnp.minimum(jnp.sum(ends_before, axis=1), N_EXPERTS - 1)

    xmid, codes = _rowmap(dest, pstart + counts, padded - counts, h2, x1, mod3,
                          shared_w_gate[layer].astype(_BF16), shared_w_up[layer].astype(_BF16),
                          shared_w_down[layer].astype(_BF16))
    first = jnp.concatenate([jnp.ones((1,), jnp.int32), (block_e[1:] != block_e[:-1]).astype(jnp.int32)])
    slot = (jnp.cumsum(first) - 1) % 2
    e_ids = jnp.arange(N_EXPERTS, dtype=jnp.int32)
    owner = jnp.where(padded > 0, e_ids, N_EXPERTS)
    later = jnp.min(jnp.where(e_ids[None, :] > e_ids[:, None], owner[None, :], N_EXPERTS), axis=1)
    nxt_e = jnp.where(later == N_EXPERTS, -1, later)
    nxt = jnp.sum(jnp.where(block_e[:, None] == e_ids[None, :], nxt_e[None, :], 0), axis=1)
    ys_rows = _experts(block_e, n_active[None], first, slot, nxt, codes, h2,
                       exp_w_gate[layer], exp_w_up[layer], exp_w_down[layer])
    out = _final(dest, xmid, w_k.T, mod3, final_g[None, :], ys_rows)
    return out.reshape(BATCH, SEQ, D_MODEL)
```

```python
import functools
import math

import jax
import jax.numpy as jnp
from jax import lax
from jax.experimental import pallas as pl
from jax.experimental.pallas import tpu as pltpu

D_MODEL = 2048
BATCH = 4
SEQ = 2048
CTX_LEN = 256
GRID_W = 64
D_SSM = 1024
D_CONV = 1024
SSM_GROUP = 16
SSM_GROUPS = 64
SSM_STATE = 64
N_EXPERTS = 64
N_EXPERT_GROUPS = 8
GROUP_SIZE = N_EXPERTS // N_EXPERT_GROUPS
TOPK_GROUPS = 4
TOP_K = 8
D_EXPERT = 512
D_SHARED = 512
ROUTED_SCALE = 2.5
N_MOD = 6
EPS = 1e-6

N_TOK = BATCH * SEQ
N_CTX_TOK = BATCH * CTX_LEN

CHUNK = 16
CHUNK_W = CHUNK * SSM_GROUP
STATE_W = 2 * SSM_STATE
N_CTX_CHUNKS = CTX_LEN // CHUNK
N_X_CHUNKS = SEQ // CHUNK
N_SEQ_CHUNKS = N_CTX_CHUNKS + N_X_CHUNKS
SCAN_ROWS = N_SEQ_CHUNKS * BATCH
S5_GROUPS_PER_STEP = 8

TM = 256
EXPERT_BM = 256
N_ASSIGN = N_TOK * TOP_K
EXPERT_DEPTH = 3
N_BLOCKS = -(-(N_ASSIGN // EXPERT_BM + N_EXPERTS) // EXPERT_DEPTH) * EXPERT_DEPTH
N_ROWS = N_BLOCKS * EXPERT_BM
TM_COMBINE = 128
VMEM_LIMIT = 56 * 1024 * 1024

_F32 = jnp.float32
_BF16 = jnp.bfloat16


def _cparams(*sem):
    return pltpu.CompilerParams(dimension_semantics=sem, vmem_limit_bytes=VMEM_LIMIT)


def _const_spec(shape):
    nd = len(shape)
    return pl.BlockSpec(shape, lambda *_: (0,) * nd, pipeline_mode=pl.Buffered(1))


def _rms_scale(xf):
    return lax.rsqrt(jnp.mean(xf * xf, axis=-1, keepdims=True) + EPS)


def _silu(x):
    return x * jax.nn.sigmoid(x)


def _ada_kernel(c_ref, w_ref, b_ref, o_ref):
    s = _silu(c_ref[...])
    o_ref[...] = jnp.dot(s, w_ref[...], preferred_element_type=_F32) + b_ref[...]


def _ada_mod(c8, w_ada, b_ada):
    n = w_ada.shape[1]
    tn = 1024
    return pl.pallas_call(
        _ada_kernel,
        grid=(n // tn,),
        in_specs=[pl.BlockSpec((8, D_MODEL), lambda j: (0, 0)),
                  pl.BlockSpec((D_MODEL, tn), lambda j: (0, j)),
                  pl.BlockSpec((1, tn), lambda j: (0, j))],
        out_specs=pl.BlockSpec((8, tn), lambda j: (0, j)),
        out_shape=jax.ShapeDtypeStruct((8, n), _F32),
        compiler_params=_cparams("arbitrary"),
        name="ada_mod",
    )(c8, w_ada, b_ada)


def _modulated_norm(x_ref, mod_ref, g_ref, shift_row, scale_row):
    xf = x_ref[...]
    m = mod_ref[0]
    h = xf * _rms_scale(xf) * g_ref[...]
    return h * (1.0 + m[scale_row:scale_row + 1]) + m[shift_row:shift_row + 1]


def _inproj_kernel(x_ref, mod_ref, g_ref, w_ref, cw_ref, cb_ref, u_ref, conv_ref):
    h = _modulated_norm(x_ref, mod_ref, g_ref, 0, 1).astype(_BF16)
    u_ref[...] = jnp.dot(h, w_ref[:, 0:D_SSM], preferred_element_type=_F32)
    tm = x_ref.shape[0]
    pos = lax.broadcasted_iota(jnp.int32, (tm, 1), 0) % GRID_W
    not_first = (pos != 0).astype(_F32)
    not_last = (pos != GRID_W - 1).astype(_F32)
    cw = cw_ref[...]
    nc = 256
    for j in range(D_CONV // nc):
        lo = j * nc
        bg = jnp.dot(h, w_ref[:, D_SSM + lo:D_SSM + lo + nc], preferred_element_type=_F32)
        cg = jnp.dot(h, w_ref[:, D_SSM + D_CONV + lo:D_SSM + D_CONV + lo + nc], preferred_element_type=_F32)
        v = jnp.dot(h, w_ref[:, D_SSM + 2 * D_CONV + lo:D_SSM + 2 * D_CONV + lo + nc],
                    preferred_element_type=_F32)
        z = cg * v
        z_prev = pltpu.roll(z, 1, axis=0) * not_first
        z_next = pltpu.roll(z, tm - 1, axis=0) * not_last
        y = (cb_ref[:, lo:lo + nc] + z_prev * cw[0:1, lo:lo + nc] + z * cw[1:2, lo:lo + nc]
             + z_next * cw[2:3, lo:lo + nc])
        conv_ref[:, lo:lo + nc] = bg * y


def _inproj(x2, mod3, norm_g, w_in_bf, conv_w, conv_b):
    d_in = w_in_bf.shape[1]
    tiles_per_batch = SEQ // TM
    return pl.pallas_call(
        _inproj_kernel,
        grid=(N_TOK // TM,),
        in_specs=[pl.BlockSpec((TM, D_MODEL), lambda i: (i, 0)),
                  pl.BlockSpec((1, N_MOD, D_MODEL), lambda i: (i // tiles_per_batch, 0, 0)),
                  _const_spec((1, D_MODEL)),
                  _const_spec((D_MODEL, d_in)),
                  _const_spec((8, D_CONV)),
                  _const_spec((1, D_CONV))],
        out_specs=[pl.BlockSpec((TM, D_SSM), lambda i: (i, 0)),
                   pl.BlockSpec((TM, D_CONV), lambda i: (i, 0))],
        out_shape=[jax.ShapeDtypeStruct((N_TOK, D_SSM), _F32),
                   jax.ShapeDtypeStruct((N_TOK, D_CONV), _F32)],
        compiler_params=_cparams("arbitrary"),
        name="in_proj",
    )(x2, mod3, norm_g, w_in_bf, conv_w, conv_b)


def _ctxproj_kernel(x_ref, mod_ref, g_ref, w_ref, u_ref):
    h = _modulated_norm(x_ref, mod_ref, g_ref, 0, 1).astype(_BF16)
    u_ref[...] = jnp.dot(h, w_ref[...], preferred_element_type=_F32)


def _ctxproj(ctx2, mod3, norm_g, w_u_bf):
    return pl.pallas_call(
        _ctxproj_kernel,
        grid=(N_CTX_TOK // TM,),
        in_specs=[pl.BlockSpec((TM, D_MODEL), lambda i: (i, 0)),
                  pl.BlockSpec((1, N_MOD, D_MODEL), lambda i: (BATCH, 0, 0)),
                  _const_spec((1, D_MODEL)),
                  _const_spec((D_MODEL, D_SSM))],
        out_specs=pl.BlockSpec((TM, D_SSM), lambda i: (i, 0)),
        out_shape=jax.ShapeDtypeStruct((N_CTX_TOK, D_SSM), _F32),
        compiler_params=_cparams("arbitrary"),
        name="ctx_proj",
    )(ctx2, mod3, norm_g, w_u_bf)


N_POW = 24


def _s5_param_kernel(prow_ref, ct_ref, bt_ref, toep_ref, bpow_ref, cpow_ref, dec_ref):
    lane = lambda shape: lax.broadcasted_iota(jnp.int32, shape, 1)
    sub = lambda shape: lax.broadcasted_iota(jnp.int32, shape, 0)

    def split2(a):
        a_hi = a.astype(_BF16)
        return a_hi, (a - a_hi.astype(_F32)).astype(_BF16)

    rep = (lane((SSM_GROUP, CHUNK_W)) % SSM_GROUP == sub((SSM_GROUP, CHUNK_W))).astype(_BF16)
    sgn_col = jnp.where(sub((STATE_W, 1)) < SSM_STATE, 1.0, -1.0).astype(_F32)
    sgn_row = jnp.where(lane((1, STATE_W)) < SSM_STATE, -1.0, 1.0).astype(_F32)
    tau_col = sub((N_POW, 1)).astype(_F32)
    blk_l = lane((N_POW, CHUNK_W)) // SSM_GROUP
    tau_s = sub((N_POW, CHUNK_W))
    blk_r = sub((CHUNK_W, N_POW)) // SSM_GROUP
    tau_l = lane((CHUNK_W, N_POW))
    pick_l = lambda e: (tau_s == e).astype(_BF16)
    pick_r = lambda e: (tau_l == e).astype(_BF16)
    tn = (((0,), (0,)), ((), ()))

    strips = []
    for d in range(2):
        pr = prow_ref[d, 0]
        lam_re, lam_im, dt = pr[0:1], pr[1:2], jnp.exp(pr[2:3])
        mag = jnp.exp(tau_col * (lam_re * dt))
        ang = tau_col * (lam_im * dt)
        pw_re = mag * jnp.cos(ang)
        pw_im = mag * jnp.sin(ang)
        pw4 = jnp.concatenate(split2(pw_re) + split2(pw_im), axis=1)

        c_hi, c_lo = split2(ct_ref[d, 0])
        ct = (jnp.dot(c_hi, rep, preferred_element_type=_F32)
              + jnp.dot(c_lo, rep, preferred_element_type=_F32))
        ca = ct * sgn_col
        cb = -pltpu.roll(ct, SSM_STATE, axis=0)

        def cpow(sel):
            o = lax.dot_general(pw4, sel, tn, preferred_element_type=_F32)
            x = o[0:STATE_W] + o[STATE_W:2 * STATE_W]
            y = o[2 * STATE_W:3 * STATE_W] + o[3 * STATE_W:4 * STATE_W]
            return x * ca + y * cb

        nr = pw_re[1:2] - 1.0
        ni = pw_im[1:2]
        den = lam_re * lam_re + lam_im * lam_im
        kr = (nr * lam_re + ni * lam_im) / den
        ki = (ni * lam_re - nr * lam_im) / den
        b1 = bt_ref[d, 0]
        b2 = pltpu.roll(b1, SSM_STATE, axis=1) * sgn_row
        u1 = kr * b1 + ki * b2
        u2 = kr * b2 - ki * b1
        u1t = jnp.concatenate([u1] * CHUNK, axis=0)
        u2t = jnp.concatenate([u2] * CHUNK, axis=0)

        def bpow(sel):
            o = jnp.dot(sel, pw4, preferred_element_type=_F32)
            x = o[:, 0:STATE_W] + o[:, STATE_W:2 * STATE_W]
            y = o[:, 2 * STATE_W:3 * STATE_W] + o[:, 3 * STATE_W:4 * STATE_W]
            return x * u1t + y * u2t

        if d == 0:
            e_strip, e_b, e_c = blk_l, (CHUNK - 1) - blk_r, blk_l + 1
        else:
            e_strip, e_b, e_c = (CHUNK - 1) - blk_l, blk_r, CHUNK - blk_l
        u_hi, u_lo = split2(u1)
        k_hi, k_lo = split2(cpow(pick_l(e_strip)))
        strips.append(jnp.dot(u_hi, k_hi, preferred_element_type=_F32)
                      + jnp.dot(u_hi, k_lo, preferred_element_type=_F32)
                      + jnp.dot(u_lo, k_hi, preferred_element_type=_F32))
        bp = bpow(pick_r(e_b))
        bpow_ref[0, :, d * STATE_W:(d + 1) * STATE_W] = bp.astype(_BF16)
        bpow_ref[0, :, (2 + d) * STATE_W:(3 + d) * STATE_W] = pltpu.roll(bp, SSM_STATE, axis=1).astype(_BF16)
        cpow_ref[0, d * STATE_W:(d + 1) * STATE_W, :] = cpow(pick_l(e_c)).astype(_BF16)

        dec_ref[0, 2 * d:2 * d + 1, :] = pw_re[CHUNK:CHUNK + 1]
        dec_ref[0, 2 * d + 1:2 * d + 2, :] = pw_im[CHUNK:CHUNK + 1] * sgn_row

    zeros = jnp.zeros((SSM_GROUP, CHUNK_W), _F32)
    strip = (jnp.concatenate([strips[1], zeros], axis=1)
             + pltpu.roll(jnp.concatenate([strips[0], zeros], axis=1), CHUNK_W - SSM_GROUP, axis=1))
    for i in range(CHUNK):
        off = (CHUNK - 1 - i) * SSM_GROUP
        win = strip if off == 0 else pltpu.roll(strip, 2 * CHUNK_W - off, axis=1)
        toep_ref[0, i * SSM_GROUP:(i + 1) * SSM_GROUP, :] = win[:, 0:CHUNK_W].astype(_BF16)


def _s5_params(prow, ct2, bt1):
    g_spec = lambda shape: pl.BlockSpec(shape, lambda g: (0, g, 0, 0))
    o_spec = lambda shape: pl.BlockSpec(shape, lambda g: (g, 0, 0))
    return pl.pallas_call(
        _s5_param_kernel,
        grid=(SSM_GROUPS,),
        in_specs=[g_spec((2, 1, 8, STATE_W)),
                  g_spec((2, 1, STATE_W, SSM_GROUP)), g_spec((2, 1, SSM_GROUP, STATE_W))],
        out_specs=[o_spec((1, CHUNK_W, CHUNK_W)), o_spec((1, CHUNK_W, 4 * STATE_W)),
                   o_spec((1, 2 * STATE_W, CHUNK_W)), o_spec((1, 4, STATE_W))],
        out_shape=[jax.ShapeDtypeStruct((SSM_GROUPS, CHUNK_W, CHUNK_W), _BF16),
                   jax.ShapeDtypeStruct((SSM_GROUPS, CHUNK_W, 4 * STATE_W), _BF16),
                   jax.ShapeDtypeStruct((SSM_GROUPS, 2 * STATE_W, CHUNK_W), _BF16),
                   jax.ShapeDtypeStruct((SSM_GROUPS, 4, STATE_W), _F32)],
        compiler_params=_cparams("arbitrary"),
        name="s5_params",
    )(prow, ct2, bt1)


def _s5_kernel(ux_ref, uc_ref, toep_ref, bpow_ref, cpow_ref, dec_ref, y_ref, u_ref, s_ref, h_ref, y2_ref):
    ng = S5_GROUPS_PER_STEP
    gw = SSM_GROUP
    x0 = N_CTX_CHUNKS * BATCH

    half = CHUNK // 2
    assert half == ng
    lanes = ng * 128

    r = lax.broadcasted_iota(jnp.int32, (lanes, 1), 0)
    swapped = ((r // gw) % ng) * 128 + (r // 128) * gw + r % gw
    perm = (lax.broadcasted_iota(jnp.int32, (lanes, lanes), 1) == swapped).astype(_BF16)

    def to_chunk_layout(src_ref, seq_len, n_chunks, row0):
        for hf in range(2):
            p = jnp.concatenate(
                [jnp.concatenate([src_ref[pl.ds(b * seq_len + hf * half + il, n_chunks, stride=CHUNK), :]
                                  for il in range(half)], axis=1) for b in range(BATCH)], axis=0).astype(_BF16)
            q = jnp.dot(p, perm, preferred_element_type=_F32)
            for b in range(BATCH):
                for g in range(ng):
                    u_ref[g, hf, pl.ds(row0 + b, n_chunks, stride=BATCH), :] = (
                        q[b * n_chunks:(b + 1) * n_chunks, g * 128:(g + 1) * 128])

    to_chunk_layout(uc_ref, CTX_LEN, N_CTX_CHUNKS, 0)
    to_chunk_layout(ux_ref, SEQ, N_X_CHUNKS, x0)

    chunk_rows = lambda ref, g, lo: jnp.concatenate([ref[g, 0, lo:, :], ref[g, 1, lo:, :]], axis=1)
    for g in range(ng):
        s_ref[g] = jnp.dot(chunk_rows(u_ref, g, 0).astype(_BF16), bpow_ref[g], preferred_element_type=_F32)

    pair = 2 * BATCH
    n_ctx_pairs = N_CTX_CHUNKS // 2
    n_pairs = N_SEQ_CHUNKS // 2
    lower = lax.broadcasted_iota(jnp.int32, (pair, STATE_W), 0) < BATCH
    zero = jnp.zeros((pair, STATE_W), _F32)

    def step(t, carry):
        tb = jnp.where(t < n_ctx_pairs, n_ctx_pairs - 1 - t, n_pairs + n_ctx_pairs - 1 - t)
        rf = pl.multiple_of(t * pair, pair)
        rb = pl.multiple_of(tb * pair, pair)
        out = []
        swap = lambda v: pltpu.roll(v, BATCH, axis=0)
        for g in range(ng):
            hf, hfs, hb, hbs = carry[4 * g:4 * g + 4]
            dg = dec_ref[g]

            def advance(h, hs, a_re, a_im, s, ss):
                return a_re * h + a_im * hs + s, a_re * hs - a_im * h + ss

            col = lambda r, k: s_ref[g, pl.ds(r, pair), k * STATE_W:(k + 1) * STATE_W]
            sf, sfs = col(rf, 0), col(rf, 2)
            mid, mids = advance(hf, hfs, dg[0:1], dg[1:2], jnp.where(lower, sf, swap(sf)),
                                jnp.where(lower, sfs, swap(sfs)))
            h_ref[g, pl.ds(rf, pair), 0:STATE_W] = jnp.where(lower, hf, mid)
            end, ends = advance(mid, mids, dg[0:1], dg[1:2], sf, sfs)
            hf, hfs = jnp.where(lower, swap(end), end), jnp.where(lower, swap(ends), ends)

            sb, sbs = col(rb, 1), col(rb, 3)
            mid, mids = advance(hb, hbs, dg[2:3], dg[3:4], jnp.where(lower, swap(sb), sb),
                                jnp.where(lower, swap(sbs), sbs))
            h_ref[g, pl.ds(rb, pair), STATE_W:2 * STATE_W] = jnp.where(lower, mid, hb)
            end, ends = advance(mid, mids, dg[2:3], dg[3:4], sb, sbs)
            hb, hbs = jnp.where(lower, end, swap(end)), jnp.where(lower, ends, swap(ends))
            out += [hf, hfs, hb, hbs]
        return tuple(out)

    lax.fori_loop(0, n_pairs, step, (zero,) * (4 * ng))

    for g in range(ng):
        ux = chunk_rows(u_ref, g, x0).astype(_BF16)
        hx = h_ref[g, x0:SCAN_ROWS, :].astype(_BF16)
        y2 = (jnp.dot(ux, toep_ref[g], preferred_element_type=_F32)
              + jnp.dot(hx, cpow_ref[g], preferred_element_type=_F32))
        y2_ref[g, 0] = y2[:, 0:128]
        y2_ref[g, 1] = y2[:, 128:256]

    for hf in range(2):
        yin = jnp.concatenate(
            [jnp.concatenate([y2_ref[g, hf, pl.ds(b, N_X_CHUNKS, stride=BATCH), :] for g in range(ng)], axis=1)
             for b in range(BATCH)], axis=0)
        y_hi = yin.astype(_BF16)
        y_lo = (yin - y_hi.astype(_F32)).astype(_BF16)
        q = (jnp.dot(y_hi, perm, preferred_element_type=_F32)
             + jnp.dot(y_lo, perm, preferred_element_type=_F32))
        for b in range(BATCH):
            for jl in range(half):
                y_ref[pl.ds(b * SEQ + hf * half + jl, N_X_CHUNKS, stride=CHUNK), :] = (
                    q[b * N_X_CHUNKS:(b + 1) * N_X_CHUNKS, jl * 128:(jl + 1) * 128])


def _s5_scan(u_x, u_c, toep, bpow, cpow, dec):
    ng = S5_GROUPS_PER_STEP
    n_x_rows = N_X_CHUNKS * BATCH
    lanes = ng * SSM_GROUP
    spec = lambda r, c: pl.BlockSpec((ng, r, c), lambda i: (i, 0, 0))
    col = lambda rows: pl.BlockSpec((rows, lanes), lambda i: (0, i))
    return pl.pallas_call(
        _s5_kernel,
        grid=(SSM_GROUPS // ng,),
        in_specs=[col(N_TOK), col(N_CTX_TOK), spec(CHUNK_W, CHUNK_W), spec(CHUNK_W, 4 * STATE_W),
                  spec(2 * STATE_W, CHUNK_W), spec(4, STATE_W)],
        out_specs=col(N_TOK),
        out_shape=jax.ShapeDtypeStruct((N_TOK, D_SSM), _F32),
        scratch_shapes=[pltpu.VMEM((ng, 2, SCAN_ROWS, 128), _F32),
                        pltpu.VMEM((ng, SCAN_ROWS, 4 * STATE_W), _F32),
                        pltpu.VMEM((ng, SCAN_ROWS, 2 * STATE_W), _F32),
                        pltpu.VMEM((ng, 2, n_x_rows, 128), _F32)],
        compiler_params=_cparams("arbitrary"),
        name="s5_scan",
    )(u_x, u_c, toep, bpow, cpow, dec)


def _merge_kernel(ys_ref, u_ref, conv_ref, x_ref, mod_ref, dskip_ref, mixg_ref, n2g_ref,
                  wglu_ref, wout_ref, rwt_ref, x1_ref, h2_ref, logit_ref):
    m = mod_ref[0]
    yx = dskip_ref[...] * u_ref[...] + ys_ref[...]
    c0 = math.sqrt(2.0 / math.pi)
    ge = 0.5 * yx * (1.0 + jnp.tanh(c0 * (yx + 0.044715 * (yx * yx * yx))))
    z = jnp.dot(ge.astype(_BF16), wglu_ref[...], preferred_element_type=_F32)
    ssm_y = z[:, 0:D_SSM] * jax.nn.sigmoid(z[:, D_SSM:2 * D_SSM])
    conv_y = conv_ref[...]
    mixg = mixg_ref[...]
    heads_a = (ssm_y * _rms_scale(ssm_y) * mixg[:, 0:D_SSM]).astype(_BF16)
    heads_b = (conv_y * _rms_scale(conv_y) * mixg[:, D_SSM:]).astype(_BF16)
    mix = (jnp.dot(heads_a, wout_ref[0:D_SSM, :], preferred_element_type=_F32)
           + jnp.dot(heads_b, wout_ref[D_SSM:, :], preferred_element_type=_F32))
    x1 = x_ref[...] + m[2:3] * mix
    x1_ref[...] = x1
    h2 = x1 * _rms_scale(x1) * n2g_ref[...] * (1.0 + m[4:5]) + m[3:4]
    h2_ref[...] = h2
    h_hi = h2.astype(_BF16)
    h_lo = (h2 - h_hi.astype(_F32)).astype(_BF16)
    nt = (((1,), (1,)), ((), ()))
    p = lax.dot_general(rwt_ref[...], h_hi, nt, preferred_element_type=_F32)
    q = lax.dot_general(rwt_ref[0:N_EXPERTS, :], h_lo, nt, preferred_element_type=_F32)
    logit_ref[...] = p[0:N_EXPERTS] + p[N_EXPERTS:2 * N_EXPERTS] + q


def _merge(ys, u, conv, x2, mod3, dskip, mixg, n2g, wglu_bf, wout_bf, rwt):
    tiles_per_batch = SEQ // TM
    tok = lambda w: pl.BlockSpec((TM, w), lambda i: (i, 0))
    return pl.pallas_call(
        _merge_kernel,
        grid=(N_TOK // TM,),
        in_specs=[tok(D_SSM), tok(D_SSM), tok(D_CONV), tok(D_MODEL),
                  pl.BlockSpec((1, N_MOD, D_MODEL), lambda i: (i // tiles_per_batch, 0, 0)),
                  _const_spec((1, D_SSM)), _const_spec((1, D_MODEL)), _const_spec((1, D_MODEL)),
                  _const_spec((D_SSM, 2 * D_SSM)), _const_spec((D_MODEL, D_MODEL)),
                  _const_spec((2 * N_EXPERTS, D_MODEL))],
        out_specs=[tok(D_MODEL), tok(D_MODEL),
                   pl.BlockSpec((N_EXPERTS, TM), lambda i: (0, i))],
        out_shape=[jax.ShapeDtypeStruct((N_TOK, D_MODEL), _F32),
                   jax.ShapeDtypeStruct((N_TOK, D_MODEL), _F32),
                   jax.ShapeDtypeStruct((N_EXPERTS, N_TOK), _F32)],
        compiler_params=_cparams("arbitrary"),
        name="merge_heads",
    )(ys, u, conv, x2, mod3, dskip, mixg, n2g, wglu_bf, wout_bf, rwt)


def _route_kernel(logit_ref, bias_ref, eid_ref, w_ref, rank_ref, cnt_ref, carry_ref):
    @pl.when(pl.program_id(0) == 0)
    def _():
        carry_ref[...] = jnp.zeros_like(carry_ref)

    tm = logit_ref.shape[1]
    neg = jnp.float32(-jnp.inf)
    scores = jax.nn.sigmoid(logit_ref[...])
    biased = scores + bias_ref[:, 0:1]
    sub = lax.broadcasted_iota(jnp.int32, (GROUP_SIZE, tm), 0)
    rows = lambda a, g: a[g * GROUP_SIZE:(g + 1) * GROUP_SIZE]
    ngrp = N_EXPERT_GROUPS

    gscore = []
    for g in range(ngrp):
        bg = rows(biased, g)
        m1 = jnp.max(bg, axis=0, keepdims=True)
        first = jnp.min(jnp.where(bg == m1, sub, GROUP_SIZE), axis=0, keepdims=True)
        m2 = jnp.max(jnp.where(sub == first, neg, bg), axis=0, keepdims=True)
        gscore.append(m1 + m2)
    v = []
    for g in range(ngrp):
        beaten = jnp.zeros((1, tm), jnp.int32)
        for o in range(ngrp):
            if o != g:
                beats = (gscore[o] >= gscore[g]) if o < g else (gscore[o] > gscore[g])
                beaten = beaten + beats.astype(jnp.int32)
        v.append(jnp.where(beaten < TOPK_GROUPS, rows(biased, g), neg))
    eids = [sub + g * GROUP_SIZE for g in range(ngrp)]

    pick_ids, pick_masks = [], []
    for _ in range(TOP_K):
        m = functools.reduce(jnp.maximum, v)
        m = jnp.max(m, axis=0, keepdims=True)
        cand = functools.reduce(jnp.minimum, [jnp.where(v[g] == m, eids[g], N_EXPERTS) for g in range(ngrp)])
        pick_id = jnp.min(cand, axis=0, keepdims=True)
        masks = [eids[g] == pick_id for g in range(ngrp)]
        v = [jnp.where(masks[g], neg, v[g]) for g in range(ngrp)]
        pick_ids.append(pick_id)
        pick_masks.append(masks)

    sel = jnp.concatenate(
        [functools.reduce(jnp.logical_or, [pick_masks[k][g] for k in range(TOP_K)]).astype(_F32)
         for g in range(ngrp)], axis=0)
    before = (lax.broadcasted_iota(jnp.int32, (tm, tm), 0)
              < lax.broadcasted_iota(jnp.int32, (tm, tm), 1)).astype(_BF16)
    base = jnp.dot(sel.astype(_BF16), before, preferred_element_type=_F32) + carry_ref[:, 0:1]

    def gather_pick(a, k):
        parts = [jnp.sum(jnp.where(pick_masks[k][g], rows(a, g), 0.0), axis=0, keepdims=True)
                 for g in range(ngrp)]
        return functools.reduce(jnp.add, parts)

    picked = [gather_pick(scores, k) for k in range(TOP_K)]
    denom = functools.reduce(jnp.add, picked)
    for k in range(TOP_K):
        eid_ref[k:k + 1, :] = pick_ids[k]
        w_ref[k:k + 1, :] = picked[k] / denom * ROUTED_SCALE
        rank_ref[k:k + 1, :] = gather_pick(base, k).astype(jnp.int32)
    carry_ref[...] = carry_ref[...] + jnp.sum(sel, axis=1, keepdims=True)
    cnt_ref[...] = carry_ref[...]


def _route(logits_t, bias_col):
    tok = pl.BlockSpec((TOP_K, TM), lambda i: (0, i))
    return pl.pallas_call(
        _route_kernel,
        grid=(N_TOK // TM,),
        in_specs=[pl.BlockSpec((N_EXPERTS, TM), lambda i: (0, i)), _const_spec((N_EXPERTS, 128))],
        out_specs=[tok, tok, tok, pl.BlockSpec((N_EXPERTS, 128), lambda i: (0, 0))],
        out_shape=[jax.ShapeDtypeStruct((TOP_K, N_TOK), jnp.int32),
                   jax.ShapeDtypeStruct((TOP_K, N_TOK), _F32),
                   jax.ShapeDtypeStruct((TOP_K, N_TOK), jnp.int32),
                   jax.ShapeDtypeStruct((N_EXPERTS, 128), _F32)],
        scratch_shapes=[pltpu.VMEM((N_EXPERTS, 128), _F32)],
        compiler_params=_cparams("arbitrary"),
        name="route",
    )(logits_t, bias_col)


def _rowmap_kernel(dest_ref, padstart_ref, padn_ref, h_ref, x1_ref, mod_ref, wsg_ref, wsu_ref, wsd_ref,
                   xmid_ref, codes_ref):
    i = pl.program_id(0)
    tm = h_ref.shape[0]
    base = i * tm

    @pl.when(i == 0)
    def _():
        def fill(e, done):
            def one(r, c):
                codes_ref[padstart_ref[e] + r] = (done + r) & (N_TOK - 1)
                return c
            lax.fori_loop(0, padn_ref[e], one, 0)
            return done + padn_ref[e]
        lax.fori_loop(0, N_EXPERTS, fill, 0)

    group = 2
    for t0 in range(0, tm, group):
        ds = [[dest_ref[k * N_TOK + base + t0 + u] for k in range(TOP_K)] for u in range(group)]
        for u in range(group):
            for k in range(TOP_K):
                codes_ref[ds[u][k]] = base + t0 + u

    h = h_ref[...].astype(_BF16)
    a = jnp.dot(h, wsg_ref[...], preferred_element_type=_F32)
    b = jnp.dot(h, wsu_ref[...], preferred_element_type=_F32)
    shared = jnp.dot((_silu(a) * b).astype(_BF16), wsd_ref[...], preferred_element_type=_F32)
    xmid_ref[...] = x1_ref[...] + mod_ref[0][5:6] * shared


def _rowmap(dest, pad_start, pad_n, h2, x1, mod3, wsg_bf, wsu_bf, wsd_bf):
    tm = TM
    tiles_per_batch = SEQ // tm
    tok = pl.BlockSpec((tm, D_MODEL), lambda i, *_: (i, 0))
    const = lambda shape: pl.BlockSpec(shape, lambda i, *_: (0,) * len(shape), pipeline_mode=pl.Buffered(1))
    grid_spec = pltpu.PrefetchScalarGridSpec(
        num_scalar_prefetch=3,
        grid=(N_TOK // tm,),
        in_specs=[tok, tok,
                  pl.BlockSpec((1, N_MOD, D_MODEL), lambda i, *_: (i // tiles_per_batch, 0, 0)),
                  const((D_MODEL, D_SHARED)), const((D_MODEL, D_SHARED)), const((D_SHARED, D_MODEL))],
        out_specs=[tok, pl.BlockSpec(memory_space=pltpu.SMEM)],
    )
    return pl.pallas_call(
        _rowmap_kernel,
        grid_spec=grid_spec,
        out_shape=[jax.ShapeDtypeStruct((N_TOK, D_MODEL), _F32),
                   jax.ShapeDtypeStruct((N_ROWS,), jnp.int32)],
        compiler_params=_cparams("arbitrary"),
        name="rowmap",
    )(dest.reshape(N_ASSIGN), pad_start, pad_n, h2, x1, mod3, wsg_bf, wsu_bf, wsd_bf)


def _experts_kernel(be_ref, nact_ref, first_ref, slot_ref, nxt_ref, codes_ref, h_hbm, wg_hbm, wu_hbm, wd_hbm,
                    out_hbm, x0, x1, x2, y0, y1, y2, wg_f, wu_f, wd_f, wg_bf, wu_bf, wd_bf, sem, gsem, ssem):
    n_act = nact_ref[0]
    bm = EXPERT_BM
    depth = EXPERT_DEPTH
    xbuf, ybuf = (x0, x1, x2), (y0, y1, y2)

    def fetch(e, s):
        return (pltpu.make_async_copy(wg_hbm.at[e], wg_f.at[s], sem.at[s, 0]),
                pltpu.make_async_copy(wu_hbm.at[e], wu_f.at[s], sem.at[s, 1]),
                pltpu.make_async_copy(wd_hbm.at[e], wd_f.at[s], sem.at[s, 2]))

    def gather(blk, p):
        for r in range(bm):
            tok = codes_ref[blk * bm + r]
            pltpu.make_async_copy(h_hbm.at[pl.ds(tok, 1), :], xbuf[p].at[pl.ds(r, 1), :],
                                  gsem.at[p]).start(priority=0)

    def gather_wait(p):
        pltpu.make_async_copy(h_hbm.at[pl.ds(0, bm), :], xbuf[p], gsem.at[p]).wait()

    def writeback(blk, p):
        return pltpu.make_async_copy(ybuf[p], out_hbm.at[pl.ds(pl.multiple_of(blk * bm, bm), bm), :], ssem.at[p])

    def block(blk, p):
        nxt1, nxt2 = (p + 1) % depth, (p + 2) % depth

        @pl.when(blk < n_act)
        def _():
            @pl.when(blk == 0)
            def _():
                for cp in fetch(be_ref[0], 0):
                    cp.start(priority=1)
                gather(0, 0)
                gather(jnp.minimum(1, n_act - 1), 1)

            @pl.when(first_ref[blk] == 1)
            def _():
                s = slot_ref[blk]
                for cp in fetch(be_ref[blk], s):
                    cp.wait()

                @pl.when(nxt_ref[blk] >= 0)
                def _():
                    for cp in fetch(nxt_ref[blk], 1 - s):
                        cp.start(priority=1)

                wg_bf[...] = wg_f[s].astype(_BF16)
                wu_bf[...] = wu_f[s].astype(_BF16)
                wd_bf[...] = wd_f[s].astype(_BF16)

            gather_wait(p)

            @pl.when(blk >= depth)
            def _():
                writeback(blk - depth, p).wait()

            gather(jnp.minimum(blk + 2, n_act - 1), nxt2)
            x = xbuf[p][...].astype(_BF16)
            a = jnp.dot(x, wg_bf[...], preferred_element_type=_F32)
            b = jnp.dot(x, wu_bf[...], preferred_element_type=_F32)
            ybuf[p][...] = jnp.dot((_silu(a) * b).astype(_BF16), wd_bf[...], preferred_element_type=_F32)
            writeback(blk, p).start(priority=1)

            @pl.when(blk == n_act - 1)
            def _():
                gather_wait(nxt1)
                gather_wait(nxt2)
                writeback(blk, p).wait()

                @pl.when(blk >= 1)
                def _():
                    writeback(blk - 1, nxt2).wait()

                @pl.when(blk >= 2)
                def _():
                    writeback(blk - 2, nxt1).wait()

    i = pl.program_id(0)
    for q in range(depth):
        block(depth * i + q, q)


def _experts(block_e, n_active, first, slot, nxt, codes, h2, w_gate, w_up, w_down):
    bm = EXPERT_BM
    depth = EXPERT_DEPTH
    hbm = pl.BlockSpec(memory_space=pl.ANY)
    grid_spec = pltpu.PrefetchScalarGridSpec(
        num_scalar_prefetch=6,
        grid=(N_BLOCKS // depth,),
        in_specs=[hbm, hbm, hbm, hbm],
        out_specs=hbm,
        scratch_shapes=[pltpu.VMEM((bm, D_MODEL), _F32)] * (2 * depth) + [
                        pltpu.VMEM((2, D_MODEL, D_EXPERT), _F32), pltpu.VMEM((2, D_MODEL, D_EXPERT), _F32),
                        pltpu.VMEM((2, D_EXPERT, D_MODEL), _F32),
                        pltpu.VMEM((D_MODEL, D_EXPERT), _BF16), pltpu.VMEM((D_MODEL, D_EXPERT), _BF16),
                        pltpu.VMEM((D_EXPERT, D_MODEL), _BF16),
                        pltpu.SemaphoreType.DMA((2, 3)), pltpu.SemaphoreType.DMA((depth,)),
                        pltpu.SemaphoreType.DMA((depth,))],
    )
    return pl.pallas_call(
        _experts_kernel,
        grid_spec=grid_spec,
        out_shape=jax.ShapeDtypeStruct((N_ROWS, D_MODEL), _F32),
        compiler_params=_cparams("arbitrary"),
        name="experts",
    )(block_e, n_active, first, slot, nxt, codes, h2, w_gate, w_up, w_down)


def _final_kernel(dest_ref, xmid_ref, wt_ref, mod_ref, fg_ref, ys_ref, o_ref, buf0, buf1, sem):
    i = pl.program_id(0)
    n_tiles = 2 * pl.num_programs(0)
    tm = TM_COMBINE
    bufs = (buf0, buf1)

    def gather(tile, p):
        base = tile * tm
        for t in range(tm):
            for k in range(TOP_K):
                pltpu.make_async_copy(ys_ref.at[pl.ds(dest_ref[k * N_TOK + base + t], 1), :],
                                      bufs[p].at[pl.ds(k * tm + t, 1), :], sem.at[p]).start(priority=k % 2)

    def drain(p):
        pltpu.make_async_copy(ys_ref.at[pl.ds(0, TOP_K * tm), :], bufs[p], sem.at[p]).wait()

    @pl.when(i == 0)
    def _():
        gather(0, 0)

    g2 = mod_ref[0][5:6]
    for p in range(2):
        tile = 2 * i + p
        drain(p)
        gather(jnp.minimum(tile + 1, n_tiles - 1), 1 - p)
        rows = slice(p * tm, (p + 1) * tm)
        wt = wt_ref[rows, :]
        routed = wt[:, 0:1] * bufs[p][0:tm, :]
        for k in range(1, TOP_K):
            routed = routed + wt[:, k:k + 1] * bufs[p][k * tm:(k + 1) * tm, :]
        y = xmid_ref[rows, :] + g2 * routed
        o_ref[rows, :] = y * _rms_scale(y) * fg_ref[...]

    @pl.when(i == pl.num_programs(0) - 1)
    def _():
        drain(0)


def _final(dest, xmid, w_tok, mod3, final_g, ys):
    tm = 2 * TM_COMBINE
    tiles_per_batch = SEQ // tm
    tok = lambda w: pl.BlockSpec((tm, w), lambda i, *_: (i, 0))
    grid_spec = pltpu.PrefetchScalarGridSpec(
        num_scalar_prefetch=1,
        grid=(N_TOK // tm,),
        in_specs=[tok(D_MODEL), tok(TOP_K),
                  pl.BlockSpec((1, N_MOD, D_MODEL), lambda i, *_: (i // tiles_per_batch, 0, 0)),
                  pl.BlockSpec((1, D_MODEL), lambda i, *_: (0, 0), pipeline_mode=pl.Buffered(1)),
                  pl.BlockSpec(memory_space=pl.ANY)],
        out_specs=tok(D_MODEL),
        scratch_shapes=[pltpu.VMEM((TOP_K * TM_COMBINE, D_MODEL), _F32),
                        pltpu.VMEM((TOP_K * TM_COMBINE, D_MODEL), _F32), pltpu.SemaphoreType.DMA((2,))],
    )
    return pl.pallas_call(
        _final_kernel,
        grid_spec=grid_spec,
        out_shape=jax.ShapeDtypeStruct((N_TOK, D_MODEL), _F32),
        compiler_params=_cparams("arbitrary"),
        name="final",
    )(dest.reshape(N_ASSIGN), xmid, w_tok, mod3, final_g, ys)


def _s5_param_layouts(lam_re, lam_im, b_re, b_im, c_re, c_im, log_dt):
    two = lambda a: jnp.concatenate([a, a], axis=-1)
    lr, li = two(lam_re), two(lam_im)
    dtb = jnp.broadcast_to(log_dt.astype(_F32)[:, :, None], lr.shape)
    zeros = jnp.zeros_like(lr)
    prow = jnp.stack([lr, li, dtb] + [zeros] * 5, axis=2)
    ct2 = jnp.concatenate([jnp.swapaxes(c_re, -1, -2), jnp.swapaxes(c_im, -1, -2)], axis=2)
    bt1 = jnp.concatenate([jnp.swapaxes(b_re, -1, -2), jnp.swapaxes(b_im, -1, -2)], axis=3)
    return prow, ct2, bt1


def kernel(x, c, ctx, c_ctx, norm1_g, norm2_g, w_ada, b_ada, w_in, ssm_lam_re, ssm_lam_im, ssm_b_re, ssm_b_im, ssm_c_re, ssm_c_im, ssm_log_dt, ssm_d, ssm_w_glu, conv_w, conv_b, mix_norm_g, w_out, router_w, router_bias, exp_w_gate, exp_w_up, exp_w_down, shared_w_gate, shared_w_up, shared_w_down, final_g):
    layer = 0
    x2 = x.reshape(N_TOK, D_MODEL)
    ctx2 = ctx.reshape(N_CTX_TOK, D_MODEL)

    c8 = jnp.concatenate([c, c_ctx[None, :], jnp.zeros((8 - BATCH - 1, D_MODEL), _F32)], axis=0)
    mod = _ada_mod(c8, w_ada[layer], b_ada[layer][None, :])
    mod3 = mod.reshape(8, N_MOD, D_MODEL)

    w_in_bf = w_in[layer].astype(_BF16)
    conv_w8 = jnp.concatenate([conv_w[layer], jnp.zeros((8 - conv_w.shape[1], D_CONV), _F32)], axis=0)
    u_x, conv_x = _inproj(x2, mod3, norm1_g[layer][None, :], w_in_bf, conv_w8, conv_b[layer][None, :])
    u_c = _ctxproj(ctx2, mod3, norm1_g[layer][None, :], w_in_bf)

    prow, ct2, bt1 = _s5_param_layouts(ssm_lam_re[layer], ssm_lam_im[layer], ssm_b_re[layer],
                                       ssm_b_im[layer], ssm_c_re[layer], ssm_c_im[layer], ssm_log_dt[layer])
    toep, bpow, cpow, dec = _s5_params(prow, ct2, bt1)
    ys = _s5_scan(u_x, u_c, toep, bpow, cpow, dec)

    rw = router_w[layer]
    rw_hi = rw.astype(_BF16)
    rw_lo = (rw - rw_hi.astype(_F32)).astype(_BF16)
    rwt = jnp.concatenate([rw_hi.T, rw_lo.T], axis=0)
    x1, h2, logits_t = _merge(ys, u_x, conv_x, x2, mod3, ssm_d[layer][None, :],
                              mix_norm_g[layer][None, :], norm2_g[layer][None, :],
                              ssm_w_glu[layer].astype(_BF16), w_out[layer].astype(_BF16), rwt)

    bias_col = jnp.broadcast_to(router_bias[layer][:, None], (N_EXPERTS, 128))
    eid, w_k, rank, cnt = _route(logits_t, bias_col)

    counts = cnt[:, 0].astype(jnp.int32)
    padded = (counts + EXPERT_BM - 1) // EXPERT_BM * EXPERT_BM
    pend = jnp.cumsum(padded)
    pstart = pend - padded
    is_e = eid[:, :, None] == jnp.arange(N_EXPERTS, dtype=jnp.int32)
    dest = jnp.sum(jnp.where(is_e, pstart, 0), axis=-1) + rank
    n_active = (pend[-1] // EXPERT_BM).astype(jnp.int32)
    blk = jnp.minimum(jnp.arange(N_BLOCKS, dtype=jnp.int32), n_active - 1)
    ends_before = (pend[None, :] <= (blk * EXPERT_BM)[:, None]).astype(jnp.int32)
    block_e = jnp.minimum(jnp.sum(ends_before, axis=1), N_EXPERTS - 1)

    xmid, codes = _rowmap(dest, pstart + counts, padded - counts, h2, x1, mod3,
                          shared_w_gate[layer].astype(_BF16), shared_w_up[layer].astype(_BF16),
                          shared_w_down[layer].astype(_BF16))
    first = jnp.concatenate([jnp.ones((1,), jnp.int32), (block_e[1:] != block_e[:-1]).astype(jnp.int32)])
    slot = (jnp.cumsum(first) - 1) % 2
    e_ids = jnp.arange(N_EXPERTS, dtype=jnp.int32)
    owner = jnp.where(padded > 0, e_ids, N_EXPERTS)
    later = jnp.min(jnp.where(e_ids[None, :] > e_ids[:, None], owner[None, :], N_EXPERTS), axis=1)
    nxt_e = jnp.where(later == N_EXPERTS, -1, later)
    nxt = jnp.sum(jnp.where(block_e[:, None] == e_ids[None, :], nxt_e[None, :], 0), axis=1)
    ys_rows = _experts(block_e, n_active[None], first, slot, nxt, codes, h2,
                       exp_w_gate[layer], exp_w_up[layer], exp_w_down[layer])
    out = _final(dest, xmid, w_k.T, mod3, final_g[None, :], ys_rows)
    return out.reshape(BATCH, SEQ, D_MODEL)
```

```python
import functools
import math

import jax
import jax.numpy as jnp
from jax import lax
from jax.experimental import pallas as pl
from jax.experimental.pallas import tpu as pltpu

D_MODEL = 2048
BATCH = 4
SEQ = 2048
CTX_LEN = 256
GRID_W = 64
D_SSM = 1024
D_CONV = 1024
SSM_GROUP = 16
SSM_GROUPS = 64
SSM_STATE = 64
N_EXPERTS = 64
N_EXPERT_GROUPS = 8
GROUP_SIZE = N_EXPERTS // N_EXPERT_GROUPS
TOPK_GROUPS = 4
TOP_K = 8
D_EXPERT = 512
D_SHARED = 512
ROUTED_SCALE = 2.5
N_MOD = 6
EPS = 1e-6

N_TOK = BATCH * SEQ
N_CTX_TOK = BATCH * CTX_LEN

CHUNK = 16
CHUNK_W = CHUNK * SSM_GROUP
STATE_W = 2 * SSM_STATE
N_CTX_CHUNKS = CTX_LEN // CHUNK
N_X_CHUNKS = SEQ // CHUNK
N_SEQ_CHUNKS = N_CTX_CHUNKS + N_X_CHUNKS
SCAN_ROWS = N_SEQ_CHUNKS * BATCH
S5_GROUPS_PER_STEP = 8

TM = 256
EXPERT_BM = 256
N_ASSIGN = N_TOK * TOP_K
EXPERT_DEPTH = 4
N_BLOCKS = -(-(N_ASSIGN // EXPERT_BM + N_EXPERTS) // EXPERT_DEPTH) * EXPERT_DEPTH
N_ROWS = N_BLOCKS * EXPERT_BM
TM_COMBINE = 128
VMEM_LIMIT = 56 * 1024 * 1024

_F32 = jnp.float32
_BF16 = jnp.bfloat16


def _cparams(*sem):
    return pltpu.CompilerParams(dimension_semantics=sem, vmem_limit_bytes=VMEM_LIMIT)


def _const_spec(shape):
    nd = len(shape)
    return pl.BlockSpec(shape, lambda *_: (0,) * nd, pipeline_mode=pl.Buffered(1))


def _rms_scale(xf):
    return lax.rsqrt(jnp.mean(xf * xf, axis=-1, keepdims=True) + EPS)


def _silu(x):
    return x * jax.nn.sigmoid(x)


def _ada_kernel(c_ref, w_ref, b_ref, o_ref):
    s = _silu(c_ref[...])
    o_ref[...] = jnp.dot(s, w_ref[...], preferred_element_type=_F32) + b_ref[...]


def _ada_mod(c8, w_ada, b_ada):
    n = w_ada.shape[1]
    tn = 1024
    return pl.pallas_call(
        _ada_kernel,
        grid=(n // tn,),
        in_specs=[pl.BlockSpec((8, D_MODEL), lambda j: (0, 0)),
                  pl.BlockSpec((D_MODEL, tn), lambda j: (0, j)),
                  pl.BlockSpec((1, tn), lambda j: (0, j))],
        out_specs=pl.BlockSpec((8, tn), lambda j: (0, j)),
        out_shape=jax.ShapeDtypeStruct((8, n), _F32),
        compiler_params=_cparams("arbitrary"),
        name="ada_mod",
    )(c8, w_ada, b_ada)


def _modulated_norm(x_ref, mod_ref, g_ref, shift_row, scale_row):
    xf = x_ref[...]
    m = mod_ref[0]
    h = xf * _rms_scale(xf) * g_ref[...]
    return h * (1.0 + m[scale_row:scale_row + 1]) + m[shift_row:shift_row + 1]


def _inproj_kernel(x_ref, mod_ref, g_ref, w_ref, cw_ref, cb_ref, u_ref, conv_ref):
    h = _modulated_norm(x_ref, mod_ref, g_ref, 0, 1).astype(_BF16)
    u_ref[...] = jnp.dot(h, w_ref[:, 0:D_SSM], preferred_element_type=_F32)
    tm = x_ref.shape[0]
    pos = lax.broadcasted_iota(jnp.int32, (tm, 1), 0) % GRID_W
    not_first = (pos != 0).astype(_F32)
    not_last = (pos != GRID_W - 1).astype(_F32)
    cw = cw_ref[...]
    nc = 256
    for j in range(D_CONV // nc):
        lo = j * nc
        bg = jnp.dot(h, w_ref[:, D_SSM + lo:D_SSM + lo + nc], preferred_element_type=_F32)
        cg = jnp.dot(h, w_ref[:, D_SSM + D_CONV + lo:D_SSM + D_CONV + lo + nc], preferred_element_type=_F32)
        v = jnp.dot(h, w_ref[:, D_SSM + 2 * D_CONV + lo:D_SSM + 2 * D_CONV + lo + nc],
                    preferred_element_type=_F32)
        z = cg * v
        z_prev = pltpu.roll(z, 1, axis=0) * not_first
        z_next = pltpu.roll(z, tm - 1, axis=0) * not_last
        y = (cb_ref[:, lo:lo + nc] + z_prev * cw[0:1, lo:lo + nc] + z * cw[1:2, lo:lo + nc]
             + z_next * cw[2:3, lo:lo + nc])
        conv_ref[:, lo:lo + nc] = bg * y


def _inproj(x2, mod3, norm_g, w_in_bf, conv_w, conv_b):
    d_in = w_in_bf.shape[1]
    tiles_per_batch = SEQ // TM
    return pl.pallas_call(
        _inproj_kernel,
        grid=(N_TOK // TM,),
        in_specs=[pl.BlockSpec((TM, D_MODEL), lambda i: (i, 0)),
                  pl.BlockSpec((1, N_MOD, D_MODEL), lambda i: (i // tiles_per_batch, 0, 0)),
                  _const_spec((1, D_MODEL)),
                  _const_spec((D_MODEL, d_in)),
                  _const_spec((8, D_CONV)),
                  _const_spec((1, D_CONV))],
        out_specs=[pl.BlockSpec((TM, D_SSM), lambda i: (i, 0)),
                   pl.BlockSpec((TM, D_CONV), lambda i: (i, 0))],
        out_shape=[jax.ShapeDtypeStruct((N_TOK, D_SSM), _F32),
                   jax.ShapeDtypeStruct((N_TOK, D_CONV), _F32)],
        compiler_params=_cparams("arbitrary"),
        name="in_proj",
    )(x2, mod3, norm_g, w_in_bf, conv_w, conv_b)


def _ctxproj_kernel(x_ref, mod_ref, g_ref, w_ref, u_ref):
    h = _modulated_norm(x_ref, mod_ref, g_ref, 0, 1).astype(_BF16)
    u_ref[...] = jnp.dot(h, w_ref[...], preferred_element_type=_F32)


def _ctxproj(ctx2, mod3, norm_g, w_u_bf):
    return pl.pallas_call(
        _ctxproj_kernel,
        grid=(N_CTX_TOK // TM,),
        in_specs=[pl.BlockSpec((TM, D_MODEL), lambda i: (i, 0)),
                  pl.BlockSpec((1, N_MOD, D_MODEL), lambda i: (BATCH, 0, 0)),
                  _const_spec((1, D_MODEL)),
                  _const_spec((D_MODEL, D_SSM))],
        out_specs=pl.BlockSpec((TM, D_SSM), lambda i: (i, 0)),
        out_shape=jax.ShapeDtypeStruct((N_CTX_TOK, D_SSM), _F32),
        compiler_params=_cparams("arbitrary"),
        name="ctx_proj",
    )(ctx2, mod3, norm_g, w_u_bf)


N_POW = 32


def _s5_param_kernel(prow_ref, ct_ref, bt_ref, toep_ref, bpow_ref, cpow_ref, dec_ref):
    lane = lambda shape: lax.broadcasted_iota(jnp.int32, shape, 1)
    sub = lambda shape: lax.broadcasted_iota(jnp.int32, shape, 0)

    def split2(a):
        a_hi = a.astype(_BF16)
        return a_hi, (a - a_hi.astype(_F32)).astype(_BF16)

    rep = (lane((SSM_GROUP, CHUNK_W)) % SSM_GROUP == sub((SSM_GROUP, CHUNK_W))).astype(_BF16)
    sgn_col = jnp.where(sub((STATE_W, 1)) < SSM_STATE, 1.0, -1.0).astype(_F32)
    sgn_row = jnp.where(lane((1, STATE_W)) < SSM_STATE, -1.0, 1.0).astype(_F32)
    tau_col = sub((N_POW, 1)).astype(_F32)
    blk_l = lane((N_POW, CHUNK_W)) // SSM_GROUP
    tau_s = sub((N_POW, CHUNK_W))
    blk_r = sub((CHUNK_W, N_POW)) // SSM_GROUP
    tau_l = lane((CHUNK_W, N_POW))
    pick_l = lambda e: (tau_s == e).astype(_BF16)
    pick_r = lambda e: (tau_l == e).astype(_BF16)
    tn = (((0,), (0,)), ((), ()))

    strips = []
    for d in range(2):
        pr = prow_ref[d, 0]
        lam_re, lam_im, dt = pr[0:1], pr[1:2], jnp.exp(pr[2:3])
        mag = jnp.exp(tau_col * (lam_re * dt))
        ang = tau_col * (lam_im * dt)
        pw_re = mag * jnp.cos(ang)
        pw_im = mag * jnp.sin(ang)
        pw4 = jnp.concatenate(split2(pw_re) + split2(pw_im), axis=1)

        c_hi, c_lo = split2(ct_ref[d, 0])
        ct = (jnp.dot(c_hi, rep, preferred_element_type=_F32)
              + jnp.dot(c_lo, rep, preferred_element_type=_F32))
        ca = ct * sgn_col
        cb = -pltpu.roll(ct, SSM_STATE, axis=0)

        def cpow(sel):
            o = lax.dot_general(pw4, sel, tn, preferred_element_type=_F32)
            x = o[0:STATE_W] + o[STATE_W:2 * STATE_W]
            y = o[2 * STATE_W:3 * STATE_W] + o[3 * STATE_W:4 * STATE_W]
            return x * ca + y * cb

        nr = pw_re[1:2] - 1.0
        ni = pw_im[1:2]
        den = lam_re * lam_re + lam_im * lam_im
        kr = (nr * lam_re + ni * lam_im) / den
        ki = (ni * lam_re - nr * lam_im) / den
        b1 = bt_ref[d, 0]
        b2 = pltpu.roll(b1, SSM_STATE, axis=1) * sgn_row
        u1 = kr * b1 + ki * b2
        u2 = kr * b2 - ki * b1
        u1t = jnp.concatenate([u1] * CHUNK, axis=0)
        u2t = jnp.concatenate([u2] * CHUNK, axis=0)

        def bpow(sel):
            o = jnp.dot(sel, pw4, preferred_element_type=_F32)
            x = o[:, 0:STATE_W] + o[:, STATE_W:2 * STATE_W]
            y = o[:, 2 * STATE_W:3 * STATE_W] + o[:, 3 * STATE_W:4 * STATE_W]
            return x * u1t + y * u2t

        if d == 0:
            e_strip, e_b, e_c = blk_l, (CHUNK - 1) - blk_r, blk_l + 1
        else:
            e_strip, e_b, e_c = (CHUNK - 1) - blk_l, blk_r, CHUNK - blk_l
        u_hi, u_lo = split2(u1)
        k_hi, k_lo = split2(cpow(pick_l(e_strip)))
        strips.append(jnp.dot(u_hi, k_hi, preferred_element_type=_F32)
                      + jnp.dot(u_hi, k_lo, preferred_element_type=_F32)
                      + jnp.dot(u_lo, k_hi, preferred_element_type=_F32))
        bp = bpow(pick_r(e_b))
        bpow_ref[0, :, d * STATE_W:(d + 1) * STATE_W] = bp.astype(_BF16)
        bpow_ref[0, :, (2 + d) * STATE_W:(3 + d) * STATE_W] = pltpu.roll(bp, SSM_STATE, axis=1).astype(_BF16)
        cpow_ref[0, d * STATE_W:(d + 1) * STATE_W, :] = cpow(pick_l(e_c)).astype(_BF16)

        dec_ref[0, 2 * d:2 * d + 1, :] = pw_re[CHUNK:CHUNK + 1]
        dec_ref[0, 2 * d + 1:2 * d + 2, :] = pw_im[CHUNK:CHUNK + 1] * sgn_row

    zeros = jnp.zeros((SSM_GROUP, CHUNK_W), _F32)
    strip = (jnp.concatenate([strips[1], zeros], axis=1)
             + pltpu.roll(jnp.concatenate([strips[0], zeros], axis=1), CHUNK_W - SSM_GROUP, axis=1))
    for i in range(CHUNK):
        off = (CHUNK - 1 - i) * SSM_GROUP
        win = strip if off == 0 else pltpu.roll(strip, 2 * CHUNK_W - off, axis=1)
        toep_ref[0, i * SSM_GROUP:(i + 1) * SSM_GROUP, :] = win[:, 0:CHUNK_W].astype(_BF16)


def _s5_params(prow, ct2, bt1):
    g_spec = lambda shape: pl.BlockSpec(shape, lambda g: (0, g, 0, 0))
    o_spec = lambda shape: pl.BlockSpec(shape, lambda g: (g, 0, 0))
    return pl.pallas_call(
        _s5_param_kernel,
        grid=(SSM_GROUPS,),
        in_specs=[g_spec((2, 1, 8, STATE_W)),
                  g_spec((2, 1, STATE_W, SSM_GROUP)), g_spec((2, 1, SSM_GROUP, STATE_W))],
        out_specs=[o_spec((1, CHUNK_W, CHUNK_W)), o_spec((1, CHUNK_W, 4 * STATE_W)),
                   o_spec((1, 2 * STATE_W, CHUNK_W)), o_spec((1, 4, STATE_W))],
        out_shape=[jax.ShapeDtypeStruct((SSM_GROUPS, CHUNK_W, CHUNK_W), _BF16),
                   jax.ShapeDtypeStruct((SSM_GROUPS, CHUNK_W, 4 * STATE_W), _BF16),
                   jax.ShapeDtypeStruct((SSM_GROUPS, 2 * STATE_W, CHUNK_W), _BF16),
                   jax.ShapeDtypeStruct((SSM_GROUPS, 4, STATE_W), _F32)],
        compiler_params=_cparams("arbitrary"),
        name="s5_params",
    )(prow, ct2, bt1)


def _s5_kernel(ux_ref, uc_ref, toep_ref, bpow_ref, cpow_ref, dec_ref, y_ref, u_ref, s_ref, h_ref, y2_ref):
    ng = S5_GROUPS_PER_STEP
    gw = SSM_GROUP
    x0 = N_CTX_CHUNKS * BATCH

    half = CHUNK // 2
    assert half == ng
    lanes = ng * 128

    r = lax.broadcasted_iota(jnp.int32, (lanes, 1), 0)
    swapped = ((r // gw) % ng) * 128 + (r // 128) * gw + r % gw
    perm = (lax.broadcasted_iota(jnp.int32, (lanes, lanes), 1) == swapped).astype(_BF16)

    def to_chunk_layout(src_ref, seq_len, n_chunks, row0):
        for hf in range(2):
            p = jnp.concatenate(
                [jnp.concatenate([src_ref[pl.ds(b * seq_len + hf * half + il, n_chunks, stride=CHUNK), :]
                                  for il in range(half)], axis=1) for b in range(BATCH)], axis=0).astype(_BF16)
            q = jnp.dot(p, perm, preferred_element_type=_F32)
            for b in range(BATCH):
                for g in range(ng):
                    u_ref[g, hf, pl.ds(row0 + b, n_chunks, stride=BATCH), :] = (
                        q[b * n_chunks:(b + 1) * n_chunks, g * 128:(g + 1) * 128])

    to_chunk_layout(uc_ref, CTX_LEN, N_CTX_CHUNKS, 0)
    to_chunk_layout(ux_ref, SEQ, N_X_CHUNKS, x0)

    chunk_rows = lambda ref, g, lo: jnp.concatenate([ref[g, 0, lo:, :], ref[g, 1, lo:, :]], axis=1)
    for g in range(ng):
        s_ref[g] = jnp.dot(chunk_rows(u_ref, g, 0).astype(_BF16), bpow_ref[g], preferred_element_type=_F32)

    pair = 2 * BATCH
    n_ctx_pairs = N_CTX_CHUNKS // 2
    n_pairs = N_SEQ_CHUNKS // 2
    lower = lax.broadcasted_iota(jnp.int32, (pair, STATE_W), 0) < BATCH
    zero = jnp.zeros((pair, STATE_W), _F32)

    def step(t, carry):
        tb = jnp.where(t < n_ctx_pairs, n_ctx_pairs - 1 - t, n_pairs + n_ctx_pairs - 1 - t)
        rf = pl.multiple_of(t * pair, pair)
        rb = pl.multiple_of(tb * pair, pair)
        out = []
        swap = lambda v: pltpu.roll(v, BATCH, axis=0)
        for g in range(ng):
            hf, hfs, hb, hbs = carry[4 * g:4 * g + 4]
            dg = dec_ref[g]

            def advance(h, hs, a_re, a_im, s, ss):
                return a_re * h + a_im * hs + s, a_re * hs - a_im * h + ss

            col = lambda r, k: s_ref[g, pl.ds(r, pair), k * STATE_W:(k + 1) * STATE_W]
            sf, sfs = col(rf, 0), col(rf, 2)
            mid, mids = advance(hf, hfs, dg[0:1], dg[1:2], jnp.where(lower, sf, swap(sf)),
                                jnp.where(lower, sfs, swap(sfs)))
            h_ref[g, pl.ds(rf, pair), 0:STATE_W] = jnp.where(lower, hf, mid)
            end, ends = advance(mid, mids, dg[0:1], dg[1:2], sf, sfs)
            hf, hfs = jnp.where(lower, swap(end), end), jnp.where(lower, swap(ends), ends)

            sb, sbs = col(rb, 1), col(rb, 3)
            mid, mids = advance(hb, hbs, dg[2:3], dg[3:4], jnp.where(lower, swap(sb), sb),
                                jnp.where(lower, swap(sbs), sbs))
            h_ref[g, pl.ds(rb, pair), STATE_W:2 * STATE_W] = jnp.where(lower, mid, hb)
            end, ends = advance(mid, mids, dg[2:3], dg[3:4], sb, sbs)
            hb, hbs = jnp.where(lower, end, swap(end)), jnp.where(lower, ends, swap(ends))
            out += [hf, hfs, hb, hbs]
        return tuple(out)

    lax.fori_loop(0, n_pairs, step, (zero,) * (4 * ng))

    for g in range(ng):
        ux = chunk_rows(u_ref, g, x0).astype(_BF16)
        hx = h_ref[g, x0:SCAN_ROWS, :].astype(_BF16)
        y2 = (jnp.dot(ux, toep_ref[g], preferred_element_type=_F32)
              + jnp.dot(hx, cpow_ref[g], preferred_element_type=_F32))
        y2_ref[g, 0] = y2[:, 0:128]
        y2_ref[g, 1] = y2[:, 128:256]

    for hf in range(2):
        yin = jnp.concatenate(
            [jnp.concatenate([y2_ref[g, hf, pl.ds(b, N_X_CHUNKS, stride=BATCH), :] for g in range(ng)], axis=1)
             for b in range(BATCH)], axis=0)
        y_hi = yin.astype(_BF16)
        y_lo = (yin - y_hi.astype(_F32)).astype(_BF16)
        q = (jnp.dot(y_hi, perm, preferred_element_type=_F32)
             + jnp.dot(y_lo, perm, preferred_element_type=_F32))
        for b in range(BATCH):
            for jl in range(half):
                y_ref[pl.ds(b * SEQ + hf * half + jl, N_X_CHUNKS, stride=CHUNK), :] = (
                    q[b * N_X_CHUNKS:(b + 1) * N_X_CHUNKS, jl * 128:(jl + 1) * 128])


def _s5_scan(u_x, u_c, toep, bpow, cpow, dec):
    ng = S5_GROUPS_PER_STEP
    n_x_rows = N_X_CHUNKS * BATCH
    lanes = ng * SSM_GROUP
    spec = lambda r, c: pl.BlockSpec((ng, r, c), lambda i: (i, 0, 0))
    col = lambda rows: pl.BlockSpec((rows, lanes), lambda i: (0, i))
    return pl.pallas_call(
        _s5_kernel,
        grid=(SSM_GROUPS // ng,),
        in_specs=[col(N_TOK), col(N_CTX_TOK), spec(CHUNK_W, CHUNK_W), spec(CHUNK_W, 4 * STATE_W),
                  spec(2 * STATE_W, CHUNK_W), spec(4, STATE_W)],
        out_specs=col(N_TOK),
        out_shape=jax.ShapeDtypeStruct((N_TOK, D_SSM), _F32),
        scratch_shapes=[pltpu.VMEM((ng, 2, SCAN_ROWS, 128), _F32),
                        pltpu.VMEM((ng, SCAN_ROWS, 4 * STATE_W), _F32),
                        pltpu.VMEM((ng, SCAN_ROWS, 2 * STATE_W), _F32),
                        pltpu.VMEM((ng, 2, n_x_rows, 128), _F32)],
        compiler_params=_cparams("arbitrary"),
        name="s5_scan",
    )(u_x, u_c, toep, bpow, cpow, dec)


def _merge_kernel(ys_ref, u_ref, conv_ref, x_ref, mod_ref, dskip_ref, mixg_ref, n2g_ref,
                  wglu_ref, wout_ref, rwt_ref, x1_ref, h2_ref, logit_ref):
    m = mod_ref[0]
    yx = dskip_ref[...] * u_ref[...] + ys_ref[...]
    c0 = math.sqrt(2.0 / math.pi)
    ge = 0.5 * yx * (1.0 + jnp.tanh(c0 * (yx + 0.044715 * (yx * yx * yx))))
    z = jnp.dot(ge.astype(_BF16), wglu_ref[...], preferred_element_type=_F32)
    ssm_y = z[:, 0:D_SSM] * jax.nn.sigmoid(z[:, D_SSM:2 * D_SSM])
    conv_y = conv_ref[...]
    mixg = mixg_ref[...]
    heads_a = (ssm_y * _rms_scale(ssm_y) * mixg[:, 0:D_SSM]).astype(_BF16)
    heads_b = (conv_y * _rms_scale(conv_y) * mixg[:, D_SSM:]).astype(_BF16)
    mix = (jnp.dot(heads_a, wout_ref[0:D_SSM, :], preferred_element_type=_F32)
           + jnp.dot(heads_b, wout_ref[D_SSM:, :], preferred_element_type=_F32))
    x1 = x_ref[...] + m[2:3] * mix
    x1_ref[...] = x1
    h2 = x1 * _rms_scale(x1) * n2g_ref[...] * (1.0 + m[4:5]) + m[3:4]
    h2_ref[...] = h2
    h_hi = h2.astype(_BF16)
    h_lo = (h2 - h_hi.astype(_F32)).astype(_BF16)
    nt = (((1,), (1,)), ((), ()))
    p = lax.dot_general(rwt_ref[...], h_hi, nt, preferred_element_type=_F32)
    q = lax.dot_general(rwt_ref[0:N_EXPERTS, :], h_lo, nt, preferred_element_type=_F32)
    logit_ref[...] = p[0:N_EXPERTS] + p[N_EXPERTS:2 * N_EXPERTS] + q


def _merge(ys, u, conv, x2, mod3, dskip, mixg, n2g, wglu_bf, wout_bf, rwt):
    tiles_per_batch = SEQ // TM
    tok = lambda w: pl.BlockSpec((TM, w), lambda i: (i, 0))
    return pl.pallas_call(
        _merge_kernel,
        grid=(N_TOK // TM,),
        in_specs=[tok(D_SSM), tok(D_SSM), tok(D_CONV), tok(D_MODEL),
                  pl.BlockSpec((1, N_MOD, D_MODEL), lambda i: (i // tiles_per_batch, 0, 0)),
                  _const_spec((1, D_SSM)), _const_spec((1, D_MODEL)), _const_spec((1, D_MODEL)),
                  _const_spec((D_SSM, 2 * D_SSM)), _const_spec((D_MODEL, D_MODEL)),
                  _const_spec((2 * N_EXPERTS, D_MODEL))],
        out_specs=[tok(D_MODEL), tok(D_MODEL),
                   pl.BlockSpec((N_EXPERTS, TM), lambda i: (0, i))],
        out_shape=[jax.ShapeDtypeStruct((N_TOK, D_MODEL), _F32),
                   jax.ShapeDtypeStruct((N_TOK, D_MODEL), _F32),
                   jax.ShapeDtypeStruct((N_EXPERTS, N_TOK), _F32)],
        compiler_params=_cparams("arbitrary"),
        name="merge_heads",
    )(ys, u, conv, x2, mod3, dskip, mixg, n2g, wglu_bf, wout_bf, rwt)


def _route_kernel(logit_ref, bias_ref, eid_ref, w_ref, rank_ref, cnt_ref, carry_ref):
    @pl.when(pl.program_id(0) == 0)
    def _():
        carry_ref[...] = jnp.zeros_like(carry_ref)

    tm = logit_ref.shape[1]
    neg = jnp.float32(-jnp.inf)
    scores = jax.nn.sigmoid(logit_ref[...])
    biased = scores + bias_ref[:, 0:1]
    sub = lax.broadcasted_iota(jnp.int32, (GROUP_SIZE, tm), 0)
    rows = lambda a, g: a[g * GROUP_SIZE:(g + 1) * GROUP_SIZE]
    ngrp = N_EXPERT_GROUPS

    gscore = []
    for g in range(ngrp):
        bg = rows(biased, g)
        m1 = jnp.max(bg, axis=0, keepdims=True)
        first = jnp.min(jnp.where(bg == m1, sub, GROUP_SIZE), axis=0, keepdims=True)
        m2 = jnp.max(jnp.where(sub == first, neg, bg), axis=0, keepdims=True)
        gscore.append(m1 + m2)
    v = []
    for g in range(ngrp):
        beaten = jnp.zeros((1, tm), jnp.int32)
        for o in range(ngrp):
            if o != g:
                beats = (gscore[o] >= gscore[g]) if o < g else (gscore[o] > gscore[g])
                beaten = beaten + beats.astype(jnp.int32)
        v.append(jnp.where(beaten < TOPK_GROUPS, rows(biased, g), neg))
    eids = [sub + g * GROUP_SIZE for g in range(ngrp)]

    pick_ids, pick_masks = [], []
    for _ in range(TOP_K):
        m = functools.reduce(jnp.maximum, v)
        m = jnp.max(m, axis=0, keepdims=True)
        cand = functools.reduce(jnp.minimum, [jnp.where(v[g] == m, eids[g], N_EXPERTS) for g in range(ngrp)])
        pick_id = jnp.min(cand, axis=0, keepdims=True)
        masks = [eids[g] == pick_id for g in range(ngrp)]
        v = [jnp.where(masks[g], neg, v[g]) for g in range(ngrp)]
        pick_ids.append(pick_id)
        pick_masks.append(masks)

    sel = jnp.concatenate(
        [functools.reduce(jnp.logical_or, [pick_masks[k][g] for k in range(TOP_K)]).astype(_F32)
         for g in range(ngrp)], axis=0)
    before = (lax.broadcasted_iota(jnp.int32, (tm, tm), 0)
              < lax.broadcasted_iota(jnp.int32, (tm, tm), 1)).astype(_BF16)
    base = jnp.dot(sel.astype(_BF16), before, preferred_element_type=_F32) + carry_ref[:, 0:1]

    def gather_pick(a, k):
        parts = [jnp.sum(jnp.where(pick_masks[k][g], rows(a, g), 0.0), axis=0, keepdims=True)
                 for g in range(ngrp)]
        return functools.reduce(jnp.add, parts)

    picked = [gather_pick(scores, k) for k in range(TOP_K)]
    denom = functools.reduce(jnp.add, picked)
    for k in range(TOP_K):
        eid_ref[k:k + 1, :] = pick_ids[k]
        w_ref[k:k + 1, :] = picked[k] / denom * ROUTED_SCALE
        rank_ref[k:k + 1, :] = gather_pick(base, k).astype(jnp.int32)
    carry_ref[...] = carry_ref[...] + jnp.sum(sel, axis=1, keepdims=True)
    cnt_ref[...] = carry_ref[...]


def _route(logits_t, bias_col):
    tok = pl.BlockSpec((TOP_K, TM), lambda i: (0, i))
    return pl.pallas_call(
        _route_kernel,
        grid=(N_TOK // TM,),
        in_specs=[pl.BlockSpec((N_EXPERTS, TM), lambda i: (0, i)), _const_spec((N_EXPERTS, 128))],
        out_specs=[tok, tok, tok, pl.BlockSpec((N_EXPERTS, 128), lambda i: (0, 0))],
        out_shape=[jax.ShapeDtypeStruct((TOP_K, N_TOK), jnp.int32),
                   jax.ShapeDtypeStruct((TOP_K, N_TOK), _F32),
                   jax.ShapeDtypeStruct((TOP_K, N_TOK), jnp.int32),
                   jax.ShapeDtypeStruct((N_EXPERTS, 128), _F32)],
        scratch_shapes=[pltpu.VMEM((N_EXPERTS, 128), _F32)],
        compiler_params=_cparams("arbitrary"),
        name="route",
    )(logits_t, bias_col)


def _rowmap_kernel(dest_ref, padstart_ref, padn_ref, h_ref, x1_ref, mod_ref, wsg_ref, wsu_ref, wsd_ref,
                   xmid_ref, codes_ref):
    i = pl.program_id(0)
    tm = h_ref.shape[0]
    base = i * tm

    @pl.when(i == 0)
    def _():
        def fill(e, done):
            def one(r, c):
                codes_ref[padstart_ref[e] + r] = N_ASSIGN + done + r
                return c
            lax.fori_loop(0, padn_ref[e], one, 0)
            return done + padn_ref[e]
        lax.fori_loop(0, N_EXPERTS, fill, 0)
        for r in range(EXPERT_BM):
            codes_ref[N_ROWS + r] = N_ROWS + r

    for t in range(tm):
        ds = [dest_ref[k * N_TOK + base + t] for k in range(TOP_K)]
        for k in range(TOP_K):
            codes_ref[ds[k]] = k * N_TOK + base + t

    h = h_ref[...].astype(_BF16)
    a = jnp.dot(h, wsg_ref[...], preferred_element_type=_F32)
    b = jnp.dot(h, wsu_ref[...], preferred_element_type=_F32)
    shared = jnp.dot((_silu(a) * b).astype(_BF16), wsd_ref[...], preferred_element_type=_F32)
    xmid_ref[...] = x1_ref[...] + mod_ref[0][5:6] * shared


def _rowmap(dest, pad_start, pad_n, h2, x1, mod3, wsg_bf, wsu_bf, wsd_bf):
    tm = TM
    tiles_per_batch = SEQ // tm
    tok = pl.BlockSpec((tm, D_MODEL), lambda i, *_: (i, 0))
    const = lambda shape: pl.BlockSpec(shape, lambda i, *_: (0,) * len(shape), pipeline_mode=pl.Buffered(1))
    grid_spec = pltpu.PrefetchScalarGridSpec(
        num_scalar_prefetch=3,
        grid=(N_TOK // tm,),
        in_specs=[tok, tok,
                  pl.BlockSpec((1, N_MOD, D_MODEL), lambda i, *_: (i // tiles_per_batch, 0, 0)),
                  const((D_MODEL, D_SHARED)), const((D_MODEL, D_SHARED)), const((D_SHARED, D_MODEL))],
        out_specs=[tok, pl.BlockSpec(memory_space=pltpu.SMEM)],
    )
    return pl.pallas_call(
        _rowmap_kernel,
        grid_spec=grid_spec,
        out_shape=[jax.ShapeDtypeStruct((N_TOK, D_MODEL), _F32),
                   jax.ShapeDtypeStruct((N_ROWS + EXPERT_BM,), jnp.int32)],
        compiler_params=_cparams("arbitrary"),
        name="rowmap",
    )(dest.reshape(N_ASSIGN), pad_start, pad_n, h2, x1, mod3, wsg_bf, wsu_bf, wsd_bf)


def _experts_kernel(be_ref, nact_ref, first_ref, slot_ref, nxt_ref, codes_ref, h_hbm, wg_hbm, wu_hbm, wd_hbm,
                    out_hbm, *scratch):
    depth = EXPERT_DEPTH
    xbuf, ybuf = scratch[:depth], scratch[depth:2 * depth]
    wg_f, wu_f, wd_f, wg_bf, wu_bf, wd_bf, sem, gsem, ssem = scratch[2 * depth:]
    n_act = nact_ref[0]
    bm = EXPERT_BM

    def fetch(e, s):
        return (pltpu.make_async_copy(wg_hbm.at[e], wg_f.at[s], sem.at[s, 0]),
                pltpu.make_async_copy(wu_hbm.at[e], wu_f.at[s], sem.at[s, 1]),
                pltpu.make_async_copy(wd_hbm.at[e], wd_f.at[s], sem.at[s, 2]))

    def gather(blk, p):
        for r in range(bm):
            tok = codes_ref[blk * bm + r] & (N_TOK - 1)
            pltpu.make_async_copy(h_hbm.at[pl.ds(tok, 1), :], xbuf[p].at[pl.ds(r, 1), :],
                                  gsem.at[p]).start(priority=r % 2)

    def gather_wait(p):
        pltpu.make_async_copy(h_hbm.at[pl.ds(0, bm), :], xbuf[p], gsem.at[p]).wait()

    def scatter(blk, p):
        for r in range(bm):
            pltpu.make_async_copy(ybuf[p].at[pl.ds(r, 1), :], out_hbm.at[pl.ds(codes_ref[blk * bm + r], 1), :],
                                  ssem.at[p]).start(priority=r % 2)

    def scatter_wait(p):
        pltpu.make_async_copy(ybuf[p], out_hbm.at[pl.ds(0, bm), :], ssem.at[p]).wait()

    def block(blk, p):
        prev = (p + depth - 1) % depth

        @pl.when(blk < n_act)
        def _():
            @pl.when(blk == 0)
            def _():
                for cp in fetch(be_ref[0], 0):
                    cp.start()
                for j in range(depth - 1):
                    gather(jnp.minimum(j, n_act - 1), j)
                ybuf[depth - 1][...] = jnp.zeros_like(ybuf[depth - 1])

            @pl.when(first_ref[blk] == 1)
            def _():
                s = slot_ref[blk]
                for cp in fetch(be_ref[blk], s):
                    cp.wait()

                @pl.when(nxt_ref[blk] >= 0)
                def _():
                    for cp in fetch(nxt_ref[blk], 1 - s):
                        cp.start()

                wg_bf[...] = wg_f[s].astype(_BF16)
                wu_bf[...] = wu_f[s].astype(_BF16)
                wd_bf[...] = wd_f[s].astype(_BF16)

            gather_wait(p)

            @pl.when(blk >= depth - 1)
            def _():
                scatter_wait(p)

            gather(jnp.minimum(blk + depth - 1, n_act - 1), prev)
            scatter(jnp.where(blk == 0, N_BLOCKS, blk - 1), prev)
            x = xbuf[p][...].astype(_BF16)
            a = jnp.dot(x, wg_bf[...], preferred_element_type=_F32)
            b = jnp.dot(x, wu_bf[...], preferred_element_type=_F32)
            ybuf[p][...] = jnp.dot((_silu(a) * b).astype(_BF16), wd_bf[...], preferred_element_type=_F32)

            @pl.when(blk == n_act - 1)
            def _():
                scatter(blk, p)
                for j in range(1, depth):
                    gather_wait((p + j) % depth)
                for j in range(depth - 1):
                    pl.when(blk >= j)(functools.partial(scatter_wait, (p + depth - 1 - j) % depth))
                scatter_wait(p)

    i = pl.program_id(0)
    for q in range(depth):
        block(depth * i + q, q)


def _experts(block_e, n_active, first, slot, nxt, codes, h2, w_gate, w_up, w_down):
    bm = EXPERT_BM
    depth = EXPERT_DEPTH
    hbm = pl.BlockSpec(memory_space=pl.ANY)
    grid_spec = pltpu.PrefetchScalarGridSpec(
        num_scalar_prefetch=6,
        grid=(N_BLOCKS // depth,),
        in_specs=[hbm, hbm, hbm, hbm],
        out_specs=hbm,
        scratch_shapes=[pltpu.VMEM((bm, D_MODEL), _F32)] * (2 * depth) + [
                        pltpu.VMEM((2, D_MODEL, D_EXPERT), _F32), pltpu.VMEM((2, D_MODEL, D_EXPERT), _F32),
                        pltpu.VMEM((2, D_EXPERT, D_MODEL), _F32),
                        pltpu.VMEM((D_MODEL, D_EXPERT), _BF16), pltpu.VMEM((D_MODEL, D_EXPERT), _BF16),
                        pltpu.VMEM((D_EXPERT, D_MODEL), _BF16),
                        pltpu.SemaphoreType.DMA((2, 3)), pltpu.SemaphoreType.DMA((depth,)),
                        pltpu.SemaphoreType.DMA((depth,))],
    )
    return pl.pallas_call(
        _experts_kernel,
        grid_spec=grid_spec,
        out_shape=jax.ShapeDtypeStruct((N_ROWS + bm, D_MODEL), _F32),
        compiler_params=_cparams("arbitrary"),
        name="experts",
    )(block_e, n_active, first, slot, nxt, codes, h2, w_gate, w_up, w_down)


def _final_kernel(xmid_ref, wt_ref, mod_ref, fg_ref, *refs):
    picks, o_ref = refs[:TOP_K], refs[TOP_K]
    wt = wt_ref[...]
    routed = wt[:, 0:1] * picks[0][...]
    for k in range(1, TOP_K):
        routed = routed + wt[:, k:k + 1] * picks[k][...]
    y = xmid_ref[...] + mod_ref[0][5:6] * routed
    o_ref[...] = y * _rms_scale(y) * fg_ref[...]


def _final(xmid, w_tok, mod3, final_g, ys):
    tm = TM_COMBINE
    tiles = N_TOK // tm
    tiles_per_batch = SEQ // tm
    tok = lambda w: pl.BlockSpec((tm, w), lambda i: (i, 0))
    pick = lambda k: pl.BlockSpec((tm, D_MODEL), lambda i: (k * tiles + i, 0))
    return pl.pallas_call(
        _final_kernel,
        grid=(tiles,),
        in_specs=[tok(D_MODEL), tok(TOP_K),
                  pl.BlockSpec((1, N_MOD, D_MODEL), lambda i: (i // tiles_per_batch, 0, 0)),
                  _const_spec((1, D_MODEL))] + [pick(k) for k in range(TOP_K)],
        out_specs=tok(D_MODEL),
        out_shape=jax.ShapeDtypeStruct((N_TOK, D_MODEL), _F32),
        compiler_params=_cparams("arbitrary"),
        name="final",
    )(xmid, w_tok, mod3, final_g, *([ys] * TOP_K))


def _s5_param_layouts(lam_re, lam_im, b_re, b_im, c_re, c_im, log_dt):
    two = lambda a: jnp.concatenate([a, a], axis=-1)
    lr, li = two(lam_re), two(lam_im)
    dtb = jnp.broadcast_to(log_dt.astype(_F32)[:, :, None], lr.shape)
    zeros = jnp.zeros_like(lr)
    prow = jnp.stack([lr, li, dtb] + [zeros] * 5, axis=2)
    ct2 = jnp.concatenate([jnp.swapaxes(c_re, -1, -2), jnp.swapaxes(c_im, -1, -2)], axis=2)
    bt1 = jnp.concatenate([jnp.swapaxes(b_re, -1, -2), jnp.swapaxes(b_im, -1, -2)], axis=3)
    return prow, ct2, bt1


def kernel(x, c, ctx, c_ctx, norm1_g, norm2_g, w_ada, b_ada, w_in, ssm_lam_re, ssm_lam_im, ssm_b_re, ssm_b_im, ssm_c_re, ssm_c_im, ssm_log_dt, ssm_d, ssm_w_glu, conv_w, conv_b, mix_norm_g, w_out, router_w, router_bias, exp_w_gate, exp_w_up, exp_w_down, shared_w_gate, shared_w_up, shared_w_down, final_g):
    layer = 0
    x2 = x.reshape(N_TOK, D_MODEL)
    ctx2 = ctx.reshape(N_CTX_TOK, D_MODEL)

    c8 = jnp.concatenate([c, c_ctx[None, :], jnp.zeros((8 - BATCH - 1, D_MODEL), _F32)], axis=0)
    mod = _ada_mod(c8, w_ada[layer], b_ada[layer][None, :])
    mod3 = mod.reshape(8, N_MOD, D_MODEL)

    w_in_bf = w_in[layer].astype(_BF16)
    conv_w8 = jnp.concatenate([conv_w[layer], jnp.zeros((8 - conv_w.shape[1], D_CONV), _F32)], axis=0)
    u_x, conv_x = _inproj(x2, mod3, norm1_g[layer][None, :], w_in_bf, conv_w8, conv_b[layer][None, :])
    u_c = _ctxproj(ctx2, mod3, norm1_g[layer][None, :], w_in_bf[:, :D_SSM])

    prow, ct2, bt1 = _s5_param_layouts(ssm_lam_re[layer], ssm_lam_im[layer], ssm_b_re[layer],
                                       ssm_b_im[layer], ssm_c_re[layer], ssm_c_im[layer], ssm_log_dt[layer])
    toep, bpow, cpow, dec = _s5_params(prow, ct2, bt1)
    ys = _s5_scan(u_x, u_c, toep, bpow, cpow, dec)

    rw = router_w[layer]
    rw_hi = rw.astype(_BF16)
    rw_lo = (rw - rw_hi.astype(_F32)).astype(_BF16)
    rwt = jnp.concatenate([rw_hi.T, rw_lo.T], axis=0)
    x1, h2, logits_t = _merge(ys, u_x, conv_x, x2, mod3, ssm_d[layer][None, :],
                              mix_norm_g[layer][None, :], norm2_g[layer][None, :],
                              ssm_w_glu[layer].astype(_BF16), w_out[layer].astype(_BF16), rwt)

    bias_col = jnp.broadcast_to(router_bias[layer][:, None], (N_EXPERTS, 128))
    eid, w_k, rank, cnt = _route(logits_t, bias_col)

    counts = cnt[:, 0].astype(jnp.int32)
    padded = (counts + EXPERT_BM - 1) // EXPERT_BM * EXPERT_BM
    pend = jnp.cumsum(padded)
    pstart = pend - padded
    is_e = eid[:, :, None] == jnp.arange(N_EXPERTS, dtype=jnp.int32)
    dest = jnp.sum(jnp.where(is_e, pstart, 0), axis=-1) + rank
    n_active = (pend[-1] // EXPERT_BM).astype(jnp.int32)
    blk = jnp.minimum(jnp.arange(N_BLOCKS, dtype=jnp.int32), n_active - 1)
    ends_before = (pend[None, :] <= (blk * EXPERT_BM)[:, None]).astype(jnp.int32)
    block_e = jnp.minimum(jnp.sum(ends_before, axis=1), N_EXPERTS - 1)

    xmid, codes = _rowmap(dest, pstart + counts, padded - counts, h2, x1, mod3,
                          shared_w_gate[layer].astype(_BF16), shared_w_up[layer].astype(_BF16),
                          shared_w_down[layer].astype(_BF16))
    first = jnp.concatenate([jnp.ones((1,), jnp.int32), (block_e[1:] != block_e[:-1]).astype(jnp.int32)])
    slot = (jnp.cumsum(first) - 1) % 2
    e_ids = jnp.arange(N_EXPERTS, dtype=jnp.int32)
    owner = jnp.where(padded > 0, e_ids, N_EXPERTS)
    later = jnp.min(jnp.where(e_ids[None, :] > e_ids[:, None], owner[None, :], N_EXPERTS), axis=1)
    nxt_e = jnp.where(later == N_EXPERTS, -1, later)
    nxt = jnp.sum(jnp.where(block_e[:, None] == e_ids[None, :], nxt_e[None, :], 0), axis=1)
    ys_rows = _experts(block_e, n_active[None], first, slot, nxt, codes, h2,
                       exp_w_gate[layer], exp_w_up[layer], exp_w_down[layer])
    out = _final(xmid, w_k.T, mod3, final_g[None, :], ys_rows)
    return out.reshape(BATCH, SEQ, D_MODEL)
```

```python
import functools
import math

import jax
import jax.numpy as jnp
from jax import lax
from jax.experimental import pallas as pl
from jax.experimental.pallas import tpu as pltpu

D_MODEL = 2048
BATCH = 4
SEQ = 2048
CTX_LEN = 256
GRID_W = 64
D_SSM = 1024
D_CONV = 1024
SSM_GROUP = 16
SSM_GROUPS = 64
SSM_STATE = 64
N_EXPERTS = 64
N_EXPERT_GROUPS = 8
GROUP_SIZE = N_EXPERTS // N_EXPERT_GROUPS
TOPK_GROUPS = 4
TOP_K = 8
D_EXPERT = 512
D_SHARED = 512
ROUTED_SCALE = 2.5
N_MOD = 6
EPS = 1e-6

N_TOK = BATCH * SEQ
N_CTX_TOK = BATCH * CTX_LEN

CHUNK = 16
CHUNK_W = CHUNK * SSM_GROUP
STATE_W = 2 * SSM_STATE
N_CTX_CHUNKS = CTX_LEN // CHUNK
N_X_CHUNKS = SEQ // CHUNK
N_SEQ_CHUNKS = N_CTX_CHUNKS + N_X_CHUNKS
SCAN_ROWS = N_SEQ_CHUNKS * BATCH
S5_GROUPS_PER_STEP = 8

TM = 256
EXPERT_BM = 256
N_ASSIGN = N_TOK * TOP_K
EXPERT_DEPTH = 4
N_BLOCKS = -(-(N_ASSIGN // EXPERT_BM + N_EXPERTS) // EXPERT_DEPTH) * EXPERT_DEPTH
N_ROWS = N_BLOCKS * EXPERT_BM
TM_COMBINE = 128
VMEM_LIMIT = 56 * 1024 * 1024

_F32 = jnp.float32
_BF16 = jnp.bfloat16


def _cparams(*sem):
    return pltpu.CompilerParams(dimension_semantics=sem, vmem_limit_bytes=VMEM_LIMIT)


def _const_spec(shape):
    nd = len(shape)
    return pl.BlockSpec(shape, lambda *_: (0,) * nd, pipeline_mode=pl.Buffered(1))


def _rms_scale(xf):
    return lax.rsqrt(jnp.mean(xf * xf, axis=-1, keepdims=True) + EPS)


def _silu(x):
    return x * jax.nn.sigmoid(x)


def _ada_kernel(c_ref, w_ref, b_ref, o_ref):
    s = _silu(c_ref[...])
    o_ref[...] = jnp.dot(s, w_ref[...], preferred_element_type=_F32) + b_ref[...]


def _ada_mod(c8, w_ada, b_ada):
    n = w_ada.shape[1]
    tn = 1024
    return pl.pallas_call(
        _ada_kernel,
        grid=(n // tn,),
        in_specs=[pl.BlockSpec((8, D_MODEL), lambda j: (0, 0)),
                  pl.BlockSpec((D_MODEL, tn), lambda j: (0, j)),
                  pl.BlockSpec((1, tn), lambda j: (0, j))],
        out_specs=pl.BlockSpec((8, tn), lambda j: (0, j)),
        out_shape=jax.ShapeDtypeStruct((8, n), _F32),
        compiler_params=_cparams("arbitrary"),
        name="ada_mod",
    )(c8, w_ada, b_ada)


def _modulated_norm(x_ref, mod_ref, g_ref, shift_row, scale_row):
    xf = x_ref[...]
    m = mod_ref[0]
    h = xf * _rms_scale(xf) * g_ref[...]
    return h * (1.0 + m[scale_row:scale_row + 1]) + m[shift_row:shift_row + 1]


def _inproj_kernel(x_ref, mod_ref, g_ref, w_ref, cw_ref, cb_ref, u_ref, conv_ref):
    h = _modulated_norm(x_ref, mod_ref, g_ref, 0, 1).astype(_BF16)
    u_ref[...] = jnp.dot(h, w_ref[:, 0:D_SSM], preferred_element_type=_F32)
    tm = x_ref.shape[0]
    pos = lax.broadcasted_iota(jnp.int32, (tm, 1), 0) % GRID_W
    not_first = (pos != 0).astype(_F32)
    not_last = (pos != GRID_W - 1).astype(_F32)
    cw = cw_ref[...]
    nc = 256
    for j in range(D_CONV // nc):
        lo = j * nc
        bg = jnp.dot(h, w_ref[:, D_SSM + lo:D_SSM + lo + nc], preferred_element_type=_F32)
        cg = jnp.dot(h, w_ref[:, D_SSM + D_CONV + lo:D_SSM + D_CONV + lo + nc], preferred_element_type=_F32)
        v = jnp.dot(h, w_ref[:, D_SSM + 2 * D_CONV + lo:D_SSM + 2 * D_CONV + lo + nc],
                    preferred_element_type=_F32)
        z = cg * v
        z_prev = pltpu.roll(z, 1, axis=0) * not_first
        z_next = pltpu.roll(z, tm - 1, axis=0) * not_last
        y = (cb_ref[:, lo:lo + nc] + z_prev * cw[0:1, lo:lo + nc] + z * cw[1:2, lo:lo + nc]
             + z_next * cw[2:3, lo:lo + nc])
        conv_ref[:, lo:lo + nc] = bg * y


def _inproj(x2, mod3, norm_g, w_in_bf, conv_w, conv_b):
    d_in = w_in_bf.shape[1]
    tm = 2 * TM
    tiles_per_batch = SEQ // tm
    return pl.pallas_call(
        _inproj_kernel,
        grid=(N_TOK // tm,),
        in_specs=[pl.BlockSpec((tm, D_MODEL), lambda i: (i, 0)),
                  pl.BlockSpec((1, N_MOD, D_MODEL), lambda i: (i // tiles_per_batch, 0, 0)),
                  _const_spec((1, D_MODEL)),
                  _const_spec((D_MODEL, d_in)),
                  _const_spec((8, D_CONV)),
                  _const_spec((1, D_CONV))],
        out_specs=[pl.BlockSpec((tm, D_SSM), lambda i: (i, 0)),
                   pl.BlockSpec((tm, D_CONV), lambda i: (i, 0))],
        out_shape=[jax.ShapeDtypeStruct((N_TOK, D_SSM), _F32),
                   jax.ShapeDtypeStruct((N_TOK, D_CONV), _F32)],
        compiler_params=_cparams("arbitrary"),
        name="in_proj",
    )(x2, mod3, norm_g, w_in_bf, conv_w, conv_b)


def _ctxproj_kernel(x_ref, mod_ref, g_ref, w_ref, u_ref):
    h = _modulated_norm(x_ref, mod_ref, g_ref, 0, 1).astype(_BF16)
    u_ref[...] = jnp.dot(h, w_ref[...], preferred_element_type=_F32)


def _ctxproj(ctx2, mod3, norm_g, w_u_bf):
    return pl.pallas_call(
        _ctxproj_kernel,
        grid=(N_CTX_TOK // TM,),
        in_specs=[pl.BlockSpec((TM, D_MODEL), lambda i: (i, 0)),
                  pl.BlockSpec((1, N_MOD, D_MODEL), lambda i: (BATCH, 0, 0)),
                  _const_spec((1, D_MODEL)),
                  _const_spec((D_MODEL, D_SSM))],
        out_specs=pl.BlockSpec((TM, D_SSM), lambda i: (i, 0)),
        out_shape=jax.ShapeDtypeStruct((N_CTX_TOK, D_SSM), _F32),
        compiler_params=_cparams("arbitrary"),
        name="ctx_proj",
    )(ctx2, mod3, norm_g, w_u_bf)


N_POW = 32


def _s5_param_kernel(prow_ref, ct_ref, bt_ref, toep_ref, bpow_ref, cpow_ref, dec_ref):
    for gi in range(toep_ref.shape[0]):
        one = pl.ds(gi, 1)
        _s5_param_group(prow_ref.at[:, one], ct_ref.at[:, one], bt_ref.at[:, one], toep_ref.at[one],
                        bpow_ref.at[one], cpow_ref.at[one], dec_ref.at[one])


def _s5_param_group(prow_ref, ct_ref, bt_ref, toep_ref, bpow_ref, cpow_ref, dec_ref):
    lane = lambda shape: lax.broadcasted_iota(jnp.int32, shape, 1)
    sub = lambda shape: lax.broadcasted_iota(jnp.int32, shape, 0)

    def split2(a):
        a_hi = a.astype(_BF16)
        return a_hi, (a - a_hi.astype(_F32)).astype(_BF16)

    rep = (lane((SSM_GROUP, CHUNK_W)) % SSM_GROUP == sub((SSM_GROUP, CHUNK_W))).astype(_BF16)
    sgn_col = jnp.where(sub((STATE_W, 1)) < SSM_STATE, 1.0, -1.0).astype(_F32)
    sgn_row = jnp.where(lane((1, STATE_W)) < SSM_STATE, -1.0, 1.0).astype(_F32)
    tau_col = sub((N_POW, 1)).astype(_F32)
    blk_l = lane((N_POW, CHUNK_W)) // SSM_GROUP
    tau_s = sub((N_POW, CHUNK_W))
    blk_r = sub((CHUNK_W, N_POW)) // SSM_GROUP
    tau_l = lane((CHUNK_W, N_POW))
    pick_l = lambda e: (tau_s == e).astype(_BF16)
    pick_r = lambda e: (tau_l == e).astype(_BF16)
    tn = (((0,), (0,)), ((), ()))

    strips = []
    for d in range(2):
        pr = prow_ref[d, 0]
        lam_re, lam_im, dt = pr[0:1], pr[1:2], jnp.exp(pr[2:3])
        mag = jnp.exp(tau_col * (lam_re * dt))
        ang = tau_col * (lam_im * dt)
        pw_re = mag * jnp.cos(ang)
        pw_im = mag * jnp.sin(ang)
        pw4 = jnp.concatenate(split2(pw_re) + split2(pw_im), axis=1)

        c_hi, c_lo = split2(ct_ref[d, 0])
        ct = (jnp.dot(c_hi, rep, preferred_element_type=_F32)
              + jnp.dot(c_lo, rep, preferred_element_type=_F32))
        ca = ct * sgn_col
        cb = -pltpu.roll(ct, SSM_STATE, axis=0)

        def cpow(sel):
            o = lax.dot_general(pw4, sel, tn, preferred_element_type=_F32)
            x = o[0:STATE_W] + o[STATE_W:2 * STATE_W]
            y = o[2 * STATE_W:3 * STATE_W] + o[3 * STATE_W:4 * STATE_W]
            return x * ca + y * cb

        nr = pw_re[1:2] - 1.0
        ni = pw_im[1:2]
        den = lam_re * lam_re + lam_im * lam_im
        kr = (nr * lam_re + ni * lam_im) / den
        ki = (ni * lam_re - nr * lam_im) / den
        b1 = bt_ref[d, 0]
        b2 = pltpu.roll(b1, SSM_STATE, axis=1) * sgn_row
        u1 = kr * b1 + ki * b2
        u2 = kr * b2 - ki * b1
        u1t = jnp.concatenate([u1] * CHUNK, axis=0)
        u2t = jnp.concatenate([u2] * CHUNK, axis=0)

        def bpow(sel):
            o = jnp.dot(sel, pw4, preferred_element_type=_F32)
            x = o[:, 0:STATE_W] + o[:, STATE_W:2 * STATE_W]
            y = o[:, 2 * STATE_W:3 * STATE_W] + o[:, 3 * STATE_W:4 * STATE_W]
            return x * u1t + y * u2t

        if d == 0:
            e_strip, e_b, e_c = blk_l, (CHUNK - 1) - blk_r, blk_l + 1
        else:
            e_strip, e_b, e_c = (CHUNK - 1) - blk_l, blk_r, CHUNK - blk_l
        u_hi, u_lo = split2(u1)
        k_hi, k_lo = split2(cpow(pick_l(e_strip)))
        strips.append(jnp.dot(u_hi, k_hi, preferred_element_type=_F32)
                      + jnp.dot(u_hi, k_lo, preferred_element_type=_F32)
                      + jnp.dot(u_lo, k_hi, preferred_element_type=_F32))
        bp = bpow(pick_r(e_b))
        bpow_ref[0, :, d * STATE_W:(d + 1) * STATE_W] = bp.astype(_BF16)
        bpow_ref[0, :, (2 + d) * STATE_W:(3 + d) * STATE_W] = pltpu.roll(bp, SSM_STATE, axis=1).astype(_BF16)
        cpow_ref[0, d * STATE_W:(d + 1) * STATE_W, :] = cpow(pick_l(e_c)).astype(_BF16)

        dec_ref[0, 2 * d:2 * d + 1, :] = pw_re[CHUNK:CHUNK + 1]
        dec_ref[0, 2 * d + 1:2 * d + 2, :] = pw_im[CHUNK:CHUNK + 1] * sgn_row

    zeros = jnp.zeros((SSM_GROUP, CHUNK_W), _F32)
    strip = (jnp.concatenate([strips[1], zeros], axis=1)
             + pltpu.roll(jnp.concatenate([strips[0], zeros], axis=1), CHUNK_W - SSM_GROUP, axis=1))
    for i in range(CHUNK):
        off = (CHUNK - 1 - i) * SSM_GROUP
        win = strip if off == 0 else pltpu.roll(strip, 2 * CHUNK_W - off, axis=1)
        toep_ref[0, i * SSM_GROUP:(i + 1) * SSM_GROUP, :] = win[:, 0:CHUNK_W].astype(_BF16)


def _s5_params(prow, ct2, bt1):
    ng = 4
    g_spec = lambda shape: pl.BlockSpec(shape, lambda g: (0, g, 0, 0))
    o_spec = lambda shape: pl.BlockSpec(shape, lambda g: (g, 0, 0))
    return pl.pallas_call(
        _s5_param_kernel,
        grid=(SSM_GROUPS // ng,),
        in_specs=[g_spec((2, ng, 8, STATE_W)),
                  g_spec((2, ng, STATE_W, SSM_GROUP)), g_spec((2, ng, SSM_GROUP, STATE_W))],
        out_specs=[o_spec((ng, CHUNK_W, CHUNK_W)), o_spec((ng, CHUNK_W, 4 * STATE_W)),
                   o_spec((ng, 2 * STATE_W, CHUNK_W)), o_spec((ng, 4, STATE_W))],
        out_shape=[jax.ShapeDtypeStruct((SSM_GROUPS, CHUNK_W, CHUNK_W), _BF16),
                   jax.ShapeDtypeStruct((SSM_GROUPS, CHUNK_W, 4 * STATE_W), _BF16),
                   jax.ShapeDtypeStruct((SSM_GROUPS, 2 * STATE_W, CHUNK_W), _BF16),
                   jax.ShapeDtypeStruct((SSM_GROUPS, 4, STATE_W), _F32)],
        compiler_params=_cparams("arbitrary"),
        name="s5_params",
    )(prow, ct2, bt1)


def _s5_kernel(ux_ref, uc_ref, toep_ref, bpow_ref, cpow_ref, dec_ref, y_ref, u_ref, s_ref, h_ref, y2_ref):
    ng = S5_GROUPS_PER_STEP
    gw = SSM_GROUP
    x0 = N_CTX_CHUNKS * BATCH

    half = CHUNK // 2
    assert half == ng
    lanes = ng * 128

    r = lax.broadcasted_iota(jnp.int32, (lanes, 1), 0)
    swapped = ((r // gw) % ng) * 128 + (r // 128) * gw + r % gw
    perm = (lax.broadcasted_iota(jnp.int32, (lanes, lanes), 1) == swapped).astype(_BF16)

    def to_chunk_layout(src_ref, seq_len, n_chunks, row0):
        for hf in range(2):
            p = jnp.concatenate(
                [jnp.concatenate([src_ref[pl.ds(b * seq_len + hf * half + il, n_chunks, stride=CHUNK), :]
                                  for il in range(half)], axis=1) for b in range(BATCH)], axis=0).astype(_BF16)
            q = jnp.dot(p, perm, preferred_element_type=_F32)
            for b in range(BATCH):
                for g in range(ng):
                    u_ref[g, hf, pl.ds(row0 + b, n_chunks, stride=BATCH), :] = (
                        q[b * n_chunks:(b + 1) * n_chunks, g * 128:(g + 1) * 128])

    to_chunk_layout(uc_ref, CTX_LEN, N_CTX_CHUNKS, 0)
    to_chunk_layout(ux_ref, SEQ, N_X_CHUNKS, x0)

    chunk_rows = lambda ref, g, lo: jnp.concatenate([ref[g, 0, lo:, :], ref[g, 1, lo:, :]], axis=1)
    for g in range(ng):
        s_ref[g] = jnp.dot(chunk_rows(u_ref, g, 0).astype(_BF16), bpow_ref[g], preferred_element_type=_F32)

    pair = 2 * BATCH
    n_ctx_pairs = N_CTX_CHUNKS // 2
    n_pairs = N_SEQ_CHUNKS // 2
    lower = lax.broadcasted_iota(jnp.int32, (pair, STATE_W), 0) < BATCH
    zero = jnp.zeros((pair, STATE_W), _F32)

    def step(t, carry):
        tb = jnp.where(t < n_ctx_pairs, n_ctx_pairs - 1 - t, n_pairs + n_ctx_pairs - 1 - t)
        rf = pl.multiple_of(t * pair, pair)
        rb = pl.multiple_of(tb * pair, pair)
        out = []
        swap = lambda v: pltpu.roll(v, BATCH, axis=0)
        for g in range(ng):
            hf, hfs, hb, hbs = carry[4 * g:4 * g + 4]
            dg = dec_ref[g]

            def advance(h, hs, a_re, a_im, s, ss):
                return a_re * h + a_im * hs + s, a_re * hs - a_im * h + ss

            col = lambda r, k: s_ref[g, pl.ds(r, pair), k * STATE_W:(k + 1) * STATE_W]
            sf, sfs = col(rf, 0), col(rf, 2)
            mid, mids = advance(hf, hfs, dg[0:1], dg[1:2], jnp.where(lower, sf, swap(sf)),
                                jnp.where(lower, sfs, swap(sfs)))
            h_ref[g, pl.ds(rf, pair), 0:STATE_W] = jnp.where(lower, hf, mid)
            end, ends = advance(mid, mids, dg[0:1], dg[1:2], sf, sfs)
            hf, hfs = jnp.where(lower, swap(end), end), jnp.where(lower, swap(ends), ends)

            sb, sbs = col(rb, 1), col(rb, 3)
            mid, mids = advance(hb, hbs, dg[2:3], dg[3:4], jnp.where(lower, swap(sb), sb),
                                jnp.where(lower, swap(sbs), sbs))
            h_ref[g, pl.ds(rb, pair), STATE_W:2 * STATE_W] = jnp.where(lower, mid, hb)
            end, ends = advance(mid, mids, dg[2:3], dg[3:4], sb, sbs)
            hb, hbs = jnp.where(lower, end, swap(end)), jnp.where(lower, ends, swap(ends))
            out += [hf, hfs, hb, hbs]
        return tuple(out)

    lax.fori_loop(0, n_pairs, step, (zero,) * (4 * ng))

    for g in range(ng):
        ux = chunk_rows(u_ref, g, x0).astype(_BF16)
        hx = h_ref[g, x0:SCAN_ROWS, :].astype(_BF16)
        y2 = (jnp.dot(ux, toep_ref[g], preferred_element_type=_F32)
              + jnp.dot(hx, cpow_ref[g], preferred_element_type=_F32))
        y2_ref[g, 0] = y2[:, 0:128]
        y2_ref[g, 1] = y2[:, 128:256]

    for hf in range(2):
        yin = jnp.concatenate(
            [jnp.concatenate([y2_ref[g, hf, pl.ds(b, N_X_CHUNKS, stride=BATCH), :] for g in range(ng)], axis=1)
             for b in range(BATCH)], axis=0)
        y_hi = yin.astype(_BF16)
        y_lo = (yin - y_hi.astype(_F32)).astype(_BF16)
        q = (jnp.dot(y_hi, perm, preferred_element_type=_F32)
             + jnp.dot(y_lo, perm, preferred_element_type=_F32))
        for b in range(BATCH):
            for jl in range(half):
                y_ref[pl.ds(b * SEQ + hf * half + jl, N_X_CHUNKS, stride=CHUNK), :] = (
                    q[b * N_X_CHUNKS:(b + 1) * N_X_CHUNKS, jl * 128:(jl + 1) * 128])


def _s5_scan(u_x, u_c, toep, bpow, cpow, dec):
    ng = S5_GROUPS_PER_STEP
    n_x_rows = N_X_CHUNKS * BATCH
    lanes = ng * SSM_GROUP
    spec = lambda r, c: pl.BlockSpec((ng, r, c), lambda i: (i, 0, 0))
    col = lambda rows: pl.BlockSpec((rows, lanes), lambda i: (0, i))
    return pl.pallas_call(
        _s5_kernel,
        grid=(SSM_GROUPS // ng,),
        in_specs=[col(N_TOK), col(N_CTX_TOK), spec(CHUNK_W, CHUNK_W), spec(CHUNK_W, 4 * STATE_W),
                  spec(2 * STATE_W, CHUNK_W), spec(4, STATE_W)],
        out_specs=col(N_TOK),
        out_shape=jax.ShapeDtypeStruct((N_TOK, D_SSM), _F32),
        scratch_shapes=[pltpu.VMEM((ng, 2, SCAN_ROWS, 128), _F32),
                        pltpu.VMEM((ng, SCAN_ROWS, 4 * STATE_W), _F32),
                        pltpu.VMEM((ng, SCAN_ROWS, 2 * STATE_W), _F32),
                        pltpu.VMEM((ng, 2, n_x_rows, 128), _F32)],
        compiler_params=_cparams("arbitrary"),
        name="s5_scan",
    )(u_x, u_c, toep, bpow, cpow, dec)


def _merge_kernel(ys_ref, u_ref, conv_ref, x_ref, mod_ref, dskip_ref, mixg_ref, n2g_ref,
                  wglu_ref, wout_ref, rwt_ref, x1_ref, h2_ref, logit_ref):
    m = mod_ref[0]
    yx = dskip_ref[...] * u_ref[...] + ys_ref[...]
    c0 = math.sqrt(2.0 / math.pi)
    ge = 0.5 * yx * (1.0 + jnp.tanh(c0 * (yx + 0.044715 * (yx * yx * yx))))
    z = jnp.dot(ge.astype(_BF16), wglu_ref[...], preferred_element_type=_F32)
    ssm_y = z[:, 0:D_SSM] * jax.nn.sigmoid(z[:, D_SSM:2 * D_SSM])
    conv_y = conv_ref[...]
    mixg = mixg_ref[...]
    heads_a = (ssm_y * _rms_scale(ssm_y) * mixg[:, 0:D_SSM]).astype(_BF16)
    heads_b = (conv_y * _rms_scale(conv_y) * mixg[:, D_SSM:]).astype(_BF16)
    mix = (jnp.dot(heads_a, wout_ref[0:D_SSM, :], preferred_element_type=_F32)
           + jnp.dot(heads_b, wout_ref[D_SSM:, :], preferred_element_type=_F32))
    x1 = x_ref[...] + m[2:3] * mix
    x1_ref[...] = x1
    h2 = x1 * _rms_scale(x1) * n2g_ref[...] * (1.0 + m[4:5]) + m[3:4]
    h2_ref[...] = h2
    h_hi = h2.astype(_BF16)
    h_lo = (h2 - h_hi.astype(_F32)).astype(_BF16)
    nt = (((1,), (1,)), ((), ()))
    p = lax.dot_general(rwt_ref[...], h_hi, nt, preferred_element_type=_F32)
    q = lax.dot_general(rwt_ref[0:N_EXPERTS, :], h_lo, nt, preferred_element_type=_F32)
    logit_ref[...] = p[0:N_EXPERTS] + p[N_EXPERTS:2 * N_EXPERTS] + q


def _merge(ys, u, conv, x2, mod3, dskip, mixg, n2g, wglu_bf, wout_bf, rwt):
    tiles_per_batch = SEQ // TM
    tok = lambda w: pl.BlockSpec((TM, w), lambda i: (i, 0))
    return pl.pallas_call(
        _merge_kernel,
        grid=(N_TOK // TM,),
        in_specs=[tok(D_SSM), tok(D_SSM), tok(D_CONV), tok(D_MODEL),
                  pl.BlockSpec((1, N_MOD, D_MODEL), lambda i: (i // tiles_per_batch, 0, 0)),
                  _const_spec((1, D_SSM)), _const_spec((1, D_MODEL)), _const_spec((1, D_MODEL)),
                  _const_spec((D_SSM, 2 * D_SSM)), _const_spec((D_MODEL, D_MODEL)),
                  _const_spec((2 * N_EXPERTS, D_MODEL))],
        out_specs=[tok(D_MODEL), tok(D_MODEL),
                   pl.BlockSpec((N_EXPERTS, TM), lambda i: (0, i))],
        out_shape=[jax.ShapeDtypeStruct((N_TOK, D_MODEL), _F32),
                   jax.ShapeDtypeStruct((N_TOK, D_MODEL), _F32),
                   jax.ShapeDtypeStruct((N_EXPERTS, N_TOK), _F32)],
        compiler_params=_cparams("arbitrary"),
        name="merge_heads",
    )(ys, u, conv, x2, mod3, dskip, mixg, n2g, wglu_bf, wout_bf, rwt)


def _route_kernel(logit_ref, bias_ref, eid_ref, w_ref, rank_ref, cnt_ref, carry_ref):
    @pl.when(pl.program_id(0) == 0)
    def _():
        carry_ref[...] = jnp.zeros_like(carry_ref)

    tm = logit_ref.shape[1]
    neg = jnp.float32(-jnp.inf)
    scores = jax.nn.sigmoid(logit_ref[...])
    biased = scores + bias_ref[:, 0:1]
    sub = lax.broadcasted_iota(jnp.int32, (GROUP_SIZE, tm), 0)
    rows = lambda a, g: a[g * GROUP_SIZE:(g + 1) * GROUP_SIZE]
    ngrp = N_EXPERT_GROUPS

    gscore = []
    for g in range(ngrp):
        bg = rows(biased, g)
        m1 = jnp.max(bg, axis=0, keepdims=True)
        first = jnp.min(jnp.where(bg == m1, sub, GROUP_SIZE), axis=0, keepdims=True)
        m2 = jnp.max(jnp.where(sub == first, neg, bg), axis=0, keepdims=True)
        gscore.append(m1 + m2)
    v = []
    for g in range(ngrp):
        beaten = jnp.zeros((1, tm), jnp.int32)
        for o in range(ngrp):
            if o != g:
                beats = (gscore[o] >= gscore[g]) if o < g else (gscore[o] > gscore[g])
                beaten = beaten + beats.astype(jnp.int32)
        v.append(jnp.where(beaten < TOPK_GROUPS, rows(biased, g), neg))
    eids = [sub + g * GROUP_SIZE for g in range(ngrp)]

    pick_ids, pick_masks = [], []
    for _ in range(TOP_K):
        m = functools.reduce(jnp.maximum, v)
        m = jnp.max(m, axis=0, keepdims=True)
        cand = functools.reduce(jnp.minimum, [jnp.where(v[g] == m, eids[g], N_EXPERTS) for g in range(ngrp)])
        pick_id = jnp.min(cand, axis=0, keepdims=True)
        masks = [eids[g] == pick_id for g in range(ngrp)]
        v = [jnp.where(masks[g], neg, v[g]) for g in range(ngrp)]
        pick_ids.append(pick_id)
        pick_masks.append(masks)

    sel = jnp.concatenate(
        [functools.reduce(jnp.logical_or, [pick_masks[k][g] for k in range(TOP_K)]).astype(_F32)
         for g in range(ngrp)], axis=0)
    before = (lax.broadcasted_iota(jnp.int32, (tm, tm), 0)
              < lax.broadcasted_iota(jnp.int32, (tm, tm), 1)).astype(_BF16)
    base = jnp.dot(sel.astype(_BF16), before, preferred_element_type=_F32) + carry_ref[:, 0:1]

    def gather_pick(a, k):
        parts = [jnp.sum(jnp.where(pick_masks[k][g], rows(a, g), 0.0), axis=0, keepdims=True)
                 for g in range(ngrp)]
        return functools.reduce(jnp.add, parts)

    picked = [gather_pick(scores, k) for k in range(TOP_K)]
    denom = functools.reduce(jnp.add, picked)
    for k in range(TOP_K):
        eid_ref[k:k + 1, :] = pick_ids[k]
        w_ref[k:k + 1, :] = picked[k] / denom * ROUTED_SCALE
        rank_ref[k:k + 1, :] = gather_pick(base, k).astype(jnp.int32)
    carry_ref[...] = carry_ref[...] + jnp.sum(sel, axis=1, keepdims=True)
    cnt_ref[...] = carry_ref[...]


def _route(logits_t, bias_col):
    tok = pl.BlockSpec((TOP_K, TM), lambda i: (0, i))
    return pl.pallas_call(
        _route_kernel,
        grid=(N_TOK // TM,),
        in_specs=[pl.BlockSpec((N_EXPERTS, TM), lambda i: (0, i)), _const_spec((N_EXPERTS, 128))],
        out_specs=[tok, tok, tok, pl.BlockSpec((N_EXPERTS, 128), lambda i: (0, 0))],
        out_shape=[jax.ShapeDtypeStruct((TOP_K, N_TOK), jnp.int32),
                   jax.ShapeDtypeStruct((TOP_K, N_TOK), _F32),
                   jax.ShapeDtypeStruct((TOP_K, N_TOK), jnp.int32),
                   jax.ShapeDtypeStruct((N_EXPERTS, 128), _F32)],
        scratch_shapes=[pltpu.VMEM((N_EXPERTS, 128), _F32)],
        compiler_params=_cparams("arbitrary"),
        name="route",
    )(logits_t, bias_col)


def _rowmap_kernel(dest_ref, padstart_ref, padn_ref, h_ref, x1_ref, mod_ref, wsg_ref, wsu_ref, wsd_ref,
                   xmid_ref, codes_ref):
    i = pl.program_id(0)
    tm = h_ref.shape[0]
    base = i * tm

    @pl.when(i == 0)
    def _():
        def fill(e, done):
            def one(r, c):
                codes_ref[padstart_ref[e] + r] = N_ASSIGN + done + r
                return c
            lax.fori_loop(0, padn_ref[e], one, 0)
            return done + padn_ref[e]
        lax.fori_loop(0, N_EXPERTS, fill, 0)
        for r in range(EXPERT_BM):
            codes_ref[N_ROWS + r] = N_ROWS + r

    for t in range(tm):
        ds = [dest_ref[k * N_TOK + base + t] for k in range(TOP_K)]
        for k in range(TOP_K):
            codes_ref[ds[k]] = k * N_TOK + base + t

    h = h_ref[...].astype(_BF16)
    a = jnp.dot(h, wsg_ref[...], preferred_element_type=_F32)
    b = jnp.dot(h, wsu_ref[...], preferred_element_type=_F32)
    shared = jnp.dot((_silu(a) * b).astype(_BF16), wsd_ref[...], preferred_element_type=_F32)
    xmid_ref[...] = x1_ref[...] + mod_ref[0][5:6] * shared


def _rowmap(dest, pad_start, pad_n, h2, x1, mod3, wsg_bf, wsu_bf, wsd_bf):
    tm = TM
    tiles_per_batch = SEQ // tm
    tok = pl.BlockSpec((tm, D_MODEL), lambda i, *_: (i, 0))
    const = lambda shape: pl.BlockSpec(shape, lambda i, *_: (0,) * len(shape), pipeline_mode=pl.Buffered(1))
    grid_spec = pltpu.PrefetchScalarGridSpec(
        num_scalar_prefetch=3,
        grid=(N_TOK // tm,),
        in_specs=[tok, tok,
                  pl.BlockSpec((1, N_MOD, D_MODEL), lambda i, *_: (i // tiles_per_batch, 0, 0)),
                  const((D_MODEL, D_SHARED)), const((D_MODEL, D_SHARED)), const((D_SHARED, D_MODEL))],
        out_specs=[tok, pl.BlockSpec(memory_space=pltpu.SMEM)],
    )
    return pl.pallas_call(
        _rowmap_kernel,
        grid_spec=grid_spec,
        out_shape=[jax.ShapeDtypeStruct((N_TOK, D_MODEL), _F32),
                   jax.ShapeDtypeStruct((N_ROWS + EXPERT_BM,), jnp.int32)],
        compiler_params=_cparams("arbitrary"),
        name="rowmap",
    )(dest.reshape(N_ASSIGN), pad_start, pad_n, h2, x1, mod3, wsg_bf, wsu_bf, wsd_bf)


def _experts_kernel(be_ref, nact_ref, first_ref, slot_ref, nxt_ref, codes_ref, h_hbm, wg_hbm, wu_hbm, wd_hbm,
                    out_hbm, *scratch):
    depth = EXPERT_DEPTH
    xbuf, ybuf = scratch[:depth], scratch[depth:2 * depth]
    wg_f, wu_f, wd_f, wg_bf, wu_bf, wd_bf, sem, gsem, ssem = scratch[2 * depth:]
    n_act = nact_ref[0]
    bm = EXPERT_BM

    def fetch(e, s):
        return (pltpu.make_async_copy(wg_hbm.at[e], wg_f.at[s], sem.at[s, 0]),
                pltpu.make_async_copy(wu_hbm.at[e], wu_f.at[s], sem.at[s, 1]),
                pltpu.make_async_copy(wd_hbm.at[e], wd_f.at[s], sem.at[s, 2]))

    def gather(blk, p):
        for r in range(bm):
            tok = codes_ref[blk * bm + r] & (N_TOK - 1)
            pltpu.make_async_copy(h_hbm.at[pl.ds(tok, 1), :], xbuf[p].at[pl.ds(r, 1), :],
                                  gsem.at[p]).start(priority=r % 2)

    def gather_wait(p):
        pltpu.make_async_copy(h_hbm.at[pl.ds(0, bm), :], xbuf[p], gsem.at[p]).wait()

    def scatter(blk, p):
        for r in range(bm):
            pltpu.make_async_copy(ybuf[p].at[pl.ds(r, 1), :], out_hbm.at[pl.ds(codes_ref[blk * bm + r], 1), :],
                                  ssem.at[p]).start(priority=r % 2)

    def scatter_wait(p):
        pltpu.make_async_copy(ybuf[p], out_hbm.at[pl.ds(0, bm), :], ssem.at[p]).wait()

    def block(blk, p):
        prev = (p + depth - 1) % depth

        @pl.when(blk < n_act)
        def _():
            @pl.when(blk == 0)
            def _():
                for cp in fetch(be_ref[0], 0):
                    cp.start()
                for j in range(depth - 1):
                    gather(jnp.minimum(j, n_act - 1), j)
                ybuf[depth - 1][...] = jnp.zeros_like(ybuf[depth - 1])

            @pl.when(first_ref[blk] == 1)
            def _():
                s = slot_ref[blk]
                for cp in fetch(be_ref[blk], s):
                    cp.wait()

                @pl.when(nxt_ref[blk] >= 0)
                def _():
                    for cp in fetch(nxt_ref[blk], 1 - s):
                        cp.start()

                wg_bf[...] = wg_f[s].astype(_BF16)
                wu_bf[...] = wu_f[s].astype(_BF16)
                wd_bf[...] = wd_f[s].astype(_BF16)

            gather_wait(p)

            @pl.when(blk >= depth - 1)
            def _():
                scatter_wait(p)

            gather(jnp.minimum(blk + depth - 1, n_act - 1), prev)
            scatter(jnp.where(blk == 0, N_BLOCKS, blk - 1), prev)
            x = xbuf[p][...].astype(_BF16)
            a = jnp.dot(x, wg_bf[...], preferred_element_type=_F32)
            b = jnp.dot(x, wu_bf[...], preferred_element_type=_F32)
            ybuf[p][...] = jnp.dot((_silu(a) * b).astype(_BF16), wd_bf[...], preferred_element_type=_F32)

            @pl.when(blk == n_act - 1)
            def _():
                scatter(blk, p)
                for j in range(1, depth):
                    gather_wait((p + j) % depth)
                for j in range(depth - 1):
                    pl.when(blk >= j)(functools.partial(scatter_wait, (p + depth - 1 - j) % depth))
                scatter_wait(p)

    i = pl.program_id(0)
    for q in range(depth):
        block(depth * i + q, q)


def _experts(block_e, n_active, first, slot, nxt, codes, h2, w_gate, w_up, w_down):
    bm = EXPERT_BM
    depth = EXPERT_DEPTH
    hbm = pl.BlockSpec(memory_space=pl.ANY)
    grid_spec = pltpu.PrefetchScalarGridSpec(
        num_scalar_prefetch=6,
        grid=(N_BLOCKS // depth,),
        in_specs=[hbm, hbm, hbm, hbm],
        out_specs=hbm,
        scratch_shapes=[pltpu.VMEM((bm, D_MODEL), _F32)] * (2 * depth) + [
                        pltpu.VMEM((2, D_MODEL, D_EXPERT), _F32), pltpu.VMEM((2, D_MODEL, D_EXPERT), _F32),
                        pltpu.VMEM((2, D_EXPERT, D_MODEL), _F32),
                        pltpu.VMEM((D_MODEL, D_EXPERT), _BF16), pltpu.VMEM((D_MODEL, D_EXPERT), _BF16),
                        pltpu.VMEM((D_EXPERT, D_MODEL), _BF16),
                        pltpu.SemaphoreType.DMA((2, 3)), pltpu.SemaphoreType.DMA((depth,)),
                        pltpu.SemaphoreType.DMA((depth,))],
    )
    return pl.pallas_call(
        _experts_kernel,
        grid_spec=grid_spec,
        out_shape=jax.ShapeDtypeStruct((N_ROWS + bm, D_MODEL), _F32),
        compiler_params=_cparams("arbitrary"),
        name="experts",
    )(block_e, n_active, first, slot, nxt, codes, h2, w_gate, w_up, w_down)


def _final_kernel(xmid_ref, wt_ref, mod_ref, fg_ref, *refs):
    picks, o_ref = refs[:TOP_K], refs[TOP_K]
    wt = wt_ref[...]
    routed = wt[:, 0:1] * picks[0][...]
    for k in range(1, TOP_K):
        routed = routed + wt[:, k:k + 1] * picks[k][...]
    y = xmid_ref[...] + mod_ref[0][5:6] * routed
    o_ref[...] = y * _rms_scale(y) * fg_ref[...]


def _final(xmid, w_tok, mod3, final_g, ys):
    tm = TM_COMBINE
    tiles = N_TOK // tm
    tiles_per_batch = SEQ // tm
    tok = lambda w: pl.BlockSpec((tm, w), lambda i: (i, 0))
    pick = lambda k: pl.BlockSpec((tm, D_MODEL), lambda i: (k * tiles + i, 0))
    return pl.pallas_call(
        _final_kernel,
        grid=(tiles,),
        in_specs=[tok(D_MODEL), tok(TOP_K),
                  pl.BlockSpec((1, N_MOD, D_MODEL), lambda i: (i // tiles_per_batch, 0, 0)),
                  _const_spec((1, D_MODEL))] + [pick(k) for k in range(TOP_K)],
        out_specs=tok(D_MODEL),
        out_shape=jax.ShapeDtypeStruct((N_TOK, D_MODEL), _F32),
        compiler_params=_cparams("arbitrary"),
        name="final",
    )(xmid, w_tok, mod3, final_g, *([ys] * TOP_K))


def _s5_param_layouts(lam_re, lam_im, b_re, b_im, c_re, c_im, log_dt):
    two = lambda a: jnp.concatenate([a, a], axis=-1)
    lr, li = two(lam_re), two(lam_im)
    dtb = jnp.broadcast_to(log_dt.astype(_F32)[:, :, None], lr.shape)
    zeros = jnp.zeros_like(lr)
    prow = jnp.stack([lr, li, dtb] + [zeros] * 5, axis=2)
    ct2 = jnp.concatenate([jnp.swapaxes(c_re, -1, -2), jnp.swapaxes(c_im, -1, -2)], axis=2)
    bt1 = jnp.concatenate([jnp.swapaxes(b_re, -1, -2), jnp.swapaxes(b_im, -1, -2)], axis=3)
    return prow, ct2, bt1


def kernel(x, c, ctx, c_ctx, norm1_g, norm2_g, w_ada, b_ada, w_in, ssm_lam_re, ssm_lam_im, ssm_b_re, ssm_b_im, ssm_c_re, ssm_c_im, ssm_log_dt, ssm_d, ssm_w_glu, conv_w, conv_b, mix_norm_g, w_out, router_w, router_bias, exp_w_gate, exp_w_up, exp_w_down, shared_w_gate, shared_w_up, shared_w_down, final_g):
    layer = 0
    x2 = x.reshape(N_TOK, D_MODEL)
    ctx2 = ctx.reshape(N_CTX_TOK, D_MODEL)

    c8 = jnp.concatenate([c, c_ctx[None, :], jnp.zeros((8 - BATCH - 1, D_MODEL), _F32)], axis=0)
    mod = _ada_mod(c8, w_ada[layer], b_ada[layer][None, :])
    mod3 = mod.reshape(8, N_MOD, D_MODEL)

    w_in_bf = w_in[layer].astype(_BF16)
    conv_w8 = jnp.concatenate([conv_w[layer], jnp.zeros((8 - conv_w.shape[1], D_CONV), _F32)], axis=0)
    u_x, conv_x = _inproj(x2, mod3, norm1_g[layer][None, :], w_in_bf, conv_w8, conv_b[layer][None, :])
    u_c = _ctxproj(ctx2, mod3, norm1_g[layer][None, :], w_in_bf[:, :D_SSM])

    prow, ct2, bt1 = _s5_param_layouts(ssm_lam_re[layer], ssm_lam_im[layer], ssm_b_re[layer],
                                       ssm_b_im[layer], ssm_c_re[layer], ssm_c_im[layer], ssm_log_dt[layer])
    toep, bpow, cpow, dec = _s5_params(prow, ct2, bt1)
    ys = _s5_scan(u_x, u_c, toep, bpow, cpow, dec)

    rw = router_w[layer]
    rw_hi = rw.astype(_BF16)
    rw_lo = (rw - rw_hi.astype(_F32)).astype(_BF16)
    rwt = jnp.concatenate([rw_hi.T, rw_lo.T], axis=0)
    x1, h2, logits_t = _merge(ys, u_x, conv_x, x2, mod3, ssm_d[layer][None, :],
                              mix_norm_g[layer][None, :], norm2_g[layer][None, :],
                              ssm_w_glu[layer].astype(_BF16), w_out[layer].astype(_BF16), rwt)

    bias_col = jnp.broadcast_to(router_bias[layer][:, None], (N_EXPERTS, 128))
    eid, w_k, rank, cnt = _route(logits_t, bias_col)

    counts = cnt[:, 0].astype(jnp.int32)
    padded = (counts + EXPERT_BM - 1) // EXPERT_BM * EXPERT_BM
    pend = jnp.cumsum(padded)
    pstart = pend - padded
    is_e = eid[:, :, None] == jnp.arange(N_EXPERTS, dtype=jnp.int32)
    dest = jnp.sum(jnp.where(is_e, pstart, 0), axis=-1) + rank
    n_active = (pend[-1] // EXPERT_BM).astype(jnp.int32)
    blk = jnp.minimum(jnp.arange(N_BLOCKS, dtype=jnp.int32), n_active - 1)
    ends_before = (pend[None, :] <= (blk * EXPERT_BM)[:, None]).astype(jnp.int32)
    block_e = jnp.minimum(jnp.sum(ends_before, axis=1), N_EXPERTS - 1)

    xmid, codes = _rowmap(dest, pstart + counts, padded - counts, h2, x1, mod3,
                          shared_w_gate[layer].astype(_BF16), shared_w_up[layer].astype(_BF16),
                          shared_w_down[layer].astype(_BF16))
    first = jnp.concatenate([jnp.ones((1,), jnp.int32), (block_e[1:] != block_e[:-1]).astype(jnp.int32)])
    slot = (jnp.cumsum(first) - 1) % 2
    e_ids = jnp.arange(N_EXPERTS, dtype=jnp.int32)
    owner = jnp.where(padded > 0, e_ids, N_EXPERTS)
    later = jnp.min(jnp.where(e_ids[None, :] > e_ids[:, None], owner[None, :], N_EXPERTS), axis=1)
    nxt_e = jnp.where(later == N_EXPERTS, -1, later)
    nxt = jnp.sum(jnp.where(block_e[:, None] == e_ids[None, :], nxt_e[None, :], 0), axis=1)
    ys_rows = _experts(block_e, n_active[None], first, slot, nxt, codes, h2,
                       exp_w_gate[layer], exp_w_up[layer], exp_w_down[layer])
    out = _final(xmid, w_k.T, mod3, final_g[None, :], ys_rows)
    return out.reshape(BATCH, SEQ, D_MODEL)
```

```python
import functools
import math

import jax
import jax.numpy as jnp
from jax import lax
from jax.experimental import pallas as pl
from jax.experimental.pallas import tpu as pltpu

D_MODEL = 2048
BATCH = 4
SEQ = 2048
CTX_LEN = 256
GRID_W = 64
D_SSM = 1024
D_CONV = 1024
SSM_GROUP = 16
SSM_GROUPS = 64
SSM_STATE = 64
N_EXPERTS = 64
N_EXPERT_GROUPS = 8
GROUP_SIZE = N_EXPERTS // N_EXPERT_GROUPS
TOPK_GROUPS = 4
TOP_K = 8
D_EXPERT = 512
D_SHARED = 512
ROUTED_SCALE = 2.5
N_MOD = 6
EPS = 1e-6

N_TOK = BATCH * SEQ
N_CTX_TOK = BATCH * CTX_LEN

CHUNK = 16
CHUNK_W = CHUNK * SSM_GROUP
STATE_W = 2 * SSM_STATE
N_CTX_CHUNKS = CTX_LEN // CHUNK
N_X_CHUNKS = SEQ // CHUNK
N_SEQ_CHUNKS = N_CTX_CHUNKS + N_X_CHUNKS
SCAN_ROWS = N_SEQ_CHUNKS * BATCH
S5_GROUPS_PER_STEP = 8

TM = 256
EXPERT_BM = 256
N_ASSIGN = N_TOK * TOP_K
EXPERT_DEPTH = 4
N_BLOCKS = -(-(N_ASSIGN // EXPERT_BM + N_EXPERTS) // EXPERT_DEPTH) * EXPERT_DEPTH
N_ROWS = N_BLOCKS * EXPERT_BM
TM_COMBINE = 128
VMEM_LIMIT = 56 * 1024 * 1024

_F32 = jnp.float32
_BF16 = jnp.bfloat16


def _cparams(*sem):
    return pltpu.CompilerParams(dimension_semantics=sem, vmem_limit_bytes=VMEM_LIMIT)


def _const_spec(shape):
    nd = len(shape)
    return pl.BlockSpec(shape, lambda *_: (0,) * nd, pipeline_mode=pl.Buffered(1))


def _rms_scale(xf):
    return lax.rsqrt(jnp.mean(xf * xf, axis=-1, keepdims=True) + EPS)


def _silu(x):
    return x * jax.nn.sigmoid(x)


def _ada_kernel(c_ref, w_ref, b_ref, o_ref):
    s = _silu(c_ref[...])
    o_ref[...] = jnp.dot(s, w_ref[...], preferred_element_type=_F32) + b_ref[...]


def _ada_mod(c8, w_ada, b_ada):
    n = w_ada.shape[1]
    tn = 1024
    return pl.pallas_call(
        _ada_kernel,
        grid=(n // tn,),
        in_specs=[pl.BlockSpec((8, D_MODEL), lambda j: (0, 0)),
                  pl.BlockSpec((D_MODEL, tn), lambda j: (0, j)),
                  pl.BlockSpec((1, tn), lambda j: (0, j))],
        out_specs=pl.BlockSpec((8, tn), lambda j: (0, j)),
        out_shape=jax.ShapeDtypeStruct((8, n), _F32),
        compiler_params=_cparams("arbitrary"),
        name="ada_mod",
    )(c8, w_ada, b_ada)


def _modulated_norm(x_ref, mod_ref, g_ref, shift_row, scale_row):
    xf = x_ref[...]
    m = mod_ref[0]
    h = xf * _rms_scale(xf) * g_ref[...]
    return h * (1.0 + m[scale_row:scale_row + 1]) + m[shift_row:shift_row + 1]


def _inproj_kernel(x_ref, mod_ref, g_ref, w_ref, cw_ref, cb_ref, u_ref, conv_ref):
    h = _modulated_norm(x_ref, mod_ref, g_ref, 0, 1).astype(_BF16)
    u_ref[...] = jnp.dot(h, w_ref[:, 0:D_SSM], preferred_element_type=_F32)
    tm = x_ref.shape[0]
    pos = lax.broadcasted_iota(jnp.int32, (tm, 1), 0) % GRID_W
    not_first = (pos != 0).astype(_F32)
    not_last = (pos != GRID_W - 1).astype(_F32)
    cw = cw_ref[...]
    nc = 256
    for j in range(D_CONV // nc):
        lo = j * nc
        bg = jnp.dot(h, w_ref[:, D_SSM + lo:D_SSM + lo + nc], preferred_element_type=_F32)
        cg = jnp.dot(h, w_ref[:, D_SSM + D_CONV + lo:D_SSM + D_CONV + lo + nc], preferred_element_type=_F32)
        v = jnp.dot(h, w_ref[:, D_SSM + 2 * D_CONV + lo:D_SSM + 2 * D_CONV + lo + nc],
                    preferred_element_type=_F32)
        z = cg * v
        z_prev = pltpu.roll(z, 1, axis=0) * not_first
        z_next = pltpu.roll(z, tm - 1, axis=0) * not_last
        y = (cb_ref[:, lo:lo + nc] + z_prev * cw[0:1, lo:lo + nc] + z * cw[1:2, lo:lo + nc]
             + z_next * cw[2:3, lo:lo + nc])
        conv_ref[:, lo:lo + nc] = bg * y


def _inproj(x2, mod3, norm_g, w_in_bf, conv_w, conv_b):
    d_in = w_in_bf.shape[1]
    tm = 2 * TM
    tiles_per_batch = SEQ // tm
    return pl.pallas_call(
        _inproj_kernel,
        grid=(N_TOK // tm,),
        in_specs=[pl.BlockSpec((tm, D_MODEL), lambda i: (i, 0)),
                  pl.BlockSpec((1, N_MOD, D_MODEL), lambda i: (i // tiles_per_batch, 0, 0)),
                  _const_spec((1, D_MODEL)),
                  _const_spec((D_MODEL, d_in)),
                  _const_spec((8, D_CONV)),
                  _const_spec((1, D_CONV))],
        out_specs=[pl.BlockSpec((tm, D_SSM), lambda i: (i, 0)),
                   pl.BlockSpec((tm, D_CONV), lambda i: (i, 0))],
        out_shape=[jax.ShapeDtypeStruct((N_TOK, D_SSM), _F32),
                   jax.ShapeDtypeStruct((N_TOK, D_CONV), _F32)],
        compiler_params=_cparams("arbitrary"),
        name="in_proj",
    )(x2, mod3, norm_g, w_in_bf, conv_w, conv_b)


def _ctxproj_kernel(x_ref, mod_ref, g_ref, w_ref, u_ref):
    h = _modulated_norm(x_ref, mod_ref, g_ref, 0, 1).astype(_BF16)
    u_ref[...] = jnp.dot(h, w_ref[...], preferred_element_type=_F32)


def _ctxproj(ctx2, mod3, norm_g, w_u_bf):
    return pl.pallas_call(
        _ctxproj_kernel,
        grid=(N_CTX_TOK // TM,),
        in_specs=[pl.BlockSpec((TM, D_MODEL), lambda i: (i, 0)),
                  pl.BlockSpec((1, N_MOD, D_MODEL), lambda i: (BATCH, 0, 0)),
                  _const_spec((1, D_MODEL)),
                  _const_spec((D_MODEL, D_SSM))],
        out_specs=pl.BlockSpec((TM, D_SSM), lambda i: (i, 0)),
        out_shape=jax.ShapeDtypeStruct((N_CTX_TOK, D_SSM), _F32),
        compiler_params=_cparams("arbitrary"),
        name="ctx_proj",
    )(ctx2, mod3, norm_g, w_u_bf)


N_POW = 32


def _s5_param_kernel(prow_ref, ct_ref, bt_ref, toep_ref, bpow_ref, cpow_ref, dec_ref):
    for gi in range(toep_ref.shape[0]):
        one = pl.ds(gi, 1)
        _s5_param_group(prow_ref.at[:, one], ct_ref.at[:, one], bt_ref.at[:, one], toep_ref.at[one],
                        bpow_ref.at[one], cpow_ref.at[one], dec_ref.at[one])


def _s5_param_group(prow_ref, ct_ref, bt_ref, toep_ref, bpow_ref, cpow_ref, dec_ref):
    lane = lambda shape: lax.broadcasted_iota(jnp.int32, shape, 1)
    sub = lambda shape: lax.broadcasted_iota(jnp.int32, shape, 0)

    def split2(a):
        a_hi = a.astype(_BF16)
        return a_hi, (a - a_hi.astype(_F32)).astype(_BF16)

    rep = (lane((SSM_GROUP, CHUNK_W)) % SSM_GROUP == sub((SSM_GROUP, CHUNK_W))).astype(_BF16)
    sgn_col = jnp.where(sub((STATE_W, 1)) < SSM_STATE, 1.0, -1.0).astype(_F32)
    sgn_row = jnp.where(lane((1, STATE_W)) < SSM_STATE, -1.0, 1.0).astype(_F32)
    tau_col = sub((N_POW, 1)).astype(_F32)
    blk_l = lane((N_POW, CHUNK_W)) // SSM_GROUP
    tau_s = sub((N_POW, CHUNK_W))
    blk_r = sub((CHUNK_W, N_POW)) // SSM_GROUP
    tau_l = lane((CHUNK_W, N_POW))
    pick_l = lambda e: (tau_s == e).astype(_BF16)
    pick_r = lambda e: (tau_l == e).astype(_BF16)
    tn = (((0,), (0,)), ((), ()))

    strips = []
    for d in range(2):
        pr = prow_ref[d, 0]
        lam_re, lam_im, dt = pr[0:1], pr[1:2], jnp.exp(pr[2:3])
        mag = jnp.exp(tau_col * (lam_re * dt))
        ang = tau_col * (lam_im * dt)
        pw_re = mag * jnp.cos(ang)
        pw_im = mag * jnp.sin(ang)
        pw4 = jnp.concatenate(split2(pw_re) + split2(pw_im), axis=1)

        c_hi, c_lo = split2(ct_ref[d, 0])
        ct = (jnp.dot(c_hi, rep, preferred_element_type=_F32)
              + jnp.dot(c_lo, rep, preferred_element_type=_F32))
        ca = ct * sgn_col
        cb = -pltpu.roll(ct, SSM_STATE, axis=0)

        def cpow(sel):
            o = lax.dot_general(pw4, sel, tn, preferred_element_type=_F32)
            x = o[0:STATE_W] + o[STATE_W:2 * STATE_W]
            y = o[2 * STATE_W:3 * STATE_W] + o[3 * STATE_W:4 * STATE_W]
            return x * ca + y * cb

        nr = pw_re[1:2] - 1.0
        ni = pw_im[1:2]
        den = lam_re * lam_re + lam_im * lam_im
        kr = (nr * lam_re + ni * lam_im) / den
        ki = (ni * lam_re - nr * lam_im) / den
        b1 = bt_ref[d, 0]
        b2 = pltpu.roll(b1, SSM_STATE, axis=1) * sgn_row
        u1 = kr * b1 + ki * b2
        u2 = kr * b2 - ki * b1
        u1t = jnp.concatenate([u1] * CHUNK, axis=0)
        u2t = jnp.concatenate([u2] * CHUNK, axis=0)

        def bpow(sel):
            o = jnp.dot(sel, pw4, preferred_element_type=_F32)
            x = o[:, 0:STATE_W] + o[:, STATE_W:2 * STATE_W]
            y = o[:, 2 * STATE_W:3 * STATE_W] + o[:, 3 * STATE_W:4 * STATE_W]
            return x * u1t + y * u2t

        if d == 0:
            e_strip, e_b, e_c = blk_l, (CHUNK - 1) - blk_r, blk_l + 1
        else:
            e_strip, e_b, e_c = (CHUNK - 1) - blk_l, blk_r, CHUNK - blk_l
        u_hi, u_lo = split2(u1)
        k_hi, k_lo = split2(cpow(pick_l(e_strip)))
        strips.append(jnp.dot(u_hi, k_hi, preferred_element_type=_F32)
                      + jnp.dot(u_hi, k_lo, preferred_element_type=_F32)
                      + jnp.dot(u_lo, k_hi, preferred_element_type=_F32))
        bp = bpow(pick_r(e_b))
        bpow_ref[0, :, d * STATE_W:(d + 1) * STATE_W] = bp.astype(_BF16)
        bpow_ref[0, :, (2 + d) * STATE_W:(3 + d) * STATE_W] = pltpu.roll(bp, SSM_STATE, axis=1).astype(_BF16)
        cpow_ref[0, d * STATE_W:(d + 1) * STATE_W, :] = cpow(pick_l(e_c)).astype(_BF16)

        dec_ref[0, 2 * d:2 * d + 1, :] = pw_re[CHUNK:CHUNK + 1]
        dec_ref[0, 2 * d + 1:2 * d + 2, :] = pw_im[CHUNK:CHUNK + 1] * sgn_row

    zeros = jnp.zeros((SSM_GROUP, CHUNK_W), _F32)
    strip = (jnp.concatenate([strips[1], zeros], axis=1)
             + pltpu.roll(jnp.concatenate([strips[0], zeros], axis=1), CHUNK_W - SSM_GROUP, axis=1))
    for i in range(CHUNK):
        off = (CHUNK - 1 - i) * SSM_GROUP
        win = strip if off == 0 else pltpu.roll(strip, 2 * CHUNK_W - off, axis=1)
        toep_ref[0, i * SSM_GROUP:(i + 1) * SSM_GROUP, :] = win[:, 0:CHUNK_W].astype(_BF16)


def _s5_params(prow, ct2, bt1):
    ng = 4
    g_spec = lambda shape: pl.BlockSpec(shape, lambda g: (0, g, 0, 0))
    o_spec = lambda shape: pl.BlockSpec(shape, lambda g: (g, 0, 0))
    return pl.pallas_call(
        _s5_param_kernel,
        grid=(SSM_GROUPS // ng,),
        in_specs=[g_spec((2, ng, 8, STATE_W)),
                  g_spec((2, ng, STATE_W, SSM_GROUP)), g_spec((2, ng, SSM_GROUP, STATE_W))],
        out_specs=[o_spec((ng, CHUNK_W, CHUNK_W)), o_spec((ng, CHUNK_W, 4 * STATE_W)),
                   o_spec((ng, 2 * STATE_W, CHUNK_W)), o_spec((ng, 4, STATE_W))],
        out_shape=[jax.ShapeDtypeStruct((SSM_GROUPS, CHUNK_W, CHUNK_W), _BF16),
                   jax.ShapeDtypeStruct((SSM_GROUPS, CHUNK_W, 4 * STATE_W), _BF16),
                   jax.ShapeDtypeStruct((SSM_GROUPS, 2 * STATE_W, CHUNK_W), _BF16),
                   jax.ShapeDtypeStruct((SSM_GROUPS, 4, STATE_W), _F32)],
        compiler_params=_cparams("arbitrary"),
        name="s5_params",
    )(prow, ct2, bt1)


def _s5_kernel(ux_ref, uc_ref, toep_ref, bpow_ref, cpow_ref, dec_ref, y_ref, u_ref, s_ref, h_ref, y2_ref):
    ng = S5_GROUPS_PER_STEP
    gw = SSM_GROUP
    x0 = N_CTX_CHUNKS * BATCH

    half = CHUNK // 2
    assert half == ng
    lanes = ng * 128

    r = lax.broadcasted_iota(jnp.int32, (lanes, 1), 0)
    swapped = ((r // gw) % ng) * 128 + (r // 128) * gw + r % gw
    perm = (lax.broadcasted_iota(jnp.int32, (lanes, lanes), 1) == swapped).astype(_BF16)

    def to_chunk_layout(src_ref, seq_len, n_chunks, row0):
        for hf in range(2):
            p = jnp.concatenate(
                [jnp.concatenate([src_ref[pl.ds(b * seq_len + hf * half + il, n_chunks, stride=CHUNK), :]
                                  for il in range(half)], axis=1) for b in range(BATCH)], axis=0).astype(_BF16)
            q = jnp.dot(p, perm, preferred_element_type=_F32)
            for b in range(BATCH):
                for g in range(ng):
                    u_ref[g, hf, pl.ds(row0 + b, n_chunks, stride=BATCH), :] = (
                        q[b * n_chunks:(b + 1) * n_chunks, g * 128:(g + 1) * 128])

    to_chunk_layout(uc_ref, CTX_LEN, N_CTX_CHUNKS, 0)
    to_chunk_layout(ux_ref, SEQ, N_X_CHUNKS, x0)

    chunk_rows = lambda ref, g, lo: jnp.concatenate([ref[g, 0, lo:, :], ref[g, 1, lo:, :]], axis=1)
    for g in range(ng):
        s_ref[g] = jnp.dot(chunk_rows(u_ref, g, 0).astype(_BF16), bpow_ref[g], preferred_element_type=_F32)

    pair = 2 * BATCH
    n_ctx_pairs = N_CTX_CHUNKS // 2
    n_pairs = N_SEQ_CHUNKS // 2
    lower = lax.broadcasted_iota(jnp.int32, (pair, STATE_W), 0) < BATCH
    zero = jnp.zeros((pair, STATE_W), _F32)

    def step(t, carry):
        tb = jnp.where(t < n_ctx_pairs, n_ctx_pairs - 1 - t, n_pairs + n_ctx_pairs - 1 - t)
        rf = pl.multiple_of(t * pair, pair)
        rb = pl.multiple_of(tb * pair, pair)
        out = []
        swap = lambda v: pltpu.roll(v, BATCH, axis=0)
        for g in range(ng):
            hf, hfs, hb, hbs = carry[4 * g:4 * g + 4]
            dg = dec_ref[g]

            def advance(h, hs, a_re, a_im, s, ss):
                return a_re * h + a_im * hs + s, a_re * hs - a_im * h + ss

            col = lambda r, k: s_ref[g, pl.ds(r, pair), k * STATE_W:(k + 1) * STATE_W]
            sf, sfs = col(rf, 0), col(rf, 2)
            mid, mids = advance(hf, hfs, dg[0:1], dg[1:2], jnp.where(lower, sf, swap(sf)),
                                jnp.where(lower, sfs, swap(sfs)))
            h_ref[g, pl.ds(rf, pair), 0:STATE_W] = jnp.where(lower, hf, mid)
            end, ends = advance(mid, mids, dg[0:1], dg[1:2], sf, sfs)
            hf, hfs = jnp.where(lower, swap(end), end), jnp.where(lower, swap(ends), ends)

            sb, sbs = col(rb, 1), col(rb, 3)
            mid, mids = advance(hb, hbs, dg[2:3], dg[3:4], jnp.where(lower, swap(sb), sb),
                                jnp.where(lower, swap(sbs), sbs))
            h_ref[g, pl.ds(rb, pair), STATE_W:2 * STATE_W] = jnp.where(lower, mid, hb)
            end, ends = advance(mid, mids, dg[2:3], dg[3:4], sb, sbs)
            hb, hbs = jnp.where(lower, end, swap(end)), jnp.where(lower, ends, swap(ends))
            out += [hf, hfs, hb, hbs]
        return tuple(out)

    lax.fori_loop(0, n_pairs, step, (zero,) * (4 * ng))

    for g in range(ng):
        ux = chunk_rows(u_ref, g, x0).astype(_BF16)
        hx = h_ref[g, x0:SCAN_ROWS, :].astype(_BF16)
        y2 = (jnp.dot(ux, toep_ref[g], preferred_element_type=_F32)
              + jnp.dot(hx, cpow_ref[g], preferred_element_type=_F32))
        y2_ref[g, 0] = y2[:, 0:128]
        y2_ref[g, 1] = y2[:, 128:256]

    for hf in range(2):
        yin = jnp.concatenate(
            [jnp.concatenate([y2_ref[g, hf, pl.ds(b, N_X_CHUNKS, stride=BATCH), :] for g in range(ng)], axis=1)
             for b in range(BATCH)], axis=0)
        y_hi = yin.astype(_BF16)
        y_lo = (yin - y_hi.astype(_F32)).astype(_BF16)
        q = (jnp.dot(y_hi, perm, preferred_element_type=_F32)
             + jnp.dot(y_lo, perm, preferred_element_type=_F32))
        for b in range(BATCH):
            for jl in range(half):
                y_ref[pl.ds(b * SEQ + hf * half + jl, N_X_CHUNKS, stride=CHUNK), :] = (
                    q[b * N_X_CHUNKS:(b + 1) * N_X_CHUNKS, jl * 128:(jl + 1) * 128])


def _s5_scan(u_x, u_c, toep, bpow, cpow, dec):
    ng = S5_GROUPS_PER_STEP
    n_x_rows = N_X_CHUNKS * BATCH
    lanes = ng * SSM_GROUP
    spec = lambda r, c: pl.BlockSpec((ng, r, c), lambda i: (i, 0, 0))
    col = lambda rows: pl.BlockSpec((rows, lanes), lambda i: (0, i))
    return pl.pallas_call(
        _s5_kernel,
        grid=(SSM_GROUPS // ng,),
        in_specs=[col(N_TOK), col(N_CTX_TOK), spec(CHUNK_W, CHUNK_W), spec(CHUNK_W, 4 * STATE_W),
                  spec(2 * STATE_W, CHUNK_W), spec(4, STATE_W)],
        out_specs=col(N_TOK),
        out_shape=jax.ShapeDtypeStruct((N_TOK, D_SSM), _F32),
        scratch_shapes=[pltpu.VMEM((ng, 2, SCAN_ROWS, 128), _F32),
                        pltpu.VMEM((ng, SCAN_ROWS, 4 * STATE_W), _F32),
                        pltpu.VMEM((ng, SCAN_ROWS, 2 * STATE_W), _F32),
                        pltpu.VMEM((ng, 2, n_x_rows, 128), _F32)],
        compiler_params=_cparams("arbitrary"),
        name="s5_scan",
    )(u_x, u_c, toep, bpow, cpow, dec)


def _merge_kernel(ys_ref, u_ref, conv_ref, x_ref, mod_ref, dskip_ref, mixg_ref, n2g_ref,
                  wglu_ref, wout_ref, rwt_ref, x1_ref, h2_ref, logit_ref):
    m = mod_ref[0]
    yx = dskip_ref[...] * u_ref[...] + ys_ref[...]
    c0 = math.sqrt(2.0 / math.pi)
    ge = 0.5 * yx * (1.0 + jnp.tanh(c0 * (yx + 0.044715 * (yx * yx * yx))))
    z = jnp.dot(ge.astype(_BF16), wglu_ref[...], preferred_element_type=_F32)
    ssm_y = z[:, 0:D_SSM] * jax.nn.sigmoid(z[:, D_SSM:2 * D_SSM])
    conv_y = conv_ref[...]
    mixg = mixg_ref[...]
    heads_a = (ssm_y * _rms_scale(ssm_y) * mixg[:, 0:D_SSM]).astype(_BF16)
    heads_b = (conv_y * _rms_scale(conv_y) * mixg[:, D_SSM:]).astype(_BF16)
    mix = (jnp.dot(heads_a, wout_ref[0:D_SSM, :], preferred_element_type=_F32)
           + jnp.dot(heads_b, wout_ref[D_SSM:, :], preferred_element_type=_F32))
    x1 = x_ref[...] + m[2:3] * mix
    x1_ref[...] = x1
    h2 = x1 * _rms_scale(x1) * n2g_ref[...] * (1.0 + m[4:5]) + m[3:4]
    h2_ref[...] = h2
    h_hi = h2.astype(_BF16)
    h_lo = (h2 - h_hi.astype(_F32)).astype(_BF16)
    nt = (((1,), (1,)), ((), ()))
    p = lax.dot_general(rwt_ref[...], h_hi, nt, preferred_element_type=_F32)
    q = lax.dot_general(rwt_ref[0:N_EXPERTS, :], h_lo, nt, preferred_element_type=_F32)
    logit_ref[...] = p[0:N_EXPERTS] + p[N_EXPERTS:2 * N_EXPERTS] + q


def _merge(ys, u, conv, x2, mod3, dskip, mixg, n2g, wglu_bf, wout_bf, rwt):
    tiles_per_batch = SEQ // TM
    tok = lambda w: pl.BlockSpec((TM, w), lambda i: (i, 0))
    return pl.pallas_call(
        _merge_kernel,
        grid=(N_TOK // TM,),
        in_specs=[tok(D_SSM), tok(D_SSM), tok(D_CONV), tok(D_MODEL),
                  pl.BlockSpec((1, N_MOD, D_MODEL), lambda i: (i // tiles_per_batch, 0, 0)),
                  _const_spec((1, D_SSM)), _const_spec((1, D_MODEL)), _const_spec((1, D_MODEL)),
                  _const_spec((D_SSM, 2 * D_SSM)), _const_spec((D_MODEL, D_MODEL)),
                  _const_spec((2 * N_EXPERTS, D_MODEL))],
        out_specs=[tok(D_MODEL), tok(D_MODEL),
                   pl.BlockSpec((N_EXPERTS, TM), lambda i: (0, i))],
        out_shape=[jax.ShapeDtypeStruct((N_TOK, D_MODEL), _F32),
                   jax.ShapeDtypeStruct((N_TOK, D_MODEL), _F32),
                   jax.ShapeDtypeStruct((N_EXPERTS, N_TOK), _F32)],
        compiler_params=_cparams("arbitrary"),
        name="merge_heads",
    )(ys, u, conv, x2, mod3, dskip, mixg, n2g, wglu_bf, wout_bf, rwt)


def _route_kernel(logit_ref, bias_ref, eid_ref, w_ref, rank_ref, cnt_ref, carry_ref):
    @pl.when(pl.program_id(0) == 0)
    def _():
        carry_ref[...] = jnp.zeros_like(carry_ref)

    tm = logit_ref.shape[1]
    neg = jnp.float32(-jnp.inf)
    scores = jax.nn.sigmoid(logit_ref[...])
    biased = scores + bias_ref[:, 0:1]
    sub = lax.broadcasted_iota(jnp.int32, (GROUP_SIZE, tm), 0)
    rows = lambda a, g: a[g * GROUP_SIZE:(g + 1) * GROUP_SIZE]
    ngrp = N_EXPERT_GROUPS

    gscore = []
    for g in range(ngrp):
        bg = rows(biased, g)
        m1 = jnp.max(bg, axis=0, keepdims=True)
        first = jnp.min(jnp.where(bg == m1, sub, GROUP_SIZE), axis=0, keepdims=True)
        m2 = jnp.max(jnp.where(sub == first, neg, bg), axis=0, keepdims=True)
        gscore.append(m1 + m2)
    v = []
    for g in range(ngrp):
        beaten = jnp.zeros((1, tm), jnp.int32)
        for o in range(ngrp):
            if o != g:
                beats = (gscore[o] >= gscore[g]) if o < g else (gscore[o] > gscore[g])
                beaten = beaten + beats.astype(jnp.int32)
        v.append(jnp.where(beaten < TOPK_GROUPS, rows(biased, g), neg))
    eids = [sub + g * GROUP_SIZE for g in range(ngrp)]

    pick_ids, pick_masks = [], []
    for _ in range(TOP_K):
        m = functools.reduce(jnp.maximum, v)
        m = jnp.max(m, axis=0, keepdims=True)
        cand = functools.reduce(jnp.minimum, [jnp.where(v[g] == m, eids[g], N_EXPERTS) for g in range(ngrp)])
        pick_id = jnp.min(cand, axis=0, keepdims=True)
        masks = [eids[g] == pick_id for g in range(ngrp)]
        v = [jnp.where(masks[g], neg, v[g]) for g in range(ngrp)]
        pick_ids.append(pick_id)
        pick_masks.append(masks)

    sel = jnp.concatenate(
        [functools.reduce(jnp.logical_or, [pick_masks[k][g] for k in range(TOP_K)]).astype(_F32)
         for g in range(ngrp)], axis=0)
    before = (lax.broadcasted_iota(jnp.int32, (tm, tm), 0)
              < lax.broadcasted_iota(jnp.int32, (tm, tm), 1)).astype(_BF16)
    base = jnp.dot(sel.astype(_BF16), before, preferred_element_type=_F32) + carry_ref[:, 0:1]

    def gather_pick(a, k):
        parts = [jnp.sum(jnp.where(pick_masks[k][g], rows(a, g), 0.0), axis=0, keepdims=True)
                 for g in range(ngrp)]
        return functools.reduce(jnp.add, parts)

    picked = [gather_pick(scores, k) for k in range(TOP_K)]
    denom = functools.reduce(jnp.add, picked)
    for k in range(TOP_K):
        eid_ref[k:k + 1, :] = pick_ids[k]
        w_ref[k:k + 1, :] = picked[k] / denom * ROUTED_SCALE
        rank_ref[k:k + 1, :] = gather_pick(base, k).astype(jnp.int32)
    carry_ref[...] = carry_ref[...] + jnp.sum(sel, axis=1, keepdims=True)
    cnt_ref[...] = carry_ref[...]


def _route(logits_t, bias_col):
    tok = pl.BlockSpec((TOP_K, TM), lambda i: (0, i))
    return pl.pallas_call(
        _route_kernel,
        grid=(N_TOK // TM,),
        in_specs=[pl.BlockSpec((N_EXPERTS, TM), lambda i: (0, i)), _const_spec((N_EXPERTS, 128))],
        out_specs=[tok, tok, tok, pl.BlockSpec((N_EXPERTS, 128), lambda i: (0, 0))],
        out_shape=[jax.ShapeDtypeStruct((TOP_K, N_TOK), jnp.int32),
                   jax.ShapeDtypeStruct((TOP_K, N_TOK), _F32),
                   jax.ShapeDtypeStruct((TOP_K, N_TOK), jnp.int32),
                   jax.ShapeDtypeStruct((N_EXPERTS, 128), _F32)],
        scratch_shapes=[pltpu.VMEM((N_EXPERTS, 128), _F32)],
        compiler_params=_cparams("arbitrary"),
        name="route",
    )(logits_t, bias_col)


def _rowmap_kernel(dest_ref, padstart_ref, padn_ref, h_ref, x1_ref, mod_ref, wsg_ref, wsu_ref, wsd_ref,
                   xmid_ref, codes_ref):
    i = pl.program_id(0)
    tm = h_ref.shape[0]
    base = i * tm

    @pl.when(i == 0)
    def _():
        def fill(e, done):
            def one(r, c):
                codes_ref[padstart_ref[e] + r] = N_ASSIGN + done + r
                return c
            lax.fori_loop(0, padn_ref[e], one, 0)
            return done + padn_ref[e]
        lax.fori_loop(0, N_EXPERTS, fill, 0)
        for r in range(EXPERT_BM):
            codes_ref[N_ROWS + r] = N_ROWS + r

    for t in range(tm):
        ds = [dest_ref[k * N_TOK + base + t] for k in range(TOP_K)]
        for k in range(TOP_K):
            codes_ref[ds[k]] = k * N_TOK + base + t

    h = h_ref[...].astype(_BF16)
    a = jnp.dot(h, wsg_ref[...], preferred_element_type=_F32)
    b = jnp.dot(h, wsu_ref[...], preferred_element_type=_F32)
    shared = jnp.dot((_silu(a) * b).astype(_BF16), wsd_ref[...], preferred_element_type=_F32)
    xmid_ref[...] = x1_ref[...] + mod_ref[0][5:6] * shared


def _rowmap(dest, pad_start, pad_n, h2, x1, mod3, wsg_bf, wsu_bf, wsd_bf):
    tm = 2 * TM
    tiles_per_batch = SEQ // tm
    tok = pl.BlockSpec((tm, D_MODEL), lambda i, *_: (i, 0))
    const = lambda shape: pl.BlockSpec(shape, lambda i, *_: (0,) * len(shape), pipeline_mode=pl.Buffered(1))
    grid_spec = pltpu.PrefetchScalarGridSpec(
        num_scalar_prefetch=3,
        grid=(N_TOK // tm,),
        in_specs=[tok, tok,
                  pl.BlockSpec((1, N_MOD, D_MODEL), lambda i, *_: (i // tiles_per_batch, 0, 0)),
                  const((D_MODEL, D_SHARED)), const((D_MODEL, D_SHARED)), const((D_SHARED, D_MODEL))],
        out_specs=[tok, pl.BlockSpec(memory_space=pltpu.SMEM)],
    )
    return pl.pallas_call(
        _rowmap_kernel,
        grid_spec=grid_spec,
        out_shape=[jax.ShapeDtypeStruct((N_TOK, D_MODEL), _F32),
                   jax.ShapeDtypeStruct((N_ROWS + EXPERT_BM,), jnp.int32)],
        compiler_params=_cparams("arbitrary"),
        name="rowmap",
    )(dest.reshape(N_ASSIGN), pad_start, pad_n, h2, x1, mod3, wsg_bf, wsu_bf, wsd_bf)


def _experts_kernel(be_ref, nact_ref, first_ref, slot_ref, nxt_ref, codes_ref, h_hbm, wg_hbm, wu_hbm, wd_hbm,
                    out_hbm, *scratch):
    depth = EXPERT_DEPTH
    xbuf, ybuf = scratch[:depth], scratch[depth:2 * depth]
    wg_f, wu_f, wd_f, wg_bf, wu_bf, wd_bf, sem, gsem, ssem = scratch[2 * depth:]
    n_act = nact_ref[0]
    bm = EXPERT_BM

    def fetch(e, s):
        return (pltpu.make_async_copy(wg_hbm.at[e], wg_f.at[s], sem.at[s, 0]),
                pltpu.make_async_copy(wu_hbm.at[e], wu_f.at[s], sem.at[s, 1]),
                pltpu.make_async_copy(wd_hbm.at[e], wd_f.at[s], sem.at[s, 2]))

    def gather(blk, p):
        for r in range(bm):
            tok = codes_ref[blk * bm + r] & (N_TOK - 1)
            pltpu.make_async_copy(h_hbm.at[pl.ds(tok, 1), :], xbuf[p].at[pl.ds(r, 1), :],
                                  gsem.at[p]).start(priority=r % 2)

    def gather_wait(p):
        pltpu.make_async_copy(h_hbm.at[pl.ds(0, bm), :], xbuf[p], gsem.at[p]).wait()

    def scatter(blk, p):
        for r in range(bm):
            pltpu.make_async_copy(ybuf[p].at[pl.ds(r, 1), :], out_hbm.at[pl.ds(codes_ref[blk * bm + r], 1), :],
                                  ssem.at[p]).start(priority=r % 2)

    def scatter_wait(p):
        pltpu.make_async_copy(ybuf[p], out_hbm.at[pl.ds(0, bm), :], ssem.at[p]).wait()

    def block(blk, p):
        prev = (p + depth - 1) % depth

        @pl.when(blk < n_act)
        def _():
            @pl.when(blk == 0)
            def _():
                for cp in fetch(be_ref[0], 0):
                    cp.start()
                for j in range(depth - 1):
                    gather(jnp.minimum(j, n_act - 1), j)
                ybuf[depth - 1][...] = jnp.zeros_like(ybuf[depth - 1])

            @pl.when(first_ref[blk] == 1)
            def _():
                s = slot_ref[blk]
                for cp in fetch(be_ref[blk], s):
                    cp.wait()

                @pl.when(nxt_ref[blk] >= 0)
                def _():
                    for cp in fetch(nxt_ref[blk], 1 - s):
                        cp.start()

                wg_bf[...] = wg_f[s].astype(_BF16)
                wu_bf[...] = wu_f[s].astype(_BF16)
                wd_bf[...] = wd_f[s].astype(_BF16)

            gather_wait(p)

            @pl.when(blk >= depth - 1)
            def _():
                scatter_wait(p)

            gather(jnp.minimum(blk + depth - 1, n_act - 1), prev)
            scatter(jnp.where(blk == 0, N_BLOCKS, blk - 1), prev)
            x = xbuf[p][...].astype(_BF16)
            a = jnp.dot(x, wg_bf[...], preferred_element_type=_F32)
            b = jnp.dot(x, wu_bf[...], preferred_element_type=_F32)
            ybuf[p][...] = jnp.dot((_silu(a) * b).astype(_BF16), wd_bf[...], preferred_element_type=_F32)

            @pl.when(blk == n_act - 1)
            def _():
                scatter(blk, p)
                for j in range(1, depth):
                    gather_wait((p + j) % depth)
                for j in range(depth - 1):
                    pl.when(blk >= j)(functools.partial(scatter_wait, (p + depth - 1 - j) % depth))
                scatter_wait(p)

    i = pl.program_id(0)
    for q in range(depth):
        block(depth * i + q, q)


def _experts(block_e, n_active, first, slot, nxt, codes, h2, w_gate, w_up, w_down):
    bm = EXPERT_BM
    depth = EXPERT_DEPTH
    hbm = pl.BlockSpec(memory_space=pl.ANY)
    grid_spec = pltpu.PrefetchScalarGridSpec(
        num_scalar_prefetch=6,
        grid=(N_BLOCKS // depth,),
        in_specs=[hbm, hbm, hbm, hbm],
        out_specs=hbm,
        scratch_shapes=[pltpu.VMEM((bm, D_MODEL), _F32)] * (2 * depth) + [
                        pltpu.VMEM((2, D_MODEL, D_EXPERT), _F32), pltpu.VMEM((2, D_MODEL, D_EXPERT), _F32),
                        pltpu.VMEM((2, D_EXPERT, D_MODEL), _F32),
                        pltpu.VMEM((D_MODEL, D_EXPERT), _BF16), pltpu.VMEM((D_MODEL, D_EXPERT), _BF16),
                        pltpu.VMEM((D_EXPERT, D_MODEL), _BF16),
                        pltpu.SemaphoreType.DMA((2, 3)), pltpu.SemaphoreType.DMA((depth,)),
                        pltpu.SemaphoreType.DMA((depth,))],
    )
    return pl.pallas_call(
        _experts_kernel,
        grid_spec=grid_spec,
        out_shape=jax.ShapeDtypeStruct((N_ROWS + bm, D_MODEL), _F32),
        compiler_params=_cparams("arbitrary"),
        name="experts",
    )(block_e, n_active, first, slot, nxt, codes, h2, w_gate, w_up, w_down)


def _final_kernel(xmid_ref, wt_ref, mod_ref, fg_ref, *refs):
    picks, o_ref = refs[:TOP_K], refs[TOP_K]
    wt = wt_ref[...]
    routed = wt[:, 0:1] * picks[0][...]
    for k in range(1, TOP_K):
        routed = routed + wt[:, k:k + 1] * picks[k][...]
    y = xmid_ref[...] + mod_ref[0][5:6] * routed
    o_ref[...] = y * _rms_scale(y) * fg_ref[...]


def _final(xmid, w_tok, mod3, final_g, ys):
    tm = TM_COMBINE
    tiles = N_TOK // tm
    tiles_per_batch = SEQ // tm
    tok = lambda w: pl.BlockSpec((tm, w), lambda i: (i, 0))
    pick = lambda k: pl.BlockSpec((tm, D_MODEL), lambda i: (k * tiles + i, 0))
    return pl.pallas_call(
        _final_kernel,
        grid=(tiles,),
        in_specs=[tok(D_MODEL), tok(TOP_K),
                  pl.BlockSpec((1, N_MOD, D_MODEL), lambda i: (i // tiles_per_batch, 0, 0)),
                  _const_spec((1, D_MODEL))] + [pick(k) for k in range(TOP_K)],
        out_specs=tok(D_MODEL),
        out_shape=jax.ShapeDtypeStruct((N_TOK, D_MODEL), _F32),
        compiler_params=_cparams("arbitrary"),
        name="final",
    )(xmid, w_tok, mod3, final_g, *([ys] * TOP_K))


def _s5_param_layouts(lam_re, lam_im, b_re, b_im, c_re, c_im, log_dt):
    two = lambda a: jnp.concatenate([a, a], axis=-1)
    lr, li = two(lam_re), two(lam_im)
    dtb = jnp.broadcast_to(log_dt.astype(_F32)[:, :, None], lr.shape)
    zeros = jnp.zeros_like(lr)
    prow = jnp.stack([lr, li, dtb] + [zeros] * 5, axis=2)
    ct2 = jnp.concatenate([jnp.swapaxes(c_re, -1, -2), jnp.swapaxes(c_im, -1, -2)], axis=2)
    bt1 = jnp.concatenate([jnp.swapaxes(b_re, -1, -2), jnp.swapaxes(b_im, -1, -2)], axis=3)
    return prow, ct2, bt1


def kernel(x, c, ctx, c_ctx, norm1_g, norm2_g, w_ada, b_ada, w_in, ssm_lam_re, ssm_lam_im, ssm_b_re, ssm_b_im, ssm_c_re, ssm_c_im, ssm_log_dt, ssm_d, ssm_w_glu, conv_w, conv_b, mix_norm_g, w_out, router_w, router_bias, exp_w_gate, exp_w_up, exp_w_down, shared_w_gate, shared_w_up, shared_w_down, final_g):
    layer = 0
    x2 = x.reshape(N_TOK, D_MODEL)
    ctx2 = ctx.reshape(N_CTX_TOK, D_MODEL)

    c8 = jnp.concatenate([c, c_ctx[None, :], jnp.zeros((8 - BATCH - 1, D_MODEL), _F32)], axis=0)
    mod = _ada_mod(c8, w_ada[layer], b_ada[layer][None, :])
    mod3 = mod.reshape(8, N_MOD, D_MODEL)

    w_in_bf = w_in[layer].astype(_BF16)
    conv_w8 = jnp.concatenate([conv_w[layer], jnp.zeros((8 - conv_w.shape[1], D_CONV), _F32)], axis=0)
    u_x, conv_x = _inproj(x2, mod3, norm1_g[layer][None, :], w_in_bf, conv_w8, conv_b[layer][None, :])
    u_c = _ctxproj(ctx2, mod3, norm1_g[layer][None, :], w_in_bf[:, :D_SSM])

    prow, ct2, bt1 = _s5_param_layouts(ssm_lam_re[layer], ssm_lam_im[layer], ssm_b_re[layer],
                                       ssm_b_im[layer], ssm_c_re[layer], ssm_c_im[layer], ssm_log_dt[layer])
    toep, bpow, cpow, dec = _s5_params(prow, ct2, bt1)
    ys = _s5_scan(u_x, u_c, toep, bpow, cpow, dec)

    rw = router_w[layer]
    rw_hi = rw.astype(_BF16)
    rw_lo = (rw - rw_hi.astype(_F32)).astype(_BF16)
    rwt = jnp.concatenate([rw_hi.T, rw_lo.T], axis=0)
    x1, h2, logits_t = _merge(ys, u_x, conv_x, x2, mod3, ssm_d[layer][None, :],
                              mix_norm_g[layer][None, :], norm2_g[layer][None, :],
                              ssm_w_glu[layer].astype(_BF16), w_out[layer].astype(_BF16), rwt)

    bias_col = jnp.broadcast_to(router_bias[layer][:, None], (N_EXPERTS, 128))
    eid, w_k, rank, cnt = _route(logits_t, bias_col)

    counts = cnt[:, 0].astype(jnp.int32)
    padded = (counts + EXPERT_BM - 1) // EXPERT_BM * EXPERT_BM
    pend = jnp.cumsum(padded)
    pstart = pend - padded
    is_e = eid[:, :, None] == jnp.arange(N_EXPERTS, dtype=jnp.int32)
    dest = jnp.sum(jnp.where(is_e, pstart, 0), axis=-1) + rank
    n_active = (pend[-1] // EXPERT_BM).astype(jnp.int32)
    blk = jnp.minimum(jnp.arange(N_BLOCKS, dtype=jnp.int32), n_active - 1)
    ends_before = (pend[None, :] <= (blk * EXPERT_BM)[:, None]).astype(jnp.int32)
    block_e = jnp.minimum(jnp.sum(ends_before, axis=1), N_EXPERTS - 1)

    xmid, codes = _rowmap(dest, pstart + counts, padded - counts, h2, x1, mod3,
                          shared_w_gate[layer].astype(_BF16), shared_w_up[layer].astype(_BF16),
                          shared_w_down[layer].astype(_BF16))
    first = jnp.concatenate([jnp.ones((1,), jnp.int32), (block_e[1:] != block_e[:-1]).astype(jnp.int32)])
    slot = (jnp.cumsum(first) - 1) % 2
    e_ids = jnp.arange(N_EXPERTS, dtype=jnp.int32)
    owner = jnp.where(padded > 0, e_ids, N_EXPERTS)
    later = jnp.min(jnp.where(e_ids[None, :] > e_ids[:, None], owner[None, :], N_EXPERTS), axis=1)
    nxt_e = jnp.where(later == N_EXPERTS, -1, later)
    nxt = jnp.sum(jnp.where(block_e[:, None] == e_ids[None, :], nxt_e[None, :], 0), axis=1)
    ys_rows = _experts(block_e, n_active[None], first, slot, nxt, codes, h2,
                       exp_w_gate[layer], exp_w_up[layer], exp_w_down[layer])
    out = _final(xmid, w_k.T, mod3, final_g[None, :], ys_rows)
    return out.reshape(BATCH, SEQ, D_MODEL)
```

```python
import functools
import math

import jax
import jax.numpy as jnp
from jax import lax
from jax.experimental import pallas as pl
from jax.experimental.pallas import tpu as pltpu

D_MODEL = 2048
BATCH = 4
SEQ = 2048
CTX_LEN = 256
GRID_W = 64
D_SSM = 1024
D_CONV = 1024
SSM_GROUP = 16
SSM_GROUPS = 64
SSM_STATE = 64
N_EXPERTS = 64
N_EXPERT_GROUPS = 8
GROUP_SIZE = N_EXPERTS // N_EXPERT_GROUPS
TOPK_GROUPS = 4
TOP_K = 8
D_EXPERT = 512
D_SHARED = 512
ROUTED_SCALE = 2.5
N_MOD = 6
EPS = 1e-6

N_TOK = BATCH * SEQ
N_CTX_TOK = BATCH * CTX_LEN

CHUNK = 16
CHUNK_W = CHUNK * SSM_GROUP
STATE_W = 2 * SSM_STATE
N_CTX_CHUNKS = CTX_LEN // CHUNK
N_X_CHUNKS = SEQ // CHUNK
N_SEQ_CHUNKS = N_CTX_CHUNKS + N_X_CHUNKS
SCAN_ROWS = N_SEQ_CHUNKS * BATCH
S5_GROUPS_PER_STEP = 8

TM = 256
EXPERT_BM = 256
N_ASSIGN = N_TOK * TOP_K
EXPERT_DEPTH = 4
N_BLOCKS = -(-(N_ASSIGN // EXPERT_BM + N_EXPERTS) // EXPERT_DEPTH) * EXPERT_DEPTH
N_ROWS = N_BLOCKS * EXPERT_BM
TM_COMBINE = 128
VMEM_LIMIT = 56 * 1024 * 1024

_F32 = jnp.float32
_BF16 = jnp.bfloat16


def _cparams(*sem):
    return pltpu.CompilerParams(dimension_semantics=sem, vmem_limit_bytes=VMEM_LIMIT)


def _const_spec(shape):
    nd = len(shape)
    return pl.BlockSpec(shape, lambda *_: (0,) * nd, pipeline_mode=pl.Buffered(1))


def _rms_scale(xf):
    return lax.rsqrt(jnp.mean(xf * xf, axis=-1, keepdims=True) + EPS)


def _silu(x):
    return x * jax.nn.sigmoid(x)


def _ada_kernel(c_ref, w_ref, b_ref, o_ref):
    s = _silu(c_ref[...])
    o_ref[...] = jnp.dot(s, w_ref[...], preferred_element_type=_F32) + b_ref[...]


def _ada_mod(c8, w_ada, b_ada):
    n = w_ada.shape[1]
    tn = 1024
    return pl.pallas_call(
        _ada_kernel,
        grid=(n // tn,),
        in_specs=[pl.BlockSpec((8, D_MODEL), lambda j: (0, 0)),
                  pl.BlockSpec((D_MODEL, tn), lambda j: (0, j)),
                  pl.BlockSpec((1, tn), lambda j: (0, j))],
        out_specs=pl.BlockSpec((8, tn), lambda j: (0, j)),
        out_shape=jax.ShapeDtypeStruct((8, n), _F32),
        compiler_params=_cparams("arbitrary"),
        name="ada_mod",
    )(c8, w_ada, b_ada)


def _modulated_norm(x_ref, mod_ref, g_ref, shift_row, scale_row):
    xf = x_ref[...]
    m = mod_ref[0]
    h = xf * _rms_scale(xf) * g_ref[...]
    return h * (1.0 + m[scale_row:scale_row + 1]) + m[shift_row:shift_row + 1]


def _inproj_kernel(x_ref, mod_ref, g_ref, w_ref, cw_ref, cb_ref, u_ref, conv_ref):
    h = _modulated_norm(x_ref, mod_ref, g_ref, 0, 1).astype(_BF16)
    u_ref[...] = jnp.dot(h, w_ref[:, 0:D_SSM], preferred_element_type=_F32)
    tm = x_ref.shape[0]
    pos = lax.broadcasted_iota(jnp.int32, (tm, 1), 0) % GRID_W
    not_first = (pos != 0).astype(_F32)
    not_last = (pos != GRID_W - 1).astype(_F32)
    cw = cw_ref[...]
    nc = 256
    for j in range(D_CONV // nc):
        lo = j * nc
        bg = jnp.dot(h, w_ref[:, D_SSM + lo:D_SSM + lo + nc], preferred_element_type=_F32)
        cg = jnp.dot(h, w_ref[:, D_SSM + D_CONV + lo:D_SSM + D_CONV + lo + nc], preferred_element_type=_F32)
        v = jnp.dot(h, w_ref[:, D_SSM + 2 * D_CONV + lo:D_SSM + 2 * D_CONV + lo + nc],
                    preferred_element_type=_F32)
        z = cg * v
        z_prev = pltpu.roll(z, 1, axis=0) * not_first
        z_next = pltpu.roll(z, tm - 1, axis=0) * not_last
        y = (cb_ref[:, lo:lo + nc] + z_prev * cw[0:1, lo:lo + nc] + z * cw[1:2, lo:lo + nc]
             + z_next * cw[2:3, lo:lo + nc])
        conv_ref[:, lo:lo + nc] = bg * y


def _inproj(x2, mod3, norm_g, w_in_bf, conv_w, conv_b):
    d_in = w_in_bf.shape[1]
    tm = 2 * TM
    tiles_per_batch = SEQ // tm
    return pl.pallas_call(
        _inproj_kernel,
        grid=(N_TOK // tm,),
        in_specs=[pl.BlockSpec((tm, D_MODEL), lambda i: (i, 0)),
                  pl.BlockSpec((1, N_MOD, D_MODEL), lambda i: (i // tiles_per_batch, 0, 0)),
                  _const_spec((1, D_MODEL)),
                  _const_spec((D_MODEL, d_in)),
                  _const_spec((8, D_CONV)),
                  _const_spec((1, D_CONV))],
        out_specs=[pl.BlockSpec((tm, D_SSM), lambda i: (i, 0)),
                   pl.BlockSpec((tm, D_CONV), lambda i: (i, 0))],
        out_shape=[jax.ShapeDtypeStruct((N_TOK, D_SSM), _F32),
                   jax.ShapeDtypeStruct((N_TOK, D_CONV), _F32)],
        compiler_params=_cparams("arbitrary"),
        name="in_proj",
    )(x2, mod3, norm_g, w_in_bf, conv_w, conv_b)


def _ctxproj_kernel(x_ref, mod_ref, g_ref, w_ref, u_ref):
    h = _modulated_norm(x_ref, mod_ref, g_ref, 0, 1).astype(_BF16)
    u_ref[...] = jnp.dot(h, w_ref[...], preferred_element_type=_F32)


def _ctxproj(ctx2, mod3, norm_g, w_u_bf):
    return pl.pallas_call(
        _ctxproj_kernel,
        grid=(N_CTX_TOK // TM,),
        in_specs=[pl.BlockSpec((TM, D_MODEL), lambda i: (i, 0)),
                  pl.BlockSpec((1, N_MOD, D_MODEL), lambda i: (BATCH, 0, 0)),
                  _const_spec((1, D_MODEL)),
                  _const_spec((D_MODEL, D_SSM))],
        out_specs=pl.BlockSpec((TM, D_SSM), lambda i: (i, 0)),
        out_shape=jax.ShapeDtypeStruct((N_CTX_TOK, D_SSM), _F32),
        compiler_params=_cparams("arbitrary"),
        name="ctx_proj",
    )(ctx2, mod3, norm_g, w_u_bf)


N_POW = 32


def _s5_param_kernel(prow_ref, ct_ref, bt_ref, toep_ref, bpow_ref, cpow_ref, dec_ref):
    for gi in range(toep_ref.shape[0]):
        one = pl.ds(gi, 1)
        _s5_param_group(prow_ref.at[:, one], ct_ref.at[:, one], bt_ref.at[:, one], toep_ref.at[one],
                        bpow_ref.at[one], cpow_ref.at[one], dec_ref.at[one])


def _s5_param_group(prow_ref, ct_ref, bt_ref, toep_ref, bpow_ref, cpow_ref, dec_ref):
    lane = lambda shape: lax.broadcasted_iota(jnp.int32, shape, 1)
    sub = lambda shape: lax.broadcasted_iota(jnp.int32, shape, 0)

    def split2(a):
        a_hi = a.astype(_BF16)
        return a_hi, (a - a_hi.astype(_F32)).astype(_BF16)

    rep = (lane((SSM_GROUP, CHUNK_W)) % SSM_GROUP == sub((SSM_GROUP, CHUNK_W))).astype(_BF16)
    sgn_col = jnp.where(sub((STATE_W, 1)) < SSM_STATE, 1.0, -1.0).astype(_F32)
    sgn_row = jnp.where(lane((1, STATE_W)) < SSM_STATE, -1.0, 1.0).astype(_F32)
    tau_col = sub((N_POW, 1)).astype(_F32)
    blk_l = lane((N_POW, CHUNK_W)) // SSM_GROUP
    tau_s = sub((N_POW, CHUNK_W))
    blk_r = sub((CHUNK_W, N_POW)) // SSM_GROUP
    tau_l = lane((CHUNK_W, N_POW))
    pick_l = lambda e: (tau_s == e).astype(_BF16)
    pick_r = lambda e: (tau_l == e).astype(_BF16)
    tn = (((0,), (0,)), ((), ()))

    strips = []
    for d in range(2):
        pr = prow_ref[d, 0]
        lam_re, lam_im, dt = pr[0:1], pr[1:2], jnp.exp(pr[2:3])
        mag = jnp.exp(tau_col * (lam_re * dt))
        ang = tau_col * (lam_im * dt)
        pw_re = mag * jnp.cos(ang)
        pw_im = mag * jnp.sin(ang)
        pw4 = jnp.concatenate(split2(pw_re) + split2(pw_im), axis=1)

        c_hi, c_lo = split2(ct_ref[d, 0])
        ct = (jnp.dot(c_hi, rep, preferred_element_type=_F32)
              + jnp.dot(c_lo, rep, preferred_element_type=_F32))
        ca = ct * sgn_col
        cb = -pltpu.roll(ct, SSM_STATE, axis=0)

        def cpow(sel):
            o = lax.dot_general(pw4, sel, tn, preferred_element_type=_F32)
            x = o[0:STATE_W] + o[STATE_W:2 * STATE_W]
            y = o[2 * STATE_W:3 * STATE_W] + o[3 * STATE_W:4 * STATE_W]
            return x * ca + y * cb

        nr = pw_re[1:2] - 1.0
        ni = pw_im[1:2]
        den = lam_re * lam_re + lam_im * lam_im
        kr = (nr * lam_re + ni * lam_im) / den
        ki = (ni * lam_re - nr * lam_im) / den
        b1 = bt_ref[d, 0]
        b2 = pltpu.roll(b1, SSM_STATE, axis=1) * sgn_row
        u1 = kr * b1 + ki * b2
        u2 = kr * b2 - ki * b1
        u1t = jnp.concatenate([u1] * CHUNK, axis=0)
        u2t = jnp.concatenate([u2] * CHUNK, axis=0)

        def bpow(sel):
            o = jnp.dot(sel, pw4, preferred_element_type=_F32)
            x = o[:, 0:STATE_W] + o[:, STATE_W:2 * STATE_W]
            y = o[:, 2 * STATE_W:3 * STATE_W] + o[:, 3 * STATE_W:4 * STATE_W]
            return x * u1t + y * u2t

        if d == 0:
            e_strip, e_b, e_c = blk_l, (CHUNK - 1) - blk_r, blk_l + 1
        else:
            e_strip, e_b, e_c = (CHUNK - 1) - blk_l, blk_r, CHUNK - blk_l
        u_hi, u_lo = split2(u1)
        k_hi, k_lo = split2(cpow(pick_l(e_strip)))
        strips.append(jnp.dot(u_hi, k_hi, preferred_element_type=_F32)
                      + jnp.dot(u_hi, k_lo, preferred_element_type=_F32)
                      + jnp.dot(u_lo, k_hi, preferred_element_type=_F32))
        bp = bpow(pick_r(e_b))
        bpow_ref[0, :, d * STATE_W:(d + 1) * STATE_W] = bp.astype(_BF16)
        bpow_ref[0, :, (2 + d) * STATE_W:(3 + d) * STATE_W] = pltpu.roll(bp, SSM_STATE, axis=1).astype(_BF16)
        cpow_ref[0, d * STATE_W:(d + 1) * STATE_W, :] = cpow(pick_l(e_c)).astype(_BF16)

        dec_ref[0, 2 * d:2 * d + 1, :] = pw_re[CHUNK:CHUNK + 1]
        dec_ref[0, 2 * d + 1:2 * d + 2, :] = pw_im[CHUNK:CHUNK + 1] * sgn_row

    zeros = jnp.zeros((SSM_GROUP, CHUNK_W), _F32)
    strip = (jnp.concatenate([strips[1], zeros], axis=1)
             + pltpu.roll(jnp.concatenate([strips[0], zeros], axis=1), CHUNK_W - SSM_GROUP, axis=1))
    for i in range(CHUNK):
        off = (CHUNK - 1 - i) * SSM_GROUP
        win = strip if off == 0 else pltpu.roll(strip, 2 * CHUNK_W - off, axis=1)
        toep_ref[0, i * SSM_GROUP:(i + 1) * SSM_GROUP, :] = win[:, 0:CHUNK_W].astype(_BF16)


def _s5_kernel(ux_ref, uc_ref, prow_ref, ct_ref, bt_ref, y_ref, u_ref, s_ref, h_ref, y2_ref,
               toep_ref, bpow_ref, cpow_ref, dec_ref):
    ng = S5_GROUPS_PER_STEP
    gw = SSM_GROUP
    x0 = N_CTX_CHUNKS * BATCH

    _s5_param_kernel(prow_ref, ct_ref, bt_ref, toep_ref, bpow_ref, cpow_ref, dec_ref)

    half = CHUNK // 2
    assert half == ng
    lanes = ng * 128

    r = lax.broadcasted_iota(jnp.int32, (lanes, 1), 0)
    swapped = ((r // gw) % ng) * 128 + (r // 128) * gw + r % gw
    perm = (lax.broadcasted_iota(jnp.int32, (lanes, lanes), 1) == swapped).astype(_BF16)

    def to_chunk_layout(src_ref, seq_len, n_chunks, row0):
        for hf in range(2):
            p = jnp.concatenate(
                [jnp.concatenate([src_ref[pl.ds(b * seq_len + hf * half + il, n_chunks, stride=CHUNK), :]
                                  for il in range(half)], axis=1) for b in range(BATCH)], axis=0).astype(_BF16)
            q = jnp.dot(p, perm, preferred_element_type=_F32)
            for b in range(BATCH):
                for g in range(ng):
                    u_ref[g, hf, pl.ds(row0 + b, n_chunks, stride=BATCH), :] = (
                        q[b * n_chunks:(b + 1) * n_chunks, g * 128:(g + 1) * 128])

    to_chunk_layout(uc_ref, CTX_LEN, N_CTX_CHUNKS, 0)
    to_chunk_layout(ux_ref, SEQ, N_X_CHUNKS, x0)

    chunk_rows = lambda ref, g, lo: jnp.concatenate([ref[g, 0, lo:, :], ref[g, 1, lo:, :]], axis=1)
    for g in range(ng):
        s_ref[g] = jnp.dot(chunk_rows(u_ref, g, 0).astype(_BF16), bpow_ref[g], preferred_element_type=_F32)

    pair = 2 * BATCH
    n_ctx_pairs = N_CTX_CHUNKS // 2
    n_pairs = N_SEQ_CHUNKS // 2
    lower = lax.broadcasted_iota(jnp.int32, (pair, STATE_W), 0) < BATCH
    zero = jnp.zeros((pair, STATE_W), _F32)

    def step(t, carry):
        tb = jnp.where(t < n_ctx_pairs, n_ctx_pairs - 1 - t, n_pairs + n_ctx_pairs - 1 - t)
        rf = pl.multiple_of(t * pair, pair)
        rb = pl.multiple_of(tb * pair, pair)
        out = []
        swap = lambda v: pltpu.roll(v, BATCH, axis=0)
        for g in range(ng):
            hf, hfs, hb, hbs = carry[4 * g:4 * g + 4]
            dg = dec_ref[g]

            def advance(h, hs, a_re, a_im, s, ss):
                return a_re * h + a_im * hs + s, a_re * hs - a_im * h + ss

            col = lambda r, k: s_ref[g, pl.ds(r, pair), k * STATE_W:(k + 1) * STATE_W]
            sf, sfs = col(rf, 0), col(rf, 2)
            mid, mids = advance(hf, hfs, dg[0:1], dg[1:2], jnp.where(lower, sf, swap(sf)),
                                jnp.where(lower, sfs, swap(sfs)))
            h_ref[g, pl.ds(rf, pair), 0:STATE_W] = jnp.where(lower, hf, mid)
            end, ends = advance(mid, mids, dg[0:1], dg[1:2], sf, sfs)
            hf, hfs = jnp.where(lower, swap(end), end), jnp.where(lower, swap(ends), ends)

            sb, sbs = col(rb, 1), col(rb, 3)
            mid, mids = advance(hb, hbs, dg[2:3], dg[3:4], jnp.where(lower, swap(sb), sb),
                                jnp.where(lower, swap(sbs), sbs))
            h_ref[g, pl.ds(rb, pair), STATE_W:2 * STATE_W] = jnp.where(lower, mid, hb)
            end, ends = advance(mid, mids, dg[2:3], dg[3:4], sb, sbs)
            hb, hbs = jnp.where(lower, end, swap(end)), jnp.where(lower, ends, swap(ends))
            out += [hf, hfs, hb, hbs]
        return tuple(out)

    lax.fori_loop(0, n_pairs, step, (zero,) * (4 * ng))

    for g in range(ng):
        ux = chunk_rows(u_ref, g, x0).astype(_BF16)
        hx = h_ref[g, x0:SCAN_ROWS, :].astype(_BF16)
        y2 = (jnp.dot(ux, toep_ref[g], preferred_element_type=_F32)
              + jnp.dot(hx, cpow_ref[g], preferred_element_type=_F32))
        y2_ref[g, 0] = y2[:, 0:128]
        y2_ref[g, 1] = y2[:, 128:256]

    for hf in range(2):
        yin = jnp.concatenate(
            [jnp.concatenate([y2_ref[g, hf, pl.ds(b, N_X_CHUNKS, stride=BATCH), :] for g in range(ng)], axis=1)
             for b in range(BATCH)], axis=0)
        y_hi = yin.astype(_BF16)
        y_lo = (yin - y_hi.astype(_F32)).astype(_BF16)
        q = (jnp.dot(y_hi, perm, preferred_element_type=_F32)
             + jnp.dot(y_lo, perm, preferred_element_type=_F32))
        for b in range(BATCH):
            for jl in range(half):
                y_ref[pl.ds(b * SEQ + hf * half + jl, N_X_CHUNKS, stride=CHUNK), :] = (
                    q[b * N_X_CHUNKS:(b + 1) * N_X_CHUNKS, jl * 128:(jl + 1) * 128])


def _s5_scan(u_x, u_c, prow, ct2, bt1):
    ng = S5_GROUPS_PER_STEP
    n_x_rows = N_X_CHUNKS * BATCH
    lanes = ng * SSM_GROUP
    g_spec = lambda shape: pl.BlockSpec(shape, lambda i: (0, i, 0, 0))
    col = lambda rows: pl.BlockSpec((rows, lanes), lambda i: (0, i))
    return pl.pallas_call(
        _s5_kernel,
        grid=(SSM_GROUPS // ng,),
        in_specs=[col(N_TOK), col(N_CTX_TOK), g_spec((2, ng, 8, STATE_W)),
                  g_spec((2, ng, STATE_W, SSM_GROUP)), g_spec((2, ng, SSM_GROUP, STATE_W))],
        out_specs=col(N_TOK),
        out_shape=jax.ShapeDtypeStruct((N_TOK, D_SSM), _F32),
        scratch_shapes=[pltpu.VMEM((ng, 2, SCAN_ROWS, 128), _F32),
                        pltpu.VMEM((ng, SCAN_ROWS, 4 * STATE_W), _F32),
                        pltpu.VMEM((ng, SCAN_ROWS, 2 * STATE_W), _F32),
                        pltpu.VMEM((ng, 2, n_x_rows, 128), _F32),
                        pltpu.VMEM((ng, CHUNK_W, CHUNK_W), _BF16),
                        pltpu.VMEM((ng, CHUNK_W, 4 * STATE_W), _BF16),
                        pltpu.VMEM((ng, 2 * STATE_W, CHUNK_W), _BF16),
                        pltpu.VMEM((ng, 4, STATE_W), _F32)],
        compiler_params=_cparams("arbitrary"),
        name="s5_scan",
    )(u_x, u_c, prow, ct2, bt1)


def _merge_kernel(ys_ref, u_ref, conv_ref, x_ref, mod_ref, dskip_ref, mixg_ref, n2g_ref,
                  wglu_ref, wout_ref, rwt_ref, x1_ref, h2_ref, logit_ref):
    m = mod_ref[0]
    yx = dskip_ref[...] * u_ref[...] + ys_ref[...]
    c0 = math.sqrt(2.0 / math.pi)
    ge = 0.5 * yx * (1.0 + jnp.tanh(c0 * (yx + 0.044715 * (yx * yx * yx))))
    z = jnp.dot(ge.astype(_BF16), wglu_ref[...], preferred_element_type=_F32)
    ssm_y = z[:, 0:D_SSM] * jax.nn.sigmoid(z[:, D_SSM:2 * D_SSM])
    conv_y = conv_ref[...]
    mixg = mixg_ref[...]
    heads_a = (ssm_y * _rms_scale(ssm_y) * mixg[:, 0:D_SSM]).astype(_BF16)
    heads_b = (conv_y * _rms_scale(conv_y) * mixg[:, D_SSM:]).astype(_BF16)
    mix = (jnp.dot(heads_a, wout_ref[0:D_SSM, :], preferred_element_type=_F32)
           + jnp.dot(heads_b, wout_ref[D_SSM:, :], preferred_element_type=_F32))
    x1 = x_ref[...] + m[2:3] * mix
    x1_ref[...] = x1
    h2 = x1 * _rms_scale(x1) * n2g_ref[...] * (1.0 + m[4:5]) + m[3:4]
    h2_ref[...] = h2
    h_hi = h2.astype(_BF16)
    h_lo = (h2 - h_hi.astype(_F32)).astype(_BF16)
    nt = (((1,), (1,)), ((), ()))
    p = lax.dot_general(rwt_ref[...], h_hi, nt, preferred_element_type=_F32)
    q = lax.dot_general(rwt_ref[0:N_EXPERTS, :], h_lo, nt, preferred_element_type=_F32)
    logit_ref[...] = p[0:N_EXPERTS] + p[N_EXPERTS:2 * N_EXPERTS] + q


def _merge(ys, u, conv, x2, mod3, dskip, mixg, n2g, wglu_bf, wout_bf, rwt):
    tiles_per_batch = SEQ // TM
    tok = lambda w: pl.BlockSpec((TM, w), lambda i: (i, 0))
    return pl.pallas_call(
        _merge_kernel,
        grid=(N_TOK // TM,),
        in_specs=[tok(D_SSM), tok(D_SSM), tok(D_CONV), tok(D_MODEL),
                  pl.BlockSpec((1, N_MOD, D_MODEL), lambda i: (i // tiles_per_batch, 0, 0)),
                  _const_spec((1, D_SSM)), _const_spec((1, D_MODEL)), _const_spec((1, D_MODEL)),
                  _const_spec((D_SSM, 2 * D_SSM)), _const_spec((D_MODEL, D_MODEL)),
                  _const_spec((2 * N_EXPERTS, D_MODEL))],
        out_specs=[tok(D_MODEL), tok(D_MODEL),
                   pl.BlockSpec((N_EXPERTS, TM), lambda i: (0, i))],
        out_shape=[jax.ShapeDtypeStruct((N_TOK, D_MODEL), _F32),
                   jax.ShapeDtypeStruct((N_TOK, D_MODEL), _F32),
                   jax.ShapeDtypeStruct((N_EXPERTS, N_TOK), _F32)],
        compiler_params=_cparams("arbitrary"),
        name="merge_heads",
    )(ys, u, conv, x2, mod3, dskip, mixg, n2g, wglu_bf, wout_bf, rwt)


def _route_kernel(logit_ref, bias_ref, eid_ref, w_ref, rank_ref, cnt_ref, carry_ref):
    @pl.when(pl.program_id(0) == 0)
    def _():
        carry_ref[...] = jnp.zeros_like(carry_ref)

    tm = logit_ref.shape[1]
    neg = jnp.float32(-jnp.inf)
    scores = jax.nn.sigmoid(logit_ref[...])
    biased = scores + bias_ref[:, 0:1]
    sub = lax.broadcasted_iota(jnp.int32, (GROUP_SIZE, tm), 0)
    rows = lambda a, g: a[g * GROUP_SIZE:(g + 1) * GROUP_SIZE]
    ngrp = N_EXPERT_GROUPS

    gscore = []
    for g in range(ngrp):
        bg = rows(biased, g)
        m1 = jnp.max(bg, axis=0, keepdims=True)
        first = jnp.min(jnp.where(bg == m1, sub, GROUP_SIZE), axis=0, keepdims=True)
        m2 = jnp.max(jnp.where(sub == first, neg, bg), axis=0, keepdims=True)
        gscore.append(m1 + m2)
    v = []
    for g in range(ngrp):
        beaten = jnp.zeros((1, tm), jnp.int32)
        for o in range(ngrp):
            if o != g:
                beats = (gscore[o] >= gscore[g]) if o < g else (gscore[o] > gscore[g])
                beaten = beaten + beats.astype(jnp.int32)
        v.append(jnp.where(beaten < TOPK_GROUPS, rows(biased, g), neg))
    eids = [sub + g * GROUP_SIZE for g in range(ngrp)]

    pick_ids, pick_masks = [], []
    for _ in range(TOP_K):
        m = functools.reduce(jnp.maximum, v)
        m = jnp.max(m, axis=0, keepdims=True)
        cand = functools.reduce(jnp.minimum, [jnp.where(v[g] == m, eids[g], N_EXPERTS) for g in range(ngrp)])
        pick_id = jnp.min(cand, axis=0, keepdims=True)
        masks = [eids[g] == pick_id for g in range(ngrp)]
        v = [jnp.where(masks[g], neg, v[g]) for g in range(ngrp)]
        pick_ids.append(pick_id)
        pick_masks.append(masks)

    sel = jnp.concatenate(
        [functools.reduce(jnp.logical_or, [pick_masks[k][g] for k in range(TOP_K)]).astype(_F32)
         for g in range(ngrp)], axis=0)
    before = (lax.broadcasted_iota(jnp.int32, (tm, tm), 0)
              < lax.broadcasted_iota(jnp.int32, (tm, tm), 1)).astype(_BF16)
    base = jnp.dot(sel.astype(_BF16), before, preferred_element_type=_F32) + carry_ref[:, 0:1]

    def gather_pick(a, k):
        parts = [jnp.sum(jnp.where(pick_masks[k][g], rows(a, g), 0.0), axis=0, keepdims=True)
                 for g in range(ngrp)]
        return functools.reduce(jnp.add, parts)

    picked = [gather_pick(scores, k) for k in range(TOP_K)]
    denom = functools.reduce(jnp.add, picked)
    for k in range(TOP_K):
        eid_ref[k:k + 1, :] = pick_ids[k]
        w_ref[k:k + 1, :] = picked[k] / denom * ROUTED_SCALE
        rank_ref[k:k + 1, :] = gather_pick(base, k).astype(jnp.int32)
    carry_ref[...] = carry_ref[...] + jnp.sum(sel, axis=1, keepdims=True)
    cnt_ref[...] = carry_ref[...]


def _route(logits_t, bias_col):
    tok = pl.BlockSpec((TOP_K, TM), lambda i: (0, i))
    return pl.pallas_call(
        _route_kernel,
        grid=(N_TOK // TM,),
        in_specs=[pl.BlockSpec((N_EXPERTS, TM), lambda i: (0, i)), _const_spec((N_EXPERTS, 128))],
        out_specs=[tok, tok, tok, pl.BlockSpec((N_EXPERTS, 128), lambda i: (0, 0))],
        out_shape=[jax.ShapeDtypeStruct((TOP_K, N_TOK), jnp.int32),
                   jax.ShapeDtypeStruct((TOP_K, N_TOK), _F32),
                   jax.ShapeDtypeStruct((TOP_K, N_TOK), jnp.int32),
                   jax.ShapeDtypeStruct((N_EXPERTS, 128), _F32)],
        scratch_shapes=[pltpu.VMEM((N_EXPERTS, 128), _F32)],
        compiler_params=_cparams("arbitrary"),
        name="route",
    )(logits_t, bias_col)


def _rowmap_kernel(dest_ref, padstart_ref, padn_ref, h_ref, x1_ref, mod_ref, wsg_ref, wsu_ref, wsd_ref,
                   xmid_ref, codes_ref):
    i = pl.program_id(0)
    tm = h_ref.shape[0]
    base = i * tm

    @pl.when(i == 0)
    def _():
        def fill(e, done):
            def one(r, c):
                codes_ref[padstart_ref[e] + r] = N_ASSIGN + done + r
                return c
            lax.fori_loop(0, padn_ref[e], one, 0)
            return done + padn_ref[e]
        lax.fori_loop(0, N_EXPERTS, fill, 0)
        for r in range(EXPERT_BM):
            codes_ref[N_ROWS + r] = N_ROWS + r

    for t in range(tm):
        ds = [dest_ref[k * N_TOK + base + t] for k in range(TOP_K)]
        for k in range(TOP_K):
            codes_ref[ds[k]] = k * N_TOK + base + t

    h = h_ref[...].astype(_BF16)
    a = jnp.dot(h, wsg_ref[...], preferred_element_type=_F32)
    b = jnp.dot(h, wsu_ref[...], preferred_element_type=_F32)
    shared = jnp.dot((_silu(a) * b).astype(_BF16), wsd_ref[...], preferred_element_type=_F32)
    xmid_ref[...] = x1_ref[...] + mod_ref[0][5:6] * shared


def _rowmap(dest, pad_start, pad_n, h2, x1, mod3, wsg_bf, wsu_bf, wsd_bf):
    tm = 2 * TM
    tiles_per_batch = SEQ // tm
    tok = pl.BlockSpec((tm, D_MODEL), lambda i, *_: (i, 0))
    const = lambda shape: pl.BlockSpec(shape, lambda i, *_: (0,) * len(shape), pipeline_mode=pl.Buffered(1))
    grid_spec = pltpu.PrefetchScalarGridSpec(
        num_scalar_prefetch=3,
        grid=(N_TOK // tm,),
        in_specs=[tok, tok,
                  pl.BlockSpec((1, N_MOD, D_MODEL), lambda i, *_: (i // tiles_per_batch, 0, 0)),
                  const((D_MODEL, D_SHARED)), const((D_MODEL, D_SHARED)), const((D_SHARED, D_MODEL))],
        out_specs=[tok, pl.BlockSpec(memory_space=pltpu.SMEM)],
    )
    return pl.pallas_call(
        _rowmap_kernel,
        grid_spec=grid_spec,
        out_shape=[jax.ShapeDtypeStruct((N_TOK, D_MODEL), _F32),
                   jax.ShapeDtypeStruct((N_ROWS + EXPERT_BM,), jnp.int32)],
        compiler_params=_cparams("arbitrary"),
        name="rowmap",
    )(dest.reshape(N_ASSIGN), pad_start, pad_n, h2, x1, mod3, wsg_bf, wsu_bf, wsd_bf)


def _experts_kernel(be_ref, nact_ref, first_ref, slot_ref, nxt_ref, codes_ref, h_hbm, wg_hbm, wu_hbm, wd_hbm,
                    out_hbm, *scratch):
    depth = EXPERT_DEPTH
    xbuf, ybuf = scratch[:depth], scratch[depth:2 * depth]
    wg_f, wu_f, wd_f, wg_bf, wu_bf, wd_bf, sem, gsem, ssem = scratch[2 * depth:]
    n_act = nact_ref[0]
    bm = EXPERT_BM

    def fetch(e, s):
        return (pltpu.make_async_copy(wg_hbm.at[e], wg_f.at[s], sem.at[s, 0]),
                pltpu.make_async_copy(wu_hbm.at[e], wu_f.at[s], sem.at[s, 1]),
                pltpu.make_async_copy(wd_hbm.at[e], wd_f.at[s], sem.at[s, 2]))

    def gather(blk, p):
        for r in range(bm):
            tok = codes_ref[blk * bm + r] & (N_TOK - 1)
            pltpu.make_async_copy(h_hbm.at[pl.ds(tok, 1), :], xbuf[p].at[pl.ds(r, 1), :],
                                  gsem.at[p]).start(priority=r % 2)

    def gather_wait(p):
        pltpu.make_async_copy(h_hbm.at[pl.ds(0, bm), :], xbuf[p], gsem.at[p]).wait()

    def scatter(blk, p):
        for r in range(bm):
            pltpu.make_async_copy(ybuf[p].at[pl.ds(r, 1), :], out_hbm.at[pl.ds(codes_ref[blk * bm + r], 1), :],
                                  ssem.at[p]).start(priority=r % 2)

    def scatter_wait(p):
        pltpu.make_async_copy(ybuf[p], out_hbm.at[pl.ds(0, bm), :], ssem.at[p]).wait()

    def block(blk, p):
        prev = (p + depth - 1) % depth

        @pl.when(blk < n_act)
        def _():
            @pl.when(blk == 0)
            def _():
                for cp in fetch(be_ref[0], 0):
                    cp.start()
                for j in range(depth - 1):
                    gather(jnp.minimum(j, n_act - 1), j)
                ybuf[depth - 1][...] = jnp.zeros_like(ybuf[depth - 1])

            @pl.when(first_ref[blk] == 1)
            def _():
                s = slot_ref[blk]
                for cp in fetch(be_ref[blk], s):
                    cp.wait()

                @pl.when(nxt_ref[blk] >= 0)
                def _():
                    for cp in fetch(nxt_ref[blk], 1 - s):
                        cp.start()

                wg_bf[...] = wg_f[s].astype(_BF16)
                wu_bf[...] = wu_f[s].astype(_BF16)
                wd_bf[...] = wd_f[s].astype(_BF16)

            gather_wait(p)

            @pl.when(blk >= depth - 1)
            def _():
                scatter_wait(p)

            gather(jnp.minimum(blk + depth - 1, n_act - 1), prev)
            scatter(jnp.where(blk == 0, N_BLOCKS, blk - 1), prev)
            x = xbuf[p][...].astype(_BF16)
            a = jnp.dot(x, wg_bf[...], preferred_element_type=_F32)
            b = jnp.dot(x, wu_bf[...], preferred_element_type=_F32)
            ybuf[p][...] = jnp.dot((_silu(a) * b).astype(_BF16), wd_bf[...], preferred_element_type=_F32)

            @pl.when(blk == n_act - 1)
            def _():
                scatter(blk, p)
                for j in range(1, depth):
                    gather_wait((p + j) % depth)
                for j in range(depth - 1):
                    pl.when(blk >= j)(functools.partial(scatter_wait, (p + depth - 1 - j) % depth))
                scatter_wait(p)

    i = pl.program_id(0)
    for q in range(depth):
        block(depth * i + q, q)


def _experts(block_e, n_active, first, slot, nxt, codes, h2, w_gate, w_up, w_down):
    bm = EXPERT_BM
    depth = EXPERT_DEPTH
    hbm = pl.BlockSpec(memory_space=pl.ANY)
    grid_spec = pltpu.PrefetchScalarGridSpec(
        num_scalar_prefetch=6,
        grid=(N_BLOCKS // depth,),
        in_specs=[hbm, hbm, hbm, hbm],
        out_specs=hbm,
        scratch_shapes=[pltpu.VMEM((bm, D_MODEL), _F32)] * (2 * depth) + [
                        pltpu.VMEM((2, D_MODEL, D_EXPERT), _F32), pltpu.VMEM((2, D_MODEL, D_EXPERT), _F32),
                        pltpu.VMEM((2, D_EXPERT, D_MODEL), _F32),
                        pltpu.VMEM((D_MODEL, D_EXPERT), _BF16), pltpu.VMEM((D_MODEL, D_EXPERT), _BF16),
                        pltpu.VMEM((D_EXPERT, D_MODEL), _BF16),
                        pltpu.SemaphoreType.DMA((2, 3)), pltpu.SemaphoreType.DMA((depth,)),
                        pltpu.SemaphoreType.DMA((depth,))],
    )
    return pl.pallas_call(
        _experts_kernel,
        grid_spec=grid_spec,
        out_shape=jax.ShapeDtypeStruct((N_ROWS + bm, D_MODEL), _F32),
        compiler_params=_cparams("arbitrary"),
        name="experts",
    )(block_e, n_active, first, slot, nxt, codes, h2, w_gate, w_up, w_down)


def _final_kernel(xmid_ref, wt_ref, mod_ref, fg_ref, *refs):
    picks, o_ref = refs[:TOP_K], refs[TOP_K]
    wt = wt_ref[...]
    routed = wt[:, 0:1] * picks[0][...]
    for k in range(1, TOP_K):
        routed = routed + wt[:, k:k + 1] * picks[k][...]
    y = xmid_ref[...] + mod_ref[0][5:6] * routed
    o_ref[...] = y * _rms_scale(y) * fg_ref[...]


def _final(xmid, w_tok, mod3, final_g, ys):
    tm = TM_COMBINE
    tiles = N_TOK // tm
    tiles_per_batch = SEQ // tm
    tok = lambda w: pl.BlockSpec((tm, w), lambda i: (i, 0))
    pick = lambda k: pl.BlockSpec((tm, D_MODEL), lambda i: (k * tiles + i, 0))
    return pl.pallas_call(
        _final_kernel,
        grid=(tiles,),
        in_specs=[tok(D_MODEL), tok(TOP_K),
                  pl.BlockSpec((1, N_MOD, D_MODEL), lambda i: (i // tiles_per_batch, 0, 0)),
                  _const_spec((1, D_MODEL))] + [pick(k) for k in range(TOP_K)],
        out_specs=tok(D_MODEL),
        out_shape=jax.ShapeDtypeStruct((N_TOK, D_MODEL), _F32),
        compiler_params=_cparams("arbitrary"),
        name="final",
    )(xmid, w_tok, mod3, final_g, *([ys] * TOP_K))


def _s5_param_layouts(lam_re, lam_im, b_re, b_im, c_re, c_im, log_dt):
    two = lambda a: jnp.concatenate([a, a], axis=-1)
    lr, li = two(lam_re), two(lam_im)
    dtb = jnp.broadcast_to(log_dt.astype(_F32)[:, :, None], lr.shape)
    zeros = jnp.zeros_like(lr)
    prow = jnp.stack([lr, li, dtb] + [zeros] * 5, axis=2)
    ct2 = jnp.concatenate([jnp.swapaxes(c_re, -1, -2), jnp.swapaxes(c_im, -1, -2)], axis=2)
    bt1 = jnp.concatenate([jnp.swapaxes(b_re, -1, -2), jnp.swapaxes(b_im, -1, -2)], axis=3)
    return prow, ct2, bt1


def kernel(x, c, ctx, c_ctx, norm1_g, norm2_g, w_ada, b_ada, w_in, ssm_lam_re, ssm_lam_im, ssm_b_re, ssm_b_im, ssm_c_re, ssm_c_im, ssm_log_dt, ssm_d, ssm_w_glu, conv_w, conv_b, mix_norm_g, w_out, router_w, router_bias, exp_w_gate, exp_w_up, exp_w_down, shared_w_gate, shared_w_up, shared_w_down, final_g):
    layer = 0
    x2 = x.reshape(N_TOK, D_MODEL)
    ctx2 = ctx.reshape(N_CTX_TOK, D_MODEL)

    c8 = jnp.concatenate([c, c_ctx[None, :], jnp.zeros((8 - BATCH - 1, D_MODEL), _F32)], axis=0)
    mod = _ada_mod(c8, w_ada[layer], b_ada[layer][None, :])
    mod3 = mod.reshape(8, N_MOD, D_MODEL)

    w_in_bf = w_in[layer].astype(_BF16)
    conv_w8 = jnp.concatenate([conv_w[layer], jnp.zeros((8 - conv_w.shape[1], D_CONV), _F32)], axis=0)
    u_x, conv_x = _inproj(x2, mod3, norm1_g[layer][None, :], w_in_bf, conv_w8, conv_b[layer][None, :])
    u_c = _ctxproj(ctx2, mod3, norm1_g[layer][None, :], w_in_bf[:, :D_SSM])

    prow, ct2, bt1 = _s5_param_layouts(ssm_lam_re[layer], ssm_lam_im[layer], ssm_b_re[layer],
                                       ssm_b_im[layer], ssm_c_re[layer], ssm_c_im[layer], ssm_log_dt[layer])
    ys = _s5_scan(u_x, u_c, prow, ct2, bt1)

    rw = router_w[layer]
    rw_hi = rw.astype(_BF16)
    rw_lo = (rw - rw_hi.astype(_F32)).astype(_BF16)
    rwt = jnp.concatenate([rw_hi.T, rw_lo.T], axis=0)
    x1, h2, logits_t = _merge(ys, u_x, conv_x, x2, mod3, ssm_d[layer][None, :],
                              mix_norm_g[layer][None, :], norm2_g[layer][None, :],
                              ssm_w_glu[layer].astype(_BF16), w_out[layer].astype(_BF16), rwt)

    bias_col = jnp.broadcast_to(router_bias[layer][:, None], (N_EXPERTS, 128))
    eid, w_k, rank, cnt = _route(logits_t, bias_col)

    counts = cnt[:, 0].astype(jnp.int32)
    padded = (counts + EXPERT_BM - 1) // EXPERT_BM * EXPERT_BM
    pend = jnp.cumsum(padded)
    pstart = pend - padded
    is_e = eid[:, :, None] == jnp.arange(N_EXPERTS, dtype=jnp.int32)
    dest = jnp.sum(jnp.where(is_e, pstart, 0), axis=-1) + rank
    n_active = (pend[-1] // EXPERT_BM).astype(jnp.int32)
    blk = jnp.minimum(jnp.arange(N_BLOCKS, dtype=jnp.int32), n_active - 1)
    ends_before = (pend[None, :] <= (blk * EXPERT_BM)[:, None]).astype(jnp.int32)
    block_e = jnp.minimum(jnp.sum(ends_before, axis=1), N_EXPERTS - 1)

    xmid, codes = _rowmap(dest, pstart + counts, padded - counts, h2, x1, mod3,
                          shared_w_gate[layer].astype(_BF16), shared_w_up[layer].astype(_BF16),
                          shared_w_down[layer].astype(_BF16))
    first = jnp.concatenate([jnp.ones((1,), jnp.int32), (block_e[1:] != block_e[:-1]).astype(jnp.int32)])
    slot = (jnp.cumsum(first) - 1) % 2
    e_ids = jnp.arange(N_EXPERTS, dtype=jnp.int32)
    owner = jnp.where(padded > 0, e_ids, N_EXPERTS)
    later = jnp.min(jnp.where(e_ids[None, :] > e_ids[:, None], owner[None, :], N_EXPERTS), axis=1)
    nxt_e = jnp.where(later == N_EXPERTS, -1, later)
    nxt = jnp.sum(jnp.where(block_e[:, None] == e_ids[None, :], nxt_e[None, :], 0), axis=1)
    ys_rows = _experts(block_e, n_active[None], first, slot, nxt, codes, h2,
                       exp_w_gate[layer], exp_w_up[layer], exp_w_down[layer])
    out = _final(xmid, w_k.T, mod3, final_g[None, :], ys_rows)
    return out.reshape(BATCH, SEQ, D_MODEL)
```
